```python
import jax
import jax.numpy as jnp
from jax import lax
import numpy as np

D_MODEL = 2048
BATCH = 2
SEQ = 4096
DEPTH = 1

N_DIR = 2
HG_HEADS = 8
HG_DK = 128
HG_DV = 128
HG_KW = HG_HEADS * HG_DK
HG_VW = HG_HEADS * HG_DV
HG_CHUNK = 64
RG_WIDTH = D_MODEL // 2
RG_BLOCKS = 8
RG_BW = RG_WIDTH // RG_BLOCKS
RG_CONV = 4
RG_C = 8.0
N_GROUPS = 4
EXPERTS_PER_GROUP = 8
N_EXPERTS = N_GROUPS * EXPERTS_PER_GROUP
TOP_K = 2
D_EXPERT = 1024
MOE_BLOCK = 128
EPS = 1e-6
IN_SPLITS = (HG_KW, HG_VW, HG_KW, HG_KW, HG_VW, RG_WIDTH, RG_WIDTH, D_MODEL, D_MODEL)
IN_COLS = 3 * HG_KW + 2 * HG_VW + 2 * RG_WIDTH + 2 * D_MODEL

kernel_name = 'hybrid_hgrn2_rglru_hmoe_encoder'


def rmsnorm(x, gain):
    xf = x.astype(jnp.float32)
    y = xf * lax.rsqrt(jnp.mean(xf * xf, axis=-1, keepdims=True) + EPS)
    return (y * gain.astype(jnp.float32)).astype(x.dtype)


def modulate(h, shift, scale):
    return h * (1 + scale[:, None, :]) + shift[:, None, :]


def align_dirs(t, seq_axis):
    return jnp.stack([t[0], jnp.flip(t[1], axis=seq_axis)])


def to_heads(t, n_heads):
    t = t.reshape(t.shape[:-1] + (n_heads, t.shape[-1] // n_heads))
    return jnp.swapaxes(t, -3, -2)


def hgrn2_chunk_scan(q, k, v, log_f):
    nd, b, h, s, dk = q.shape
    dv = v.shape[-1]
    nc = s // HG_CHUNK

    def to_chunks(t):
        t = t.astype(jnp.float32).reshape(nd, b, h, nc, HG_CHUNK, t.shape[-1])
        return jnp.moveaxis(t, 3, 0)

    lower = jnp.tril(jnp.ones((HG_CHUNK, HG_CHUNK), dtype=bool))

    def step(state, chunk):
        qc, kc, vc, lfc = chunk
        cum = jnp.cumsum(lfc, axis=-2)
        o_inter = jnp.einsum('...tk,...kv->...tv', qc * jnp.exp(cum), state)
        diff = cum[..., :, None, :] - cum[..., None, :, :]
        decay = jnp.exp(jnp.where(lower[:, :, None], diff, -jnp.inf))
        scores = jnp.einsum('...tk,...tsk,...sk->...ts', qc, decay, kc)
        o_intra = jnp.einsum('...ts,...sv->...tv', scores, vc)
        last = cum[..., -1:, :]
        k_dec = kc * jnp.exp(last - cum)
        new_state = (jnp.exp(last[..., 0, :])[..., None] * state
                     + jnp.einsum('...sk,...sv->...kv', k_dec, vc))
        return new_state, o_inter + o_intra

    state0 = jnp.zeros((nd, b, h, dk, dv), jnp.float32)
    _, out = lax.scan(step, state0, (to_chunks(q), to_chunks(k), to_chunks(v), to_chunks(log_f)))
    return jnp.moveaxis(out, 0, 3).reshape(nd, b, h, s, dv)


def hgrn2_branch(q_pre, i_pre, f_fwd, f_bwd, g_pre, lb, hg_norm):
    bsz, s, _ = q_pre.shape
    q = to_heads(jax.nn.silu(q_pre), HG_HEADS)
    v = to_heads(i_pre, HG_HEADS)
    lbb = lb[:, None, None, :]
    f = lbb + (1 - lbb) * jax.nn.sigmoid(jnp.stack([f_fwd, f_bwd]).astype(jnp.float32))
    log_f = to_heads(jnp.log(f), HG_HEADS)
    k = to_heads(1 - f, HG_HEADS)
    out = hgrn2_chunk_scan(align_dirs(jnp.stack([q, q]), 2), align_dirs(k, 2),
                           align_dirs(jnp.stack([v, v]), 2), align_dirs(log_f, 2))
    out = align_dirs(out, 2)
    o = jnp.swapaxes(out[0] + out[1], 1, 2)
    o = rmsnorm(o, hg_norm).reshape(bsz, s, HG_VW)
    return o * jax.nn.silu(g_pre)


def lru_combine(e1, e2):
    a1, b1 = e1
    a2, b2 = e2
    return a1 * a2, a2 * b1 + b2


def rg_lru_branch(x_pre, y_pre, conv_w, conv_b, w_a, b_a, w_x, b_x, lam):
    bsz, s, _ = x_pre.shape
    pad = (RG_CONV // 2, RG_CONV - 1 - RG_CONV // 2)
    xc = lax.conv_general_dilated(x_pre, conv_w[:, None, :].astype(x_pre.dtype), (1,), [pad],
                                  dimension_numbers=('NWC', 'WIO', 'NWC'),
                                  feature_group_count=RG_WIDTH) + conv_b
    xb = xc.reshape(bsz, s, RG_BLOCKS, RG_BW)
    r = jax.nn.sigmoid(jnp.einsum('bshi,zhij->zbshj', xb, w_a) + b_a[:, None, None])
    gi = jax.nn.sigmoid(jnp.einsum('bshi,zhij->zbshj', xb, w_x) + b_x[:, None, None])
    r = r.reshape(N_DIR, bsz, s, RG_WIDTH).astype(jnp.float32)
    gi = gi.reshape(N_DIR, bsz, s, RG_WIDTH)
    log_a = -RG_C * r * jax.nn.softplus(-lam.astype(jnp.float32))[:, None, None, :]
    a = jnp.exp(log_a)
    u = jnp.sqrt(-jnp.expm1(2.0 * log_a)) * (gi * xc[None])
    _, hs = lax.associative_scan(lru_combine, (align_dirs(a, 1), align_dirs(u, 1)), axis=2)
    hs = align_dirs(hs, 1)
    return (hs[0] + hs[1]) * jax.nn.gelu(y_pre)


def hybrid_mixer(h, w_in, lb, hg_norm, conv_w, conv_b, w_a, b_a, w_x, b_x, lam,
                 w_proj_a, w_proj_b, w_out):
    split_points = [int(p) for p in np.cumsum(IN_SPLITS)[:-1]]
    proj = h @ w_in
    q_pre, i_pre, f_fwd, f_bwd, g_pre, rx, ry, ga, gb = jnp.split(proj, split_points, axis=-1)
    y_a = hgrn2_branch(q_pre, i_pre, f_fwd, f_bwd, g_pre, lb, hg_norm) @ w_proj_a
    y_b = rg_lru_branch(rx, ry, conv_w, conv_b, w_a, b_a, w_x, b_x, lam) @ w_proj_b
    merged = jax.nn.sigmoid(ga) * y_a + jax.nn.sigmoid(gb) * y_b
    return merged @ w_out


def hier_moe(h, w_group, w_router, w_gate, w_up, w_down):
    bsz, s, d = h.shape
    n = bsz * s
    hf = h.reshape(n, d)
    hf32 = hf.astype(jnp.float32)
    group_logits = hf32 @ w_group.astype(jnp.float32)
    group_prob = jax.nn.softmax(group_logits, axis=-1)
    _, g_idx = lax.top_k(group_logits, 1)
    g_sel = g_idx[:, 0]
    p_group = jnp.max(group_prob, axis=-1)
    all_logits = jnp.einsum('nd,dge->nge', hf32, w_router.astype(jnp.float32))
    sel_logits = jnp.einsum('nge,ng->ne', all_logits, jax.nn.one_hot(g_sel, N_GROUPS, dtype=jnp.float32))
    top_vals, top_idx = lax.top_k(sel_logits, TOP_K)
    weights = p_group[:, None] * jax.nn.softmax(top_vals, axis=-1)
    expert_id = g_sel[:, None] * EXPERTS_PER_GROUP + top_idx

    a = n * TOP_K
    e_flat = expert_id.reshape(a)
    tok_flat = jnp.repeat(jnp.arange(n, dtype=jnp.int32), TOP_K)
    w_flat = weights.reshape(a)
    order = jnp.argsort(e_flat)
    e_sorted = e_flat[order]
    counts = jnp.bincount(e_flat, length=N_EXPERTS)
    padded = (counts + MOE_BLOCK - 1) // MOE_BLOCK * MOE_BLOCK
    starts = jnp.cumsum(counts) - counts
    pad_ends = jnp.cumsum(padded)
    pad_starts = pad_ends - padded
    dest = pad_starts[e_sorted] + (jnp.arange(a) - starts[e_sorted])
    p = ((a + MOE_BLOCK - 1) // MOE_BLOCK + N_EXPERTS) * MOE_BLOCK
    row_tok = jnp.full((p,), n, jnp.int32).at[dest].set(tok_flat[order])
    row_w = jnp.zeros((p,), jnp.float32).at[dest].set(w_flat[order])
    nb = p // MOE_BLOCK
    block_expert = jnp.minimum(jnp.searchsorted(pad_ends, jnp.arange(nb) * MOE_BLOCK, side='right'),
                               N_EXPERTS - 1)
    h_pad = jnp.concatenate([hf, jnp.zeros((1, d), hf.dtype)], axis=0)
    xs = h_pad[row_tok].reshape(nb, MOE_BLOCK, d)

    def expert_block(args):
        xb, e = args
        hid = jax.nn.silu(xb @ w_gate[e]) * (xb @ w_up[e])
        return hid @ w_down[e]

    ys = lax.map(expert_block, (xs, block_expert)).reshape(p, d)
    out = jax.ops.segment_sum(ys * row_w[:, None], row_tok, num_segments=n + 1)[:n]
    return out.reshape(bsz, s, d).astype(h.dtype)


def setup_inputs(seed: int = 0) -> dict:
    key = jax.random.key(seed)
    ks = jax.random.split(key, 32)
    f32 = jnp.float32
    L = DEPTH

    def nrm(k, shape, fan_in):
        return jax.random.normal(k, shape, f32) * fan_in ** -0.5

    def small(k, shape):
        return 0.01 * jax.random.normal(k, shape, f32)

    a0 = jax.random.uniform(ks[15], (L, N_DIR, RG_WIDTH), f32, minval=0.9, maxval=0.999)
    return {
        'x': jax.random.normal(ks[0], (BATCH, SEQ, D_MODEL), f32),
        'c': jax.random.normal(ks[1], (BATCH, D_MODEL), f32),
        'w_ada': nrm(ks[2], (L, D_MODEL, 6 * D_MODEL), D_MODEL),
        'b_ada': small(ks[3], (L, 6 * D_MODEL)),
        'norm1': 1.0 + small(ks[4], (L, D_MODEL)),
        'w_in': nrm(ks[5], (L, D_MODEL, IN_COLS), D_MODEL),
        'hg_lb': 0.5 * jax.random.normal(ks[6], (L + 1, N_DIR, HG_KW), f32),
        'hg_norm': 1.0 + small(ks[7], (L, HG_HEADS, HG_DV)),
        'rg_conv_w': nrm(ks[8], (L, RG_CONV, RG_WIDTH), RG_CONV),
        'rg_conv_b': small(ks[9], (L, RG_WIDTH)),
        'rg_w_a': nrm(ks[10], (L, N_DIR, RG_BLOCKS, RG_BW, RG_BW), RG_BW),
        'rg_b_a': small(ks[11], (L, N_DIR, RG_BLOCKS, RG_BW)),
        'rg_w_x': nrm(ks[12], (L, N_DIR, RG_BLOCKS, RG_BW, RG_BW), RG_BW),
        'rg_b_x': small(ks[13], (L, N_DIR, RG_BLOCKS, RG_BW)),
        'rg_lambda': jnp.log(a0) - jnp.log1p(-a0),
        'w_proj_a': nrm(ks[16], (L, HG_VW, D_MODEL), HG_VW),
        'w_proj_b': nrm(ks[17], (L, RG_WIDTH, D_MODEL), RG_WIDTH),
        'w_out': nrm(ks[18], (L, D_MODEL, D_MODEL), D_MODEL),
        'norm2': 1.0 + small(ks[19], (L, D_MODEL)),
        'w_group': nrm(ks[20], (L, D_MODEL, N_GROUPS), D_MODEL),
        'w_router': nrm(ks[21], (L, D_MODEL, N_GROUPS, EXPERTS_PER_GROUP), D_MODEL),
        'w_gate': nrm(ks[22], (L, N_EXPERTS, D_MODEL, D_EXPERT), D_MODEL),
        'w_up': nrm(ks[23], (L, N_EXPERTS, D_MODEL, D_EXPERT), D_MODEL),
        'w_down': nrm(ks[24], (L, N_EXPERTS, D_EXPERT, D_MODEL), D_EXPERT),
        'final_norm': 1.0 + small(ks[25], (D_MODEL,)),
    }


def reference(x, c, w_ada, b_ada, norm1, w_in, hg_lb, hg_norm, rg_conv_w, rg_conv_b,
              rg_w_a, rg_b_a, rg_w_x, rg_b_x, rg_lambda, w_proj_a, w_proj_b, w_out,
              norm2, w_group, w_router, w_gate, w_up, w_down, final_norm):
    lb_all = jnp.cumsum(jax.nn.softmax(hg_lb.astype(jnp.float32), axis=0), axis=0)
    cond = jax.nn.silu(c)
    for layer in range(DEPTH):
        mod = cond @ w_ada[layer] + b_ada[layer]
        shift1, scale1, gate1, shift2, scale2, gate2 = jnp.split(mod, 6, axis=-1)
        h = modulate(rmsnorm(x, norm1[layer]), shift1, scale1)
        mix = hybrid_mixer(h, w_in[layer], lb_all[layer], hg_norm[layer], rg_conv_w[layer],
                           rg_conv_b[layer], rg_w_a[layer], rg_b_a[layer], rg_w_x[layer],
                           rg_b_x[layer], rg_lambda[layer], w_proj_a[layer], w_proj_b[layer],
                           w_out[layer])
        x = x + gate1[:, None, :] * mix
        h = modulate(rmsnorm(x, norm2[layer]), shift2, scale2)
        x = x + gate2[:, None, :] * hier_moe(h, w_group[layer], w_router[layer], w_gate[layer],
                                             w_up[layer], w_down[layer])
    return rmsnorm(x, final_norm)
```

```python
import functools

import jax
import jax.numpy as jnp
import numpy as np
from jax import lax
from jax.experimental import pallas as pl
from jax.experimental.pallas import tpu as pltpu

EPS = 1e-6
LANES = 128
HG_HEADS = 8
RG_BLOCKS = 8
RG_CONV = 4
RG_C = 8.0
N_GROUPS = 4
EXPERTS_PER_GROUP = 8
N_EXPERTS = N_GROUPS * EXPERTS_PER_GROUP
TOP_K = 2
HG_CHUNK = 64
MOE_BLOCK = 128
VMEM_LIMIT = 56 * 1024 * 1024
ACT_Q, ACT_V, ACT_G, ACT_RX, ACT_GY = 32, 40, 48, 56, 64

F32 = jnp.float32
BF16 = jnp.bfloat16


def _sigmoid(x):
    return 1.0 / (1.0 + jnp.exp(-x))


def _silu(x):
    return x * _sigmoid(x)


def _gelu_tanh(x):
    return 0.5 * x * (1.0 + jnp.tanh(0.7978845608028654 * (x + 0.044715 * (x * x * x))))


def _params(*sem):
    return pltpu.CompilerParams(dimension_semantics=sem, vmem_limit_bytes=VMEM_LIMIT)


def _adaln_kernel(ct_ref, w_ref, b_ref, o_ref, *, kc):
    kdim, nb = ct_ref.shape
    tn = w_ref.shape[1]

    def body(i, accs):
        k0 = pl.multiple_of(i * kc, kc)
        w = w_ref[pl.ds(k0, kc), :]
        cs = _silu(ct_ref[pl.ds(k0, kc), :])
        return tuple(acc + jnp.sum(w * cs[:, b:b + 1], axis=0, keepdims=True)
                     for b, acc in enumerate(accs))

    accs = lax.fori_loop(0, kdim // kc, body, tuple(jnp.zeros((1, tn), F32) for _ in range(nb)))
    for b, acc in enumerate(accs):
        o_ref[b:b + 1, :] = acc + b_ref[...]


def _adaln(c, w, bias):
    bsz, d = c.shape
    n = w.shape[1]
    tn = 1024
    return pl.pallas_call(
        functools.partial(_adaln_kernel, kc=256),
        grid=(n // tn,),
        in_specs=[pl.BlockSpec((d, bsz), lambda j: (0, 0)),
                  pl.BlockSpec((d, tn), lambda j: (0, j)),
                  pl.BlockSpec((1, tn), lambda j: (0, j))],
        out_specs=pl.BlockSpec((bsz, tn), lambda j: (0, j)),
        out_shape=jax.ShapeDtypeStruct((bsz, n), F32),
        compiler_params=_params("parallel"),
        name="adaln",
    )(c.T, w, bias.reshape(1, n))


def _norm_mod_to_scratch(x_ref, gain_ref, scale_ref, shift_ref, h_scr):
    x = x_ref[...]
    y = x * lax.rsqrt(jnp.mean(x * x, axis=-1, keepdims=True) + EPS) * gain_ref[...]
    h_scr[...] = (y * (1.0 + scale_ref[0]) + shift_ref[0]).astype(BF16)


def _inproj_logf_kernel(x_ref, gain_ref, scale_ref, shift_ref, w_ref, lbt_ref, o_ref, h_scr, *, layer):
    @pl.when(pl.program_id(1) == 0)
    def _():
        _norm_mod_to_scratch(x_ref, gain_ref, scale_ref, shift_ref, h_scr)

    z = jnp.dot(h_scr[...], w_ref[...], preferred_element_type=F32)
    t = lbt_ref[...]
    e = jnp.exp(t - jnp.max(t, axis=0, keepdims=True))
    lb = jnp.sum(e[:layer + 1], axis=0, keepdims=True) / jnp.sum(e, axis=0, keepdims=True)
    o_ref[...] = jnp.log(lb + (1.0 - lb) * _sigmoid(z))


def _inproj_act_kernel(x_ref, gain_ref, scale_ref, shift_ref, w_ref, o_ref, h_scr):
    j = pl.program_id(1)

    @pl.when(j == 0)
    def _():
        _norm_mod_to_scratch(x_ref, gain_ref, scale_ref, shift_ref, h_scr)

    z = jnp.dot(h_scr[...], w_ref[...], preferred_element_type=F32)

    @pl.when(j < 4)
    def _():
        o_ref[...] = _sigmoid(z).astype(BF16)

    @pl.when((j == 4) | (j == 6))
    def _():
        o_ref[...] = _silu(z).astype(BF16)

    @pl.when((j == 5) | (j == 7))
    def _():
        o_ref[...] = z.astype(BF16)

    @pl.when(j == 8)
    def _():
        o_ref[...] = _gelu_tanh(z).astype(BF16)


def _inproj(x2, gain, scale, shift, w_bf, lb_table, layer, seq):
    n, d = x2.shape
    bsz = n // seq
    tm, tn = 512, 1024
    tiles_per_batch = seq // tm
    gain2 = gain.reshape(1, d)
    scale3 = scale.reshape(bsz, 1, d)
    shift3 = shift.reshape(bsz, 1, d)
    common = [pl.BlockSpec((tm, d), lambda i, j: (i, 0)),
              pl.BlockSpec((1, d), lambda i, j: (0, 0)),
              pl.BlockSpec((1, 1, d), lambda i, j: (i // tiles_per_batch, 0, 0)),
              pl.BlockSpec((1, 1, d), lambda i, j: (i // tiles_per_batch, 0, 0))]
    n_lb = lb_table.shape[0]
    logf = pl.pallas_call(
        functools.partial(_inproj_logf_kernel, layer=layer),
        grid=(n // tm, 2),
        in_specs=common + [pl.BlockSpec((d, tn), lambda i, j: (0, j + 2)),
                           pl.BlockSpec((n_lb, tn), lambda i, j: (0, j))],
        out_specs=pl.BlockSpec((tm, tn), lambda i, j: (i, j)),
        out_shape=jax.ShapeDtypeStruct((n, 2 * tn), F32),
        scratch_shapes=[pltpu.VMEM((tm, d), BF16)],
        compiler_params=_params("parallel", "arbitrary"),
        name="inproj_logf",
    )(x2, gain2, scale3, shift3, w_bf, lb_table)
    acts = pl.pallas_call(
        _inproj_act_kernel,
        grid=(n // tm, 9),
        in_specs=common + [pl.BlockSpec(
            (d, tn), lambda i, j: (0, jnp.where(j < 4, j + 7, jnp.where(j < 6, j - 4, j - 2))))],
        out_specs=pl.BlockSpec((tm, tn), lambda i, j: (i, j)),
        out_shape=jax.ShapeDtypeStruct((n, 9 * tn), BF16),
        scratch_shapes=[pltpu.VMEM((tm, d), BF16)],
        compiler_params=_params("parallel", "arbitrary"),
        name="inproj_act",
    )(x2, gain2, scale3, shift3, w_bf)
    return logf, acts


def _hgrn_level_halves(chunk):
    return [chunk >> (i + 1) for i in range(chunk.bit_length() - 1)]


def _hgrn_sum_matrices(chunk):
    r = np.arange(chunk)[:, None]
    j = np.arange(chunk)[None, :]
    blocks = [j <= r, j > r]
    for m in _hgrn_level_halves(chunk):
        bnd = (r // (2 * m)) * (2 * m) + m - 1
        later = (r & m) != 0
        blocks.append(np.where(later, (j > bnd) & (j <= r), (j > r) & (j <= bnd)))
    fwd = np.concatenate(blocks, axis=0).astype(np.float32)
    bwd = np.concatenate([b[::-1, ::-1] for b in blocks], axis=0).astype(np.float32)
    return fwd, bwd


def _hgrn_kernel(q_ref, v_ref, g_ref, lff_ref, lfb_ref, gain_ref, af_ref, ab_ref, o_ref,
                 oacc, qb_scr, kb_scr, *, chunk):
    seq = q_ref.shape[1]
    nchunks = seq // chunk
    halves = _hgrn_level_halves(chunk)
    nt = (((1,), (1,)), ((), ()))
    tn = (((0,), (0,)), ((), ()))

    row = lax.broadcasted_iota(jnp.int32, (chunk, LANES), 0)
    ti = lax.broadcasted_iota(jnp.int32, (chunk, chunk), 0)
    si = lax.broadcasted_iota(jnp.int32, (chunk, chunk), 1)
    txs = ti ^ si

    def exps(lf, a_ref):
        hi = lf.astype(BF16)
        mid = (lf - hi.astype(F32)).astype(BF16)
        r = jnp.dot(a_ref[...], jnp.concatenate([hi, mid], axis=1), preferred_element_type=F32)
        return jnp.exp(r[:, :LANES] + r[:, LANES:])

    def blk(e, i):
        return e[i * chunk:(i + 1) * chunk]

    def fwd_body(c, st_f):
        r0 = pl.multiple_of(c * chunk, chunk)
        q = q_ref[0, pl.ds(r0, chunk), :].astype(F32)
        v = v_ref[0, pl.ds(r0, chunk), :]
        lff = lff_ref[0, pl.ds(r0, chunk), :]
        lfb = lfb_ref[0, pl.ds(r0, chunk), :]
        ef = exps(lff, af_ref)
        eb = exps(lfb, ab_ref)
        kf = 1.0 - jnp.exp(lff)
        kb = 1.0 - jnp.exp(lfb)

        qq = jnp.concatenate([q, q], axis=1).astype(BF16)
        kk = jnp.concatenate([kf, kb], axis=1).astype(BF16)
        p0 = lax.dot_general(qq, kk, nt, preferred_element_type=F32)
        scores = jnp.where(txs == 0, p0, 0.0)
        for li, m in enumerate(halves):
            later_f = (row & m) != 0
            elf = blk(ef, 2 + li)
            elb = blk(eb, 2 + li)
            ql = jnp.concatenate([jnp.where(later_f, q * elf, 0.0),
                                  jnp.where(later_f, 0.0, q * elb)], axis=1).astype(BF16)
            kl = jnp.concatenate([jnp.where(later_f, 0.0, kf * elf),
                                  jnp.where(later_f, kb * elb, 0.0)], axis=1).astype(BF16)
            pm = lax.dot_general(ql, kl, nt, preferred_element_type=F32)
            scores = scores + jnp.where(txs < 2 * m, pm, 0.0)
        o = jnp.dot(scores.astype(BF16), v, preferred_element_type=F32)

        qi = (q * blk(ef, 0)).astype(BF16)
        o = o + lax.dot_general(qi, st_f.astype(BF16), nt, preferred_element_type=F32)
        kdec = (kf * blk(ef, 1)).astype(BF16)
        st_f = st_f * ef[chunk - 1:chunk, :] + lax.dot_general(v, kdec, tn, preferred_element_type=F32)

        oacc[pl.ds(r0, chunk), :] = o
        qb_scr[pl.ds(r0, chunk), :] = (q * blk(eb, 0)).astype(BF16)
        kb_scr[pl.ds(r0, chunk), :] = (kb * blk(eb, 1)).astype(BF16)
        return st_f

    lax.fori_loop(0, nchunks, fwd_body, jnp.zeros((LANES, LANES), F32))

    def bwd_body(i, st_b):
        c = nchunks - 1 - i
        r0 = pl.multiple_of(c * chunk, chunk)
        v = v_ref[0, pl.ds(r0, chunk), :]
        o = oacc[pl.ds(r0, chunk), :] + lax.dot_general(
            qb_scr[pl.ds(r0, chunk), :], st_b.astype(BF16), nt, preferred_element_type=F32)
        tot = jnp.sum(lfb_ref[0, pl.ds(r0, chunk), :], axis=0, keepdims=True)
        st_b = st_b * jnp.exp(tot) + lax.dot_general(
            v, kb_scr[pl.ds(r0, chunk), :], tn, preferred_element_type=F32)
        y = o * lax.rsqrt(jnp.mean(o * o, axis=-1, keepdims=True) + EPS) * gain_ref[0]
        o_ref[0, pl.ds(r0, chunk), :] = (y * g_ref[0, pl.ds(r0, chunk), :].astype(F32)).astype(BF16)
        return st_b

    lax.fori_loop(0, nchunks, bwd_body, jnp.zeros((LANES, LANES), F32))


def _hgrn(acts3, logf3, hg_norm):
    bsz, seq, _ = acts3.shape
    a_f, a_b = _hgrn_sum_matrices(HG_CHUNK)
    nrows = a_f.shape[0]
    col = lambda off: pl.BlockSpec((1, seq, LANES), lambda b, h: (b, 0, off + h))
    return pl.pallas_call(
        functools.partial(_hgrn_kernel, chunk=HG_CHUNK),
        grid=(bsz, HG_HEADS),
        in_specs=[col(ACT_Q), col(ACT_V), col(ACT_G), col(0), col(HG_HEADS),
                  pl.BlockSpec((1, 1, LANES), lambda b, h: (h, 0, 0)),
                  pl.BlockSpec((nrows, HG_CHUNK), lambda b, h: (0, 0)),
                  pl.BlockSpec((nrows, HG_CHUNK), lambda b, h: (0, 0))],
        out_specs=pl.BlockSpec((1, seq, LANES), lambda b, h: (b, 0, h)),
        out_shape=jax.ShapeDtypeStruct((bsz, seq, HG_HEADS * LANES), BF16),
        scratch_shapes=[pltpu.VMEM((seq, LANES), F32),
                        pltpu.VMEM((seq, LANES), BF16),
                        pltpu.VMEM((seq, LANES), BF16)],
        compiler_params=_params("parallel", "parallel"),
        name="hgrn2",
    )(acts3, acts3, acts3, logf3, logf3, hg_norm.reshape(HG_HEADS, 1, LANES),
      jnp.asarray(a_f, BF16), jnp.asarray(a_b, BF16))


def _scan_tile(a, u, reverse):
    rows = a.shape[0]
    row = lax.broadcasted_iota(jnp.int32, a.shape, 0)
    d = 1
    while d < rows:
        if reverse:
            ok = row < rows - d
            shift = rows - d
        else:
            ok = row >= d
            shift = d
        a_s = jnp.where(ok, pltpu.roll(a, shift, 0), 1.0)
        u_s = jnp.where(ok, pltpu.roll(u, shift, 0), 0.0)
        u = u + a * u_s
        a = a * a_s
        d *= 2
    return a, u


def _rglru_kernel(rx_ref, gy_ref, cw_ref, cb_ref, wa_ref, ba_ref, wx_ref, bx_ref, lam_ref, o_ref,
                  xpad, hf_scr, ab_scr, ub_scr, *, tile):
    seq = rx_ref.shape[1]
    ntiles = seq // tile
    halo = 8

    xpad[0:halo, :] = jnp.zeros((halo, LANES), F32)
    xpad[halo + seq:halo + seq + halo, :] = jnp.zeros((halo, LANES), F32)
    xpad[halo:halo + seq, :] = rx_ref[0].astype(F32)

    lam = lam_ref[...]
    neg_c_softplus = -RG_C * (jnp.maximum(-lam, 0.0) + jnp.log(1.0 + jnp.exp(-jnp.abs(lam))))

    def gates(xcb, z):
        r = _sigmoid(jnp.dot(xcb, wa_ref[z, 0], preferred_element_type=F32) + ba_ref[z, 0])
        gi = _sigmoid(jnp.dot(xcb, wx_ref[z, 0], preferred_element_type=F32) + bx_ref[z, 0])
        a = jnp.exp(r * neg_c_softplus[z:z + 1, :])
        return a, jnp.sqrt(1.0 - a * a) * gi

    def fwd_body(i, carry):
        t0 = pl.multiple_of(i * tile, tile)
        ext = xpad[pl.ds(t0, tile + 2 * halo), :]
        n_ext = tile + 2 * halo
        tap = lambda off: pltpu.roll(ext, (-off) % n_ext, 0)[halo:halo + tile]
        xc = (cw_ref[0:1, :] * tap(-2) + cw_ref[1:2, :] * tap(-1) + cw_ref[2:3, :] * ext[halo:halo + tile]
              + cw_ref[3:4, :] * tap(1) + cb_ref[...])
        xcb = xc.astype(BF16)
        a_f, s_f = gates(xcb, 0)
        a_b, s_b = gates(xcb, 1)
        ap, hl = _scan_tile(a_f, s_f * xc, reverse=False)
        h = hl + ap * carry
        hf_scr[pl.ds(t0, tile), :] = h
        ab_scr[pl.ds(t0, tile), :] = a_b
        ub_scr[pl.ds(t0, tile), :] = s_b * xc
        return h[tile - 1:tile, :]

    lax.fori_loop(0, ntiles, fwd_body, jnp.zeros((1, LANES), F32))

    def bwd_body(i, carry):
        t0 = pl.multiple_of((ntiles - 1 - i) * tile, tile)
        ap, hl = _scan_tile(ab_scr[pl.ds(t0, tile), :], ub_scr[pl.ds(t0, tile), :], reverse=True)
        h = hl + ap * carry
        o_ref[0, pl.ds(t0, tile), :] = (
            (hf_scr[pl.ds(t0, tile), :] + h) * gy_ref[0, pl.ds(t0, tile), :].astype(F32)).astype(BF16)
        return h[0:1, :]

    lax.fori_loop(0, ntiles, bwd_body, jnp.zeros((1, LANES), F32))


def _rglru(acts3, conv_w, conv_b, w_a, b_a, w_x, b_x, lam):
    bsz, seq, _ = acts3.shape
    width = RG_BLOCKS * LANES
    tile = 256
    rx_off, gy_off = ACT_RX, ACT_GY
    col = lambda off: pl.BlockSpec((1, seq, LANES), lambda b, j: (b, 0, off + j))
    wspec = pl.BlockSpec((2, 1, LANES, LANES), lambda b, j: (0, j, 0, 0))
    bspec = pl.BlockSpec((2, 1, 1, LANES), lambda b, j: (0, j, 0, 0))
    return pl.pallas_call(
        functools.partial(_rglru_kernel, tile=tile),
        grid=(bsz, RG_BLOCKS),
        in_specs=[col(rx_off), col(gy_off),
                  pl.BlockSpec((RG_CONV, LANES), lambda b, j: (0, j)),
                  pl.BlockSpec((1, LANES), lambda b, j: (0, j)),
                  wspec, bspec, wspec, bspec,
                  pl.BlockSpec((2, LANES), lambda b, j: (0, j))],
        out_specs=pl.BlockSpec((1, seq, LANES), lambda b, j: (b, 0, j)),
        out_shape=jax.ShapeDtypeStruct((bsz, seq, width), BF16),
        scratch_shapes=[pltpu.VMEM((seq + 16, LANES), F32),
                        pltpu.VMEM((seq, LANES), F32),
                        pltpu.VMEM((seq, LANES), F32),
                        pltpu.VMEM((seq, LANES), F32)],
        compiler_params=_params("parallel", "parallel"),
        name="rglru",
    )(acts3, acts3, conv_w, conv_b.reshape(1, width), w_a.astype(BF16),
      b_a.reshape(2, RG_BLOCKS, 1, LANES), w_x.astype(BF16), b_x.reshape(2, RG_BLOCKS, 1, LANES), lam)


def _mixout_kernel(ya_ref, yb_ref, sga_ref, sgb_ref, x_ref, gate_ref, scale_ref, shift_ref, gain_ref,
                   wpa_ref, wpb_ref, wo_ref, wr_ref, x1_ref, h2_ref, eid_ref, ew_ref):
    y_a = jnp.dot(ya_ref[...], wpa_ref[...], preferred_element_type=F32)
    y_b = jnp.dot(yb_ref[...], wpb_ref[...], preferred_element_type=F32)
    merged = sga_ref[...].astype(F32) * y_a + sgb_ref[...].astype(F32) * y_b
    mix = jnp.dot(merged.astype(BF16), wo_ref[...], preferred_element_type=F32)
    x1 = x_ref[...] + gate_ref[0] * mix
    x1_ref[...] = x1
    y = x1 * lax.rsqrt(jnp.mean(x1 * x1, axis=-1, keepdims=True) + EPS) * gain_ref[...]
    h2 = y * (1.0 + scale_ref[0]) + shift_ref[0]
    h2_ref[...] = h2

    logits = jnp.dot(h2, wr_ref[...], preferred_element_type=F32, precision=lax.Precision.HIGHEST)
    lane = lax.broadcasted_iota(jnp.int32, logits.shape, 1)
    neg = -jnp.inf
    gl = jnp.where(lane < N_GROUPS, logits, neg)
    gmax = jnp.max(gl, axis=-1, keepdims=True)
    g_sel = jnp.min(jnp.where(gl == gmax, lane, LANES), axis=-1, keepdims=True)
    p_group = 1.0 / jnp.sum(jnp.exp(gl - gmax), axis=-1, keepdims=True)
    lo = N_GROUPS + g_sel * EXPERTS_PER_GROUP
    el = jnp.where((lane >= lo) & (lane < lo + EXPERTS_PER_GROUP), logits, neg)
    v1 = jnp.max(el, axis=-1, keepdims=True)
    i1 = jnp.min(jnp.where(el == v1, lane, LANES), axis=-1, keepdims=True)
    el2 = jnp.where(lane == i1, neg, el)
    v2 = jnp.max(el2, axis=-1, keepdims=True)
    i2 = jnp.min(jnp.where(el2 == v2, lane, LANES), axis=-1, keepdims=True)
    e21 = jnp.exp(v2 - v1)
    w1 = p_group / (1.0 + e21)
    w2 = p_group * e21 / (1.0 + e21)
    eid_ref[...] = jnp.where(lane == 0, i1 - N_GROUPS, jnp.where(lane == 1, i2 - N_GROUPS, 0))
    ew_ref[...] = jnp.where(lane == 0, w1, jnp.where(lane == 1, w2, 0.0))


def _mixout(ya, yb, acts, x2, gate1, scale2, shift2, gain2, wpa, wpb, wo, w_route, seq):
    n, d = x2.shape
    bsz = n // seq
    tm = 256
    tpb = seq // tm
    half = ya.shape[1]
    ga_blk = 0
    per_batch = pl.BlockSpec((1, 1, d), lambda i: (i // tpb, 0, 0))
    const = lambda shape: pl.BlockSpec(shape, lambda i: (0, 0), pipeline_mode=pl.Buffered(1))
    return pl.pallas_call(
        _mixout_kernel,
        grid=(n // tm,),
        in_specs=[pl.BlockSpec((tm, half), lambda i: (i, 0)),
                  pl.BlockSpec((tm, half), lambda i: (i, 0)),
                  pl.BlockSpec((tm, d), lambda i: (i, ga_blk)),
                  pl.BlockSpec((tm, d), lambda i: (i, ga_blk + 1)),
                  pl.BlockSpec((tm, d), lambda i: (i, 0)),
                  per_batch, per_batch, per_batch,
                  const((1, d)), const((half, d)), const((half, d)), const((d, d)), const((d, LANES))],
        out_specs=[pl.BlockSpec((tm, d), lambda i: (i, 0)),
                   pl.BlockSpec((tm, d), lambda i: (i, 0)),
                   pl.BlockSpec((tm, LANES), lambda i: (i, 0)),
                   pl.BlockSpec((tm, LANES), lambda i: (i, 0))],
        out_shape=[jax.ShapeDtypeStruct((n, d), F32),
                   jax.ShapeDtypeStruct((n, d), F32),
                   jax.ShapeDtypeStruct((n, LANES), jnp.int32),
                   jax.ShapeDtypeStruct((n, LANES), F32)],
        compiler_params=_params("parallel"),
        name="mixout_route",
    )(ya, yb, acts, acts, x2, gate1.reshape(bsz, 1, d), scale2.reshape(bsz, 1, d),
      shift2.reshape(bsz, 1, d), gain2.reshape(1, d), wpa, wpb, wo, w_route)


def _gather_rows(idx_ref, base, src_hbm, dst, sem, nrows):
    def start(r, carry):
        pltpu.make_async_copy(src_hbm.at[pl.ds(idx_ref[base + r], 1), :], dst.at[pl.ds(r, 1), :], sem).start()
        return carry

    lax.fori_loop(0, nrows, start, 0)
    pltpu.make_async_copy(src_hbm.at[pl.ds(0, nrows), :], dst, sem).wait()


def _expert_kernel(be_ref, tok_ref, nused_ref, h_hbm, wg_ref, wu_ref, wd_ref, rw_ref, o_ref, xbuf, sem):
    b = pl.program_id(0)

    @pl.when(b < nused_ref[0])
    def _():
        _gather_rows(tok_ref, b * MOE_BLOCK, h_hbm, xbuf, sem, MOE_BLOCK)
        xb = xbuf[...].astype(BF16)
        gate = jnp.dot(xb, wg_ref[0], preferred_element_type=F32)
        up = jnp.dot(xb, wu_ref[0], preferred_element_type=F32)
        hid = (_silu(gate) * up).astype(BF16)
        o_ref[...] = jnp.dot(hid, wd_ref[0], preferred_element_type=F32) * rw_ref[...]

    @pl.when(b >= nused_ref[0])
    def _():
        o_ref[...] = jnp.zeros(o_ref.shape, o_ref.dtype)


def _experts(h2, wg, wu, wd, block_expert, row_tok, row_w, n_used):
    n, d = h2.shape
    p = row_tok.shape[0]
    nb = p // MOE_BLOCK
    de = wg.shape[2]
    grid_spec = pltpu.PrefetchScalarGridSpec(
        num_scalar_prefetch=3,
        grid=(nb,),
        in_specs=[pl.BlockSpec(memory_space=pl.ANY),
                  pl.BlockSpec((1, d, de), lambda b, be, tok, nu: (be[b], 0, 0)),
                  pl.BlockSpec((1, d, de), lambda b, be, tok, nu: (be[b], 0, 0)),
                  pl.BlockSpec((1, de, d), lambda b, be, tok, nu: (be[b], 0, 0)),
                  pl.BlockSpec((MOE_BLOCK, 1), lambda b, be, tok, nu: (b, 0))],
        out_specs=pl.BlockSpec((MOE_BLOCK, d), lambda b, be, tok, nu: (b, 0)),
        scratch_shapes=[pltpu.VMEM((MOE_BLOCK, d), F32), pltpu.SemaphoreType.DMA(())],
    )
    return pl.pallas_call(
        _expert_kernel,
        grid_spec=grid_spec,
        out_shape=jax.ShapeDtypeStruct((p, d), F32),
        compiler_params=_params("arbitrary"),
        name="experts",
    )(block_expert, row_tok, n_used, h2, wg, wu, wd, row_w.reshape(p, 1))


def _combine_kernel(pos_ref, y_hbm, x1_ref, gate_ref, gain_ref, o_ref, ybuf, sem):
    tm = x1_ref.shape[0]
    _gather_rows(pos_ref, pl.program_id(0) * (TOP_K * tm), y_hbm, ybuf, sem, TOP_K * tm)
    moe = ybuf[0:tm, :]
    for k in range(1, TOP_K):
        moe = moe + ybuf[k * tm:(k + 1) * tm, :]
    x2 = x1_ref[...] + gate_ref[0] * moe
    o_ref[...] = x2 * lax.rsqrt(jnp.mean(x2 * x2, axis=-1, keepdims=True) + EPS) * gain_ref[...]


def _combine(ys, pos_tiles, x1, gate2, final_gain, seq):
    n, d = x1.shape
    bsz = n // seq
    tm = 128
    tpb = seq // tm
    grid_spec = pltpu.PrefetchScalarGridSpec(
        num_scalar_prefetch=1,
        grid=(n // tm,),
        in_specs=[pl.BlockSpec(memory_space=pl.ANY),
                  pl.BlockSpec((tm, d), lambda i, pos: (i, 0)),
                  pl.BlockSpec((1, 1, d), lambda i, pos: (i // tpb, 0, 0)),
                  pl.BlockSpec((1, d), lambda i, pos: (0, 0))],
        out_specs=pl.BlockSpec((tm, d), lambda i, pos: (i, 0)),
        scratch_shapes=[pltpu.VMEM((TOP_K * tm, d), F32), pltpu.SemaphoreType.DMA(())],
    )
    return pl.pallas_call(
        _combine_kernel,
        grid_spec=grid_spec,
        out_shape=jax.ShapeDtypeStruct((n, d), F32),
        compiler_params=_params("arbitrary"),
        name="combine_norm",
    )(pos_tiles, ys, x1, gate2.reshape(bsz, 1, d), final_gain.reshape(1, d))


def _dispatch_plan(eid, ew, tm_combine):
    n = eid.shape[0]
    a = n * TOP_K
    e_flat = eid.reshape(a)
    onehot = (e_flat[:, None] == jnp.arange(N_EXPERTS, dtype=jnp.int32)[None, :]).astype(jnp.int32)
    csum = jnp.cumsum(onehot, axis=0)
    counts = csum[-1]
    rank = jnp.sum(onehot * csum, axis=1) - 1
    padded = (counts + MOE_BLOCK - 1) // MOE_BLOCK * MOE_BLOCK
    pad_ends = jnp.cumsum(padded)
    pad_starts = pad_ends - padded
    dest = (pad_starts[e_flat] + rank).astype(jnp.int32)
    p = ((a + MOE_BLOCK - 1) // MOE_BLOCK + N_EXPERTS) * MOE_BLOCK
    tok_flat = jnp.repeat(jnp.arange(n, dtype=jnp.int32), TOP_K)
    row_tok = jnp.zeros((p,), jnp.int32).at[dest].set(tok_flat)
    row_w = jnp.zeros((p,), F32).at[dest].set(ew.reshape(a))
    nb = p // MOE_BLOCK
    n_used = (pad_ends[-1] // MOE_BLOCK).astype(jnp.int32)
    blk_start = jnp.minimum(jnp.arange(nb, dtype=jnp.int32), n_used - 1) * MOE_BLOCK
    block_expert = jnp.minimum(jnp.searchsorted(pad_ends, blk_start, side='right'),
                               N_EXPERTS - 1).astype(jnp.int32)
    pos_tiles = dest.reshape(n // tm_combine, tm_combine, TOP_K).transpose(0, 2, 1).reshape(a)
    return block_expert, row_tok, row_w, n_used.reshape(1), pos_tiles


def kernel(x, c, w_ada, b_ada, norm1, w_in, hg_lb, hg_norm, rg_conv_w, rg_conv_b, rg_w_a, rg_b_a,
           rg_w_x, rg_b_x, rg_lambda, w_proj_a, w_proj_b, w_out, norm2, w_group, w_router, w_gate,
           w_up, w_down, final_norm):
    bsz, seq, d = x.shape
    depth = w_ada.shape[0]
    n = bsz * seq
    x2 = x.reshape(n, d)
    lb_table = hg_lb.reshape(hg_lb.shape[0], -1)
    for layer in range(depth):
        mod = _adaln(c, w_ada[layer], b_ada[layer])
        shift1, scale1, gate1, shift2, scale2, gate2 = jnp.split(mod, 6, axis=-1)
        logf, acts = _inproj(x2, norm1[layer], scale1, shift1, w_in[layer].astype(BF16), lb_table, layer, seq)
        acts3 = acts.reshape(bsz, seq, acts.shape[1])
        ya = _hgrn(acts3, logf.reshape(bsz, seq, logf.shape[1]), hg_norm[layer])
        yb = _rglru(acts3, rg_conv_w[layer], rg_conv_b[layer], rg_w_a[layer], rg_b_a[layer],
                    rg_w_x[layer], rg_b_x[layer], rg_lambda[layer])
        w_route = jnp.concatenate(
            [w_group[layer], w_router[layer].reshape(d, N_EXPERTS),
             jnp.zeros((d, LANES - N_GROUPS - N_EXPERTS), F32)], axis=1)
        x1, h2, eid, ew = _mixout(ya.reshape(n, -1), yb.reshape(n, -1), acts, x2, gate1, scale2, shift2,
                                  norm2[layer], w_proj_a[layer].astype(BF16), w_proj_b[layer].astype(BF16),
                                  w_out[layer].astype(BF16), w_route, seq)
        block_expert, row_tok, row_w, n_used, pos_tiles = _dispatch_plan(eid[:, :TOP_K], ew[:, :TOP_K], 128)
        ys = _experts(h2, w_gate[layer].astype(BF16), w_up[layer].astype(BF16), w_down[layer].astype(BF16),
                      block_expert, row_tok, row_w, n_used)
        if layer + 1 < depth:
            raise NotImplementedError("the fused combine + final norm assumes a single layer")
        x2 = _combine(ys, pos_tiles, x1, gate2, final_norm, seq)
    return x2.reshape(bsz, seq, d)
```

```python
import functools

import jax
import jax.numpy as jnp
import numpy as np
from jax import lax
from jax.experimental import pallas as pl
from jax.experimental.pallas import tpu as pltpu

EPS = 1e-6
LANES = 128
SUBLANES = 8
HG_HEADS = 8
RG_BLOCKS = 8
RG_CONV = 4
RG_C = 8.0
N_GROUPS = 4
EXPERTS_PER_GROUP = 8
N_EXPERTS = N_GROUPS * EXPERTS_PER_GROUP
TOP_K = 2
HG_CHUNK = 64
MOE_BLOCK = 128
VMEM_LIMIT = 56 * 1024 * 1024
ACT_Q, ACT_V, ACT_G, ACT_RX, ACT_GY = 32, 40, 48, 56, 64

F32 = jnp.float32
BF16 = jnp.bfloat16


def _sigmoid(x):
    return 1.0 / (1.0 + jnp.exp(-x))


def _silu(x):
    return x * _sigmoid(x)


def _gelu_tanh(x):
    return 0.5 * x * (1.0 + jnp.tanh(0.7978845608028654 * (x + 0.044715 * (x * x * x))))


def _params(*sem):
    return pltpu.CompilerParams(dimension_semantics=sem, vmem_limit_bytes=VMEM_LIMIT)


def _adaln_kernel(ct_ref, w_ref, b_ref, o_ref, *, kc):
    kdim, nb = ct_ref.shape
    tn = w_ref.shape[1]

    def body(i, accs):
        k0 = pl.multiple_of(i * kc, kc)
        w = w_ref[pl.ds(k0, kc), :]
        cs = _silu(ct_ref[pl.ds(k0, kc), :])
        return tuple(acc + jnp.sum(w * cs[:, b:b + 1], axis=0, keepdims=True)
                     for b, acc in enumerate(accs))

    accs = lax.fori_loop(0, kdim // kc, body, tuple(jnp.zeros((1, tn), F32) for _ in range(nb)))
    for b, acc in enumerate(accs):
        o_ref[b:b + 1, :] = acc + b_ref[...]


def _adaln(c, w, bias):
    bsz, d = c.shape
    n = w.shape[1]
    tn = 1024
    return pl.pallas_call(
        functools.partial(_adaln_kernel, kc=256),
        grid=(n // tn,),
        in_specs=[pl.BlockSpec((d, bsz), lambda j: (0, 0)),
                  pl.BlockSpec((d, tn), lambda j: (0, j)),
                  pl.BlockSpec((1, tn), lambda j: (0, j))],
        out_specs=pl.BlockSpec((bsz, tn), lambda j: (0, j)),
        out_shape=jax.ShapeDtypeStruct((bsz, n), F32),
        compiler_params=_params("parallel"),
        name="adaln",
    )(c.T, w, bias.reshape(1, n))


def _norm_mod_to_scratch(x_ref, gain_ref, scale_ref, shift_ref, h_scr):
    x = x_ref[...]
    y = x * lax.rsqrt(jnp.mean(x * x, axis=-1, keepdims=True) + EPS) * gain_ref[...]
    h_scr[...] = (y * (1.0 + scale_ref[0]) + shift_ref[0]).astype(BF16)


def _inproj_logf_kernel(x_ref, gain_ref, scale_ref, shift_ref, w_ref, lbt_ref, o_ref, h_scr, *, layer):
    @pl.when(pl.program_id(1) == 0)
    def _():
        _norm_mod_to_scratch(x_ref, gain_ref, scale_ref, shift_ref, h_scr)

    z = jnp.dot(h_scr[...], w_ref[...], preferred_element_type=F32)
    t = lbt_ref[...]
    e = jnp.exp(t - jnp.max(t, axis=0, keepdims=True))
    lb = jnp.sum(e[:layer + 1], axis=0, keepdims=True) / jnp.sum(e, axis=0, keepdims=True)
    o_ref[...] = jnp.log(lb + (1.0 - lb) * _sigmoid(z))


def _inproj_act_kernel(x_ref, gain_ref, scale_ref, shift_ref, w_ref, o_ref, h_scr):
    j = pl.program_id(1)

    @pl.when(j == 0)
    def _():
        _norm_mod_to_scratch(x_ref, gain_ref, scale_ref, shift_ref, h_scr)

    z = jnp.dot(h_scr[...], w_ref[...], preferred_element_type=F32)

    @pl.when(j < 4)
    def _():
        o_ref[...] = _sigmoid(z).astype(BF16)

    @pl.when((j == 4) | (j == 6))
    def _():
        o_ref[...] = _silu(z).astype(BF16)

    @pl.when((j == 5) | (j == 7))
    def _():
        o_ref[...] = z.astype(BF16)

    @pl.when(j == 8)
    def _():
        o_ref[...] = _gelu_tanh(z).astype(BF16)


def _inproj(x2, gain, scale, shift, w_bf, lb_table, layer, seq):
    n, d = x2.shape
    bsz = n // seq
    tm, tn = 512, 1024
    tiles_per_batch = seq // tm
    gain2 = gain.reshape(1, d)
    scale3 = scale.reshape(bsz, 1, d)
    shift3 = shift.reshape(bsz, 1, d)
    common = [pl.BlockSpec((tm, d), lambda i, j: (i, 0)),
              pl.BlockSpec((1, d), lambda i, j: (0, 0)),
              pl.BlockSpec((1, 1, d), lambda i, j: (i // tiles_per_batch, 0, 0)),
              pl.BlockSpec((1, 1, d), lambda i, j: (i // tiles_per_batch, 0, 0))]
    n_lb = lb_table.shape[0]
    logf = pl.pallas_call(
        functools.partial(_inproj_logf_kernel, layer=layer),
        grid=(n // tm, 2),
        in_specs=common + [pl.BlockSpec((d, tn), lambda i, j: (0, j + 2)),
                           pl.BlockSpec((n_lb, tn), lambda i, j: (0, j))],
        out_specs=pl.BlockSpec((tm, tn), lambda i, j: (i, j)),
        out_shape=jax.ShapeDtypeStruct((n, 2 * tn), F32),
        scratch_shapes=[pltpu.VMEM((tm, d), BF16)],
        compiler_params=_params("parallel", "arbitrary"),
        name="inproj_logf",
    )(x2, gain2, scale3, shift3, w_bf, lb_table)
    acts = pl.pallas_call(
        _inproj_act_kernel,
        grid=(n // tm, 9),
        in_specs=common + [pl.BlockSpec(
            (d, tn), lambda i, j: (0, jnp.where(j < 4, j + 7, jnp.where(j < 6, j - 4, j - 2))))],
        out_specs=pl.BlockSpec((tm, tn), lambda i, j: (i, j)),
        out_shape=jax.ShapeDtypeStruct((n, 9 * tn), BF16),
        scratch_shapes=[pltpu.VMEM((tm, d), BF16)],
        compiler_params=_params("parallel", "arbitrary"),
        name="inproj_act",
    )(x2, gain2, scale3, shift3, w_bf)
    return logf, acts


def _hgrn_level_halves(chunk):
    return [chunk >> (i + 1) for i in range(chunk.bit_length() - 1)]


def _hgrn_sum_matrices(chunk):
    r = np.arange(chunk)[:, None]
    j = np.arange(chunk)[None, :]
    blocks = [j <= r, j > r]
    for m in _hgrn_level_halves(chunk):
        bnd = (r // (2 * m)) * (2 * m) + m - 1
        later = (r & m) != 0
        blocks.append(np.where(later, (j > bnd) & (j <= r), (j > r) & (j <= bnd)))
    fwd = np.concatenate(blocks, axis=0).astype(np.float32)
    bwd = np.concatenate([b[::-1, ::-1] for b in blocks], axis=0).astype(np.float32)
    return fwd, bwd


def _hgrn_kernel(q_ref, v_ref, g_ref, lff_ref, lfb_ref, gain_ref, af_ref, ab_ref, o_ref,
                 oacc, qf_scr, kf_scr, qb_scr, kb_scr, *, chunk, group):
    seq = q_ref.shape[1]
    nchunks = seq // chunk
    halves = _hgrn_level_halves(chunk)
    nt = (((1,), (1,)), ((), ()))
    tn = (((0,), (0,)), ((), ()))

    row = lax.broadcasted_iota(jnp.int32, (chunk, LANES), 0)
    ti = lax.broadcasted_iota(jnp.int32, (chunk, chunk), 0)
    si = lax.broadcasted_iota(jnp.int32, (chunk, chunk), 1)
    txs = ti ^ si

    def exps(lf, a_ref):
        hi = lf.astype(BF16)
        mid = (lf - hi.astype(F32)).astype(BF16)
        r = jnp.dot(a_ref[...], jnp.concatenate([hi, mid], axis=1), preferred_element_type=F32)
        return jnp.exp(r[:, :LANES] + r[:, LANES:])

    def blk(e, i):
        return e[i * chunk:(i + 1) * chunk]

    def intra(c):
        r0 = pl.multiple_of(c * chunk, chunk)
        q = q_ref[0, pl.ds(r0, chunk), :].astype(F32)
        v = v_ref[0, pl.ds(r0, chunk), :]
        lff = lff_ref[0, pl.ds(r0, chunk), :]
        lfb = lfb_ref[0, pl.ds(r0, chunk), :]
        ef = exps(lff, af_ref)
        eb = exps(lfb, ab_ref)
        kf = 1.0 - jnp.exp(lff)
        kb = 1.0 - jnp.exp(lfb)

        qq = jnp.concatenate([q, q], axis=1).astype(BF16)
        kk = jnp.concatenate([kf, kb], axis=1).astype(BF16)
        p0 = lax.dot_general(qq, kk, nt, preferred_element_type=F32)
        scores = jnp.where(txs == 0, p0, 0.0)
        for li, m in enumerate(halves):
            later_f = (row & m) != 0
            elf = blk(ef, 2 + li)
            elb = blk(eb, 2 + li)
            ql = jnp.concatenate([jnp.where(later_f, q * elf, 0.0),
                                  jnp.where(later_f, 0.0, q * elb)], axis=1).astype(BF16)
            kl = jnp.concatenate([jnp.where(later_f, 0.0, kf * elf),
                                  jnp.where(later_f, kb * elb, 0.0)], axis=1).astype(BF16)
            pm = lax.dot_general(ql, kl, nt, preferred_element_type=F32)
            scores = scores + jnp.where(txs < 2 * m, pm, 0.0)
        oacc[pl.ds(r0, chunk), :] = jnp.dot(scores.astype(BF16), v, preferred_element_type=F32)
        qf_scr[pl.ds(r0, chunk), :] = (q * blk(ef, 0)).astype(BF16)
        kf_scr[pl.ds(r0, chunk), :] = (kf * blk(ef, 1)).astype(BF16)
        qb_scr[pl.ds(r0, chunk), :] = (q * blk(eb, 0)).astype(BF16)
        kb_scr[pl.ds(r0, chunk), :] = (kb * blk(eb, 1)).astype(BF16)

    def intra_body(i, carry):
        for u in range(group):
            intra(i * group + u)
        return carry

    lax.fori_loop(0, nchunks // group, intra_body, 0)

    def apply_state(c, st, q_scr, k_scr, lf_ref):
        r0 = pl.multiple_of(c * chunk, chunk)
        o = oacc[pl.ds(r0, chunk), :] + lax.dot_general(
            q_scr[pl.ds(r0, chunk), :], st.astype(BF16), nt, preferred_element_type=F32)
        tot = jnp.sum(lf_ref[0, pl.ds(r0, chunk), :], axis=0, keepdims=True)
        st = st * jnp.exp(tot) + lax.dot_general(
            v_ref[0, pl.ds(r0, chunk), :], k_scr[pl.ds(r0, chunk), :], tn, preferred_element_type=F32)
        return r0, o, st

    def finish(r0, o):
        y = o * lax.rsqrt(jnp.mean(o * o, axis=-1, keepdims=True) + EPS) * gain_ref[0]
        o_ref[0, pl.ds(r0, chunk), :] = (y * g_ref[0, pl.ds(r0, chunk), :].astype(F32)).astype(BF16)

    def sweep(final):
        def body(i, states):
            st_f, st_b = states
            rf, of, st_f = apply_state(i, st_f, qf_scr, kf_scr, lff_ref)
            rb, ob, st_b = apply_state(nchunks - 1 - i, st_b, qb_scr, kb_scr, lfb_ref)
            if final:
                finish(rf, of)
                finish(rb, ob)
            else:
                oacc[pl.ds(rf, chunk), :] = of
                oacc[pl.ds(rb, chunk), :] = ob
            return st_f, st_b
        return body

    zero = jnp.zeros((LANES, LANES), F32)
    states = lax.fori_loop(0, nchunks // 2, sweep(False), (zero, zero))
    lax.fori_loop(nchunks // 2, nchunks, sweep(True), states)


def _hgrn(acts3, logf3, hg_norm):
    bsz, seq, _ = acts3.shape
    a_f, a_b = _hgrn_sum_matrices(HG_CHUNK)
    nrows = a_f.shape[0]
    col = lambda off: pl.BlockSpec((1, seq, LANES), lambda b, h: (b, 0, off + h))
    return pl.pallas_call(
        functools.partial(_hgrn_kernel, chunk=HG_CHUNK, group=2),
        grid=(bsz, HG_HEADS),
        in_specs=[col(ACT_Q), col(ACT_V), col(ACT_G), col(0), col(HG_HEADS),
                  pl.BlockSpec((1, 1, LANES), lambda b, h: (h, 0, 0)),
                  pl.BlockSpec((nrows, HG_CHUNK), lambda b, h: (0, 0)),
                  pl.BlockSpec((nrows, HG_CHUNK), lambda b, h: (0, 0))],
        out_specs=pl.BlockSpec((1, seq, LANES), lambda b, h: (b, 0, h)),
        out_shape=jax.ShapeDtypeStruct((bsz, seq, HG_HEADS * LANES), BF16),
        scratch_shapes=[pltpu.VMEM((seq, LANES), F32)] + [pltpu.VMEM((seq, LANES), BF16)] * 4,
        compiler_params=_params("parallel", "parallel"),
        name="hgrn2",
    )(acts3, acts3, acts3, logf3, logf3, hg_norm.reshape(HG_HEADS, 1, LANES),
      jnp.asarray(a_f, BF16), jnp.asarray(a_b, BF16))


def _scan_tile(a, u, reverse):
    rows = a.shape[0]
    row = lax.broadcasted_iota(jnp.int32, a.shape, 0)
    d = 1
    while d < rows:
        if reverse:
            ok = row < rows - d
            shift = rows - d
        else:
            ok = row >= d
            shift = d
        a_s = jnp.where(ok, pltpu.roll(a, shift, 0), 1.0)
        u_s = jnp.where(ok, pltpu.roll(u, shift, 0), 0.0)
        u = u + a * u_s
        a = a * a_s
        d *= 2
    return a, u


def _rglru_kernel(rx_ref, gy_ref, cw_ref, cb_ref, wa_ref, ba_ref, wx_ref, bx_ref, lam_ref, o_ref,
                  xpad, hf_scr, ab_scr, ub_scr, *, tile):
    seq = rx_ref.shape[1]
    ntiles = seq // tile
    halo = 8

    xpad[0:halo, :] = jnp.zeros((halo, LANES), F32)
    xpad[halo + seq:halo + seq + halo, :] = jnp.zeros((halo, LANES), F32)
    xpad[halo:halo + seq, :] = rx_ref[0].astype(F32)

    lam = lam_ref[...]
    neg_c_softplus = -RG_C * (jnp.maximum(-lam, 0.0) + jnp.log(1.0 + jnp.exp(-jnp.abs(lam))))

    def gates(xcb, z):
        r = _sigmoid(jnp.dot(xcb, wa_ref[z, 0], preferred_element_type=F32) + ba_ref[z, 0])
        gi = _sigmoid(jnp.dot(xcb, wx_ref[z, 0], preferred_element_type=F32) + bx_ref[z, 0])
        a = jnp.exp(r * neg_c_softplus[z:z + 1, :])
        return a, jnp.sqrt(1.0 - a * a) * gi

    def fwd_body(i, carry):
        t0 = pl.multiple_of(i * tile, tile)
        ext = xpad[pl.ds(t0, tile + 2 * halo), :]
        n_ext = tile + 2 * halo
        tap = lambda off: pltpu.roll(ext, (-off) % n_ext, 0)[halo:halo + tile]
        xc = (cw_ref[0:1, :] * tap(-2) + cw_ref[1:2, :] * tap(-1) + cw_ref[2:3, :] * ext[halo:halo + tile]
              + cw_ref[3:4, :] * tap(1) + cb_ref[...])
        xcb = xc.astype(BF16)
        a_f, s_f = gates(xcb, 0)
        a_b, s_b = gates(xcb, 1)
        ap, hl = _scan_tile(a_f, s_f * xc, reverse=False)
        h = hl + ap * carry
        hf_scr[pl.ds(t0, tile), :] = h
        ab_scr[pl.ds(t0, tile), :] = a_b
        ub_scr[pl.ds(t0, tile), :] = s_b * xc
        return h[tile - 1:tile, :]

    lax.fori_loop(0, ntiles, fwd_body, jnp.zeros((1, LANES), F32))

    def bwd_body(i, carry):
        t0 = pl.multiple_of((ntiles - 1 - i) * tile, tile)
        ap, hl = _scan_tile(ab_scr[pl.ds(t0, tile), :], ub_scr[pl.ds(t0, tile), :], reverse=True)
        h = hl + ap * carry
        o_ref[0, pl.ds(t0, tile), :] = (
            (hf_scr[pl.ds(t0, tile), :] + h) * gy_ref[0, pl.ds(t0, tile), :].astype(F32)).astype(BF16)
        return h[0:1, :]

    lax.fori_loop(0, ntiles, bwd_body, jnp.zeros((1, LANES), F32))


def _rglru(acts3, conv_w, conv_b, w_a, b_a, w_x, b_x, lam):
    bsz, seq, _ = acts3.shape
    width = RG_BLOCKS * LANES
    tile = 256
    rx_off, gy_off = ACT_RX, ACT_GY
    col = lambda off: pl.BlockSpec((1, seq, LANES), lambda b, j: (b, 0, off + j))
    wspec = pl.BlockSpec((2, 1, LANES, LANES), lambda b, j: (0, j, 0, 0))
    bspec = pl.BlockSpec((2, 1, 1, LANES), lambda b, j: (0, j, 0, 0))
    return pl.pallas_call(
        functools.partial(_rglru_kernel, tile=tile),
        grid=(bsz, RG_BLOCKS),
        in_specs=[col(rx_off), col(gy_off),
                  pl.BlockSpec((RG_CONV, LANES), lambda b, j: (0, j)),
                  pl.BlockSpec((1, LANES), lambda b, j: (0, j)),
                  wspec, bspec, wspec, bspec,
                  pl.BlockSpec((2, LANES), lambda b, j: (0, j))],
        out_specs=pl.BlockSpec((1, seq, LANES), lambda b, j: (b, 0, j)),
        out_shape=jax.ShapeDtypeStruct((bsz, seq, width), BF16),
        scratch_shapes=[pltpu.VMEM((seq + 16, LANES), F32),
                        pltpu.VMEM((seq, LANES), F32),
                        pltpu.VMEM((seq, LANES), F32),
                        pltpu.VMEM((seq, LANES), F32)],
        compiler_params=_params("parallel", "parallel"),
        name="rglru",
    )(acts3, acts3, conv_w, conv_b.reshape(1, width), w_a.astype(BF16),
      b_a.reshape(2, RG_BLOCKS, 1, LANES), w_x.astype(BF16), b_x.reshape(2, RG_BLOCKS, 1, LANES), lam)


def _mixout_kernel(ya_ref, yb_ref, sga_ref, sgb_ref, x_ref, gate_ref, scale_ref, shift_ref, gain_ref,
                   wpa_ref, wpb_ref, wo_ref, wr_ref, x1_ref, h2_ref, eid_ref, ew_ref, cnt_ref, running):
    y_a = jnp.dot(ya_ref[...], wpa_ref[...], preferred_element_type=F32)
    y_b = jnp.dot(yb_ref[...], wpb_ref[...], preferred_element_type=F32)
    merged = sga_ref[...].astype(F32) * y_a + sgb_ref[...].astype(F32) * y_b
    mix = jnp.dot(merged.astype(BF16), wo_ref[...], preferred_element_type=F32)
    x1 = x_ref[...] + gate_ref[0] * mix
    x1_ref[...] = x1
    y = x1 * lax.rsqrt(jnp.mean(x1 * x1, axis=-1, keepdims=True) + EPS) * gain_ref[...]
    h2 = y * (1.0 + scale_ref[0]) + shift_ref[0]
    h2_ref[...] = h2

    logits = jnp.dot(h2, wr_ref[...], preferred_element_type=F32, precision=lax.Precision.HIGHEST)
    lane = lax.broadcasted_iota(jnp.int32, logits.shape, 1)
    neg = -jnp.inf
    gl = jnp.where(lane < N_GROUPS, logits, neg)
    gmax = jnp.max(gl, axis=-1, keepdims=True)
    g_sel = jnp.min(jnp.where(gl == gmax, lane, LANES), axis=-1, keepdims=True)
    p_group = 1.0 / jnp.sum(jnp.exp(gl - gmax), axis=-1, keepdims=True)
    lo = N_GROUPS + g_sel * EXPERTS_PER_GROUP
    el = jnp.where((lane >= lo) & (lane < lo + EXPERTS_PER_GROUP), logits, neg)
    v1 = jnp.max(el, axis=-1, keepdims=True)
    i1 = jnp.min(jnp.where(el == v1, lane, LANES), axis=-1, keepdims=True)
    el2 = jnp.where(lane == i1, neg, el)
    v2 = jnp.max(el2, axis=-1, keepdims=True)
    i2 = jnp.min(jnp.where(el2 == v2, lane, LANES), axis=-1, keepdims=True)
    e21 = jnp.exp(v2 - v1)
    w1 = p_group / (1.0 + e21)
    w2 = p_group * e21 / (1.0 + e21)
    ew_ref[...] = jnp.where(lane == 0, w1, jnp.where(lane == 1, w2, 0.0))

    @pl.when(pl.program_id(0) == 0)
    def _():
        running[...] = jnp.zeros(running.shape, F32)

    e1, e2 = i1 - N_GROUPS, i2 - N_GROUPS
    tm = logits.shape[0]
    before = (lax.broadcasted_iota(jnp.int32, (tm, tm), 0)
              > lax.broadcasted_iota(jnp.int32, (tm, tm), 1))
    before = jnp.where(before, 1.0, 0.0).astype(BF16)
    oh1 = jnp.where(lane == e1, 1.0, 0.0)
    oh2 = jnp.where(lane == e2, 1.0, 0.0)
    pre1 = jnp.dot(before, oh1.astype(BF16), preferred_element_type=F32)
    pre2 = jnp.dot(before, oh2.astype(BF16), preferred_element_type=F32)
    cnt1 = jnp.sum(oh1, axis=0, keepdims=True)
    base = running[...]
    r1 = jnp.sum(oh1 * (pre1 + base), axis=-1, keepdims=True).astype(jnp.int32)
    r2 = jnp.sum(oh2 * (pre2 + base + cnt1), axis=-1, keepdims=True).astype(jnp.int32)
    total = base + cnt1 + jnp.sum(oh2, axis=0, keepdims=True)
    running[...] = total
    cnt_ref[...] = total.astype(jnp.int32)
    eid_ref[...] = jnp.where(lane == 0, e1, jnp.where(lane == 1, e2,
                             jnp.where(lane == 2, r1, jnp.where(lane == 3, r2, 0))))


def _mixout(ya, yb, acts, x2, gate1, scale2, shift2, gain2, wpa, wpb, wo, w_route, seq):
    n, d = x2.shape
    bsz = n // seq
    tm = 256
    tpb = seq // tm
    half = ya.shape[1]
    ga_blk = 0
    per_batch = pl.BlockSpec((1, 1, d), lambda i: (i // tpb, 0, 0))
    const = lambda shape: pl.BlockSpec(shape, lambda i: (0, 0), pipeline_mode=pl.Buffered(1))
    return pl.pallas_call(
        _mixout_kernel,
        grid=(n // tm,),
        in_specs=[pl.BlockSpec((tm, half), lambda i: (i, 0)),
                  pl.BlockSpec((tm, half), lambda i: (i, 0)),
                  pl.BlockSpec((tm, d), lambda i: (i, ga_blk)),
                  pl.BlockSpec((tm, d), lambda i: (i, ga_blk + 1)),
                  pl.BlockSpec((tm, d), lambda i: (i, 0)),
                  per_batch, per_batch, per_batch,
                  const((1, d)), const((half, d)), const((half, d)), const((d, d)), const((d, LANES))],
        out_specs=[pl.BlockSpec((tm, d), lambda i: (i, 0)),
                   pl.BlockSpec((tm, d), lambda i: (i, 0)),
                   pl.BlockSpec((tm, LANES), lambda i: (i, 0)),
                   pl.BlockSpec((tm, LANES), lambda i: (i, 0)),
                   pl.BlockSpec((1, LANES), lambda i: (0, 0))],
        out_shape=[jax.ShapeDtypeStruct((n, d), F32),
                   jax.ShapeDtypeStruct((n, d), F32),
                   jax.ShapeDtypeStruct((n, LANES), jnp.int32),
                   jax.ShapeDtypeStruct((n, LANES), F32),
                   jax.ShapeDtypeStruct((1, LANES), jnp.int32)],
        scratch_shapes=[pltpu.VMEM((1, LANES), F32)],
        compiler_params=_params("arbitrary"),
        name="mixout_route",
    )(ya, yb, acts, acts, x2, gate1.reshape(bsz, 1, d), scale2.reshape(bsz, 1, d),
      shift2.reshape(bsz, 1, d), gain2.reshape(1, d), wpa, wpb, wo, w_route)


def _start_row_gather(idx_ref, base, nrows, src_hbm, dst, sem):
    def start(r, carry):
        pltpu.make_async_copy(src_hbm.at[pl.ds(idx_ref[base + r], 1), :], dst.at[pl.ds(r, 1), :], sem).start()
        return carry

    lax.fori_loop(0, nrows, start, 0)


def _wait_row_gather(nrows, src_hbm, dst, sem):
    pltpu.make_async_copy(src_hbm.at[pl.ds(0, nrows), :], dst.at[pl.ds(0, nrows), :], sem).wait()


def _expert_kernel(be_ref, nv_ref, dest_ref, nused_ref, h_hbm, wg_ref, wu_ref, wd_ref, o_ref,
                   xbuf, sems, row_tok):
    b = pl.program_id(0)
    n_used = nused_ref[0]
    unroll = 8

    def gathered_rows(blk):
        return pl.multiple_of((nv_ref[blk] + SUBLANES - 1) // SUBLANES * SUBLANES, SUBLANES)

    def gather(blk, slot):
        _start_row_gather(row_tok, blk * MOE_BLOCK, gathered_rows(blk), h_hbm, xbuf.at[slot], sems.at[slot])

    @pl.when(b == 0)
    def _():
        def clear(i, carry):
            for u in range(unroll):
                row_tok[i * unroll + u] = 0
            return carry

        lax.fori_loop(0, row_tok.shape[0] // unroll, clear, 0)

        def invert(i, carry):
            for u in range(unroll):
                row_tok[dest_ref[i * unroll + u]] = i * (unroll // TOP_K) + u // TOP_K
            return carry

        lax.fori_loop(0, dest_ref.shape[0] // unroll, invert, 0)
        xbuf[...] = jnp.zeros(xbuf.shape, xbuf.dtype)
        gather(0, 0)

    @pl.when(b + 1 < n_used)
    def _():
        gather(b + 1, (b + 1) % 2)

    @pl.when(b < n_used)
    def _():
        slot = b % 2
        nv = nv_ref[b]
        _wait_row_gather(gathered_rows(b), h_hbm, xbuf.at[slot], sems.at[slot])
        valid = lax.broadcasted_iota(jnp.int32, (MOE_BLOCK, 1), 0) < nv
        xb = jnp.where(valid, xbuf[slot], 0.0).astype(BF16)
        gate = jnp.dot(xb, wg_ref[0], preferred_element_type=F32)
        up = jnp.dot(xb, wu_ref[0], preferred_element_type=F32)
        hid = (_silu(gate) * up).astype(BF16)
        o_ref[...] = jnp.dot(hid, wd_ref[0], preferred_element_type=F32)

    @pl.when(b >= n_used)
    def _():
        o_ref[...] = jnp.zeros(o_ref.shape, o_ref.dtype)


def _experts(h2, wg, wu, wd, block_expert, n_valid, dest, n_used, p):
    n, d = h2.shape
    nb = p // MOE_BLOCK
    de = wg.shape[2]
    wmap = lambda b, be, nv, dst, nu: (be[b], 0, 0)
    grid_spec = pltpu.PrefetchScalarGridSpec(
        num_scalar_prefetch=4,
        grid=(nb,),
        in_specs=[pl.BlockSpec(memory_space=pl.ANY),
                  pl.BlockSpec((1, d, de), wmap),
                  pl.BlockSpec((1, d, de), wmap),
                  pl.BlockSpec((1, de, d), wmap)],
        out_specs=pl.BlockSpec((MOE_BLOCK, d), lambda b, be, nv, dst, nu: (b, 0)),
        scratch_shapes=[pltpu.VMEM((2, MOE_BLOCK, d), F32), pltpu.SemaphoreType.DMA((2,)),
                        pltpu.SMEM((p,), jnp.int32)],
    )
    return pl.pallas_call(
        _expert_kernel,
        grid_spec=grid_spec,
        out_shape=jax.ShapeDtypeStruct((p, d), F32),
        compiler_params=_params("arbitrary"),
        name="experts",
    )(block_expert, n_valid, dest, n_used, h2, wg, wu, wd)


def _combine_kernel(pos_ref, y_hbm, x1_ref, ew_ref, gate_ref, gain_ref, o_ref, ybuf, sems):
    i = pl.program_id(0)
    tm = x1_ref.shape[0]
    rows = TOP_K * tm

    def gather(tile, slot):
        _start_row_gather(pos_ref, tile * rows, rows, y_hbm, ybuf.at[slot], sems.at[slot])

    @pl.when(i == 0)
    def _():
        gather(0, 0)

    @pl.when(i + 1 < pl.num_programs(0))
    def _():
        gather(i + 1, (i + 1) % 2)

    slot = i % 2
    _wait_row_gather(rows, y_hbm, ybuf.at[slot], sems.at[slot])
    ew = ew_ref[...]
    moe = ew[:, 0:1] * ybuf[slot, 0:tm, :]
    for k in range(1, TOP_K):
        moe = moe + ew[:, k:k + 1] * ybuf[slot, k * tm:(k + 1) * tm, :]
    x2 = x1_ref[...] + gate_ref[0] * moe
    o_ref[...] = x2 * lax.rsqrt(jnp.mean(x2 * x2, axis=-1, keepdims=True) + EPS) * gain_ref[...]


COMBINE_TILE = 128


def _combine(ys, pos_tiles, x1, ew, gate2, final_gain, seq):
    n, d = x1.shape
    bsz = n // seq
    tm = COMBINE_TILE
    tpb = seq // tm
    grid_spec = pltpu.PrefetchScalarGridSpec(
        num_scalar_prefetch=1,
        grid=(n // tm,),
        in_specs=[pl.BlockSpec(memory_space=pl.ANY),
                  pl.BlockSpec((tm, d), lambda i, pos: (i, 0)),
                  pl.BlockSpec((tm, LANES), lambda i, pos: (i, 0)),
                  pl.BlockSpec((1, 1, d), lambda i, pos: (i // tpb, 0, 0)),
                  pl.BlockSpec((1, d), lambda i, pos: (0, 0))],
        out_specs=pl.BlockSpec((tm, d), lambda i, pos: (i, 0)),
        scratch_shapes=[pltpu.VMEM((2, TOP_K * tm, d), F32), pltpu.SemaphoreType.DMA((2,))],
    )
    return pl.pallas_call(
        _combine_kernel,
        grid_spec=grid_spec,
        out_shape=jax.ShapeDtypeStruct((n, d), F32),
        compiler_params=_params("arbitrary"),
        name="combine_norm",
    )(pos_tiles, ys, x1, ew, gate2.reshape(bsz, 1, d), final_gain.reshape(1, d))


def _dispatch_plan(route, counts):
    n = route.shape[0]
    a = n * TOP_K
    eid = route[:, :TOP_K]
    rank = route[:, TOP_K:2 * TOP_K]
    cnt = counts[0, :N_EXPERTS]
    padded = (cnt + MOE_BLOCK - 1) // MOE_BLOCK * MOE_BLOCK
    pad_ends = jnp.cumsum(padded)
    pad_starts = pad_ends - padded
    dest = (pad_starts[eid] + rank).astype(jnp.int32)
    p = ((a + MOE_BLOCK - 1) // MOE_BLOCK + N_EXPERTS) * MOE_BLOCK
    nb = p // MOE_BLOCK
    n_used = (pad_ends[-1] // MOE_BLOCK).astype(jnp.int32)
    blk_start = jnp.minimum(jnp.arange(nb, dtype=jnp.int32), n_used - 1) * MOE_BLOCK
    block_expert = jnp.minimum(jnp.sum((pad_ends[None, :] <= blk_start[:, None]).astype(jnp.int32), axis=1),
                               N_EXPERTS - 1).astype(jnp.int32)
    n_valid = jnp.clip(pad_starts[block_expert] + cnt[block_expert] - blk_start, 0, MOE_BLOCK).astype(jnp.int32)
    pos_tiles = dest.reshape(n // COMBINE_TILE, COMBINE_TILE, TOP_K).transpose(0, 2, 1).reshape(a)
    return block_expert, n_valid, dest.reshape(a), n_used.reshape(1), pos_tiles, p


def kernel(x, c, w_ada, b_ada, norm1, w_in, hg_lb, hg_norm, rg_conv_w, rg_conv_b, rg_w_a, rg_b_a,
           rg_w_x, rg_b_x, rg_lambda, w_proj_a, w_proj_b, w_out, norm2, w_group, w_router, w_gate,
           w_up, w_down, final_norm):
    bsz, seq, d = x.shape
    depth = w_ada.shape[0]
    n = bsz * seq
    x2 = x.reshape(n, d)
    lb_table = hg_lb.reshape(hg_lb.shape[0], -1)
    for layer in range(depth):
        mod = _adaln(c, w_ada[layer], b_ada[layer])
        shift1, scale1, gate1, shift2, scale2, gate2 = jnp.split(mod, 6, axis=-1)
        logf, acts = _inproj(x2, norm1[layer], scale1, shift1, w_in[layer].astype(BF16), lb_table, layer, seq)
        acts3 = acts.reshape(bsz, seq, acts.shape[1])
        ya = _hgrn(acts3, logf.reshape(bsz, seq, logf.shape[1]), hg_norm[layer])
        yb = _rglru(acts3, rg_conv_w[layer], rg_conv_b[layer], rg_w_a[layer], rg_b_a[layer],
                    rg_w_x[layer], rg_b_x[layer], rg_lambda[layer])
        w_route = jnp.concatenate(
            [w_group[layer], w_router[layer].reshape(d, N_EXPERTS),
             jnp.zeros((d, LANES - N_GROUPS - N_EXPERTS), F32)], axis=1)
        x1, h2, route, ew, counts = _mixout(
            ya.reshape(n, -1), yb.reshape(n, -1), acts, x2, gate1, scale2, shift2, norm2[layer],
            w_proj_a[layer].astype(BF16), w_proj_b[layer].astype(BF16), w_out[layer].astype(BF16), w_route, seq)
        block_expert, n_valid, dest, n_used, pos_tiles, p = _dispatch_plan(route, counts)
        ys = _experts(h2, w_gate[layer].astype(BF16), w_up[layer].astype(BF16), w_down[layer].astype(BF16),
                      block_expert, n_valid, dest, n_used, p)
        if layer + 1 < depth:
            raise NotImplementedError("the fused combine + final norm assumes a single layer")
        x2 = _combine(ys, pos_tiles, x1, ew, gate2, final_norm, seq)
    return x2.reshape(bsz, seq, d)
```

```python
import functools

import jax
import jax.numpy as jnp
import numpy as np
from jax import lax
from jax.experimental import pallas as pl
from jax.experimental.pallas import tpu as pltpu

EPS = 1e-6
LANES = 128
SUBLANES = 8
HG_HEADS = 8
RG_BLOCKS = 8
RG_CONV = 4
RG_C = 8.0
N_GROUPS = 4
EXPERTS_PER_GROUP = 8
N_EXPERTS = N_GROUPS * EXPERTS_PER_GROUP
TOP_K = 2
HG_CHUNK = 128
MOE_BLOCK = 128
VMEM_LIMIT = 56 * 1024 * 1024
ACT_Q, ACT_V, ACT_G, ACT_RX, ACT_GY = 32, 40, 48, 56, 64

F32 = jnp.float32
BF16 = jnp.bfloat16


def _sigmoid(x):
    return 1.0 / (1.0 + jnp.exp(-x))


def _silu(x):
    return x * _sigmoid(x)


def _gelu_tanh(x):
    return 0.5 * x * (1.0 + jnp.tanh(0.7978845608028654 * (x + 0.044715 * (x * x * x))))


def _params(*sem):
    return pltpu.CompilerParams(dimension_semantics=sem, vmem_limit_bytes=VMEM_LIMIT)


def _adaln_kernel(ct_ref, w_ref, b_ref, o_ref, *, kc):
    kdim, nb = ct_ref.shape
    tn = w_ref.shape[1]

    def body(i, accs):
        k0 = pl.multiple_of(i * kc, kc)
        w = w_ref[pl.ds(k0, kc), :]
        cs = _silu(ct_ref[pl.ds(k0, kc), :])
        return tuple(acc + jnp.sum(w * cs[:, b:b + 1], axis=0, keepdims=True)
                     for b, acc in enumerate(accs))

    accs = lax.fori_loop(0, kdim // kc, body, tuple(jnp.zeros((1, tn), F32) for _ in range(nb)))
    for b, acc in enumerate(accs):
        o_ref[b:b + 1, :] = acc + b_ref[...]


def _adaln(c, w, bias):
    bsz, d = c.shape
    n = w.shape[1]
    tn = 1024
    return pl.pallas_call(
        functools.partial(_adaln_kernel, kc=256),
        grid=(n // tn,),
        in_specs=[pl.BlockSpec((d, bsz), lambda j: (0, 0)),
                  pl.BlockSpec((d, tn), lambda j: (0, j)),
                  pl.BlockSpec((1, tn), lambda j: (0, j))],
        out_specs=pl.BlockSpec((bsz, tn), lambda j: (0, j)),
        out_shape=jax.ShapeDtypeStruct((bsz, n), F32),
        compiler_params=_params("parallel"),
        name="adaln",
    )(c.T, w, bias.reshape(1, n))


def _norm_mod_to_scratch(x_ref, gain_ref, scale_ref, shift_ref, h_scr):
    x = x_ref[...]
    y = x * lax.rsqrt(jnp.mean(x * x, axis=-1, keepdims=True) + EPS) * gain_ref[...]
    h_scr[...] = (y * (1.0 + scale_ref[0]) + shift_ref[0]).astype(BF16)


def _inproj_logf_kernel(x_ref, gain_ref, scale_ref, shift_ref, w_ref, lbt_ref, o_ref, h_scr, *, layer):
    @pl.when(pl.program_id(1) == 0)
    def _():
        _norm_mod_to_scratch(x_ref, gain_ref, scale_ref, shift_ref, h_scr)

    z = jnp.dot(h_scr[...], w_ref[...], preferred_element_type=F32)
    t = lbt_ref[...]
    e = jnp.exp(t - jnp.max(t, axis=0, keepdims=True))
    lb = jnp.sum(e[:layer + 1], axis=0, keepdims=True) / jnp.sum(e, axis=0, keepdims=True)
    o_ref[...] = jnp.log(lb + (1.0 - lb) * _sigmoid(z))


def _inproj_act_kernel(x_ref, gain_ref, scale_ref, shift_ref, w_ref, o_ref, h_scr):
    j = pl.program_id(1)

    @pl.when(j == 0)
    def _():
        _norm_mod_to_scratch(x_ref, gain_ref, scale_ref, shift_ref, h_scr)

    z = jnp.dot(h_scr[...], w_ref[...], preferred_element_type=F32)

    @pl.when(j < 4)
    def _():
        o_ref[...] = _sigmoid(z).astype(BF16)

    @pl.when((j == 4) | (j == 6))
    def _():
        o_ref[...] = _silu(z).astype(BF16)

    @pl.when((j == 5) | (j == 7))
    def _():
        o_ref[...] = z.astype(BF16)

    @pl.when(j == 8)
    def _():
        o_ref[...] = _gelu_tanh(z).astype(BF16)


def _inproj(x2, gain, scale, shift, w_bf, lb_table, layer, seq):
    n, d = x2.shape
    bsz = n // seq
    tm, tn = 512, 1024
    tiles_per_batch = seq // tm
    gain2 = gain.reshape(1, d)
    scale3 = scale.reshape(bsz, 1, d)
    shift3 = shift.reshape(bsz, 1, d)
    common = [pl.BlockSpec((tm, d), lambda i, j: (i, 0)),
              pl.BlockSpec((1, d), lambda i, j: (0, 0)),
              pl.BlockSpec((1, 1, d), lambda i, j: (i // tiles_per_batch, 0, 0)),
              pl.BlockSpec((1, 1, d), lambda i, j: (i // tiles_per_batch, 0, 0))]
    n_lb = lb_table.shape[0]
    logf = pl.pallas_call(
        functools.partial(_inproj_logf_kernel, layer=layer),
        grid=(n // tm, 2),
        in_specs=common + [pl.BlockSpec((d, tn), lambda i, j: (0, j + 2)),
                           pl.BlockSpec((n_lb, tn), lambda i, j: (0, j))],
        out_specs=pl.BlockSpec((tm, tn), lambda i, j: (i, j)),
        out_shape=jax.ShapeDtypeStruct((n, 2 * tn), F32),
        scratch_shapes=[pltpu.VMEM((tm, d), BF16)],
        compiler_params=_params("parallel", "arbitrary"),
        name="inproj_logf",
    )(x2, gain2, scale3, shift3, w_bf, lb_table)
    acts = pl.pallas_call(
        _inproj_act_kernel,
        grid=(n // tm, 9),
        in_specs=common + [pl.BlockSpec(
            (d, tn), lambda i, j: (0, jnp.where(j < 4, j + 7, jnp.where(j < 6, j - 4, j - 2))))],
        out_specs=pl.BlockSpec((tm, tn), lambda i, j: (i, j)),
        out_shape=jax.ShapeDtypeStruct((n, 9 * tn), BF16),
        scratch_shapes=[pltpu.VMEM((tm, d), BF16)],
        compiler_params=_params("parallel", "arbitrary"),
        name="inproj_act",
    )(x2, gain2, scale3, shift3, w_bf)
    return logf, acts


def _hgrn_level_halves(chunk):
    return [chunk >> (i + 1) for i in range(chunk.bit_length() - 1)]


def _hgrn_kernel(q_ref, v_ref, g_ref, lff_ref, lfb_ref, gain_ref, o_ref,
                 oacc, qf_scr, kf_scr, qb_scr, kb_scr, *, chunk, group):
    seq = q_ref.shape[1]
    nchunks = seq // chunk
    halves = _hgrn_level_halves(chunk)
    nt = (((1,), (1,)), ((), ()))
    tn = (((0,), (0,)), ((), ()))

    row = lax.broadcasted_iota(jnp.int32, (chunk, LANES), 0)
    ti = lax.broadcasted_iota(jnp.int32, (chunk, chunk), 0)
    si = lax.broadcasted_iota(jnp.int32, (chunk, chunk), 1)
    txs = ti ^ si
    tri_f = jnp.where(si <= ti, 1.0, 0.0).astype(BF16)
    tri_b = jnp.where(si >= ti, 1.0, 0.0).astype(BF16)

    def cumulative(lf, tri):
        hi = lf.astype(BF16)
        mid = (lf - hi.astype(F32)).astype(BF16)
        r = jnp.dot(tri, jnp.concatenate([hi, mid], axis=1), preferred_element_type=F32)
        return r[:, :LANES] + r[:, LANES:]

    lid = functools.reduce(lambda a, b: a + b, [(txs >= h).astype(jnp.int32) for h in halves])

    def boundary_diff(cum, m, reverse):
        pieces = []
        for base in range(0, chunk, 2 * m):
            bnd = base + (m if reverse else m - 1)
            pieces.append(jnp.broadcast_to(cum[bnd:bnd + 1, :], (2 * m, LANES)))
        return cum - jnp.concatenate(pieces, axis=0)

    def intra(c):
        r0 = pl.multiple_of(c * chunk, chunk)
        q = q_ref[0, pl.ds(r0, chunk), :].astype(F32)
        v = v_ref[0, pl.ds(r0, chunk), :]
        lff = lff_ref[0, pl.ds(r0, chunk), :]
        lfb = lfb_ref[0, pl.ds(r0, chunk), :]
        cum_f = cumulative(lff, tri_f)
        cum_b = cumulative(lfb, tri_b)
        ff = jnp.exp(lff)
        fb = jnp.exp(lfb)
        kf = 1.0 - ff
        kb = 1.0 - fb

        p0 = lax.dot_general(q.astype(BF16), (kf + kb).astype(BF16), nt, preferred_element_type=F32)
        scores = jnp.where(lid == 0, p0, 0.0)
        for m in halves:
            later = (row & m) != 0
            if 2 * m >= SUBLANES:
                d_f = boundary_diff(cum_f, m, False)
                d_b = boundary_diff(cum_b, m, True)
                x = q * jnp.exp(jnp.where(later, d_f, d_b))
                y = jnp.where(later, kb, kf) * jnp.exp(-jnp.where(later, d_b, d_f))
            elif m == 2:
                r4 = row & 3
                prev = lambda a: pltpu.roll(a, 1, 0)
                nxt = lambda a: pltpu.roll(a, chunk - 1, 0)
                x = q * jnp.where(r4 == 0, fb * nxt(fb), jnp.where(r4 == 1, fb,
                                  jnp.where(r4 == 2, ff, ff * prev(ff))))
                y = jnp.where(r4 == 0, kf * nxt(ff), jnp.where(r4 == 1, kf,
                              jnp.where(r4 == 2, kb, kb * prev(fb))))
            else:
                x = q * jnp.where(later, ff, fb)
                y = jnp.where(later, kb, kf)
            pm = lax.dot_general(x.astype(BF16), y.astype(BF16), nt, preferred_element_type=F32)
            scores = jnp.where(lid == m.bit_length(), pm, scores)
        oacc[pl.ds(r0, chunk), :] = jnp.dot(scores.astype(BF16), v, preferred_element_type=F32)
        tot_f = cum_f[chunk - 1:chunk, :]
        tot_b = cum_b[0:1, :]
        qf_scr[pl.ds(r0, chunk), :] = (q * jnp.exp(cum_f)).astype(BF16)
        kf_scr[pl.ds(r0, chunk), :] = (kf * jnp.exp(tot_f - cum_f)).astype(BF16)
        qb_scr[pl.ds(r0, chunk), :] = (q * jnp.exp(cum_b)).astype(BF16)
        kb_scr[pl.ds(r0, chunk), :] = (kb * jnp.exp(tot_b - cum_b)).astype(BF16)

    def intra_body(i, carry):
        for u in range(group):
            intra(i * group + u)
        return carry

    lax.fori_loop(0, nchunks // group, intra_body, 0)

    def apply_state(c, st, q_scr, k_scr, lf_ref):
        r0 = pl.multiple_of(c * chunk, chunk)
        o = oacc[pl.ds(r0, chunk), :] + lax.dot_general(
            q_scr[pl.ds(r0, chunk), :], st.astype(BF16), nt, preferred_element_type=F32)
        tot = jnp.sum(lf_ref[0, pl.ds(r0, chunk), :], axis=0, keepdims=True)
        st = st * jnp.exp(tot) + lax.dot_general(
            v_ref[0, pl.ds(r0, chunk), :], k_scr[pl.ds(r0, chunk), :], tn, preferred_element_type=F32)
        return r0, o, st

    def finish(r0, o):
        y = o * lax.rsqrt(jnp.mean(o * o, axis=-1, keepdims=True) + EPS) * gain_ref[0]
        o_ref[0, pl.ds(r0, chunk), :] = (y * g_ref[0, pl.ds(r0, chunk), :].astype(F32)).astype(BF16)

    def sweep(final):
        def body(i, states):
            st_f, st_b = states
            rf, of, st_f = apply_state(i, st_f, qf_scr, kf_scr, lff_ref)
            rb, ob, st_b = apply_state(nchunks - 1 - i, st_b, qb_scr, kb_scr, lfb_ref)
            if final:
                finish(rf, of)
                finish(rb, ob)
            else:
                oacc[pl.ds(rf, chunk), :] = of
                oacc[pl.ds(rb, chunk), :] = ob
            return st_f, st_b
        return body

    zero = jnp.zeros((LANES, LANES), F32)
    states = lax.fori_loop(0, nchunks // 2, sweep(False), (zero, zero), unroll=2)
    lax.fori_loop(nchunks // 2, nchunks, sweep(True), states, unroll=2)


def _hgrn(acts3, logf3, hg_norm):
    bsz, seq, _ = acts3.shape
    col = lambda off: pl.BlockSpec((1, seq, LANES), lambda b, h: (b, 0, off + h))
    return pl.pallas_call(
        functools.partial(_hgrn_kernel, chunk=HG_CHUNK, group=4),
        grid=(bsz, HG_HEADS),
        in_specs=[col(ACT_Q), col(ACT_V), col(ACT_G), col(0), col(HG_HEADS),
                  pl.BlockSpec((1, 1, LANES), lambda b, h: (h, 0, 0))],
        out_specs=pl.BlockSpec((1, seq, LANES), lambda b, h: (b, 0, h)),
        out_shape=jax.ShapeDtypeStruct((bsz, seq, HG_HEADS * LANES), BF16),
        scratch_shapes=[pltpu.VMEM((seq, LANES), F32)] + [pltpu.VMEM((seq, LANES), BF16)] * 4,
        compiler_params=_params("parallel", "parallel"),
        name="hgrn2",
    )(acts3, acts3, acts3, logf3, logf3, hg_norm.reshape(HG_HEADS, 1, LANES))


def _scan_tile(a, u, reverse):
    rows = a.shape[0]
    row = lax.broadcasted_iota(jnp.int32, a.shape, 0)
    d = 1
    while d < rows:
        if reverse:
            ok = row < rows - d
            shift = rows - d
        else:
            ok = row >= d
            shift = d
        a_s = jnp.where(ok, pltpu.roll(a, shift, 0), 1.0)
        u_s = jnp.where(ok, pltpu.roll(u, shift, 0), 0.0)
        u = u + a * u_s
        a = a * a_s
        d *= 2
    return a, u


def _rglru_kernel(rx_ref, gy_ref, cw_ref, cb_ref, wa_ref, ba_ref, wx_ref, bx_ref, lam_ref, o_ref,
                  xpad, hf_scr, ab_scr, ub_scr, *, tile):
    seq = rx_ref.shape[1]
    ntiles = seq // tile
    halo = 8

    xpad[0:halo, :] = jnp.zeros((halo, LANES), F32)
    xpad[halo + seq:halo + seq + halo, :] = jnp.zeros((halo, LANES), F32)
    xpad[halo:halo + seq, :] = rx_ref[0].astype(F32)

    lam = lam_ref[...]
    neg_c_softplus = -RG_C * (jnp.maximum(-lam, 0.0) + jnp.log(1.0 + jnp.exp(-jnp.abs(lam))))

    def gates(xcb, z):
        r = _sigmoid(jnp.dot(xcb, wa_ref[z, 0], preferred_element_type=F32) + ba_ref[z, 0])
        gi = _sigmoid(jnp.dot(xcb, wx_ref[z, 0], preferred_element_type=F32) + bx_ref[z, 0])
        a = jnp.exp(r * neg_c_softplus[z:z + 1, :])
        return a, jnp.sqrt(1.0 - a * a) * gi

    def fwd_body(i, carry):
        t0 = pl.multiple_of(i * tile, tile)
        ext = xpad[pl.ds(t0, tile + 2 * halo), :]
        n_ext = tile + 2 * halo
        tap = lambda off: pltpu.roll(ext, (-off) % n_ext, 0)[halo:halo + tile]
        xc = (cw_ref[0:1, :] * tap(-2) + cw_ref[1:2, :] * tap(-1) + cw_ref[2:3, :] * ext[halo:halo + tile]
              + cw_ref[3:4, :] * tap(1) + cb_ref[...])
        xcb = xc.astype(BF16)
        a_f, s_f = gates(xcb, 0)
        a_b, s_b = gates(xcb, 1)
        ap, hl = _scan_tile(a_f, s_f * xc, reverse=False)
        h = hl + ap * carry
        hf_scr[pl.ds(t0, tile), :] = h
        ab_scr[pl.ds(t0, tile), :] = a_b
        ub_scr[pl.ds(t0, tile), :] = s_b * xc
        return h[tile - 1:tile, :]

    lax.fori_loop(0, ntiles, fwd_body, jnp.zeros((1, LANES), F32))

    def bwd_body(i, carry):
        t0 = pl.multiple_of((ntiles - 1 - i) * tile, tile)
        ap, hl = _scan_tile(ab_scr[pl.ds(t0, tile), :], ub_scr[pl.ds(t0, tile), :], reverse=True)
        h = hl + ap * carry
        o_ref[0, pl.ds(t0, tile), :] = (
            (hf_scr[pl.ds(t0, tile), :] + h) * gy_ref[0, pl.ds(t0, tile), :].astype(F32)).astype(BF16)
        return h[0:1, :]

    lax.fori_loop(0, ntiles, bwd_body, jnp.zeros((1, LANES), F32))


def _rglru(acts3, conv_w, conv_b, w_a, b_a, w_x, b_x, lam):
    bsz, seq, _ = acts3.shape
    width = RG_BLOCKS * LANES
    tile = 256
    rx_off, gy_off = ACT_RX, ACT_GY
    col = lambda off: pl.BlockSpec((1, seq, LANES), lambda b, j: (b, 0, off + j))
    wspec = pl.BlockSpec((2, 1, LANES, LANES), lambda b, j: (0, j, 0, 0))
    bspec = pl.BlockSpec((2, 1, 1, LANES), lambda b, j: (0, j, 0, 0))
    return pl.pallas_call(
        functools.partial(_rglru_kernel, tile=tile),
        grid=(bsz, RG_BLOCKS),
        in_specs=[col(rx_off), col(gy_off),
                  pl.BlockSpec((RG_CONV, LANES), lambda b, j: (0, j)),
                  pl.BlockSpec((1, LANES), lambda b, j: (0, j)),
                  wspec, bspec, wspec, bspec,
                  pl.BlockSpec((2, LANES), lambda b, j: (0, j))],
        out_specs=pl.BlockSpec((1, seq, LANES), lambda b, j: (b, 0, j)),
        out_shape=jax.ShapeDtypeStruct((bsz, seq, width), BF16),
        scratch_shapes=[pltpu.VMEM((seq + 16, LANES), F32),
                        pltpu.VMEM((seq, LANES), F32),
                        pltpu.VMEM((seq, LANES), F32),
                        pltpu.VMEM((seq, LANES), F32)],
        compiler_params=_params("parallel", "parallel"),
        name="rglru",
    )(acts3, acts3, conv_w, conv_b.reshape(1, width), w_a.astype(BF16),
      b_a.reshape(2, RG_BLOCKS, 1, LANES), w_x.astype(BF16), b_x.reshape(2, RG_BLOCKS, 1, LANES), lam)


def _mixout_kernel(ya_ref, yb_ref, sga_ref, sgb_ref, x_ref, gate_ref, scale_ref, shift_ref, gain_ref,
                   wpa_ref, wpb_ref, wo_ref, wr_ref, x1_ref, h2_ref, eid_ref, ew_ref, cnt_ref, running):
    y_a = jnp.dot(ya_ref[...], wpa_ref[...], preferred_element_type=F32)
    y_b = jnp.dot(yb_ref[...], wpb_ref[...], preferred_element_type=F32)
    merged = sga_ref[...].astype(F32) * y_a + sgb_ref[...].astype(F32) * y_b
    mix = jnp.dot(merged.astype(BF16), wo_ref[...], preferred_element_type=F32)
    x1 = x_ref[...] + gate_ref[0] * mix
    x1_ref[...] = x1
    y = x1 * lax.rsqrt(jnp.mean(x1 * x1, axis=-1, keepdims=True) + EPS) * gain_ref[...]
    h2 = y * (1.0 + scale_ref[0]) + shift_ref[0]
    h2_ref[...] = h2

    logits = jnp.dot(h2, wr_ref[...], preferred_element_type=F32, precision=lax.Precision.HIGHEST)
    lane = lax.broadcasted_iota(jnp.int32, logits.shape, 1)
    neg = -jnp.inf
    gl = jnp.where(lane < N_GROUPS, logits, neg)
    gmax = jnp.max(gl, axis=-1, keepdims=True)
    g_sel = jnp.min(jnp.where(gl == gmax, lane, LANES), axis=-1, keepdims=True)
    p_group = 1.0 / jnp.sum(jnp.exp(gl - gmax), axis=-1, keepdims=True)
    lo = N_GROUPS + g_sel * EXPERTS_PER_GROUP
    el = jnp.where((lane >= lo) & (lane < lo + EXPERTS_PER_GROUP), logits, neg)
    v1 = jnp.max(el, axis=-1, keepdims=True)
    i1 = jnp.min(jnp.where(el == v1, lane, LANES), axis=-1, keepdims=True)
    el2 = jnp.where(lane == i1, neg, el)
    v2 = jnp.max(el2, axis=-1, keepdims=True)
    i2 = jnp.min(jnp.where(el2 == v2, lane, LANES), axis=-1, keepdims=True)
    e21 = jnp.exp(v2 - v1)
    w1 = p_group / (1.0 + e21)
    w2 = p_group * e21 / (1.0 + e21)
    ew_ref[...] = jnp.where(lane == 0, w1, jnp.where(lane == 1, w2, 0.0))

    @pl.when(pl.program_id(0) == 0)
    def _():
        running[...] = jnp.zeros(running.shape, F32)

    e1, e2 = i1 - N_GROUPS, i2 - N_GROUPS
    tm = logits.shape[0]
    before = (lax.broadcasted_iota(jnp.int32, (tm, tm), 0)
              > lax.broadcasted_iota(jnp.int32, (tm, tm), 1))
    before = jnp.where(before, 1.0, 0.0).astype(BF16)
    oh1 = jnp.where(lane == e1, 1.0, 0.0)
    oh2 = jnp.where(lane == e2, 1.0, 0.0)
    pre1 = jnp.dot(before, oh1.astype(BF16), preferred_element_type=F32)
    pre2 = jnp.dot(before, oh2.astype(BF16), preferred_element_type=F32)
    cnt1 = jnp.sum(oh1, axis=0, keepdims=True)
    base = running[...]
    r1 = jnp.sum(oh1 * (pre1 + base), axis=-1, keepdims=True).astype(jnp.int32)
    r2 = jnp.sum(oh2 * (pre2 + base + cnt1), axis=-1, keepdims=True).astype(jnp.int32)
    total = base + cnt1 + jnp.sum(oh2, axis=0, keepdims=True)
    running[...] = total
    cnt_ref[...] = total.astype(jnp.int32)
    eid_ref[...] = jnp.where(lane == 0, e1, jnp.where(lane == 1, e2,
                             jnp.where(lane == 2, r1, jnp.where(lane == 3, r2, 0))))


def _mixout(ya, yb, acts, x2, gate1, scale2, shift2, gain2, wpa, wpb, wo, w_route, seq):
    n, d = x2.shape
    bsz = n // seq
    tm = 256
    tpb = seq // tm
    half = ya.shape[1]
    ga_blk = 0
    per_batch = pl.BlockSpec((1, 1, d), lambda i: (i // tpb, 0, 0))
    const = lambda shape: pl.BlockSpec(shape, lambda i: (0, 0), pipeline_mode=pl.Buffered(1))
    return pl.pallas_call(
        _mixout_kernel,
        grid=(n // tm,),
        in_specs=[pl.BlockSpec((tm, half), lambda i: (i, 0)),
                  pl.BlockSpec((tm, half), lambda i: (i, 0)),
                  pl.BlockSpec((tm, d), lambda i: (i, ga_blk)),
                  pl.BlockSpec((tm, d), lambda i: (i, ga_blk + 1)),
                  pl.BlockSpec((tm, d), lambda i: (i, 0)),
                  per_batch, per_batch, per_batch,
                  const((1, d)), const((half, d)), const((half, d)), const((d, d)), const((d, LANES))],
        out_specs=[pl.BlockSpec((tm, d), lambda i: (i, 0)),
                   pl.BlockSpec((tm, d), lambda i: (i, 0)),
                   pl.BlockSpec((tm, LANES), lambda i: (i, 0)),
                   pl.BlockSpec((tm, LANES), lambda i: (i, 0)),
                   pl.BlockSpec((1, LANES), lambda i: (0, 0))],
        out_shape=[jax.ShapeDtypeStruct((n, d), F32),
                   jax.ShapeDtypeStruct((n, d), F32),
                   jax.ShapeDtypeStruct((n, LANES), jnp.int32),
                   jax.ShapeDtypeStruct((n, LANES), F32),
                   jax.ShapeDtypeStruct((1, LANES), jnp.int32)],
        scratch_shapes=[pltpu.VMEM((1, LANES), F32)],
        compiler_params=_params("arbitrary"),
        name="mixout_route",
    )(ya, yb, acts, acts, x2, gate1.reshape(bsz, 1, d), scale2.reshape(bsz, 1, d),
      shift2.reshape(bsz, 1, d), gain2.reshape(1, d), wpa, wpb, wo, w_route)


def _start_row_gather(idx_ref, base, nrows, src_hbm, dst, sem):
    def start(r, carry):
        pltpu.make_async_copy(src_hbm.at[pl.ds(idx_ref[base + r], 1), :], dst.at[pl.ds(r, 1), :], sem).start()
        return carry

    lax.fori_loop(0, nrows, start, 0)


def _wait_row_gather(nrows, src_hbm, dst, sem):
    pltpu.make_async_copy(src_hbm.at[pl.ds(0, nrows), :], dst.at[pl.ds(0, nrows), :], sem).wait()


def _start_row_gather_unrolled(idx_ref, base, nrows, src_hbm, dst, sem):
    for r in range(nrows):
        pltpu.make_async_copy(src_hbm.at[pl.ds(idx_ref[base + r], 1), :], dst.at[pl.ds(r, 1), :], sem).start()


def _expert_kernel(be_ref, nv_ref, first_ref, next_ref, dest_ref, nused_ref,
                   h_hbm, wg_hbm, wu_hbm, wd_hbm, o_ref,
                   xbuf, xsems, row_tok, stage_g, stage_u, stage_d, wg_bf, wu_bf, wd_bf, wsems, *, layer):
    b = pl.program_id(0)
    nb = pl.num_programs(0)
    n_used = nused_ref[0]
    unroll = 8
    cast_rows = 256

    def weight_copies(e):
        return (pltpu.make_async_copy(wg_hbm.at[layer, e], stage_g, wsems.at[0]),
                pltpu.make_async_copy(wu_hbm.at[layer, e], stage_u, wsems.at[1]),
                pltpu.make_async_copy(wd_hbm.at[layer, e], stage_d, wsems.at[2]))

    def gather(blk, slot):
        _start_row_gather_unrolled(row_tok, blk * MOE_BLOCK, MOE_BLOCK, h_hbm, xbuf.at[slot], xsems.at[slot])

    def wait_gather(slot):
        _wait_row_gather(MOE_BLOCK, h_hbm, xbuf.at[slot], xsems.at[slot])

    @pl.when(b == 0)
    def _():
        for cp in weight_copies(be_ref[0]):
            cp.start()

        def clear(i, carry):
            for u in range(unroll):
                row_tok[i * unroll + u] = 0
            return carry

        lax.fori_loop(0, row_tok.shape[0] // unroll, clear, 0)

        def invert(i, carry):
            for u in range(unroll):
                row_tok[dest_ref[i * unroll + u]] = i * (unroll // TOP_K) + u // TOP_K
            return carry

        lax.fori_loop(0, dest_ref.shape[0] // unroll, invert, 0)
        gather(0, 0)

    @pl.when(b < n_used)
    def _():
        slot = b % 2

        @pl.when(first_ref[b] == 1)
        def _():
            for cp in weight_copies(0):
                cp.wait()
            for stage, dst in ((stage_g, wg_bf), (stage_u, wu_bf), (stage_d, wd_bf)):
                def cast(i, carry, stage=stage, dst=dst):
                    r0 = pl.multiple_of(i * cast_rows, cast_rows)
                    dst[pl.ds(r0, cast_rows), :] = stage[pl.ds(r0, cast_rows), :].astype(BF16)
                    return carry

                lax.fori_loop(0, stage.shape[0] // cast_rows, cast, 0)

            @pl.when(next_ref[b] >= 0)
            def _():
                for cp in weight_copies(next_ref[b]):
                    cp.start()

        wait_gather(slot)
        valid = lax.broadcasted_iota(jnp.int32, (MOE_BLOCK, 1), 0) < nv_ref[b]
        xb = jnp.where(valid, xbuf[slot], 0.0).astype(BF16)
        gather(b + 1, 1 - slot)
        gate = jnp.dot(xb, wg_bf[...], preferred_element_type=F32)
        up = jnp.dot(xb, wu_bf[...], preferred_element_type=F32)
        hid = (_silu(gate) * up).astype(BF16)
        o_ref[...] = jnp.dot(hid, wd_bf[...], preferred_element_type=F32)

        @pl.when(b == nb - 1)
        def _():
            wait_gather(1 - slot)

    @pl.when(b >= n_used)
    def _():
        @pl.when(b == n_used)
        def _():
            wait_gather(b % 2)

        o_ref[...] = jnp.zeros(o_ref.shape, o_ref.dtype)


def _experts(h2, wg, wu, wd, layer, plan):
    n, d = h2.shape
    p = plan["rows"]
    nb = p // MOE_BLOCK
    de = wg.shape[-1]
    hbm = pl.BlockSpec(memory_space=pl.ANY)
    grid_spec = pltpu.PrefetchScalarGridSpec(
        num_scalar_prefetch=6,
        grid=(nb,),
        in_specs=[hbm, hbm, hbm, hbm],
        out_specs=pl.BlockSpec((MOE_BLOCK, d), lambda b, *_: (b, 0)),
        scratch_shapes=[pltpu.VMEM((2, MOE_BLOCK, d), F32), pltpu.SemaphoreType.DMA((2,)),
                        pltpu.SMEM((p + MOE_BLOCK,), jnp.int32),
                        pltpu.VMEM((d, de), F32), pltpu.VMEM((d, de), F32), pltpu.VMEM((de, d), F32),
                        pltpu.VMEM((d, de), BF16), pltpu.VMEM((d, de), BF16), pltpu.VMEM((de, d), BF16),
                        pltpu.SemaphoreType.DMA((3,))],
    )
    return pl.pallas_call(
        functools.partial(_expert_kernel, layer=layer),
        grid_spec=grid_spec,
        out_shape=jax.ShapeDtypeStruct((p, d), F32),
        compiler_params=_params("arbitrary"),
        name="experts",
    )(plan["block_expert"], plan["n_valid"], plan["first"], plan["next_expert"], plan["dest"],
      plan["n_used"], h2, wg, wu, wd)


COMBINE_TILE = 64


def _combine_kernel(pos_ref, y_hbm, x1_ref, ew_ref, gate_ref, gain_ref, o_ref, ybuf0, ybuf1, sems):
    i = pl.program_id(0)
    tm = COMBINE_TILE
    rows = TOP_K * tm
    bufs = (ybuf0, ybuf1)

    def gather(tile, k):
        _start_row_gather_unrolled(pos_ref, tile * rows, rows, y_hbm, bufs[k], sems.at[k])

    def wait(k):
        _wait_row_gather(rows, y_hbm, bufs[k], sems.at[k])

    def finish(k):
        sl = slice(k * tm, (k + 1) * tm)
        ew = ew_ref[sl, :]
        moe = ew[:, 0:1] * bufs[k][0:tm, :]
        for j in range(1, TOP_K):
            moe = moe + ew[:, j:j + 1] * bufs[k][j * tm:(j + 1) * tm, :]
        x2 = x1_ref[sl, :] + gate_ref[0] * moe
        o_ref[sl, :] = x2 * lax.rsqrt(jnp.mean(x2 * x2, axis=-1, keepdims=True) + EPS) * gain_ref[...]

    @pl.when(i == 0)
    def _():
        gather(0, 0)

    wait(0)
    gather(2 * i + 1, 1)
    finish(0)
    wait(1)
    gather(2 * i + 2, 0)
    finish(1)

    @pl.when(i == pl.num_programs(0) - 1)
    def _():
        wait(0)


def _combine(ys, pos_tiles, x1, ew, gate2, final_gain, seq):
    n, d = x1.shape
    bsz = n // seq
    tm = 2 * COMBINE_TILE
    tpb = seq // tm
    grid_spec = pltpu.PrefetchScalarGridSpec(
        num_scalar_prefetch=1,
        grid=(n // tm,),
        in_specs=[pl.BlockSpec(memory_space=pl.ANY),
                  pl.BlockSpec((tm, d), lambda i, pos: (i, 0)),
                  pl.BlockSpec((tm, LANES), lambda i, pos: (i, 0)),
                  pl.BlockSpec((1, 1, d), lambda i, pos: (i // tpb, 0, 0)),
                  pl.BlockSpec((1, d), lambda i, pos: (0, 0))],
        out_specs=pl.BlockSpec((tm, d), lambda i, pos: (i, 0)),
        scratch_shapes=[pltpu.VMEM((TOP_K * COMBINE_TILE, d), F32), pltpu.VMEM((TOP_K * COMBINE_TILE, d), F32),
                        pltpu.SemaphoreType.DMA((2,))],
    )
    return pl.pallas_call(
        _combine_kernel,
        grid_spec=grid_spec,
        out_shape=jax.ShapeDtypeStruct((n, d), F32),
        compiler_params=_params("arbitrary"),
        name="combine_norm",
    )(pos_tiles, ys, x1, ew, gate2.reshape(bsz, 1, d), final_gain.reshape(1, d))


def _dispatch_plan(route, counts):
    n = route.shape[0]
    a = n * TOP_K
    eid = route[:, :TOP_K]
    rank = route[:, TOP_K:2 * TOP_K]
    cnt = counts[0, :N_EXPERTS]
    padded = (cnt + MOE_BLOCK - 1) // MOE_BLOCK * MOE_BLOCK
    pad_ends = jnp.cumsum(padded)
    pad_starts = pad_ends - padded
    dest = (pad_starts[eid] + rank).astype(jnp.int32)
    p = ((a + MOE_BLOCK - 1) // MOE_BLOCK + N_EXPERTS) * MOE_BLOCK
    nb = p // MOE_BLOCK
    n_used = (pad_ends[-1] // MOE_BLOCK).astype(jnp.int32)
    blk = jnp.arange(nb, dtype=jnp.int32)
    blk_start = jnp.minimum(blk, n_used - 1) * MOE_BLOCK
    block_expert = jnp.minimum(jnp.sum((pad_ends[None, :] <= blk_start[:, None]).astype(jnp.int32), axis=1),
                               N_EXPERTS - 1).astype(jnp.int32)
    n_valid = jnp.clip(pad_starts[block_expert] + cnt[block_expert] - blk_start, 0, MOE_BLOCK).astype(jnp.int32)
    first = ((blk_start == pad_starts[block_expert]) & (blk < n_used)).astype(jnp.int32)
    nxt_blk = pad_ends[block_expert] // MOE_BLOCK
    next_expert = jnp.where(nxt_blk < n_used, block_expert[jnp.minimum(nxt_blk, nb - 1)], -1).astype(jnp.int32)
    pos_tiles = dest.reshape(n // COMBINE_TILE, COMBINE_TILE, TOP_K).transpose(0, 2, 1).reshape(a)
    pos_tiles = jnp.concatenate([pos_tiles, jnp.zeros((TOP_K * COMBINE_TILE,), jnp.int32)])
    plan = dict(block_expert=block_expert, n_valid=n_valid, first=first, next_expert=next_expert,
                dest=dest.reshape(a), n_used=n_used.reshape(1), rows=p)
    return plan, pos_tiles


def kernel(x, c, w_ada, b_ada, norm1, w_in, hg_lb, hg_norm, rg_conv_w, rg_conv_b, rg_w_a, rg_b_a,
           rg_w_x, rg_b_x, rg_lambda, w_proj_a, w_proj_b, w_out, norm2, w_group, w_router, w_gate,
           w_up, w_down, final_norm):
    bsz, seq, d = x.shape
    depth = w_ada.shape[0]
    n = bsz * seq
    x2 = x.reshape(n, d)
    lb_table = hg_lb.reshape(hg_lb.shape[0], -1)
    for layer in range(depth):
        mod = _adaln(c, w_ada[layer], b_ada[layer])
        shift1, scale1, gate1, shift2, scale2, gate2 = jnp.split(mod, 6, axis=-1)
        logf, acts = _inproj(x2, norm1[layer], scale1, shift1, w_in[layer].astype(BF16), lb_table, layer, seq)
        acts3 = acts.reshape(bsz, seq, acts.shape[1])
        ya = _hgrn(acts3, logf.reshape(bsz, seq, logf.shape[1]), hg_norm[layer])
        yb = _rglru(acts3, rg_conv_w[layer], rg_conv_b[layer], rg_w_a[layer], rg_b_a[layer],
                    rg_w_x[layer], rg_b_x[layer], rg_lambda[layer])
        w_route = jnp.concatenate(
            [w_group[layer], w_router[layer].reshape(d, N_EXPERTS),
             jnp.zeros((d, LANES - N_GROUPS - N_EXPERTS), F32)], axis=1)
        x1, h2, route, ew, counts = _mixout(
            ya.reshape(n, -1), yb.reshape(n, -1), acts, x2, gate1, scale2, shift2, norm2[layer],
            w_proj_a[layer].astype(BF16), w_proj_b[layer].astype(BF16), w_out[layer].astype(BF16), w_route, seq)
        plan, pos_tiles = _dispatch_plan(route, counts)
        ys = _experts(h2, w_gate, w_up, w_down, layer, plan)
        if layer + 1 < depth:
            raise NotImplementedError("the fused combine + final norm assumes a single layer")
        x2 = _combine(ys, pos_tiles, x1, ew, gate2, final_norm, seq)
    return x2.reshape(bsz, seq, d)
```

```python
import functools

import jax
import jax.numpy as jnp
import numpy as np
from jax import lax
from jax.experimental import pallas as pl
from jax.experimental.pallas import tpu as pltpu

EPS = 1e-6
LANES = 128
SUBLANES = 8
HG_HEADS = 8
RG_BLOCKS = 8
RG_CONV = 4
RG_C = 8.0
N_GROUPS = 4
EXPERTS_PER_GROUP = 8
N_EXPERTS = N_GROUPS * EXPERTS_PER_GROUP
TOP_K = 2
HG_CHUNK = 128
MOE_BLOCK = 128
VMEM_LIMIT = 56 * 1024 * 1024
ACT_Q, ACT_V, ACT_G, ACT_RX, ACT_GY = 32, 40, 48, 56, 64

F32 = jnp.float32
BF16 = jnp.bfloat16


def _sigmoid(x):
    return 1.0 / (1.0 + jnp.exp(-x))


def _silu(x):
    return x * _sigmoid(x)


def _gelu_tanh(x):
    return 0.5 * x * (1.0 + jnp.tanh(0.7978845608028654 * (x + 0.044715 * (x * x * x))))


def _params(*sem):
    return pltpu.CompilerParams(dimension_semantics=sem, vmem_limit_bytes=VMEM_LIMIT)


def _adaln_kernel(ct_ref, w_ref, b_ref, o_ref, *, kc):
    kdim, nb = ct_ref.shape
    tn = w_ref.shape[1]

    def body(i, accs):
        k0 = pl.multiple_of(i * kc, kc)
        w = w_ref[pl.ds(k0, kc), :]
        cs = _silu(ct_ref[pl.ds(k0, kc), :])
        return tuple(acc + jnp.sum(w * cs[:, b:b + 1], axis=0, keepdims=True)
                     for b, acc in enumerate(accs))

    accs = lax.fori_loop(0, kdim // kc, body, tuple(jnp.zeros((1, tn), F32) for _ in range(nb)))
    for b, acc in enumerate(accs):
        o_ref[b:b + 1, :] = acc + b_ref[...]


def _adaln(c, w, bias):
    bsz, d = c.shape
    n = w.shape[1]
    tn = 1024
    return pl.pallas_call(
        functools.partial(_adaln_kernel, kc=256),
        grid=(n // tn,),
        in_specs=[pl.BlockSpec((d, bsz), lambda j: (0, 0)),
                  pl.BlockSpec((d, tn), lambda j: (0, j)),
                  pl.BlockSpec((1, tn), lambda j: (0, j))],
        out_specs=pl.BlockSpec((bsz, tn), lambda j: (0, j)),
        out_shape=jax.ShapeDtypeStruct((bsz, n), F32),
        compiler_params=_params("parallel"),
        name="adaln",
    )(c.T, w, bias.reshape(1, n))


def _norm_mod_to_scratch(x_ref, gain_ref, scale_ref, shift_ref, h_scr):
    x = x_ref[...]
    y = x * lax.rsqrt(jnp.mean(x * x, axis=-1, keepdims=True) + EPS) * gain_ref[...]
    h_scr[...] = (y * (1.0 + scale_ref[0]) + shift_ref[0]).astype(BF16)


def _inproj_logf_kernel(x_ref, gain_ref, scale_ref, shift_ref, w_ref, lbt_ref, o_ref, h_scr, *, layer):
    @pl.when(pl.program_id(1) == 0)
    def _():
        _norm_mod_to_scratch(x_ref, gain_ref, scale_ref, shift_ref, h_scr)

    z = jnp.dot(h_scr[...], w_ref[...], preferred_element_type=F32)
    t = lbt_ref[...]
    e = jnp.exp(t - jnp.max(t, axis=0, keepdims=True))
    lb = jnp.sum(e[:layer + 1], axis=0, keepdims=True) / jnp.sum(e, axis=0, keepdims=True)
    o_ref[...] = jnp.log(lb + (1.0 - lb) * _sigmoid(z))


def _inproj_act_kernel(x_ref, gain_ref, scale_ref, shift_ref, w_ref, o_ref, h_scr):
    j = pl.program_id(1)

    @pl.when(j == 0)
    def _():
        _norm_mod_to_scratch(x_ref, gain_ref, scale_ref, shift_ref, h_scr)

    z = jnp.dot(h_scr[...], w_ref[...], preferred_element_type=F32)

    @pl.when(j < 4)
    def _():
        o_ref[...] = _sigmoid(z).astype(BF16)

    @pl.when((j == 4) | (j == 6))
    def _():
        o_ref[...] = _silu(z).astype(BF16)

    @pl.when((j == 5) | (j == 7))
    def _():
        o_ref[...] = z.astype(BF16)

    @pl.when(j == 8)
    def _():
        o_ref[...] = _gelu_tanh(z).astype(BF16)


def _inproj(x2, gain, scale, shift, w_bf, lb_table, layer, seq):
    n, d = x2.shape
    bsz = n // seq
    tm, tn = 512, 1024
    tiles_per_batch = seq // tm
    gain2 = gain.reshape(1, d)
    scale3 = scale.reshape(bsz, 1, d)
    shift3 = shift.reshape(bsz, 1, d)
    common = [pl.BlockSpec((tm, d), lambda i, j: (i, 0)),
              pl.BlockSpec((1, d), lambda i, j: (0, 0)),
              pl.BlockSpec((1, 1, d), lambda i, j: (i // tiles_per_batch, 0, 0)),
              pl.BlockSpec((1, 1, d), lambda i, j: (i // tiles_per_batch, 0, 0))]
    n_lb = lb_table.shape[0]
    logf = pl.pallas_call(
        functools.partial(_inproj_logf_kernel, layer=layer),
        grid=(n // tm, 2),
        in_specs=common + [pl.BlockSpec((d, tn), lambda i, j: (0, j + 2)),
                           pl.BlockSpec((n_lb, tn), lambda i, j: (0, j))],
        out_specs=pl.BlockSpec((tm, tn), lambda i, j: (i, j)),
        out_shape=jax.ShapeDtypeStruct((n, 2 * tn), F32),
        scratch_shapes=[pltpu.VMEM((tm, d), BF16)],
        compiler_params=_params("parallel", "arbitrary"),
        name="inproj_logf",
    )(x2, gain2, scale3, shift3, w_bf, lb_table)
    acts = pl.pallas_call(
        _inproj_act_kernel,
        grid=(n // tm, 9),
        in_specs=common + [pl.BlockSpec(
            (d, tn), lambda i, j: (0, jnp.where(j < 4, j + 7, jnp.where(j < 6, j - 4, j - 2))))],
        out_specs=pl.BlockSpec((tm, tn), lambda i, j: (i, j)),
        out_shape=jax.ShapeDtypeStruct((n, 9 * tn), BF16),
        scratch_shapes=[pltpu.VMEM((tm, d), BF16)],
        compiler_params=_params("parallel", "arbitrary"),
        name="inproj_act",
    )(x2, gain2, scale3, shift3, w_bf)
    return logf, acts


def _hgrn_level_halves(chunk):
    return [chunk >> (i + 1) for i in range(chunk.bit_length() - 1)]


def _hgrn_kernel(q_ref, v_ref, g_ref, lff_ref, lfb_ref, gain_ref, o_ref,
                 oacc, qf_scr, kf_scr, qb_scr, kb_scr, *, chunk, group):
    seq = q_ref.shape[1]
    nchunks = seq // chunk
    halves = _hgrn_level_halves(chunk)
    nt = (((1,), (1,)), ((), ()))
    tn = (((0,), (0,)), ((), ()))

    row = lax.broadcasted_iota(jnp.int32, (chunk, LANES), 0)
    ti = lax.broadcasted_iota(jnp.int32, (chunk, chunk), 0)
    si = lax.broadcasted_iota(jnp.int32, (chunk, chunk), 1)
    txs = ti ^ si
    tri_f = jnp.where(si <= ti, 1.0, 0.0).astype(BF16)
    tri_b = jnp.where(si >= ti, 1.0, 0.0).astype(BF16)

    def cumulative(lf, tri):
        hi = lf.astype(BF16)
        mid = (lf - hi.astype(F32)).astype(BF16)
        r = jnp.dot(tri, jnp.concatenate([hi, mid], axis=1), preferred_element_type=F32)
        return r[:, :LANES] + r[:, LANES:]

    lid = functools.reduce(lambda a, b: a + b, [(txs >= h).astype(jnp.int32) for h in halves])

    def boundary_diff(cum, m, reverse):
        pieces = []
        for base in range(0, chunk, 2 * m):
            bnd = base + (m if reverse else m - 1)
            pieces.append(jnp.broadcast_to(cum[bnd:bnd + 1, :], (2 * m, LANES)))
        return cum - jnp.concatenate(pieces, axis=0)

    def intra(c):
        r0 = pl.multiple_of(c * chunk, chunk)
        q = q_ref[0, pl.ds(r0, chunk), :].astype(F32)
        v = v_ref[0, pl.ds(r0, chunk), :]
        lff = lff_ref[0, pl.ds(r0, chunk), :]
        lfb = lfb_ref[0, pl.ds(r0, chunk), :]
        cum_f = cumulative(lff, tri_f)
        cum_b = cumulative(lfb, tri_b)
        ff = jnp.exp(lff)
        fb = jnp.exp(lfb)
        kf = 1.0 - ff
        kb = 1.0 - fb

        p0 = lax.dot_general(q.astype(BF16), (kf + kb).astype(BF16), nt, preferred_element_type=F32)
        scores = jnp.where(lid == 0, p0, 0.0)
        for m in halves:
            later = (row & m) != 0
            if 2 * m >= SUBLANES:
                d_f = boundary_diff(cum_f, m, False)
                d_b = boundary_diff(cum_b, m, True)
                x = q * jnp.exp(jnp.where(later, d_f, d_b))
                y = jnp.where(later, kb, kf) * jnp.exp(-jnp.where(later, d_b, d_f))
            elif m == 2:
                r4 = row & 3
                prev = lambda a: pltpu.roll(a, 1, 0)
                nxt = lambda a: pltpu.roll(a, chunk - 1, 0)
                x = q * jnp.where(r4 == 0, fb * nxt(fb), jnp.where(r4 == 1, fb,
                                  jnp.where(r4 == 2, ff, ff * prev(ff))))
                y = jnp.where(r4 == 0, kf * nxt(ff), jnp.where(r4 == 1, kf,
                              jnp.where(r4 == 2, kb, kb * prev(fb))))
            else:
                x = q * jnp.where(later, ff, fb)
                y = jnp.where(later, kb, kf)
            pm = lax.dot_general(x.astype(BF16), y.astype(BF16), nt, preferred_element_type=F32)
            scores = jnp.where(lid == m.bit_length(), pm, scores)
        oacc[pl.ds(r0, chunk), :] = jnp.dot(scores.astype(BF16), v, preferred_element_type=F32)
        tot_f = cum_f[chunk - 1:chunk, :]
        tot_b = cum_b[0:1, :]
        qf_scr[pl.ds(r0, chunk), :] = (q * jnp.exp(cum_f)).astype(BF16)
        kf_scr[pl.ds(r0, chunk), :] = (kf * jnp.exp(tot_f - cum_f)).astype(BF16)
        qb_scr[pl.ds(r0, chunk), :] = (q * jnp.exp(cum_b)).astype(BF16)
        kb_scr[pl.ds(r0, chunk), :] = (kb * jnp.exp(tot_b - cum_b)).astype(BF16)

    def intra_body(i, carry):
        for u in range(group):
            intra(i * group + u)
        return carry

    lax.fori_loop(0, nchunks // group, intra_body, 0)

    def apply_state(c, st, q_scr, k_scr, lf_ref):
        r0 = pl.multiple_of(c * chunk, chunk)
        o = oacc[pl.ds(r0, chunk), :] + lax.dot_general(
            q_scr[pl.ds(r0, chunk), :], st.astype(BF16), nt, preferred_element_type=F32)
        tot = jnp.sum(lf_ref[0, pl.ds(r0, chunk), :], axis=0, keepdims=True)
        st = st * jnp.exp(tot) + lax.dot_general(
            v_ref[0, pl.ds(r0, chunk), :], k_scr[pl.ds(r0, chunk), :], tn, preferred_element_type=F32)
        return r0, o, st

    def finish(r0, o):
        y = o * lax.rsqrt(jnp.mean(o * o, axis=-1, keepdims=True) + EPS) * gain_ref[0]
        o_ref[0, pl.ds(r0, chunk), :] = (y * g_ref[0, pl.ds(r0, chunk), :].astype(F32)).astype(BF16)

    def sweep(final):
        def body(i, states):
            st_f, st_b = states
            rf, of, st_f = apply_state(i, st_f, qf_scr, kf_scr, lff_ref)
            rb, ob, st_b = apply_state(nchunks - 1 - i, st_b, qb_scr, kb_scr, lfb_ref)
            if final:
                finish(rf, of)
                finish(rb, ob)
            else:
                oacc[pl.ds(rf, chunk), :] = of
                oacc[pl.ds(rb, chunk), :] = ob
            return st_f, st_b
        return body

    zero = jnp.zeros((LANES, LANES), F32)
    states = lax.fori_loop(0, nchunks // 2, sweep(False), (zero, zero), unroll=2)
    lax.fori_loop(nchunks // 2, nchunks, sweep(True), states, unroll=2)


def _hgrn(acts3, logf3, hg_norm):
    bsz, seq, _ = acts3.shape
    col = lambda off: pl.BlockSpec((1, seq, LANES), lambda b, h: (b, 0, off + h))
    return pl.pallas_call(
        functools.partial(_hgrn_kernel, chunk=HG_CHUNK, group=4),
        grid=(bsz, HG_HEADS),
        in_specs=[col(ACT_Q), col(ACT_V), col(ACT_G), col(0), col(HG_HEADS),
                  pl.BlockSpec((1, 1, LANES), lambda b, h: (h, 0, 0))],
        out_specs=pl.BlockSpec((1, seq, LANES), lambda b, h: (b, 0, h)),
        out_shape=jax.ShapeDtypeStruct((bsz, seq, HG_HEADS * LANES), BF16),
        scratch_shapes=[pltpu.VMEM((seq, LANES), F32)] + [pltpu.VMEM((seq, LANES), BF16)] * 4,
        compiler_params=_params("parallel", "parallel"),
        name="hgrn2",
    )(acts3, acts3, acts3, logf3, logf3, hg_norm.reshape(HG_HEADS, 1, LANES))


def _scan_tile(a, u, reverse):
    rows = a.shape[0]
    row = lax.broadcasted_iota(jnp.int32, a.shape, 0)
    d = 1
    while d < rows:
        if reverse:
            ok = row < rows - d
            shift = rows - d
        else:
            ok = row >= d
            shift = d
        a_s = jnp.where(ok, pltpu.roll(a, shift, 0), 1.0)
        u_s = jnp.where(ok, pltpu.roll(u, shift, 0), 0.0)
        u = u + a * u_s
        a = a * a_s
        d *= 2
    return a, u


def _rglru_kernel(rx_ref, gy_ref, cw_ref, cb_ref, wa_ref, ba_ref, wx_ref, bx_ref, lam_ref, o_ref,
                  xpad, hf_scr, ab_scr, ub_scr, *, tile):
    seq = rx_ref.shape[1]
    ntiles = seq // tile
    halo = 8

    xpad[0:halo, :] = jnp.zeros((halo, LANES), F32)
    xpad[halo + seq:halo + seq + halo, :] = jnp.zeros((halo, LANES), F32)
    xpad[halo:halo + seq, :] = rx_ref[0].astype(F32)

    lam = lam_ref[...]
    neg_c_softplus = -RG_C * (jnp.maximum(-lam, 0.0) + jnp.log(1.0 + jnp.exp(-jnp.abs(lam))))

    def gates(xcb, z):
        r = _sigmoid(jnp.dot(xcb, wa_ref[z, 0], preferred_element_type=F32) + ba_ref[z, 0])
        gi = _sigmoid(jnp.dot(xcb, wx_ref[z, 0], preferred_element_type=F32) + bx_ref[z, 0])
        a = jnp.exp(r * neg_c_softplus[z:z + 1, :])
        return a, jnp.sqrt(1.0 - a * a) * gi

    def fwd_body(i, carry):
        t0 = pl.multiple_of(i * tile, tile)
        ext = xpad[pl.ds(t0, tile + 2 * halo), :]
        n_ext = tile + 2 * halo
        tap = lambda off: pltpu.roll(ext, (-off) % n_ext, 0)[halo:halo + tile]
        xc = (cw_ref[0:1, :] * tap(-2) + cw_ref[1:2, :] * tap(-1) + cw_ref[2:3, :] * ext[halo:halo + tile]
              + cw_ref[3:4, :] * tap(1) + cb_ref[...])
        xcb = xc.astype(BF16)
        a_f, s_f = gates(xcb, 0)
        a_b, s_b = gates(xcb, 1)
        ap, hl = _scan_tile(a_f, s_f * xc, reverse=False)
        h = hl + ap * carry
        hf_scr[pl.ds(t0, tile), :] = h
        ab_scr[pl.ds(t0, tile), :] = a_b
        ub_scr[pl.ds(t0, tile), :] = s_b * xc
        return h[tile - 1:tile, :]

    lax.fori_loop(0, ntiles, fwd_body, jnp.zeros((1, LANES), F32))

    def bwd_body(i, carry):
        t0 = pl.multiple_of((ntiles - 1 - i) * tile, tile)
        ap, hl = _scan_tile(ab_scr[pl.ds(t0, tile), :], ub_scr[pl.ds(t0, tile), :], reverse=True)
        h = hl + ap * carry
        o_ref[0, pl.ds(t0, tile), :] = (
            (hf_scr[pl.ds(t0, tile), :] + h) * gy_ref[0, pl.ds(t0, tile), :].astype(F32)).astype(BF16)
        return h[0:1, :]

    lax.fori_loop(0, ntiles, bwd_body, jnp.zeros((1, LANES), F32))


def _rglru(acts3, conv_w, conv_b, w_a, b_a, w_x, b_x, lam):
    bsz, seq, _ = acts3.shape
    width = RG_BLOCKS * LANES
    tile = 256
    rx_off, gy_off = ACT_RX, ACT_GY
    col = lambda off: pl.BlockSpec((1, seq, LANES), lambda b, j: (b, 0, off + j))
    wspec = pl.BlockSpec((2, 1, LANES, LANES), lambda b, j: (0, j, 0, 0))
    bspec = pl.BlockSpec((2, 1, 1, LANES), lambda b, j: (0, j, 0, 0))
    return pl.pallas_call(
        functools.partial(_rglru_kernel, tile=tile),
        grid=(bsz, RG_BLOCKS),
        in_specs=[col(rx_off), col(gy_off),
                  pl.BlockSpec((RG_CONV, LANES), lambda b, j: (0, j)),
                  pl.BlockSpec((1, LANES), lambda b, j: (0, j)),
                  wspec, bspec, wspec, bspec,
                  pl.BlockSpec((2, LANES), lambda b, j: (0, j))],
        out_specs=pl.BlockSpec((1, seq, LANES), lambda b, j: (b, 0, j)),
        out_shape=jax.ShapeDtypeStruct((bsz, seq, width), BF16),
        scratch_shapes=[pltpu.VMEM((seq + 16, LANES), F32),
                        pltpu.VMEM((seq, LANES), F32),
                        pltpu.VMEM((seq, LANES), F32),
                        pltpu.VMEM((seq, LANES), F32)],
        compiler_params=_params("parallel", "parallel"),
        name="rglru",
    )(acts3, acts3, conv_w, conv_b.reshape(1, width), w_a.astype(BF16),
      b_a.reshape(2, RG_BLOCKS, 1, LANES), w_x.astype(BF16), b_x.reshape(2, RG_BLOCKS, 1, LANES), lam)


def _store_token_major(ref, x, row0=0):
    ntok, d = x.shape
    parts = d // LANES
    for c in range(parts):
        ref[pl.ds(row0 * parts + c, ntok, stride=parts), :] = x[:, c * LANES:(c + 1) * LANES]


def _load_token_major(ref, ntok, parts, row0=0):
    return jnp.concatenate([ref[pl.ds(row0 * parts + c, ntok, stride=parts), :] for c in range(parts)], axis=1)


def _mixout_kernel(ya_ref, yb_ref, sga_ref, sgb_ref, x_ref, gate_ref, scale_ref, shift_ref, gain_ref,
                   wpa_ref, wpb_ref, wo_ref, wr_ref, x1_ref, h2_ref, eid_ref, ew_ref, cnt_ref, running):
    y_a = jnp.dot(ya_ref[...], wpa_ref[...], preferred_element_type=F32)
    y_b = jnp.dot(yb_ref[...], wpb_ref[...], preferred_element_type=F32)
    merged = sga_ref[...].astype(F32) * y_a + sgb_ref[...].astype(F32) * y_b
    mix = jnp.dot(merged.astype(BF16), wo_ref[...], preferred_element_type=F32)
    x1 = x_ref[...] + gate_ref[0] * mix
    x1_ref[...] = x1
    y = x1 * lax.rsqrt(jnp.mean(x1 * x1, axis=-1, keepdims=True) + EPS) * gain_ref[...]
    h2 = y * (1.0 + scale_ref[0]) + shift_ref[0]
    _store_token_major(h2_ref, h2)

    logits = jnp.dot(h2, wr_ref[...], preferred_element_type=F32, precision=lax.Precision.HIGHEST)
    lane = lax.broadcasted_iota(jnp.int32, logits.shape, 1)
    neg = -jnp.inf
    gl = jnp.where(lane < N_GROUPS, logits, neg)
    gmax = jnp.max(gl, axis=-1, keepdims=True)
    g_sel = jnp.min(jnp.where(gl == gmax, lane, LANES), axis=-1, keepdims=True)
    p_group = 1.0 / jnp.sum(jnp.exp(gl - gmax), axis=-1, keepdims=True)
    lo = N_GROUPS + g_sel * EXPERTS_PER_GROUP
    el = jnp.where((lane >= lo) & (lane < lo + EXPERTS_PER_GROUP), logits, neg)
    v1 = jnp.max(el, axis=-1, keepdims=True)
    i1 = jnp.min(jnp.where(el == v1, lane, LANES), axis=-1, keepdims=True)
    el2 = jnp.where(lane == i1, neg, el)
    v2 = jnp.max(el2, axis=-1, keepdims=True)
    i2 = jnp.min(jnp.where(el2 == v2, lane, LANES), axis=-1, keepdims=True)
    e21 = jnp.exp(v2 - v1)
    w1 = p_group / (1.0 + e21)
    w2 = p_group * e21 / (1.0 + e21)
    ew_ref[...] = jnp.where(lane == 0, w1, jnp.where(lane == 1, w2, 0.0))

    @pl.when(pl.program_id(0) == 0)
    def _():
        running[...] = jnp.zeros(running.shape, F32)

    e1, e2 = i1 - N_GROUPS, i2 - N_GROUPS
    tm = logits.shape[0]
    before = (lax.broadcasted_iota(jnp.int32, (tm, tm), 0)
              > lax.broadcasted_iota(jnp.int32, (tm, tm), 1))
    before = jnp.where(before, 1.0, 0.0).astype(BF16)
    oh1 = jnp.where(lane == e1, 1.0, 0.0)
    oh2 = jnp.where(lane == e2, 1.0, 0.0)
    pre1 = jnp.dot(before, oh1.astype(BF16), preferred_element_type=F32)
    pre2 = jnp.dot(before, oh2.astype(BF16), preferred_element_type=F32)
    cnt1 = jnp.sum(oh1, axis=0, keepdims=True)
    base = running[...]
    r1 = jnp.sum(oh1 * (pre1 + base), axis=-1, keepdims=True).astype(jnp.int32)
    r2 = jnp.sum(oh2 * (pre2 + base + cnt1), axis=-1, keepdims=True).astype(jnp.int32)
    total = base + cnt1 + jnp.sum(oh2, axis=0, keepdims=True)
    running[...] = total
    cnt_ref[...] = total.astype(jnp.int32)
    eid_ref[...] = jnp.where(lane == 0, e1, jnp.where(lane == 1, e2,
                             jnp.where(lane == 2, r1, jnp.where(lane == 3, r2, 0))))


def _mixout(ya, yb, acts, x2, gate1, scale2, shift2, gain2, wpa, wpb, wo, w_route, seq):
    n, d = x2.shape
    bsz = n // seq
    tm = 256
    tpb = seq // tm
    half = ya.shape[1]
    ga_blk = 0
    per_batch = pl.BlockSpec((1, 1, d), lambda i: (i // tpb, 0, 0))
    const = lambda shape: pl.BlockSpec(shape, lambda i: (0, 0), pipeline_mode=pl.Buffered(1))
    return pl.pallas_call(
        _mixout_kernel,
        grid=(n // tm,),
        in_specs=[pl.BlockSpec((tm, half), lambda i: (i, 0)),
                  pl.BlockSpec((tm, half), lambda i: (i, 0)),
                  pl.BlockSpec((tm, d), lambda i: (i, ga_blk)),
                  pl.BlockSpec((tm, d), lambda i: (i, ga_blk + 1)),
                  pl.BlockSpec((tm, d), lambda i: (i, 0)),
                  per_batch, per_batch, per_batch,
                  const((1, d)), const((half, d)), const((half, d)), const((d, d)), const((d, LANES))],
        out_specs=[pl.BlockSpec((tm, d), lambda i: (i, 0)),
                   pl.BlockSpec((tm * (d // LANES), LANES), lambda i: (i, 0)),
                   pl.BlockSpec((tm, LANES), lambda i: (i, 0)),
                   pl.BlockSpec((tm, LANES), lambda i: (i, 0)),
                   pl.BlockSpec((1, LANES), lambda i: (0, 0))],
        out_shape=[jax.ShapeDtypeStruct((n, d), F32),
                   jax.ShapeDtypeStruct((n * (d // LANES), LANES), F32),
                   jax.ShapeDtypeStruct((n, LANES), jnp.int32),
                   jax.ShapeDtypeStruct((n, LANES), F32),
                   jax.ShapeDtypeStruct((1, LANES), jnp.int32)],
        scratch_shapes=[pltpu.VMEM((1, LANES), F32)],
        compiler_params=_params("arbitrary"),
        name="mixout_route",
    )(ya, yb, acts, acts, x2, gate1.reshape(bsz, 1, d), scale2.reshape(bsz, 1, d),
      shift2.reshape(bsz, 1, d), gain2.reshape(1, d), wpa, wpb, wo, w_route)


def _start_token_gather(idx_ref, base, ntok, parts, src_hbm, dst, sem):
    for r in range(ntok):
        t0 = pl.multiple_of(idx_ref[base + r] * parts, parts)
        pltpu.make_async_copy(src_hbm.at[pl.ds(t0, parts), :], dst.at[pl.ds(r * parts, parts), :], sem).start()


def _wait_token_gather(ntok, parts, src_hbm, dst, sem):
    pltpu.make_async_copy(src_hbm.at[pl.ds(0, ntok * parts), :], dst.at[pl.ds(0, ntok * parts), :], sem).wait()


def _expert_kernel(be_ref, nv_ref, first_ref, next_ref, dest_ref, nused_ref,
                   h_hbm, wg_hbm, wu_hbm, wd_hbm, o_ref,
                   xbuf, xsems, row_tok, stage_g, stage_u, stage_d, wg_bf, wu_bf, wd_bf, wsems, *, layer):
    b = pl.program_id(0)
    nb = pl.num_programs(0)
    n_used = nused_ref[0]
    unroll = 8
    cast_rows = 256

    def weight_copies(e):
        return (pltpu.make_async_copy(wg_hbm.at[layer, e], stage_g, wsems.at[0]),
                pltpu.make_async_copy(wu_hbm.at[layer, e], stage_u, wsems.at[1]),
                pltpu.make_async_copy(wd_hbm.at[layer, e], stage_d, wsems.at[2]))

    parts = xbuf.shape[1] // MOE_BLOCK

    def gather(blk, slot):
        _start_token_gather(row_tok, blk * MOE_BLOCK, MOE_BLOCK, parts, h_hbm, xbuf.at[slot], xsems.at[slot])

    def wait_gather(slot):
        _wait_token_gather(MOE_BLOCK, parts, h_hbm, xbuf.at[slot], xsems.at[slot])

    @pl.when(b == 0)
    def _():
        for cp in weight_copies(be_ref[0]):
            cp.start()

        def clear(i, carry):
            for u in range(unroll):
                row_tok[i * unroll + u] = 0
            return carry

        lax.fori_loop(0, row_tok.shape[0] // unroll, clear, 0)

        def invert(i, carry):
            for u in range(unroll):
                row_tok[dest_ref[i * unroll + u]] = i * (unroll // TOP_K) + u // TOP_K
            return carry

        lax.fori_loop(0, dest_ref.shape[0] // unroll, invert, 0)
        gather(0, 0)

    @pl.when(b < n_used)
    def _():
        slot = b % 2

        @pl.when(first_ref[b] == 1)
        def _():
            for cp in weight_copies(0):
                cp.wait()
            for stage, dst in ((stage_g, wg_bf), (stage_u, wu_bf), (stage_d, wd_bf)):
                def cast(i, carry, stage=stage, dst=dst):
                    r0 = pl.multiple_of(i * cast_rows, cast_rows)
                    dst[pl.ds(r0, cast_rows), :] = stage[pl.ds(r0, cast_rows), :].astype(BF16)
                    return carry

                lax.fori_loop(0, stage.shape[0] // cast_rows, cast, 0)

            @pl.when(next_ref[b] >= 0)
            def _():
                for cp in weight_copies(next_ref[b]):
                    cp.start()

        wait_gather(slot)
        valid = lax.broadcasted_iota(jnp.int32, (MOE_BLOCK, 1), 0) < nv_ref[b]
        xb = jnp.where(valid, _load_token_major(xbuf.at[slot], MOE_BLOCK, parts), 0.0).astype(BF16)
        gather(b + 1, 1 - slot)
        gate = jnp.dot(xb, wg_bf[...], preferred_element_type=F32)
        up = jnp.dot(xb, wu_bf[...], preferred_element_type=F32)
        hid = (_silu(gate) * up).astype(BF16)
        _store_token_major(o_ref, jnp.dot(hid, wd_bf[...], preferred_element_type=F32))

        @pl.when(b == nb - 1)
        def _():
            wait_gather(1 - slot)

    @pl.when(b >= n_used)
    def _():
        @pl.when(b == n_used)
        def _():
            wait_gather(b % 2)

        o_ref[...] = jnp.zeros(o_ref.shape, o_ref.dtype)


def _experts(h2_tm, wg, wu, wd, layer, plan):
    p = plan["rows"]
    nb = p // MOE_BLOCK
    d, de = wg.shape[-2:]
    parts = d // LANES
    hbm = pl.BlockSpec(memory_space=pl.ANY)
    grid_spec = pltpu.PrefetchScalarGridSpec(
        num_scalar_prefetch=6,
        grid=(nb,),
        in_specs=[hbm, hbm, hbm, hbm],
        out_specs=pl.BlockSpec((MOE_BLOCK * parts, LANES), lambda b, *_: (b, 0)),
        scratch_shapes=[pltpu.VMEM((2, MOE_BLOCK * parts, LANES), F32), pltpu.SemaphoreType.DMA((2,)),
                        pltpu.SMEM((p + MOE_BLOCK,), jnp.int32),
                        pltpu.VMEM((d, de), F32), pltpu.VMEM((d, de), F32), pltpu.VMEM((de, d), F32),
                        pltpu.VMEM((d, de), BF16), pltpu.VMEM((d, de), BF16), pltpu.VMEM((de, d), BF16),
                        pltpu.SemaphoreType.DMA((3,))],
    )
    return pl.pallas_call(
        functools.partial(_expert_kernel, layer=layer),
        grid_spec=grid_spec,
        out_shape=jax.ShapeDtypeStruct((p * parts, LANES), F32),
        compiler_params=_params("arbitrary"),
        name="experts",
    )(plan["block_expert"], plan["n_valid"], plan["first"], plan["next_expert"], plan["dest"],
      plan["n_used"], h2_tm, wg, wu, wd)


COMBINE_TILE = 64


def _combine_kernel(pos_ref, y_hbm, x1_ref, ew_ref, gate_ref, gain_ref, o_ref, ybuf0, ybuf1, sems):
    i = pl.program_id(0)
    tm = COMBINE_TILE
    rows = TOP_K * tm
    bufs = (ybuf0, ybuf1)

    parts = x1_ref.shape[1] // LANES

    def gather(tile, k):
        _start_token_gather(pos_ref, tile * rows, rows, parts, y_hbm, bufs[k], sems.at[k])

    def wait(k):
        _wait_token_gather(rows, parts, y_hbm, bufs[k], sems.at[k])

    def finish(k):
        sl = slice(k * tm, (k + 1) * tm)
        ew = ew_ref[sl, :]
        moe = ew[:, 0:1] * _load_token_major(bufs[k], tm, parts)
        for j in range(1, TOP_K):
            moe = moe + ew[:, j:j + 1] * _load_token_major(bufs[k], tm, parts, row0=j * tm)
        x2 = x1_ref[sl, :] + gate_ref[0] * moe
        o_ref[sl, :] = x2 * lax.rsqrt(jnp.mean(x2 * x2, axis=-1, keepdims=True) + EPS) * gain_ref[...]

    @pl.when(i == 0)
    def _():
        gather(0, 0)

    wait(0)
    gather(2 * i + 1, 1)
    finish(0)
    wait(1)
    gather(2 * i + 2, 0)
    finish(1)

    @pl.when(i == pl.num_programs(0) - 1)
    def _():
        wait(0)


def _combine(ys, pos_tiles, x1, ew, gate2, final_gain, seq):
    n, d = x1.shape
    bsz = n // seq
    tm = 2 * COMBINE_TILE
    tpb = seq // tm
    grid_spec = pltpu.PrefetchScalarGridSpec(
        num_scalar_prefetch=1,
        grid=(n // tm,),
        in_specs=[pl.BlockSpec(memory_space=pl.ANY),
                  pl.BlockSpec((tm, d), lambda i, pos: (i, 0)),
                  pl.BlockSpec((tm, LANES), lambda i, pos: (i, 0)),
                  pl.BlockSpec((1, 1, d), lambda i, pos: (i // tpb, 0, 0)),
                  pl.BlockSpec((1, d), lambda i, pos: (0, 0))],
        out_specs=pl.BlockSpec((tm, d), lambda i, pos: (i, 0)),
        scratch_shapes=[pltpu.VMEM((TOP_K * COMBINE_TILE * (d // LANES), LANES), F32)] * 2
        + [pltpu.SemaphoreType.DMA((2,))],
    )
    return pl.pallas_call(
        _combine_kernel,
        grid_spec=grid_spec,
        out_shape=jax.ShapeDtypeStruct((n, d), F32),
        compiler_params=_params("arbitrary"),
        name="combine_norm",
    )(pos_tiles, ys, x1, ew, gate2.reshape(bsz, 1, d), final_gain.reshape(1, d))


def _dispatch_plan(route, counts):
    n = route.shape[0]
    a = n * TOP_K
    eid = route[:, :TOP_K]
    rank = route[:, TOP_K:2 * TOP_K]
    cnt = counts[0, :N_EXPERTS]
    padded = (cnt + MOE_BLOCK - 1) // MOE_BLOCK * MOE_BLOCK
    pad_ends = jnp.cumsum(padded)
    pad_starts = pad_ends - padded
    dest = (pad_starts[eid] + rank).astype(jnp.int32)
    p = ((a + MOE_BLOCK - 1) // MOE_BLOCK + N_EXPERTS) * MOE_BLOCK
    nb = p // MOE_BLOCK
    n_used = (pad_ends[-1] // MOE_BLOCK).astype(jnp.int32)
    blk = jnp.arange(nb, dtype=jnp.int32)
    blk_start = jnp.minimum(blk, n_used - 1) * MOE_BLOCK
    block_expert = jnp.minimum(jnp.sum((pad_ends[None, :] <= blk_start[:, None]).astype(jnp.int32), axis=1),
                               N_EXPERTS - 1).astype(jnp.int32)
    n_valid = jnp.clip(pad_starts[block_expert] + cnt[block_expert] - blk_start, 0, MOE_BLOCK).astype(jnp.int32)
    first = ((blk_start == pad_starts[block_expert]) & (blk < n_used)).astype(jnp.int32)
    nxt_blk = pad_ends[block_expert] // MOE_BLOCK
    next_expert = jnp.where(nxt_blk < n_used, block_expert[jnp.minimum(nxt_blk, nb - 1)], -1).astype(jnp.int32)
    pos_tiles = dest.reshape(n // COMBINE_TILE, COMBINE_TILE, TOP_K).transpose(0, 2, 1).reshape(a)
    pos_tiles = jnp.concatenate([pos_tiles, jnp.zeros((TOP_K * COMBINE_TILE,), jnp.int32)])
    plan = dict(block_expert=block_expert, n_valid=n_valid, first=first, next_expert=next_expert,
                dest=dest.reshape(a), n_used=n_used.reshape(1), rows=p)
    return plan, pos_tiles


def kernel(x, c, w_ada, b_ada, norm1, w_in, hg_lb, hg_norm, rg_conv_w, rg_conv_b, rg_w_a, rg_b_a,
           rg_w_x, rg_b_x, rg_lambda, w_proj_a, w_proj_b, w_out, norm2, w_group, w_router, w_gate,
           w_up, w_down, final_norm):
    bsz, seq, d = x.shape
    depth = w_ada.shape[0]
    n = bsz * seq
    x2 = x.reshape(n, d)
    lb_table = hg_lb.reshape(hg_lb.shape[0], -1)
    for layer in range(depth):
        mod = _adaln(c, w_ada[layer], b_ada[layer])
        shift1, scale1, gate1, shift2, scale2, gate2 = jnp.split(mod, 6, axis=-1)
        logf, acts = _inproj(x2, norm1[layer], scale1, shift1, w_in[layer].astype(BF16), lb_table, layer, seq)
        acts3 = acts.reshape(bsz, seq, acts.shape[1])
        ya = _hgrn(acts3, logf.reshape(bsz, seq, logf.shape[1]), hg_norm[layer])
        yb = _rglru(acts3, rg_conv_w[layer], rg_conv_b[layer], rg_w_a[layer], rg_b_a[layer],
                    rg_w_x[layer], rg_b_x[layer], rg_lambda[layer])
        w_route = jnp.concatenate(
            [w_group[layer], w_router[layer].reshape(d, N_EXPERTS),
             jnp.zeros((d, LANES - N_GROUPS - N_EXPERTS), F32)], axis=1)
        x1, h2, route, ew, counts = _mixout(
            ya.reshape(n, -1), yb.reshape(n, -1), acts, x2, gate1, scale2, shift2, norm2[layer],
            w_proj_a[layer].astype(BF16), w_proj_b[layer].astype(BF16), w_out[layer].astype(BF16), w_route, seq)
        plan, pos_tiles = _dispatch_plan(route, counts)
        ys = _experts(h2, w_gate, w_up, w_down, layer, plan)
        if layer + 1 < depth:
            raise NotImplementedError("the fused combine + final norm assumes a single layer")
        x2 = _combine(ys, pos_tiles, x1, ew, gate2, final_norm, seq)
    return x2.reshape(bsz, seq, d)
```

```python
import functools

import jax
import jax.numpy as jnp
import numpy as np
from jax import lax
from jax.experimental import pallas as pl
from jax.experimental.pallas import tpu as pltpu

EPS = 1e-6
LANES = 128
SUBLANES = 8
HG_HEADS = 8
RG_BLOCKS = 8
RG_CONV = 4
RG_C = 8.0
N_GROUPS = 4
EXPERTS_PER_GROUP = 8
N_EXPERTS = N_GROUPS * EXPERTS_PER_GROUP
TOP_K = 2
HG_CHUNK = 128
MOE_BLOCK = 128
VMEM_LIMIT = 56 * 1024 * 1024
ACT_Q, ACT_V, ACT_G, ACT_RX, ACT_GY = 32, 40, 48, 56, 64

F32 = jnp.float32
BF16 = jnp.bfloat16


def _sigmoid(x):
    return 1.0 / (1.0 + jnp.exp(-x))


def _silu(x):
    return x * _sigmoid(x)


def _gelu_tanh(x):
    return 0.5 * x * (1.0 + jnp.tanh(0.7978845608028654 * (x + 0.044715 * (x * x * x))))


def _params(*sem):
    return pltpu.CompilerParams(dimension_semantics=sem, vmem_limit_bytes=VMEM_LIMIT)


def _adaln_kernel(ct_ref, w_ref, b_ref, o_ref, *, kc):
    kdim, nb = ct_ref.shape
    tn = w_ref.shape[1]

    def body(i, accs):
        k0 = pl.multiple_of(i * kc, kc)
        w = w_ref[pl.ds(k0, kc), :]
        cs = _silu(ct_ref[pl.ds(k0, kc), :])
        return tuple(acc + jnp.sum(w * cs[:, b:b + 1], axis=0, keepdims=True)
                     for b, acc in enumerate(accs))

    accs = lax.fori_loop(0, kdim // kc, body, tuple(jnp.zeros((1, tn), F32) for _ in range(nb)))
    for b, acc in enumerate(accs):
        o_ref[b:b + 1, :] = acc + b_ref[...]


def _adaln(c, w, bias):
    bsz, d = c.shape
    n = w.shape[1]
    tn = 1024
    return pl.pallas_call(
        functools.partial(_adaln_kernel, kc=256),
        grid=(n // tn,),
        in_specs=[pl.BlockSpec((d, bsz), lambda j: (0, 0)),
                  pl.BlockSpec((d, tn), lambda j: (0, j)),
                  pl.BlockSpec((1, tn), lambda j: (0, j))],
        out_specs=pl.BlockSpec((bsz, tn), lambda j: (0, j)),
        out_shape=jax.ShapeDtypeStruct((bsz, n), F32),
        compiler_params=_params("parallel"),
        name="adaln",
    )(c.T, w, bias.reshape(1, n))


def _norm_mod_to_scratch(x_ref, gain_ref, scale_ref, shift_ref, h_scr):
    rows = 64

    def body(i, carry):
        r0 = pl.multiple_of(i * rows, rows)
        x = x_ref[pl.ds(r0, rows), :]
        y = x * lax.rsqrt(jnp.mean(x * x, axis=-1, keepdims=True) + EPS) * gain_ref[...]
        h_scr[pl.ds(r0, rows), :] = (y * (1.0 + scale_ref[0]) + shift_ref[0]).astype(BF16)
        return carry

    lax.fori_loop(0, x_ref.shape[0] // rows, body, 0)


INPROJ_ROWS = 128
INPROJ_COLS = 256


def _inproj_logf_kernel(x_ref, gain_ref, scale_ref, shift_ref, w_ref, lbt_ref, o_ref, h_scr, *, layer):
    @pl.when(pl.program_id(1) == 0)
    def _():
        _norm_mod_to_scratch(x_ref, gain_ref, scale_ref, shift_ref, h_scr)

    t = lbt_ref[...]
    e = jnp.exp(t - jnp.max(t, axis=0, keepdims=True))
    lb = jnp.sum(e[:layer + 1], axis=0, keepdims=True) / jnp.sum(e, axis=0, keepdims=True)
    for r0 in range(0, h_scr.shape[0], INPROJ_ROWS):
        for c0 in range(0, w_ref.shape[1], INPROJ_COLS):
            z = jnp.dot(h_scr[r0:r0 + INPROJ_ROWS, :], w_ref[:, c0:c0 + INPROJ_COLS], preferred_element_type=F32)
            lbc = lb[:, c0:c0 + INPROJ_COLS]
            o_ref[r0:r0 + INPROJ_ROWS, c0:c0 + INPROJ_COLS] = jnp.log(lbc + (1.0 - lbc) * _sigmoid(z))


def _inproj_act_kernel(x_ref, gain_ref, scale_ref, shift_ref, w_ref, o_ref, h_scr):
    j = pl.program_id(1)

    @pl.when(j == 0)
    def _():
        _norm_mod_to_scratch(x_ref, gain_ref, scale_ref, shift_ref, h_scr)

    is_sig = j < 4
    is_gelu = j == 8
    is_plain = (j == 5) | (j == 7)
    a0 = jnp.where(is_sig, 1.0, 0.0)
    a1 = 1.0 - a0
    c = 0.7978845608028654
    b1 = jnp.where(is_gelu, 2.0 * c, 1.0)
    b3 = jnp.where(is_gelu, 2.0 * c * 0.044715, 0.0)
    for r0 in range(0, h_scr.shape[0], INPROJ_ROWS):
        for c0 in range(0, w_ref.shape[1], INPROJ_COLS):
            z = jnp.dot(h_scr[r0:r0 + INPROJ_ROWS, :], w_ref[:, c0:c0 + INPROJ_COLS], preferred_element_type=F32)
            act = (a0 + a1 * z) * _sigmoid(z * (b1 + b3 * (z * z)))
            o_ref[r0:r0 + INPROJ_ROWS, c0:c0 + INPROJ_COLS] = jnp.where(is_plain, z, act).astype(BF16)


def _inproj(x2, gain, scale, shift, w_bf, lb_table, layer, seq):
    n, d = x2.shape
    bsz = n // seq
    tm, tn = 1024, 1024
    tiles_per_batch = seq // tm
    gain2 = gain.reshape(1, d)
    scale3 = scale.reshape(bsz, 1, d)
    shift3 = shift.reshape(bsz, 1, d)
    common = [pl.BlockSpec((tm, d), lambda i, j: (i, 0)),
              pl.BlockSpec((1, d), lambda i, j: (0, 0)),
              pl.BlockSpec((1, 1, d), lambda i, j: (i // tiles_per_batch, 0, 0)),
              pl.BlockSpec((1, 1, d), lambda i, j: (i // tiles_per_batch, 0, 0))]
    n_lb = lb_table.shape[0]
    logf = pl.pallas_call(
        functools.partial(_inproj_logf_kernel, layer=layer),
        grid=(n // tm, 2),
        in_specs=common + [pl.BlockSpec((d, tn), lambda i, j: (0, j + 2)),
                           pl.BlockSpec((n_lb, tn), lambda i, j: (0, j))],
        out_specs=pl.BlockSpec((tm, tn), lambda i, j: (i, j)),
        out_shape=jax.ShapeDtypeStruct((n, 2 * tn), F32),
        scratch_shapes=[pltpu.VMEM((tm, d), BF16)],
        compiler_params=_params("parallel", "arbitrary"),
        name="inproj_logf",
    )(x2, gain2, scale3, shift3, w_bf, lb_table)
    acts = pl.pallas_call(
        _inproj_act_kernel,
        grid=(n // tm, 9),
        in_specs=common + [pl.BlockSpec(
            (d, tn), lambda i, j: (0, jnp.where(j < 4, j + 7, jnp.where(j < 6, j - 4, j - 2))))],
        out_specs=pl.BlockSpec((tm, tn), lambda i, j: (i, j)),
        out_shape=jax.ShapeDtypeStruct((n, 9 * tn), BF16),
        scratch_shapes=[pltpu.VMEM((tm, d), BF16)],
        compiler_params=_params("parallel", "arbitrary"),
        name="inproj_act",
    )(x2, gain2, scale3, shift3, w_bf)
    return logf, acts


def _hgrn_level_halves(chunk):
    return [chunk >> (i + 1) for i in range(chunk.bit_length() - 1)]


def _hgrn_kernel(q_ref, v_ref, g_ref, lff_ref, lfb_ref, gain_ref, o_ref,
                 oacc, qf_scr, kf_scr, qb_scr, kb_scr, *, chunk, group):
    seq = q_ref.shape[1]
    nchunks = seq // chunk
    halves = _hgrn_level_halves(chunk)
    nt = (((1,), (1,)), ((), ()))
    tn = (((0,), (0,)), ((), ()))

    row = lax.broadcasted_iota(jnp.int32, (chunk, LANES), 0)
    ti = lax.broadcasted_iota(jnp.int32, (chunk, chunk), 0)
    si = lax.broadcasted_iota(jnp.int32, (chunk, chunk), 1)
    txs = ti ^ si
    tri_f = jnp.where(si <= ti, 1.0, 0.0).astype(BF16)
    tri_b = jnp.where(si >= ti, 1.0, 0.0).astype(BF16)

    def cumulative(lf, tri):
        hi = lf.astype(BF16)
        mid = (lf - hi.astype(F32)).astype(BF16)
        r = jnp.dot(tri, jnp.concatenate([hi, mid], axis=1), preferred_element_type=F32)
        return r[:, :LANES] + r[:, LANES:]

    lid = functools.reduce(lambda a, b: a + b, [(txs >= h).astype(jnp.int32) for h in halves])

    def boundary_diff(cum, m, reverse):
        pieces = []
        for base in range(0, chunk, 2 * m):
            bnd = base + (m if reverse else m - 1)
            pieces.append(jnp.broadcast_to(cum[bnd:bnd + 1, :], (2 * m, LANES)))
        return cum - jnp.concatenate(pieces, axis=0)

    def intra(c):
        r0 = pl.multiple_of(c * chunk, chunk)
        q = q_ref[0, pl.ds(r0, chunk), :].astype(F32)
        v = v_ref[0, pl.ds(r0, chunk), :]
        lff = lff_ref[0, pl.ds(r0, chunk), :]
        lfb = lfb_ref[0, pl.ds(r0, chunk), :]
        cum_f = cumulative(lff, tri_f)
        cum_b = cumulative(lfb, tri_b)
        ff = jnp.exp(lff)
        fb = jnp.exp(lfb)
        kf = 1.0 - ff
        kb = 1.0 - fb

        p0 = lax.dot_general(q.astype(BF16), (kf + kb).astype(BF16), nt, preferred_element_type=F32)
        scores = jnp.where(lid == 0, p0, 0.0)
        for m in halves:
            later = (row & m) != 0
            if 2 * m >= SUBLANES:
                d_f = boundary_diff(cum_f, m, False)
                d_b = boundary_diff(cum_b, m, True)
                x = q * jnp.exp(jnp.where(later, d_f, d_b))
                y = jnp.where(later, kb, kf) * jnp.exp(-jnp.where(later, d_b, d_f))
            elif m == 2:
                r4 = row & 3
                prev = lambda a: pltpu.roll(a, 1, 0)
                nxt = lambda a: pltpu.roll(a, chunk - 1, 0)
                x = q * jnp.where(r4 == 0, fb * nxt(fb), jnp.where(r4 == 1, fb,
                                  jnp.where(r4 == 2, ff, ff * prev(ff))))
                y = jnp.where(r4 == 0, kf * nxt(ff), jnp.where(r4 == 1, kf,
                              jnp.where(r4 == 2, kb, kb * prev(fb))))
            else:
                x = q * jnp.where(later, ff, fb)
                y = jnp.where(later, kb, kf)
            pm = lax.dot_general(x.astype(BF16), y.astype(BF16), nt, preferred_element_type=F32)
            scores = jnp.where(lid == m.bit_length(), pm, scores)
        oacc[pl.ds(r0, chunk), :] = jnp.dot(scores.astype(BF16), v, preferred_element_type=F32)
        tot_f = cum_f[chunk - 1:chunk, :]
        tot_b = cum_b[0:1, :]
        qf_scr[pl.ds(r0, chunk), :] = (q * jnp.exp(cum_f)).astype(BF16)
        kf_scr[pl.ds(r0, chunk), :] = (kf * jnp.exp(tot_f - cum_f)).astype(BF16)
        qb_scr[pl.ds(r0, chunk), :] = (q * jnp.exp(cum_b)).astype(BF16)
        kb_scr[pl.ds(r0, chunk), :] = (kb * jnp.exp(tot_b - cum_b)).astype(BF16)

    def intra_body(i, carry):
        for u in range(group):
            intra(i * group + u)
        return carry

    lax.fori_loop(0, nchunks // group, intra_body, 0)

    def apply_state(c, st, q_scr, k_scr, lf_ref):
        r0 = pl.multiple_of(c * chunk, chunk)
        o = oacc[pl.ds(r0, chunk), :] + lax.dot_general(
            q_scr[pl.ds(r0, chunk), :], st.astype(BF16), nt, preferred_element_type=F32)
        tot = jnp.sum(lf_ref[0, pl.ds(r0, chunk), :], axis=0, keepdims=True)
        st = st * jnp.exp(tot) + lax.dot_general(
            v_ref[0, pl.ds(r0, chunk), :], k_scr[pl.ds(r0, chunk), :], tn, preferred_element_type=F32)
        return r0, o, st

    def finish(r0, o):
        y = o * lax.rsqrt(jnp.mean(o * o, axis=-1, keepdims=True) + EPS) * gain_ref[0]
        o_ref[0, pl.ds(r0, chunk), :] = (y * g_ref[0, pl.ds(r0, chunk), :].astype(F32)).astype(BF16)

    def sweep(final):
        def body(i, states):
            st_f, st_b = states
            rf, of, st_f = apply_state(i, st_f, qf_scr, kf_scr, lff_ref)
            rb, ob, st_b = apply_state(nchunks - 1 - i, st_b, qb_scr, kb_scr, lfb_ref)
            if final:
                finish(rf, of)
                finish(rb, ob)
            else:
                oacc[pl.ds(rf, chunk), :] = of
                oacc[pl.ds(rb, chunk), :] = ob
            return st_f, st_b
        return body

    zero = jnp.zeros((LANES, LANES), F32)
    states = lax.fori_loop(0, nchunks // 2, sweep(False), (zero, zero), unroll=2)
    lax.fori_loop(nchunks // 2, nchunks, sweep(True), states, unroll=2)


def _hgrn(acts3, logf3, hg_norm):
    bsz, seq, _ = acts3.shape
    col = lambda off: pl.BlockSpec((1, seq, LANES), lambda b, h: (b, 0, off + h))
    return pl.pallas_call(
        functools.partial(_hgrn_kernel, chunk=HG_CHUNK, group=4),
        grid=(bsz, HG_HEADS),
        in_specs=[col(ACT_Q), col(ACT_V), col(ACT_G), col(0), col(HG_HEADS),
                  pl.BlockSpec((1, 1, LANES), lambda b, h: (h, 0, 0))],
        out_specs=pl.BlockSpec((1, seq, LANES), lambda b, h: (b, 0, h)),
        out_shape=jax.ShapeDtypeStruct((bsz, seq, HG_HEADS * LANES), BF16),
        scratch_shapes=[pltpu.VMEM((seq, LANES), F32)] + [pltpu.VMEM((seq, LANES), BF16)] * 4,
        compiler_params=_params("parallel", "parallel"),
        name="hgrn2",
    )(acts3, acts3, acts3, logf3, logf3, hg_norm.reshape(HG_HEADS, 1, LANES))


def _scan_tile(a, u, reverse):
    rows = a.shape[0]
    row = lax.broadcasted_iota(jnp.int32, a.shape, 0)
    d = 1
    while d < rows:
        if reverse:
            ok = row < rows - d
            shift = rows - d
        else:
            ok = row >= d
            shift = d
        a_s = jnp.where(ok, pltpu.roll(a, shift, 0), 1.0)
        u_s = jnp.where(ok, pltpu.roll(u, shift, 0), 0.0)
        u = u + a * u_s
        a = a * a_s
        d *= 2
    return a, u


def _rglru_kernel(rx_ref, gy_ref, cw_ref, cb_ref, wa_ref, ba_ref, wx_ref, bx_ref, lam_ref, o_ref,
                  xpad, hf_scr, ab_scr, ub_scr, *, tile):
    seq = rx_ref.shape[1]
    ntiles = seq // tile
    halo = 8

    xpad[0:halo, :] = jnp.zeros((halo, LANES), F32)
    xpad[halo + seq:halo + seq + halo, :] = jnp.zeros((halo, LANES), F32)
    xpad[halo:halo + seq, :] = rx_ref[0].astype(F32)

    lam = lam_ref[...]
    neg_c_softplus = -RG_C * (jnp.maximum(-lam, 0.0) + jnp.log(1.0 + jnp.exp(-jnp.abs(lam))))

    def gates(xcb, z):
        r = _sigmoid(jnp.dot(xcb, wa_ref[z, 0], preferred_element_type=F32) + ba_ref[z, 0])
        gi = _sigmoid(jnp.dot(xcb, wx_ref[z, 0], preferred_element_type=F32) + bx_ref[z, 0])
        a = jnp.exp(r * neg_c_softplus[z:z + 1, :])
        return a, jnp.sqrt(1.0 - a * a) * gi

    def fwd_body(i, carry):
        t0 = pl.multiple_of(i * tile, tile)
        ext = xpad[pl.ds(t0, tile + 2 * halo), :]
        n_ext = tile + 2 * halo
        tap = lambda off: pltpu.roll(ext, (-off) % n_ext, 0)[halo:halo + tile]
        xc = (cw_ref[0:1, :] * tap(-2) + cw_ref[1:2, :] * tap(-1) + cw_ref[2:3, :] * ext[halo:halo + tile]
              + cw_ref[3:4, :] * tap(1) + cb_ref[...])
        xcb = xc.astype(BF16)
        a_f, s_f = gates(xcb, 0)
        a_b, s_b = gates(xcb, 1)
        ap, hl = _scan_tile(a_f, s_f * xc, reverse=False)
        h = hl + ap * carry
        hf_scr[pl.ds(t0, tile), :] = h
        ab_scr[pl.ds(t0, tile), :] = a_b
        ub_scr[pl.ds(t0, tile), :] = s_b * xc
        return h[tile - 1:tile, :]

    lax.fori_loop(0, ntiles, fwd_body, jnp.zeros((1, LANES), F32))

    def bwd_body(i, carry):
        t0 = pl.multiple_of((ntiles - 1 - i) * tile, tile)
        ap, hl = _scan_tile(ab_scr[pl.ds(t0, tile), :], ub_scr[pl.ds(t0, tile), :], reverse=True)
        h = hl + ap * carry
        o_ref[0, pl.ds(t0, tile), :] = (
            (hf_scr[pl.ds(t0, tile), :] + h) * gy_ref[0, pl.ds(t0, tile), :].astype(F32)).astype(BF16)
        return h[0:1, :]

    lax.fori_loop(0, ntiles, bwd_body, jnp.zeros((1, LANES), F32))


def _rglru(acts3, conv_w, conv_b, w_a, b_a, w_x, b_x, lam):
    bsz, seq, _ = acts3.shape
    width = RG_BLOCKS * LANES
    tile = 256
    rx_off, gy_off = ACT_RX, ACT_GY
    col = lambda off: pl.BlockSpec((1, seq, LANES), lambda b, j: (b, 0, off + j))
    wspec = pl.BlockSpec((2, 1, LANES, LANES), lambda b, j: (0, j, 0, 0))
    bspec = pl.BlockSpec((2, 1, 1, LANES), lambda b, j: (0, j, 0, 0))
    return pl.pallas_call(
        functools.partial(_rglru_kernel, tile=tile),
        grid=(bsz, RG_BLOCKS),
        in_specs=[col(rx_off), col(gy_off),
                  pl.BlockSpec((RG_CONV, LANES), lambda b, j: (0, j)),
                  pl.BlockSpec((1, LANES), lambda b, j: (0, j)),
                  wspec, bspec, wspec, bspec,
                  pl.BlockSpec((2, LANES), lambda b, j: (0, j))],
        out_specs=pl.BlockSpec((1, seq, LANES), lambda b, j: (b, 0, j)),
        out_shape=jax.ShapeDtypeStruct((bsz, seq, width), BF16),
        scratch_shapes=[pltpu.VMEM((seq + 16, LANES), F32),
                        pltpu.VMEM((seq, LANES), F32),
                        pltpu.VMEM((seq, LANES), F32),
                        pltpu.VMEM((seq, LANES), F32)],
        compiler_params=_params("parallel", "parallel"),
        name="rglru",
    )(acts3, acts3, conv_w, conv_b.reshape(1, width), w_a.astype(BF16),
      b_a.reshape(2, RG_BLOCKS, 1, LANES), w_x.astype(BF16), b_x.reshape(2, RG_BLOCKS, 1, LANES), lam)


def _store_token_major(ref, x, row0=0):
    ntok, d = x.shape
    parts = d // LANES
    for c in range(parts):
        ref[pl.ds(row0 * parts + c, ntok, stride=parts), :] = x[:, c * LANES:(c + 1) * LANES]


def _load_token_major(ref, ntok, parts, row0=0):
    return jnp.concatenate([ref[pl.ds(row0 * parts + c, ntok, stride=parts), :] for c in range(parts)], axis=1)


def _mixout_kernel(ya_ref, yb_ref, sga_ref, sgb_ref, x_ref, gate_ref, scale_ref, shift_ref, gain_ref,
                   wpa_ref, wpb_ref, wo_ref, wr_ref, x1_ref, h2_ref, eid_ref, ew_ref, cnt_ref, running):
    y_a = jnp.dot(ya_ref[...], wpa_ref[...], preferred_element_type=F32)
    y_b = jnp.dot(yb_ref[...], wpb_ref[...], preferred_element_type=F32)
    merged = sga_ref[...].astype(F32) * y_a + sgb_ref[...].astype(F32) * y_b
    mix = jnp.dot(merged.astype(BF16), wo_ref[...], preferred_element_type=F32)
    x1 = x_ref[...] + gate_ref[0] * mix
    x1_ref[...] = x1
    y = x1 * lax.rsqrt(jnp.mean(x1 * x1, axis=-1, keepdims=True) + EPS) * gain_ref[...]
    h2 = y * (1.0 + scale_ref[0]) + shift_ref[0]
    _store_token_major(h2_ref, h2)

    h_hi = h2.astype(BF16)
    h_lo = (h2 - h_hi.astype(F32)).astype(BF16)
    p_hi = jnp.dot(h_hi, wr_ref[...], preferred_element_type=F32)
    p_lo = jnp.dot(h_lo, wr_ref[...], preferred_element_type=F32)
    logits = (p_hi[:, :LANES] + p_hi[:, LANES:]) + (p_lo[:, :LANES] + p_lo[:, LANES:])
    lane = lax.broadcasted_iota(jnp.int32, logits.shape, 1)
    neg = -jnp.inf
    gl = jnp.where(lane < N_GROUPS, logits, neg)
    gmax = jnp.max(gl, axis=-1, keepdims=True)
    g_sel = jnp.min(jnp.where(gl == gmax, lane, LANES), axis=-1, keepdims=True)
    p_group = 1.0 / jnp.sum(jnp.exp(gl - gmax), axis=-1, keepdims=True)
    lo = N_GROUPS + g_sel * EXPERTS_PER_GROUP
    el = jnp.where((lane >= lo) & (lane < lo + EXPERTS_PER_GROUP), logits, neg)
    v1 = jnp.max(el, axis=-1, keepdims=True)
    i1 = jnp.min(jnp.where(el == v1, lane, LANES), axis=-1, keepdims=True)
    el2 = jnp.where(lane == i1, neg, el)
    v2 = jnp.max(el2, axis=-1, keepdims=True)
    i2 = jnp.min(jnp.where(el2 == v2, lane, LANES), axis=-1, keepdims=True)
    e21 = jnp.exp(v2 - v1)
    w1 = p_group / (1.0 + e21)
    w2 = p_group * e21 / (1.0 + e21)
    ew_ref[...] = jnp.where(lane == 0, w1, jnp.where(lane == 1, w2, 0.0))

    @pl.when(pl.program_id(0) == 0)
    def _():
        running[...] = jnp.zeros(running.shape, F32)

    e1, e2 = i1 - N_GROUPS, i2 - N_GROUPS
    tm = logits.shape[0]
    before = (lax.broadcasted_iota(jnp.int32, (tm, tm), 0)
              > lax.broadcasted_iota(jnp.int32, (tm, tm), 1))
    before = jnp.where(before, 1.0, 0.0).astype(BF16)
    oh1 = jnp.where(lane == e1, 1.0, 0.0)
    oh2 = jnp.where(lane == e2, 1.0, 0.0)
    pre1 = jnp.dot(before, oh1.astype(BF16), preferred_element_type=F32)
    pre2 = jnp.dot(before, oh2.astype(BF16), preferred_element_type=F32)
    cnt1 = jnp.sum(oh1, axis=0, keepdims=True)
    base = running[...]
    r1 = jnp.sum(oh1 * (pre1 + base), axis=-1, keepdims=True).astype(jnp.int32)
    r2 = jnp.sum(oh2 * (pre2 + base + cnt1), axis=-1, keepdims=True).astype(jnp.int32)
    total = base + cnt1 + jnp.sum(oh2, axis=0, keepdims=True)
    running[...] = total
    cnt_ref[...] = total.astype(jnp.int32)
    eid_ref[...] = jnp.where(lane == 0, e1, jnp.where(lane == 1, e2,
                             jnp.where(lane == 2, r1, jnp.where(lane == 3, r2, 0))))


def _mixout(ya, yb, acts, x2, gate1, scale2, shift2, gain2, wpa, wpb, wo, w_route, seq):
    n, d = x2.shape
    bsz = n // seq
    tm = 256
    tpb = seq // tm
    half = ya.shape[1]
    ga_blk = 0
    per_batch = pl.BlockSpec((1, 1, d), lambda i: (i // tpb, 0, 0))
    const = lambda shape: pl.BlockSpec(shape, lambda i: (0, 0), pipeline_mode=pl.Buffered(1))
    return pl.pallas_call(
        _mixout_kernel,
        grid=(n // tm,),
        in_specs=[pl.BlockSpec((tm, half), lambda i: (i, 0)),
                  pl.BlockSpec((tm, half), lambda i: (i, 0)),
                  pl.BlockSpec((tm, d), lambda i: (i, ga_blk)),
                  pl.BlockSpec((tm, d), lambda i: (i, ga_blk + 1)),
                  pl.BlockSpec((tm, d), lambda i: (i, 0)),
                  per_batch, per_batch, per_batch,
                  const((1, d)), const((half, d)), const((half, d)), const((d, d)), const((d, 2 * LANES))],
        out_specs=[pl.BlockSpec((tm, d), lambda i: (i, 0)),
                   pl.BlockSpec((tm * (d // LANES), LANES), lambda i: (i, 0)),
                   pl.BlockSpec((tm, LANES), lambda i: (i, 0)),
                   pl.BlockSpec((tm, LANES), lambda i: (i, 0)),
                   pl.BlockSpec((1, LANES), lambda i: (0, 0))],
        out_shape=[jax.ShapeDtypeStruct((n, d), F32),
                   jax.ShapeDtypeStruct((n * (d // LANES), LANES), F32),
                   jax.ShapeDtypeStruct((n, LANES), jnp.int32),
                   jax.ShapeDtypeStruct((n, LANES), F32),
                   jax.ShapeDtypeStruct((1, LANES), jnp.int32)],
        scratch_shapes=[pltpu.VMEM((1, LANES), F32)],
        compiler_params=_params("arbitrary"),
        name="mixout_route",
    )(ya, yb, acts, acts, x2, gate1.reshape(bsz, 1, d), scale2.reshape(bsz, 1, d),
      shift2.reshape(bsz, 1, d), gain2.reshape(1, d), wpa, wpb, wo, w_route)


def _start_token_gather(idx_ref, base, ntok, parts, src_hbm, dst, sem):
    for r in range(ntok):
        t0 = pl.multiple_of(idx_ref[base + r] * parts, parts)
        pltpu.make_async_copy(src_hbm.at[pl.ds(t0, parts), :], dst.at[pl.ds(r * parts, parts), :], sem).start()


def _wait_token_gather(ntok, parts, src_hbm, dst, sem):
    pltpu.make_async_copy(src_hbm.at[pl.ds(0, ntok * parts), :], dst.at[pl.ds(0, ntok * parts), :], sem).wait()


def _expert_kernel(be_ref, nv_ref, first_ref, next_ref, dest_ref, nused_ref,
                   h_hbm, wg_hbm, wu_hbm, wd_hbm, o_ref,
                   xbuf, xsems, row_tok, stage_g, stage_u, stage_d, wg_bf, wu_bf, wd_bf, wsems, *, layer):
    b = pl.program_id(0)
    nb = pl.num_programs(0)
    n_used = nused_ref[0]
    unroll = 8
    cast_rows = 256

    def weight_copies(e):
        return (pltpu.make_async_copy(wg_hbm.at[layer, e], stage_g, wsems.at[0]),
                pltpu.make_async_copy(wu_hbm.at[layer, e], stage_u, wsems.at[1]),
                pltpu.make_async_copy(wd_hbm.at[layer, e], stage_d, wsems.at[2]))

    parts = xbuf.shape[1] // MOE_BLOCK

    def gather(blk, slot):
        _start_token_gather(row_tok, blk * MOE_BLOCK, MOE_BLOCK, parts, h_hbm, xbuf.at[slot], xsems.at[slot])

    def wait_gather(slot):
        _wait_token_gather(MOE_BLOCK, parts, h_hbm, xbuf.at[slot], xsems.at[slot])

    @pl.when(b == 0)
    def _():
        for cp in weight_copies(be_ref[0]):
            cp.start()

        def clear(i, carry):
            for u in range(unroll):
                row_tok[i * unroll + u] = 0
            return carry

        lax.fori_loop(0, row_tok.shape[0] // unroll, clear, 0)

        def invert(i, carry):
            for u in range(unroll):
                row_tok[dest_ref[i * unroll + u]] = i * (unroll // TOP_K) + u // TOP_K
            return carry

        lax.fori_loop(0, dest_ref.shape[0] // unroll, invert, 0)
        gather(0, 0)

    @pl.when(b < n_used)
    def _():
        slot = b % 2

        @pl.when(first_ref[b] == 1)
        def _():
            for cp in weight_copies(0):
                cp.wait()
            for stage, dst in ((stage_g, wg_bf), (stage_u, wu_bf), (stage_d, wd_bf)):
                def cast(i, carry, stage=stage, dst=dst):
                    r0 = pl.multiple_of(i * cast_rows, cast_rows)
                    dst[pl.ds(r0, cast_rows), :] = stage[pl.ds(r0, cast_rows), :].astype(BF16)
                    return carry

                lax.fori_loop(0, stage.shape[0] // cast_rows, cast, 0)

            @pl.when(next_ref[b] >= 0)
            def _():
                for cp in weight_copies(next_ref[b]):
                    cp.start()

        wait_gather(slot)
        valid = lax.broadcasted_iota(jnp.int32, (MOE_BLOCK, 1), 0) < nv_ref[b]
        xb = jnp.where(valid, _load_token_major(xbuf.at[slot], MOE_BLOCK, parts), 0.0).astype(BF16)
        gather(b + 1, 1 - slot)
        gate = jnp.dot(xb, wg_bf[...], preferred_element_type=F32)
        up = jnp.dot(xb, wu_bf[...], preferred_element_type=F32)
        hid = (_silu(gate) * up).astype(BF16)
        _store_token_major(o_ref, jnp.dot(hid, wd_bf[...], preferred_element_type=F32))

        @pl.when(b == nb - 1)
        def _():
            wait_gather(1 - slot)

    @pl.when(b >= n_used)
    def _():
        @pl.when(b == n_used)
        def _():
            wait_gather(b % 2)

        o_ref[...] = jnp.zeros(o_ref.shape, o_ref.dtype)


def _experts(h2_tm, wg, wu, wd, layer, plan):
    p = plan["rows"]
    nb = p // MOE_BLOCK
    d, de = wg.shape[-2:]
    parts = d // LANES
    hbm = pl.BlockSpec(memory_space=pl.ANY)
    grid_spec = pltpu.PrefetchScalarGridSpec(
        num_scalar_prefetch=6,
        grid=(nb,),
        in_specs=[hbm, hbm, hbm, hbm],
        out_specs=pl.BlockSpec((MOE_BLOCK * parts, LANES), lambda b, *_: (b, 0)),
        scratch_shapes=[pltpu.VMEM((2, MOE_BLOCK * parts, LANES), F32), pltpu.SemaphoreType.DMA((2,)),
                        pltpu.SMEM((p + MOE_BLOCK,), jnp.int32),
                        pltpu.VMEM((d, de), F32), pltpu.VMEM((d, de), F32), pltpu.VMEM((de, d), F32),
                        pltpu.VMEM((d, de), BF16), pltpu.VMEM((d, de), BF16), pltpu.VMEM((de, d), BF16),
                        pltpu.SemaphoreType.DMA((3,))],
    )
    return pl.pallas_call(
        functools.partial(_expert_kernel, layer=layer),
        grid_spec=grid_spec,
        out_shape=jax.ShapeDtypeStruct((p * parts, LANES), F32),
        compiler_params=_params("arbitrary"),
        name="experts",
    )(plan["block_expert"], plan["n_valid"], plan["first"], plan["next_expert"], plan["dest"],
      plan["n_used"], h2_tm, wg, wu, wd)


COMBINE_TILE = 64


def _combine_kernel(pos_ref, y_hbm, x1_ref, ew_ref, gate_ref, gain_ref, o_ref, ybuf0, ybuf1, sems):
    i = pl.program_id(0)
    tm = COMBINE_TILE
    rows = TOP_K * tm
    bufs = (ybuf0, ybuf1)

    parts = x1_ref.shape[1] // LANES

    def gather(tile, k):
        _start_token_gather(pos_ref, tile * rows, rows, parts, y_hbm, bufs[k], sems.at[k])

    def wait(k):
        _wait_token_gather(rows, parts, y_hbm, bufs[k], sems.at[k])

    def finish(k):
        sl = slice(k * tm, (k + 1) * tm)
        ew = ew_ref[sl, :]
        moe = ew[:, 0:1] * _load_token_major(bufs[k], tm, parts)
        for j in range(1, TOP_K):
            moe = moe + ew[:, j:j + 1] * _load_token_major(bufs[k], tm, parts, row0=j * tm)
        x2 = x1_ref[sl, :] + gate_ref[0] * moe
        o_ref[sl, :] = x2 * lax.rsqrt(jnp.mean(x2 * x2, axis=-1, keepdims=True) + EPS) * gain_ref[...]

    @pl.when(i == 0)
    def _():
        gather(0, 0)

    wait(0)
    gather(2 * i + 1, 1)
    finish(0)
    wait(1)
    gather(2 * i + 2, 0)
    finish(1)

    @pl.when(i == pl.num_programs(0) - 1)
    def _():
        wait(0)


def _combine(ys, pos_tiles, x1, ew, gate2, final_gain, seq):
    n, d = x1.shape
    bsz = n // seq
    tm = 2 * COMBINE_TILE
    tpb = seq // tm
    grid_spec = pltpu.PrefetchScalarGridSpec(
        num_scalar_prefetch=1,
        grid=(n // tm,),
        in_specs=[pl.BlockSpec(memory_space=pl.ANY),
                  pl.BlockSpec((tm, d), lambda i, pos: (i, 0)),
                  pl.BlockSpec((tm, LANES), lambda i, pos: (i, 0)),
                  pl.BlockSpec((1, 1, d), lambda i, pos: (i // tpb, 0, 0)),
                  pl.BlockSpec((1, d), lambda i, pos: (0, 0))],
        out_specs=pl.BlockSpec((tm, d), lambda i, pos: (i, 0)),
        scratch_shapes=[pltpu.VMEM((TOP_K * COMBINE_TILE * (d // LANES), LANES), F32)] * 2
        + [pltpu.SemaphoreType.DMA((2,))],
    )
    return pl.pallas_call(
        _combine_kernel,
        grid_spec=grid_spec,
        out_shape=jax.ShapeDtypeStruct((n, d), F32),
        compiler_params=_params("arbitrary"),
        name="combine_norm",
    )(pos_tiles, ys, x1, ew, gate2.reshape(bsz, 1, d), final_gain.reshape(1, d))


def _dispatch_plan(route, counts):
    n = route.shape[0]
    a = n * TOP_K
    eid = route[:, :TOP_K]
    rank = route[:, TOP_K:2 * TOP_K]
    cnt = counts[0, :N_EXPERTS]
    padded = (cnt + MOE_BLOCK - 1) // MOE_BLOCK * MOE_BLOCK
    pad_ends = jnp.cumsum(padded)
    pad_starts = pad_ends - padded
    dest = (pad_starts[eid] + rank).astype(jnp.int32)
    p = ((a + MOE_BLOCK - 1) // MOE_BLOCK + N_EXPERTS) * MOE_BLOCK
    nb = p // MOE_BLOCK
    n_used = (pad_ends[-1] // MOE_BLOCK).astype(jnp.int32)
    blk = jnp.arange(nb, dtype=jnp.int32)
    blk_start = jnp.minimum(blk, n_used - 1) * MOE_BLOCK
    block_expert = jnp.minimum(jnp.sum((pad_ends[None, :] <= blk_start[:, None]).astype(jnp.int32), axis=1),
                               N_EXPERTS - 1).astype(jnp.int32)
    n_valid = jnp.clip(pad_starts[block_expert] + cnt[block_expert] - blk_start, 0, MOE_BLOCK).astype(jnp.int32)
    first = ((blk_start == pad_starts[block_expert]) & (blk < n_used)).astype(jnp.int32)
    nxt_blk = pad_ends[block_expert] // MOE_BLOCK
    next_expert = jnp.where(nxt_blk < n_used, block_expert[jnp.minimum(nxt_blk, nb - 1)], -1).astype(jnp.int32)
    pos_tiles = dest.reshape(n // COMBINE_TILE, COMBINE_TILE, TOP_K).transpose(0, 2, 1).reshape(a)
    pos_tiles = jnp.concatenate([pos_tiles, jnp.zeros((TOP_K * COMBINE_TILE,), jnp.int32)])
    plan = dict(block_expert=block_expert, n_valid=n_valid, first=first, next_expert=next_expert,
                dest=dest.reshape(a), n_used=n_used.reshape(1), rows=p)
    return plan, pos_tiles


def kernel(x, c, w_ada, b_ada, norm1, w_in, hg_lb, hg_norm, rg_conv_w, rg_conv_b, rg_w_a, rg_b_a,
           rg_w_x, rg_b_x, rg_lambda, w_proj_a, w_proj_b, w_out, norm2, w_group, w_router, w_gate,
           w_up, w_down, final_norm):
    bsz, seq, d = x.shape
    depth = w_ada.shape[0]
    n = bsz * seq
    x2 = x.reshape(n, d)
    lb_table = hg_lb.reshape(hg_lb.shape[0], -1)
    for layer in range(depth):
        mod = _adaln(c, w_ada[layer], b_ada[layer])
        shift1, scale1, gate1, shift2, scale2, gate2 = jnp.split(mod, 6, axis=-1)
        logf, acts = _inproj(x2, norm1[layer], scale1, shift1, w_in[layer].astype(BF16), lb_table, layer, seq)
        acts3 = acts.reshape(bsz, seq, acts.shape[1])
        ya = _hgrn(acts3, logf.reshape(bsz, seq, logf.shape[1]), hg_norm[layer])
        yb = _rglru(acts3, rg_conv_w[layer], rg_conv_b[layer], rg_w_a[layer], rg_b_a[layer],
                    rg_w_x[layer], rg_b_x[layer], rg_lambda[layer])
        w_route = jnp.concatenate(
            [w_group[layer], w_router[layer].reshape(d, N_EXPERTS),
             jnp.zeros((d, LANES - N_GROUPS - N_EXPERTS), F32)], axis=1)
        w_route_hi = w_route.astype(BF16)
        w_route = jnp.concatenate([w_route_hi, (w_route - w_route_hi.astype(F32)).astype(BF16)], axis=1)
        x1, h2, route, ew, counts = _mixout(
            ya.reshape(n, -1), yb.reshape(n, -1), acts, x2, gate1, scale2, shift2, norm2[layer],
            w_proj_a[layer].astype(BF16), w_proj_b[layer].astype(BF16), w_out[layer].astype(BF16), w_route, seq)
        plan, pos_tiles = _dispatch_plan(route, counts)
        ys = _experts(h2, w_gate, w_up, w_down, layer, plan)
        if layer + 1 < depth:
            raise NotImplementedError("the fused combine + final norm assumes a single layer")
        x2 = _combine(ys, pos_tiles, x1, ew, gate2, final_norm, seq)
    return x2.reshape(bsz, seq, d)
```

```python
import functools

import jax
import jax.numpy as jnp
import numpy as np
from jax import lax
from jax.experimental import pallas as pl
from jax.experimental.pallas import tpu as pltpu

EPS = 1e-6
LANES = 128
SUBLANES = 8
HG_HEADS = 8
RG_BLOCKS = 8
RG_CONV = 4
RG_C = 8.0
N_GROUPS = 4
EXPERTS_PER_GROUP = 8
N_EXPERTS = N_GROUPS * EXPERTS_PER_GROUP
TOP_K = 2
HG_CHUNK = 128
MOE_BLOCK = 128
VMEM_LIMIT = 56 * 1024 * 1024
ACT_Q, ACT_V, ACT_G, ACT_RX, ACT_GY = 32, 40, 48, 56, 64

F32 = jnp.float32
BF16 = jnp.bfloat16


def _sigmoid(x):
    return 1.0 / (1.0 + jnp.exp(-x))


def _silu(x):
    return x * _sigmoid(x)


def _gelu_tanh(x):
    return 0.5 * x * (1.0 + jnp.tanh(0.7978845608028654 * (x + 0.044715 * (x * x * x))))


def _params(*sem):
    return pltpu.CompilerParams(dimension_semantics=sem, vmem_limit_bytes=VMEM_LIMIT)


def _adaln_kernel(ct_ref, w_ref, b_ref, o_ref, *, kc):
    kdim, nb = ct_ref.shape
    tn = w_ref.shape[1]

    def body(i, accs):
        k0 = pl.multiple_of(i * kc, kc)
        w = w_ref[pl.ds(k0, kc), :]
        cs = _silu(ct_ref[pl.ds(k0, kc), :])
        return tuple(acc + jnp.sum(w * cs[:, b:b + 1], axis=0, keepdims=True)
                     for b, acc in enumerate(accs))

    accs = lax.fori_loop(0, kdim // kc, body, tuple(jnp.zeros((1, tn), F32) for _ in range(nb)))
    for b, acc in enumerate(accs):
        o_ref[b:b + 1, :] = acc + b_ref[...]


def _adaln(c, w, bias):
    bsz, d = c.shape
    n = w.shape[1]
    tn = 1024
    return pl.pallas_call(
        functools.partial(_adaln_kernel, kc=256),
        grid=(n // tn,),
        in_specs=[pl.BlockSpec((d, bsz), lambda j: (0, 0)),
                  pl.BlockSpec((d, tn), lambda j: (0, j)),
                  pl.BlockSpec((1, tn), lambda j: (0, j))],
        out_specs=pl.BlockSpec((bsz, tn), lambda j: (0, j)),
        out_shape=jax.ShapeDtypeStruct((bsz, n), F32),
        compiler_params=_params("parallel"),
        name="adaln",
    )(c.T, w, bias.reshape(1, n))


def _norm_mod_to_scratch(x_ref, gain_ref, scale_ref, shift_ref, h_scr):
    rows = 64

    def body(i, carry):
        r0 = pl.multiple_of(i * rows, rows)
        x = x_ref[pl.ds(r0, rows), :]
        y = x * lax.rsqrt(jnp.mean(x * x, axis=-1, keepdims=True) + EPS) * gain_ref[...]
        h_scr[pl.ds(r0, rows), :] = (y * (1.0 + scale_ref[0]) + shift_ref[0]).astype(BF16)
        return carry

    lax.fori_loop(0, x_ref.shape[0] // rows, body, 0)


INPROJ_ROWS = 128
INPROJ_COLS = 256


def _inproj_logf_kernel(x_ref, gain_ref, scale_ref, shift_ref, w_ref, lbt_ref, o_ref, h_scr, *, layer):
    @pl.when(pl.program_id(1) == 0)
    def _():
        _norm_mod_to_scratch(x_ref, gain_ref, scale_ref, shift_ref, h_scr)

    t = lbt_ref[...]
    e = jnp.exp(t - jnp.max(t, axis=0, keepdims=True))
    lb = jnp.sum(e[:layer + 1], axis=0, keepdims=True) / jnp.sum(e, axis=0, keepdims=True)
    for r0 in range(0, h_scr.shape[0], INPROJ_ROWS):
        for c0 in range(0, w_ref.shape[1], INPROJ_COLS):
            z = jnp.dot(h_scr[r0:r0 + INPROJ_ROWS, :], w_ref[:, c0:c0 + INPROJ_COLS], preferred_element_type=F32)
            lbc = lb[:, c0:c0 + INPROJ_COLS]
            o_ref[r0:r0 + INPROJ_ROWS, c0:c0 + INPROJ_COLS] = jnp.log(lbc + (1.0 - lbc) * _sigmoid(z))


def _inproj_act_kernel(x_ref, gain_ref, scale_ref, shift_ref, w_ref, o_ref, h_scr):
    j = pl.program_id(1)

    @pl.when(j == 0)
    def _():
        _norm_mod_to_scratch(x_ref, gain_ref, scale_ref, shift_ref, h_scr)

    z = jnp.dot(h_scr[...], w_ref[...], preferred_element_type=F32)

    @pl.when(j < 4)
    def _():
        o_ref[...] = _sigmoid(z).astype(BF16)

    @pl.when((j == 4) | (j == 6))
    def _():
        o_ref[...] = _silu(z).astype(BF16)

    @pl.when((j == 5) | (j == 7))
    def _():
        o_ref[...] = z.astype(BF16)

    @pl.when(j == 8)
    def _():
        o_ref[...] = _gelu_tanh(z).astype(BF16)


def _inproj(x2, gain, scale, shift, w_bf, lb_table, layer, seq):
    n, d = x2.shape
    bsz = n // seq
    tn = 1024
    gain2 = gain.reshape(1, d)
    scale3 = scale.reshape(bsz, 1, d)
    shift3 = shift.reshape(bsz, 1, d)

    def common(tm):
        per_batch = pl.BlockSpec((1, 1, d), lambda i, j: (i // (seq // tm), 0, 0))
        return [pl.BlockSpec((tm, d), lambda i, j: (i, 0)), pl.BlockSpec((1, d), lambda i, j: (0, 0)),
                per_batch, per_batch]

    n_lb = lb_table.shape[0]
    tm = 1024
    logf = pl.pallas_call(
        functools.partial(_inproj_logf_kernel, layer=layer),
        grid=(n // tm, 2),
        in_specs=common(tm) + [pl.BlockSpec((d, tn), lambda i, j: (0, j + 2)),
                               pl.BlockSpec((n_lb, tn), lambda i, j: (0, j))],
        out_specs=pl.BlockSpec((tm, tn), lambda i, j: (i, j)),
        out_shape=jax.ShapeDtypeStruct((n, 2 * tn), F32),
        scratch_shapes=[pltpu.VMEM((tm, d), BF16)],
        compiler_params=_params("parallel", "arbitrary"),
        name="inproj_logf",
    )(x2, gain2, scale3, shift3, w_bf, lb_table)
    tm = 512
    acts = pl.pallas_call(
        _inproj_act_kernel,
        grid=(n // tm, 9),
        in_specs=common(tm) + [pl.BlockSpec(
            (d, tn), lambda i, j: (0, jnp.where(j < 4, j + 7, jnp.where(j < 6, j - 4, j - 2))))],
        out_specs=pl.BlockSpec((tm, tn), lambda i, j: (i, j)),
        out_shape=jax.ShapeDtypeStruct((n, 9 * tn), BF16),
        scratch_shapes=[pltpu.VMEM((tm, d), BF16)],
        compiler_params=_params("parallel", "arbitrary"),
        name="inproj_act",
    )(x2, gain2, scale3, shift3, w_bf)
    return logf, acts


def _hgrn_level_halves(chunk):
    return [chunk >> (i + 1) for i in range(chunk.bit_length() - 1)]


def _hgrn_kernel(q_ref, v_ref, g_ref, lff_ref, lfb_ref, gain_ref, o_ref,
                 oacc, qf_scr, kf_scr, qb_scr, kb_scr, *, chunk, group):
    seq = q_ref.shape[1]
    nchunks = seq // chunk
    halves = _hgrn_level_halves(chunk)
    nt = (((1,), (1,)), ((), ()))
    tn = (((0,), (0,)), ((), ()))

    row = lax.broadcasted_iota(jnp.int32, (chunk, LANES), 0)
    ti = lax.broadcasted_iota(jnp.int32, (chunk, chunk), 0)
    si = lax.broadcasted_iota(jnp.int32, (chunk, chunk), 1)
    txs = ti ^ si
    tri_f = jnp.where(si <= ti, 1.0, 0.0).astype(BF16)
    tri_b = jnp.where(si >= ti, 1.0, 0.0).astype(BF16)

    def cumulative(lf, tri):
        hi = lf.astype(BF16)
        mid = (lf - hi.astype(F32)).astype(BF16)
        r = jnp.dot(tri, jnp.concatenate([hi, mid], axis=1), preferred_element_type=F32)
        return r[:, :LANES] + r[:, LANES:]

    lid = functools.reduce(lambda a, b: a + b, [(txs >= h).astype(jnp.int32) for h in halves])

    def boundary_diff(cum, m, reverse):
        pieces = []
        for base in range(0, chunk, 2 * m):
            bnd = base + (m if reverse else m - 1)
            pieces.append(jnp.broadcast_to(cum[bnd:bnd + 1, :], (2 * m, LANES)))
        return cum - jnp.concatenate(pieces, axis=0)

    def intra(c):
        r0 = pl.multiple_of(c * chunk, chunk)
        q = q_ref[0, pl.ds(r0, chunk), :].astype(F32)
        v = v_ref[0, pl.ds(r0, chunk), :]
        lff = lff_ref[0, pl.ds(r0, chunk), :]
        lfb = lfb_ref[0, pl.ds(r0, chunk), :]
        cum_f = cumulative(lff, tri_f)
        cum_b = cumulative(lfb, tri_b)
        ff = jnp.exp(lff)
        fb = jnp.exp(lfb)
        kf = 1.0 - ff
        kb = 1.0 - fb

        p0 = lax.dot_general(q.astype(BF16), (kf + kb).astype(BF16), nt, preferred_element_type=F32)
        scores = jnp.where(lid == 0, p0, 0.0)
        for m in halves:
            later = (row & m) != 0
            if 2 * m >= SUBLANES:
                d_f = boundary_diff(cum_f, m, False)
                d_b = boundary_diff(cum_b, m, True)
                x = q * jnp.exp(jnp.where(later, d_f, d_b))
                y = jnp.where(later, kb, kf) * jnp.exp(-jnp.where(later, d_b, d_f))
            elif m == 2:
                r4 = row & 3
                prev = lambda a: pltpu.roll(a, 1, 0)
                nxt = lambda a: pltpu.roll(a, chunk - 1, 0)
                x = q * jnp.where(r4 == 0, fb * nxt(fb), jnp.where(r4 == 1, fb,
                                  jnp.where(r4 == 2, ff, ff * prev(ff))))
                y = jnp.where(r4 == 0, kf * nxt(ff), jnp.where(r4 == 1, kf,
                              jnp.where(r4 == 2, kb, kb * prev(fb))))
            else:
                x = q * jnp.where(later, ff, fb)
                y = jnp.where(later, kb, kf)
            pm = lax.dot_general(x.astype(BF16), y.astype(BF16), nt, preferred_element_type=F32)
            scores = jnp.where(lid == m.bit_length(), pm, scores)
        oacc[pl.ds(r0, chunk), :] = jnp.dot(scores.astype(BF16), v, preferred_element_type=F32)
        tot_f = cum_f[chunk - 1:chunk, :]
        tot_b = cum_b[0:1, :]
        qf_scr[pl.ds(r0, chunk), :] = (q * jnp.exp(cum_f)).astype(BF16)
        kf_scr[pl.ds(r0, chunk), :] = (kf * jnp.exp(tot_f - cum_f)).astype(BF16)
        qb_scr[pl.ds(r0, chunk), :] = (q * jnp.exp(cum_b)).astype(BF16)
        kb_scr[pl.ds(r0, chunk), :] = (kb * jnp.exp(tot_b - cum_b)).astype(BF16)

    def intra_body(i, carry):
        for u in range(group):
            intra(i * group + u)
        return carry

    lax.fori_loop(0, nchunks // group, intra_body, 0)

    def apply_state(c, st, q_scr, k_scr, lf_ref):
        r0 = pl.multiple_of(c * chunk, chunk)
        o = oacc[pl.ds(r0, chunk), :] + lax.dot_general(
            q_scr[pl.ds(r0, chunk), :], st.astype(BF16), nt, preferred_element_type=F32)
        tot = jnp.sum(lf_ref[0, pl.ds(r0, chunk), :], axis=0, keepdims=True)
        st = st * jnp.exp(tot) + lax.dot_general(
            v_ref[0, pl.ds(r0, chunk), :], k_scr[pl.ds(r0, chunk), :], tn, preferred_element_type=F32)
        return r0, o, st

    def finish(r0, o):
        y = o * lax.rsqrt(jnp.mean(o * o, axis=-1, keepdims=True) + EPS) * gain_ref[0]
        o_ref[0, pl.ds(r0, chunk), :] = (y * g_ref[0, pl.ds(r0, chunk), :].astype(F32)).astype(BF16)

    def sweep(final):
        def body(i, states):
            st_f, st_b = states
            rf, of, st_f = apply_state(i, st_f, qf_scr, kf_scr, lff_ref)
            rb, ob, st_b = apply_state(nchunks - 1 - i, st_b, qb_scr, kb_scr, lfb_ref)
            if final:
                finish(rf, of)
                finish(rb, ob)
            else:
                oacc[pl.ds(rf, chunk), :] = of
                oacc[pl.ds(rb, chunk), :] = ob
            return st_f, st_b
        return body

    zero = jnp.zeros((LANES, LANES), F32)
    states = lax.fori_loop(0, nchunks // 2, sweep(False), (zero, zero), unroll=2)
    lax.fori_loop(nchunks // 2, nchunks, sweep(True), states, unroll=2)


def _hgrn(acts3, logf3, hg_norm):
    bsz, seq, _ = acts3.shape
    col = lambda off: pl.BlockSpec((1, seq, LANES), lambda b, h: (b, 0, off + h))
    return pl.pallas_call(
        functools.partial(_hgrn_kernel, chunk=HG_CHUNK, group=4),
        grid=(bsz, HG_HEADS),
        in_specs=[col(ACT_Q), col(ACT_V), col(ACT_G), col(0), col(HG_HEADS),
                  pl.BlockSpec((1, 1, LANES), lambda b, h: (h, 0, 0))],
        out_specs=pl.BlockSpec((1, seq, LANES), lambda b, h: (b, 0, h)),
        out_shape=jax.ShapeDtypeStruct((bsz, seq, HG_HEADS * LANES), BF16),
        scratch_shapes=[pltpu.VMEM((seq, LANES), F32)] + [pltpu.VMEM((seq, LANES), BF16)] * 4,
        compiler_params=_params("parallel", "parallel"),
        name="hgrn2",
    )(acts3, acts3, acts3, logf3, logf3, hg_norm.reshape(HG_HEADS, 1, LANES))


def _scan_tile(a, u, reverse):
    rows = a.shape[0]
    row = lax.broadcasted_iota(jnp.int32, a.shape, 0)
    d = 1
    while d < rows:
        if reverse:
            ok = row < rows - d
            shift = rows - d
        else:
            ok = row >= d
            shift = d
        a_s = jnp.where(ok, pltpu.roll(a, shift, 0), 1.0)
        u_s = jnp.where(ok, pltpu.roll(u, shift, 0), 0.0)
        u = u + a * u_s
        a = a * a_s
        d *= 2
    return a, u


def _rglru_kernel(rx_ref, gy_ref, cw_ref, cb_ref, wa_ref, ba_ref, wx_ref, bx_ref, lam_ref, o_ref,
                  xpad, hf_scr, ab_scr, ub_scr, *, tile):
    seq = rx_ref.shape[1]
    ntiles = seq // tile
    halo = 8

    xpad[0:halo, :] = jnp.zeros((halo, LANES), F32)
    xpad[halo + seq:halo + seq + halo, :] = jnp.zeros((halo, LANES), F32)
    xpad[halo:halo + seq, :] = rx_ref[0].astype(F32)

    lam = lam_ref[...]
    neg_c_softplus = -RG_C * (jnp.maximum(-lam, 0.0) + jnp.log(1.0 + jnp.exp(-jnp.abs(lam))))

    def gates(xcb, z):
        r = _sigmoid(jnp.dot(xcb, wa_ref[z, 0], preferred_element_type=F32) + ba_ref[z, 0])
        gi = _sigmoid(jnp.dot(xcb, wx_ref[z, 0], preferred_element_type=F32) + bx_ref[z, 0])
        a = jnp.exp(r * neg_c_softplus[z:z + 1, :])
        return a, jnp.sqrt(1.0 - a * a) * gi

    def fwd_body(i, carry):
        t0 = pl.multiple_of(i * tile, tile)
        ext = xpad[pl.ds(t0, tile + 2 * halo), :]
        n_ext = tile + 2 * halo
        tap = lambda off: pltpu.roll(ext, (-off) % n_ext, 0)[halo:halo + tile]
        xc = (cw_ref[0:1, :] * tap(-2) + cw_ref[1:2, :] * tap(-1) + cw_ref[2:3, :] * ext[halo:halo + tile]
              + cw_ref[3:4, :] * tap(1) + cb_ref[...])
        xcb = xc.astype(BF16)
        a_f, s_f = gates(xcb, 0)
        a_b, s_b = gates(xcb, 1)
        ap, hl = _scan_tile(a_f, s_f * xc, reverse=False)
        h = hl + ap * carry
        hf_scr[pl.ds(t0, tile), :] = h
        ab_scr[pl.ds(t0, tile), :] = a_b
        ub_scr[pl.ds(t0, tile), :] = s_b * xc
        return h[tile - 1:tile, :]

    lax.fori_loop(0, ntiles, fwd_body, jnp.zeros((1, LANES), F32))

    def bwd_body(i, carry):
        t0 = pl.multiple_of((ntiles - 1 - i) * tile, tile)
        ap, hl = _scan_tile(ab_scr[pl.ds(t0, tile), :], ub_scr[pl.ds(t0, tile), :], reverse=True)
        h = hl + ap * carry
        o_ref[0, pl.ds(t0, tile), :] = (
            (hf_scr[pl.ds(t0, tile), :] + h) * gy_ref[0, pl.ds(t0, tile), :].astype(F32)).astype(BF16)
        return h[0:1, :]

    lax.fori_loop(0, ntiles, bwd_body, jnp.zeros((1, LANES), F32))


def _rglru(acts3, conv_w, conv_b, w_a, b_a, w_x, b_x, lam):
    bsz, seq, _ = acts3.shape
    width = RG_BLOCKS * LANES
    tile = 256
    rx_off, gy_off = ACT_RX, ACT_GY
    col = lambda off: pl.BlockSpec((1, seq, LANES), lambda b, j: (b, 0, off + j))
    wspec = pl.BlockSpec((2, 1, LANES, LANES), lambda b, j: (0, j, 0, 0))
    bspec = pl.BlockSpec((2, 1, 1, LANES), lambda b, j: (0, j, 0, 0))
    return pl.pallas_call(
        functools.partial(_rglru_kernel, tile=tile),
        grid=(bsz, RG_BLOCKS),
        in_specs=[col(rx_off), col(gy_off),
                  pl.BlockSpec((RG_CONV, LANES), lambda b, j: (0, j)),
                  pl.BlockSpec((1, LANES), lambda b, j: (0, j)),
                  wspec, bspec, wspec, bspec,
                  pl.BlockSpec((2, LANES), lambda b, j: (0, j))],
        out_specs=pl.BlockSpec((1, seq, LANES), lambda b, j: (b, 0, j)),
        out_shape=jax.ShapeDtypeStruct((bsz, seq, width), BF16),
        scratch_shapes=[pltpu.VMEM((seq + 16, LANES), F32),
                        pltpu.VMEM((seq, LANES), F32),
                        pltpu.VMEM((seq, LANES), F32),
                        pltpu.VMEM((seq, LANES), F32)],
        compiler_params=_params("parallel", "parallel"),
        name="rglru",
    )(acts3, acts3, conv_w, conv_b.reshape(1, width), w_a.astype(BF16),
      b_a.reshape(2, RG_BLOCKS, 1, LANES), w_x.astype(BF16), b_x.reshape(2, RG_BLOCKS, 1, LANES), lam)


def _store_token_major(ref, x, row0=0):
    ntok, d = x.shape
    parts = d // LANES
    for c in range(parts):
        ref[pl.ds(row0 * parts + c, ntok, stride=parts), :] = x[:, c * LANES:(c + 1) * LANES]


def _load_token_major(ref, ntok, parts, row0=0):
    return jnp.concatenate([ref[pl.ds(row0 * parts + c, ntok, stride=parts), :] for c in range(parts)], axis=1)


def _mixout_kernel(ya_ref, yb_ref, sga_ref, sgb_ref, x_ref, gate_ref, scale_ref, shift_ref, gain_ref,
                   wpa_ref, wpb_ref, wo_ref, wr_ref, x1_ref, h2_ref, eid_ref, ew_ref, cnt_ref, running):
    y_a = jnp.dot(ya_ref[...], wpa_ref[...], preferred_element_type=F32)
    y_b = jnp.dot(yb_ref[...], wpb_ref[...], preferred_element_type=F32)
    merged = sga_ref[...].astype(F32) * y_a + sgb_ref[...].astype(F32) * y_b
    mix = jnp.dot(merged.astype(BF16), wo_ref[...], preferred_element_type=F32)
    x1 = x_ref[...] + gate_ref[0] * mix
    x1_ref[...] = x1
    y = x1 * lax.rsqrt(jnp.mean(x1 * x1, axis=-1, keepdims=True) + EPS) * gain_ref[...]
    h2 = y * (1.0 + scale_ref[0]) + shift_ref[0]
    _store_token_major(h2_ref, h2)

    h_hi = h2.astype(BF16)
    h_lo = (h2 - h_hi.astype(F32)).astype(BF16)
    p_hi = jnp.dot(h_hi, wr_ref[...], preferred_element_type=F32)
    p_lo = jnp.dot(h_lo, wr_ref[...], preferred_element_type=F32)
    logits = (p_hi[:, :LANES] + p_hi[:, LANES:]) + (p_lo[:, :LANES] + p_lo[:, LANES:])
    lane = lax.broadcasted_iota(jnp.int32, logits.shape, 1)
    neg = -jnp.inf
    gl = jnp.where(lane < N_GROUPS, logits, neg)
    gmax = jnp.max(gl, axis=-1, keepdims=True)
    g_sel = jnp.min(jnp.where(gl == gmax, lane, LANES), axis=-1, keepdims=True)
    p_group = 1.0 / jnp.sum(jnp.exp(gl - gmax), axis=-1, keepdims=True)
    lo = N_GROUPS + g_sel * EXPERTS_PER_GROUP
    el = jnp.where((lane >= lo) & (lane < lo + EXPERTS_PER_GROUP), logits, neg)
    v1 = jnp.max(el, axis=-1, keepdims=True)
    i1 = jnp.min(jnp.where(el == v1, lane, LANES), axis=-1, keepdims=True)
    el2 = jnp.where(lane == i1, neg, el)
    v2 = jnp.max(el2, axis=-1, keepdims=True)
    i2 = jnp.min(jnp.where(el2 == v2, lane, LANES), axis=-1, keepdims=True)
    e21 = jnp.exp(v2 - v1)
    w1 = p_group / (1.0 + e21)
    w2 = p_group * e21 / (1.0 + e21)
    ew_ref[...] = jnp.where(lane == 0, w1, jnp.where(lane == 1, w2, 0.0))

    @pl.when(pl.program_id(0) == 0)
    def _():
        running[...] = jnp.zeros(running.shape, F32)

    e1, e2 = i1 - N_GROUPS, i2 - N_GROUPS
    tm = logits.shape[0]
    before = (lax.broadcasted_iota(jnp.int32, (tm, tm), 0)
              > lax.broadcasted_iota(jnp.int32, (tm, tm), 1))
    before = jnp.where(before, 1.0, 0.0).astype(BF16)
    oh1 = jnp.where(lane == e1, 1.0, 0.0)
    oh2 = jnp.where(lane == e2, 1.0, 0.0)
    pre1 = jnp.dot(before, oh1.astype(BF16), preferred_element_type=F32)
    pre2 = jnp.dot(before, oh2.astype(BF16), preferred_element_type=F32)
    cnt1 = jnp.sum(oh1, axis=0, keepdims=True)
    base = running[...]
    r1 = jnp.sum(oh1 * (pre1 + base), axis=-1, keepdims=True).astype(jnp.int32)
    r2 = jnp.sum(oh2 * (pre2 + base + cnt1), axis=-1, keepdims=True).astype(jnp.int32)
    total = base + cnt1 + jnp.sum(oh2, axis=0, keepdims=True)
    running[...] = total
    cnt_ref[...] = total.astype(jnp.int32)
    eid_ref[...] = jnp.where(lane == 0, e1, jnp.where(lane == 1, e2,
                             jnp.where(lane == 2, r1, jnp.where(lane == 3, r2, 0))))


def _mixout(ya, yb, acts, x2, gate1, scale2, shift2, gain2, wpa, wpb, wo, w_route, seq):
    n, d = x2.shape
    bsz = n // seq
    tm = 256
    tpb = seq // tm
    half = ya.shape[1]
    ga_blk = 0
    per_batch = pl.BlockSpec((1, 1, d), lambda i: (i // tpb, 0, 0))
    const = lambda shape: pl.BlockSpec(shape, lambda i: (0, 0), pipeline_mode=pl.Buffered(1))
    return pl.pallas_call(
        _mixout_kernel,
        grid=(n // tm,),
        in_specs=[pl.BlockSpec((tm, half), lambda i: (i, 0)),
                  pl.BlockSpec((tm, half), lambda i: (i, 0)),
                  pl.BlockSpec((tm, d), lambda i: (i, ga_blk)),
                  pl.BlockSpec((tm, d), lambda i: (i, ga_blk + 1)),
                  pl.BlockSpec((tm, d), lambda i: (i, 0)),
                  per_batch, per_batch, per_batch,
                  const((1, d)), const((half, d)), const((half, d)), const((d, d)), const((d, 2 * LANES))],
        out_specs=[pl.BlockSpec((tm, d), lambda i: (i, 0)),
                   pl.BlockSpec((tm * (d // LANES), LANES), lambda i: (i, 0)),
                   pl.BlockSpec((tm, LANES), lambda i: (i, 0)),
                   pl.BlockSpec((tm, LANES), lambda i: (i, 0)),
                   pl.BlockSpec((1, LANES), lambda i: (0, 0))],
        out_shape=[jax.ShapeDtypeStruct((n, d), F32),
                   jax.ShapeDtypeStruct((n * (d // LANES), LANES), F32),
                   jax.ShapeDtypeStruct((n, LANES), jnp.int32),
                   jax.ShapeDtypeStruct((n, LANES), F32),
                   jax.ShapeDtypeStruct((1, LANES), jnp.int32)],
        scratch_shapes=[pltpu.VMEM((1, LANES), F32)],
        compiler_params=_params("arbitrary"),
        name="mixout_route",
    )(ya, yb, acts, acts, x2, gate1.reshape(bsz, 1, d), scale2.reshape(bsz, 1, d),
      shift2.reshape(bsz, 1, d), gain2.reshape(1, d), wpa, wpb, wo, w_route)


def _start_token_gather(idx_ref, base, ntok, parts, src_hbm, dst, sem):
    for r in range(ntok):
        t0 = pl.multiple_of(idx_ref[base + r] * parts, parts)
        pltpu.make_async_copy(src_hbm.at[pl.ds(t0, parts), :], dst.at[pl.ds(r * parts, parts), :], sem).start()


def _wait_token_gather(ntok, parts, src_hbm, dst, sem):
    pltpu.make_async_copy(src_hbm.at[pl.ds(0, ntok * parts), :], dst.at[pl.ds(0, ntok * parts), :], sem).wait()


def _expert_kernel(be_ref, nv_ref, first_ref, next_ref, dest_ref, nused_ref,
                   h_hbm, wg_hbm, wu_hbm, wd_hbm, o_ref,
                   xbuf, xsems, row_tok, stage_g, stage_u, stage_d, wg_bf, wu_bf, wd_bf, wsems, *, layer):
    b = pl.program_id(0)
    nb = pl.num_programs(0)
    n_used = nused_ref[0]
    unroll = 8
    cast_rows = 256

    def weight_copies(e):
        return (pltpu.make_async_copy(wg_hbm.at[layer, e], stage_g, wsems.at[0]),
                pltpu.make_async_copy(wu_hbm.at[layer, e], stage_u, wsems.at[1]),
                pltpu.make_async_copy(wd_hbm.at[layer, e], stage_d, wsems.at[2]))

    parts = xbuf.shape[1] // MOE_BLOCK

    def gather(blk, slot):
        _start_token_gather(row_tok, blk * MOE_BLOCK, MOE_BLOCK, parts, h_hbm, xbuf.at[slot], xsems.at[slot])

    def wait_gather(slot):
        _wait_token_gather(MOE_BLOCK, parts, h_hbm, xbuf.at[slot], xsems.at[slot])

    @pl.when(b == 0)
    def _():
        for cp in weight_copies(be_ref[0]):
            cp.start()

        def clear(i, carry):
            for u in range(unroll):
                row_tok[i * unroll + u] = 0
            return carry

        lax.fori_loop(0, row_tok.shape[0] // unroll, clear, 0)

        def invert(i, carry):
            for u in range(unroll):
                row_tok[dest_ref[i * unroll + u]] = i * (unroll // TOP_K) + u // TOP_K
            return carry

        lax.fori_loop(0, dest_ref.shape[0] // unroll, invert, 0)
        gather(0, 0)

    @pl.when(b < n_used)
    def _():
        slot = b % 2

        @pl.when(first_ref[b] == 1)
        def _():
            for cp in weight_copies(0):
                cp.wait()
            for stage, dst in ((stage_g, wg_bf), (stage_u, wu_bf), (stage_d, wd_bf)):
                def cast(i, carry, stage=stage, dst=dst):
                    r0 = pl.multiple_of(i * cast_rows, cast_rows)
                    dst[pl.ds(r0, cast_rows), :] = stage[pl.ds(r0, cast_rows), :].astype(BF16)
                    return carry

                lax.fori_loop(0, stage.shape[0] // cast_rows, cast, 0)

            @pl.when(next_ref[b] >= 0)
            def _():
                for cp in weight_copies(next_ref[b]):
                    cp.start()

        wait_gather(slot)
        valid = lax.broadcasted_iota(jnp.int32, (MOE_BLOCK, 1), 0) < nv_ref[b]
        xb = jnp.where(valid, _load_token_major(xbuf.at[slot], MOE_BLOCK, parts), 0.0).astype(BF16)
        gather(b + 1, 1 - slot)
        gate = jnp.dot(xb, wg_bf[...], preferred_element_type=F32)
        up = jnp.dot(xb, wu_bf[...], preferred_element_type=F32)
        hid = (_silu(gate) * up).astype(BF16)
        _store_token_major(o_ref, jnp.dot(hid, wd_bf[...], preferred_element_type=F32))

        @pl.when(b == nb - 1)
        def _():
            wait_gather(1 - slot)

    @pl.when(b >= n_used)
    def _():
        @pl.when(b == n_used)
        def _():
            wait_gather(b % 2)

        o_ref[...] = jnp.zeros(o_ref.shape, o_ref.dtype)


def _experts(h2_tm, wg, wu, wd, layer, plan):
    p = plan["rows"]
    nb = p // MOE_BLOCK
    d, de = wg.shape[-2:]
    parts = d // LANES
    hbm = pl.BlockSpec(memory_space=pl.ANY)
    grid_spec = pltpu.PrefetchScalarGridSpec(
        num_scalar_prefetch=6,
        grid=(nb,),
        in_specs=[hbm, hbm, hbm, hbm],
        out_specs=pl.BlockSpec((MOE_BLOCK * parts, LANES), lambda b, *_: (b, 0)),
        scratch_shapes=[pltpu.VMEM((2, MOE_BLOCK * parts, LANES), F32), pltpu.SemaphoreType.DMA((2,)),
                        pltpu.SMEM((p + MOE_BLOCK,), jnp.int32),
                        pltpu.VMEM((d, de), F32), pltpu.VMEM((d, de), F32), pltpu.VMEM((de, d), F32),
                        pltpu.VMEM((d, de), BF16), pltpu.VMEM((d, de), BF16), pltpu.VMEM((de, d), BF16),
                        pltpu.SemaphoreType.DMA((3,))],
    )
    return pl.pallas_call(
        functools.partial(_expert_kernel, layer=layer),
        grid_spec=grid_spec,
        out_shape=jax.ShapeDtypeStruct((p * parts, LANES), F32),
        compiler_params=_params("arbitrary"),
        name="experts",
    )(plan["block_expert"], plan["n_valid"], plan["first"], plan["next_expert"], plan["dest"],
      plan["n_used"], h2_tm, wg, wu, wd)


COMBINE_TILE = 64
COMBINE_BATCHES = 4


def _combine_kernel(pos_ref, y_hbm, x1_ref, ew_ref, gate_ref, gain_ref, o_ref, *scratch):
    i = pl.program_id(0)
    tm = COMBINE_TILE
    nbuf = COMBINE_BATCHES
    rows = TOP_K * tm
    bufs, sems = scratch[:nbuf], scratch[nbuf]

    parts = x1_ref.shape[1] // LANES

    def gather(tile, k):
        _start_token_gather(pos_ref, tile * rows, rows, parts, y_hbm, bufs[k], sems.at[k])

    def wait(k):
        _wait_token_gather(rows, parts, y_hbm, bufs[k], sems.at[k])

    def finish(k):
        sl = slice(k * tm, (k + 1) * tm)
        ew = ew_ref[sl, :]
        moe = ew[:, 0:1] * _load_token_major(bufs[k], tm, parts)
        for j in range(1, TOP_K):
            moe = moe + ew[:, j:j + 1] * _load_token_major(bufs[k], tm, parts, row0=j * tm)
        x2 = x1_ref[sl, :] + gate_ref[0] * moe
        o_ref[sl, :] = x2 * lax.rsqrt(jnp.mean(x2 * x2, axis=-1, keepdims=True) + EPS) * gain_ref[...]

    ahead = nbuf - 1

    @pl.when(i == 0)
    def _():
        for q in range(ahead):
            gather(q, q)

    for q in range(nbuf):
        wait(q)
        gather(nbuf * i + q + ahead, (q + ahead) % nbuf)
        finish(q)

    @pl.when(i == pl.num_programs(0) - 1)
    def _():
        for q in range(ahead):
            wait(q)


def _combine(ys, pos_tiles, x1, ew, gate2, final_gain, seq):
    n, d = x1.shape
    bsz = n // seq
    tm = COMBINE_BATCHES * COMBINE_TILE
    tpb = seq // tm
    grid_spec = pltpu.PrefetchScalarGridSpec(
        num_scalar_prefetch=1,
        grid=(n // tm,),
        in_specs=[pl.BlockSpec(memory_space=pl.ANY),
                  pl.BlockSpec((tm, d), lambda i, pos: (i, 0)),
                  pl.BlockSpec((tm, LANES), lambda i, pos: (i, 0)),
                  pl.BlockSpec((1, 1, d), lambda i, pos: (i // tpb, 0, 0)),
                  pl.BlockSpec((1, d), lambda i, pos: (0, 0))],
        out_specs=pl.BlockSpec((tm, d), lambda i, pos: (i, 0)),
        scratch_shapes=[pltpu.VMEM((TOP_K * COMBINE_TILE * (d // LANES), LANES), F32)] * COMBINE_BATCHES
        + [pltpu.SemaphoreType.DMA((COMBINE_BATCHES,))],
    )
    return pl.pallas_call(
        _combine_kernel,
        grid_spec=grid_spec,
        out_shape=jax.ShapeDtypeStruct((n, d), F32),
        compiler_params=_params("arbitrary"),
        name="combine_norm",
    )(pos_tiles, ys, x1, ew, gate2.reshape(bsz, 1, d), final_gain.reshape(1, d))


def _dispatch_plan(route, counts):
    n = route.shape[0]
    a = n * TOP_K
    eid = route[:, :TOP_K]
    rank = route[:, TOP_K:2 * TOP_K]
    cnt = counts[0, :N_EXPERTS]
    padded = (cnt + MOE_BLOCK - 1) // MOE_BLOCK * MOE_BLOCK
    pad_ends = jnp.cumsum(padded)
    pad_starts = pad_ends - padded
    dest = (pad_starts[eid] + rank).astype(jnp.int32)
    p = ((a + MOE_BLOCK - 1) // MOE_BLOCK + N_EXPERTS) * MOE_BLOCK
    nb = p // MOE_BLOCK
    n_used = (pad_ends[-1] // MOE_BLOCK).astype(jnp.int32)
    blk = jnp.arange(nb, dtype=jnp.int32)
    blk_start = jnp.minimum(blk, n_used - 1) * MOE_BLOCK
    block_expert = jnp.minimum(jnp.sum((pad_ends[None, :] <= blk_start[:, None]).astype(jnp.int32), axis=1),
                               N_EXPERTS - 1).astype(jnp.int32)
    n_valid = jnp.clip(pad_starts[block_expert] + cnt[block_expert] - blk_start, 0, MOE_BLOCK).astype(jnp.int32)
    first = ((blk_start == pad_starts[block_expert]) & (blk < n_used)).astype(jnp.int32)
    nxt_blk = pad_ends[block_expert] // MOE_BLOCK
    next_expert = jnp.where(nxt_blk < n_used, block_expert[jnp.minimum(nxt_blk, nb - 1)], -1).astype(jnp.int32)
    pos_tiles = dest.reshape(n // COMBINE_TILE, COMBINE_TILE, TOP_K).transpose(0, 2, 1).reshape(a)
    pos_tiles = jnp.concatenate(
        [pos_tiles, jnp.zeros(((COMBINE_BATCHES - 1) * TOP_K * COMBINE_TILE,), jnp.int32)])
    plan = dict(block_expert=block_expert, n_valid=n_valid, first=first, next_expert=next_expert,
                dest=dest.reshape(a), n_used=n_used.reshape(1), rows=p)
    return plan, pos_tiles


def kernel(x, c, w_ada, b_ada, norm1, w_in, hg_lb, hg_norm, rg_conv_w, rg_conv_b, rg_w_a, rg_b_a,
           rg_w_x, rg_b_x, rg_lambda, w_proj_a, w_proj_b, w_out, norm2, w_group, w_router, w_gate,
           w_up, w_down, final_norm):
    bsz, seq, d = x.shape
    depth = w_ada.shape[0]
    n = bsz * seq
    x2 = x.reshape(n, d)
    lb_table = hg_lb.reshape(hg_lb.shape[0], -1)
    for layer in range(depth):
        mod = _adaln(c, w_ada[layer], b_ada[layer])
        shift1, scale1, gate1, shift2, scale2, gate2 = jnp.split(mod, 6, axis=-1)
        logf, acts = _inproj(x2, norm1[layer], scale1, shift1, w_in[layer].astype(BF16), lb_table, layer, seq)
        acts3 = acts.reshape(bsz, seq, acts.shape[1])
        ya = _hgrn(acts3, logf.reshape(bsz, seq, logf.shape[1]), hg_norm[layer])
        yb = _rglru(acts3, rg_conv_w[layer], rg_conv_b[layer], rg_w_a[layer], rg_b_a[layer],
                    rg_w_x[layer], rg_b_x[layer], rg_lambda[layer])
        w_route = jnp.concatenate(
            [w_group[layer], w_router[layer].reshape(d, N_EXPERTS),
             jnp.zeros((d, LANES - N_GROUPS - N_EXPERTS), F32)], axis=1)
        w_route_hi = w_route.astype(BF16)
        w_route = jnp.concatenate([w_route_hi, (w_route - w_route_hi.astype(F32)).astype(BF16)], axis=1)
        x1, h2, route, ew, counts = _mixout(
            ya.reshape(n, -1), yb.reshape(n, -1), acts, x2, gate1, scale2, shift2, norm2[layer],
            w_proj_a[layer].astype(BF16), w_proj_b[layer].astype(BF16), w_out[layer].astype(BF16), w_route, seq)
        plan, pos_tiles = _dispatch_plan(route, counts)
        ys = _experts(h2, w_gate, w_up, w_down, layer, plan)
        if layer + 1 < depth:
            raise NotImplementedError("the fused combine + final norm assumes a single layer")
        x2 = _combine(ys, pos_tiles, x1, ew, gate2, final_norm, seq)
    return x2.reshape(bsz, seq, d)
```

```python
import functools

import jax
import jax.numpy as jnp
import numpy as np
from jax import lax
from jax.experimental import pallas as pl
from jax.experimental.pallas import tpu as pltpu

EPS = 1e-6
LANES = 128
SUBLANES = 8
HG_HEADS = 8
RG_BLOCKS = 8
RG_CONV = 4
RG_C = 8.0
N_GROUPS = 4
EXPERTS_PER_GROUP = 8
N_EXPERTS = N_GROUPS * EXPERTS_PER_GROUP
TOP_K = 2
HG_CHUNK = 128
MOE_BLOCK = 128
EXPERT_X_SLOTS = 4
VMEM_LIMIT = 56 * 1024 * 1024
ACT_Q, ACT_V, ACT_G, ACT_RX, ACT_GY = 32, 40, 48, 56, 64

F32 = jnp.float32
BF16 = jnp.bfloat16


def _sigmoid(x):
    return 1.0 / (1.0 + jnp.exp(-x))


def _silu(x):
    return x * _sigmoid(x)


def _gelu_tanh(x):
    return 0.5 * x * (1.0 + jnp.tanh(0.7978845608028654 * (x + 0.044715 * (x * x * x))))


def _params(*sem):
    return pltpu.CompilerParams(dimension_semantics=sem, vmem_limit_bytes=VMEM_LIMIT)


def _adaln_kernel(ct_ref, w_ref, b_ref, o_ref, *, kc):
    kdim, nb = ct_ref.shape
    tn = w_ref.shape[1]

    def body(i, accs):
        k0 = pl.multiple_of(i * kc, kc)
        w = w_ref[pl.ds(k0, kc), :]
        cs = _silu(ct_ref[pl.ds(k0, kc), :])
        return tuple(acc + jnp.sum(w * cs[:, b:b + 1], axis=0, keepdims=True)
                     for b, acc in enumerate(accs))

    accs = lax.fori_loop(0, kdim // kc, body, tuple(jnp.zeros((1, tn), F32) for _ in range(nb)))
    for b, acc in enumerate(accs):
        o_ref[b:b + 1, :] = acc + b_ref[...]


def _adaln(c, w, bias):
    bsz, d = c.shape
    n = w.shape[1]
    tn = 1024
    return pl.pallas_call(
        functools.partial(_adaln_kernel, kc=256),
        grid=(n // tn,),
        in_specs=[pl.BlockSpec((d, bsz), lambda j: (0, 0)),
                  pl.BlockSpec((d, tn), lambda j: (0, j)),
                  pl.BlockSpec((1, tn), lambda j: (0, j))],
        out_specs=pl.BlockSpec((bsz, tn), lambda j: (0, j)),
        out_shape=jax.ShapeDtypeStruct((bsz, n), F32),
        compiler_params=_params("parallel"),
        name="adaln",
    )(c.T, w, bias.reshape(1, n))


def _norm_mod_to_scratch(x_ref, gain_ref, scale_ref, shift_ref, h_scr):
    rows = 64

    def body(i, carry):
        r0 = pl.multiple_of(i * rows, rows)
        x = x_ref[pl.ds(r0, rows), :]
        y = x * lax.rsqrt(jnp.mean(x * x, axis=-1, keepdims=True) + EPS) * gain_ref[...]
        h_scr[pl.ds(r0, rows), :] = (y * (1.0 + scale_ref[0]) + shift_ref[0]).astype(BF16)
        return carry

    lax.fori_loop(0, x_ref.shape[0] // rows, body, 0)


INPROJ_ROWS = 128
INPROJ_COLS = 256


def _inproj_logf_kernel(x_ref, gain_ref, scale_ref, shift_ref, w_ref, lbt_ref, o_ref, h_scr, *, layer):
    @pl.when(pl.program_id(1) == 0)
    def _():
        _norm_mod_to_scratch(x_ref, gain_ref, scale_ref, shift_ref, h_scr)

    t = lbt_ref[...]
    e = jnp.exp(t - jnp.max(t, axis=0, keepdims=True))
    lb = jnp.sum(e[:layer + 1], axis=0, keepdims=True) / jnp.sum(e, axis=0, keepdims=True)
    for r0 in range(0, h_scr.shape[0], INPROJ_ROWS):
        for c0 in range(0, w_ref.shape[1], INPROJ_COLS):
            z = jnp.dot(h_scr[r0:r0 + INPROJ_ROWS, :], w_ref[:, c0:c0 + INPROJ_COLS], preferred_element_type=F32)
            lbc = lb[:, c0:c0 + INPROJ_COLS]
            o_ref[r0:r0 + INPROJ_ROWS, c0:c0 + INPROJ_COLS] = jnp.log(lbc + (1.0 - lbc) * _sigmoid(z))


def _inproj_act_kernel(x_ref, gain_ref, scale_ref, shift_ref, w_ref, o_ref, h_scr):
    j = pl.program_id(1)

    @pl.when(j == 0)
    def _():
        _norm_mod_to_scratch(x_ref, gain_ref, scale_ref, shift_ref, h_scr)

    z = jnp.dot(h_scr[...], w_ref[...], preferred_element_type=F32)

    @pl.when(j < 4)
    def _():
        o_ref[...] = _sigmoid(z).astype(BF16)

    @pl.when((j == 4) | (j == 6))
    def _():
        o_ref[...] = _silu(z).astype(BF16)

    @pl.when((j == 5) | (j == 7))
    def _():
        o_ref[...] = z.astype(BF16)

    @pl.when(j == 8)
    def _():
        o_ref[...] = _gelu_tanh(z).astype(BF16)


def _inproj(x2, gain, scale, shift, w_bf, lb_table, layer, seq):
    n, d = x2.shape
    bsz = n // seq
    tn = 1024
    gain2 = gain.reshape(1, d)
    scale3 = scale.reshape(bsz, 1, d)
    shift3 = shift.reshape(bsz, 1, d)

    def common(tm):
        per_batch = pl.BlockSpec((1, 1, d), lambda i, j: (i // (seq // tm), 0, 0))
        return [pl.BlockSpec((tm, d), lambda i, j: (i, 0)), pl.BlockSpec((1, d), lambda i, j: (0, 0)),
                per_batch, per_batch]

    n_lb = lb_table.shape[0]
    tm = 1024
    logf = pl.pallas_call(
        functools.partial(_inproj_logf_kernel, layer=layer),
        grid=(n // tm, 2),
        in_specs=common(tm) + [pl.BlockSpec((d, tn), lambda i, j: (0, j + 2)),
                               pl.BlockSpec((n_lb, tn), lambda i, j: (0, j))],
        out_specs=pl.BlockSpec((tm, tn), lambda i, j: (i, j)),
        out_shape=jax.ShapeDtypeStruct((n, 2 * tn), F32),
        scratch_shapes=[pltpu.VMEM((tm, d), BF16)],
        compiler_params=_params("parallel", "arbitrary"),
        name="inproj_logf",
    )(x2, gain2, scale3, shift3, w_bf, lb_table)
    tm = 512
    acts = pl.pallas_call(
        _inproj_act_kernel,
        grid=(n // tm, 9),
        in_specs=common(tm) + [pl.BlockSpec(
            (d, tn), lambda i, j: (0, jnp.where(j < 4, j + 7, jnp.where(j < 6, j - 4, j - 2))))],
        out_specs=pl.BlockSpec((tm, tn), lambda i, j: (i, j)),
        out_shape=jax.ShapeDtypeStruct((n, 9 * tn), BF16),
        scratch_shapes=[pltpu.VMEM((tm, d), BF16)],
        compiler_params=_params("parallel", "arbitrary"),
        name="inproj_act",
    )(x2, gain2, scale3, shift3, w_bf)
    return logf, acts


def _hgrn_level_halves(chunk):
    return [chunk >> (i + 1) for i in range(chunk.bit_length() - 1)]


def _hgrn_kernel(q_ref, v_ref, g_ref, lff_ref, lfb_ref, gain_ref, o_ref,
                 oacc, qf_scr, kf_scr, qb_scr, kb_scr, *, chunk, group):
    seq = q_ref.shape[1]
    nchunks = seq // chunk
    halves = _hgrn_level_halves(chunk)
    nt = (((1,), (1,)), ((), ()))
    tn = (((0,), (0,)), ((), ()))

    row = lax.broadcasted_iota(jnp.int32, (chunk, LANES), 0)
    ti = lax.broadcasted_iota(jnp.int32, (chunk, chunk), 0)
    si = lax.broadcasted_iota(jnp.int32, (chunk, chunk), 1)
    txs = ti ^ si
    tri_f = jnp.where(si <= ti, 1.0, 0.0).astype(BF16)
    tri_b = jnp.where(si >= ti, 1.0, 0.0).astype(BF16)

    def cumulative(lf, tri):
        hi = lf.astype(BF16)
        mid = (lf - hi.astype(F32)).astype(BF16)
        r = jnp.dot(tri, jnp.concatenate([hi, mid], axis=1), preferred_element_type=F32)
        return r[:, :LANES] + r[:, LANES:]

    lid = functools.reduce(lambda a, b: a + b, [(txs >= h).astype(jnp.int32) for h in halves])

    def boundary_diff(cum, m, reverse):
        pieces = []
        for base in range(0, chunk, 2 * m):
            bnd = base + (m if reverse else m - 1)
            pieces.append(jnp.broadcast_to(cum[bnd:bnd + 1, :], (2 * m, LANES)))
        return cum - jnp.concatenate(pieces, axis=0)

    def intra(c):
        r0 = pl.multiple_of(c * chunk, chunk)
        q = q_ref[0, pl.ds(r0, chunk), :].astype(F32)
        v = v_ref[0, pl.ds(r0, chunk), :]
        lff = lff_ref[0, pl.ds(r0, chunk), :]
        lfb = lfb_ref[0, pl.ds(r0, chunk), :]
        cum_f = cumulative(lff, tri_f)
        cum_b = cumulative(lfb, tri_b)
        ff = jnp.exp(lff)
        fb = jnp.exp(lfb)
        kf = 1.0 - ff
        kb = 1.0 - fb

        p0 = lax.dot_general(q.astype(BF16), (kf + kb).astype(BF16), nt, preferred_element_type=F32)
        scores = jnp.where(lid == 0, p0, 0.0)
        for m in halves:
            later = (row & m) != 0
            if 2 * m >= SUBLANES:
                d_f = boundary_diff(cum_f, m, False)
                d_b = boundary_diff(cum_b, m, True)
                x = q * jnp.exp(jnp.where(later, d_f, d_b))
                y = jnp.where(later, kb, kf) * jnp.exp(-jnp.where(later, d_b, d_f))
            elif m == 2:
                r4 = row & 3
                prev = lambda a: pltpu.roll(a, 1, 0)
                nxt = lambda a: pltpu.roll(a, chunk - 1, 0)
                x = q * jnp.where(r4 == 0, fb * nxt(fb), jnp.where(r4 == 1, fb,
                                  jnp.where(r4 == 2, ff, ff * prev(ff))))
                y = jnp.where(r4 == 0, kf * nxt(ff), jnp.where(r4 == 1, kf,
                              jnp.where(r4 == 2, kb, kb * prev(fb))))
            else:
                x = q * jnp.where(later, ff, fb)
                y = jnp.where(later, kb, kf)
            pm = lax.dot_general(x.astype(BF16), y.astype(BF16), nt, preferred_element_type=F32)
            scores = jnp.where(lid == m.bit_length(), pm, scores)
        oacc[pl.ds(r0, chunk), :] = jnp.dot(scores.astype(BF16), v, preferred_element_type=F32)
        tot_f = cum_f[chunk - 1:chunk, :]
        tot_b = cum_b[0:1, :]
        qf_scr[pl.ds(r0, chunk), :] = (q * jnp.exp(cum_f)).astype(BF16)
        kf_scr[pl.ds(r0, chunk), :] = (kf * jnp.exp(tot_f - cum_f)).astype(BF16)
        qb_scr[pl.ds(r0, chunk), :] = (q * jnp.exp(cum_b)).astype(BF16)
        kb_scr[pl.ds(r0, chunk), :] = (kb * jnp.exp(tot_b - cum_b)).astype(BF16)

    def intra_body(i, carry):
        for u in range(group):
            intra(i * group + u)
        return carry

    lax.fori_loop(0, nchunks // group, intra_body, 0)

    def apply_state(c, st, q_scr, k_scr, lf_ref):
        r0 = pl.multiple_of(c * chunk, chunk)
        o = oacc[pl.ds(r0, chunk), :] + lax.dot_general(
            q_scr[pl.ds(r0, chunk), :], st.astype(BF16), nt, preferred_element_type=F32)
        tot = jnp.sum(lf_ref[0, pl.ds(r0, chunk), :], axis=0, keepdims=True)
        st = st * jnp.exp(tot) + lax.dot_general(
            v_ref[0, pl.ds(r0, chunk), :], k_scr[pl.ds(r0, chunk), :], tn, preferred_element_type=F32)
        return r0, o, st

    def finish(r0, o):
        y = o * lax.rsqrt(jnp.mean(o * o, axis=-1, keepdims=True) + EPS) * gain_ref[0]
        o_ref[0, pl.ds(r0, chunk), :] = (y * g_ref[0, pl.ds(r0, chunk), :].astype(F32)).astype(BF16)

    def sweep(final):
        def body(i, states):
            st_f, st_b = states
            rf, of, st_f = apply_state(i, st_f, qf_scr, kf_scr, lff_ref)
            rb, ob, st_b = apply_state(nchunks - 1 - i, st_b, qb_scr, kb_scr, lfb_ref)
            if final:
                finish(rf, of)
                finish(rb, ob)
            else:
                oacc[pl.ds(rf, chunk), :] = of
                oacc[pl.ds(rb, chunk), :] = ob
            return st_f, st_b
        return body

    zero = jnp.zeros((LANES, LANES), F32)
    states = lax.fori_loop(0, nchunks // 2, sweep(False), (zero, zero), unroll=2)
    lax.fori_loop(nchunks // 2, nchunks, sweep(True), states, unroll=2)


def _hgrn(acts3, logf3, hg_norm):
    bsz, seq, _ = acts3.shape
    col = lambda off: pl.BlockSpec((1, seq, LANES), lambda b, h: (b, 0, off + h))
    return pl.pallas_call(
        functools.partial(_hgrn_kernel, chunk=HG_CHUNK, group=4),
        grid=(bsz, HG_HEADS),
        in_specs=[col(ACT_Q), col(ACT_V), col(ACT_G), col(0), col(HG_HEADS),
                  pl.BlockSpec((1, 1, LANES), lambda b, h: (h, 0, 0))],
        out_specs=pl.BlockSpec((1, seq, LANES), lambda b, h: (b, 0, h)),
        out_shape=jax.ShapeDtypeStruct((bsz, seq, HG_HEADS * LANES), BF16),
        scratch_shapes=[pltpu.VMEM((seq, LANES), F32)] + [pltpu.VMEM((seq, LANES), BF16)] * 4,
        compiler_params=_params("parallel", "parallel"),
        name="hgrn2",
    )(acts3, acts3, acts3, logf3, logf3, hg_norm.reshape(HG_HEADS, 1, LANES))


def _scan_tile(a, u, reverse):
    rows = a.shape[0]
    row = lax.broadcasted_iota(jnp.int32, a.shape, 0)
    d = 1
    while d < rows:
        if reverse:
            ok = row < rows - d
            shift = rows - d
        else:
            ok = row >= d
            shift = d
        a_s = jnp.where(ok, pltpu.roll(a, shift, 0), 1.0)
        u_s = jnp.where(ok, pltpu.roll(u, shift, 0), 0.0)
        u = u + a * u_s
        a = a * a_s
        d *= 2
    return a, u


def _rglru_kernel(rx_ref, gy_ref, cw_ref, cb_ref, wa_ref, ba_ref, wx_ref, bx_ref, lam_ref, o_ref,
                  xpad, hf_scr, ab_scr, ub_scr, *, tile):
    seq = rx_ref.shape[1]
    ntiles = seq // tile
    halo = 8

    xpad[0:halo, :] = jnp.zeros((halo, LANES), F32)
    xpad[halo + seq:halo + seq + halo, :] = jnp.zeros((halo, LANES), F32)
    xpad[halo:halo + seq, :] = rx_ref[0].astype(F32)

    lam = lam_ref[...]
    neg_c_softplus = -RG_C * (jnp.maximum(-lam, 0.0) + jnp.log(1.0 + jnp.exp(-jnp.abs(lam))))

    def gates(xcb, z):
        r = _sigmoid(jnp.dot(xcb, wa_ref[z, 0], preferred_element_type=F32) + ba_ref[z, 0])
        gi = _sigmoid(jnp.dot(xcb, wx_ref[z, 0], preferred_element_type=F32) + bx_ref[z, 0])
        a = jnp.exp(r * neg_c_softplus[z:z + 1, :])
        return a, jnp.sqrt(1.0 - a * a) * gi

    def fwd_body(i, carry):
        t0 = pl.multiple_of(i * tile, tile)
        ext = xpad[pl.ds(t0, tile + 2 * halo), :]
        n_ext = tile + 2 * halo
        tap = lambda off: pltpu.roll(ext, (-off) % n_ext, 0)[halo:halo + tile]
        xc = (cw_ref[0:1, :] * tap(-2) + cw_ref[1:2, :] * tap(-1) + cw_ref[2:3, :] * ext[halo:halo + tile]
              + cw_ref[3:4, :] * tap(1) + cb_ref[...])
        xcb = xc.astype(BF16)
        a_f, s_f = gates(xcb, 0)
        a_b, s_b = gates(xcb, 1)
        ap, hl = _scan_tile(a_f, s_f * xc, reverse=False)
        h = hl + ap * carry
        hf_scr[pl.ds(t0, tile), :] = h
        ab_scr[pl.ds(t0, tile), :] = a_b
        ub_scr[pl.ds(t0, tile), :] = s_b * xc
        return h[tile - 1:tile, :]

    lax.fori_loop(0, ntiles, fwd_body, jnp.zeros((1, LANES), F32))

    def bwd_body(i, carry):
        t0 = pl.multiple_of((ntiles - 1 - i) * tile, tile)
        ap, hl = _scan_tile(ab_scr[pl.ds(t0, tile), :], ub_scr[pl.ds(t0, tile), :], reverse=True)
        h = hl + ap * carry
        o_ref[0, pl.ds(t0, tile), :] = (
            (hf_scr[pl.ds(t0, tile), :] + h) * gy_ref[0, pl.ds(t0, tile), :].astype(F32)).astype(BF16)
        return h[0:1, :]

    lax.fori_loop(0, ntiles, bwd_body, jnp.zeros((1, LANES), F32))


def _rglru(acts3, conv_w, conv_b, w_a, b_a, w_x, b_x, lam):
    bsz, seq, _ = acts3.shape
    width = RG_BLOCKS * LANES
    tile = 256
    rx_off, gy_off = ACT_RX, ACT_GY
    col = lambda off: pl.BlockSpec((1, seq, LANES), lambda b, j: (b, 0, off + j))
    wspec = pl.BlockSpec((2, 1, LANES, LANES), lambda b, j: (0, j, 0, 0))
    bspec = pl.BlockSpec((2, 1, 1, LANES), lambda b, j: (0, j, 0, 0))
    return pl.pallas_call(
        functools.partial(_rglru_kernel, tile=tile),
        grid=(bsz, RG_BLOCKS),
        in_specs=[col(rx_off), col(gy_off),
                  pl.BlockSpec((RG_CONV, LANES), lambda b, j: (0, j)),
                  pl.BlockSpec((1, LANES), lambda b, j: (0, j)),
                  wspec, bspec, wspec, bspec,
                  pl.BlockSpec((2, LANES), lambda b, j: (0, j))],
        out_specs=pl.BlockSpec((1, seq, LANES), lambda b, j: (b, 0, j)),
        out_shape=jax.ShapeDtypeStruct((bsz, seq, width), BF16),
        scratch_shapes=[pltpu.VMEM((seq + 16, LANES), F32),
                        pltpu.VMEM((seq, LANES), F32),
                        pltpu.VMEM((seq, LANES), F32),
                        pltpu.VMEM((seq, LANES), F32)],
        compiler_params=_params("parallel", "parallel"),
        name="rglru",
    )(acts3, acts3, conv_w, conv_b.reshape(1, width), w_a.astype(BF16),
      b_a.reshape(2, RG_BLOCKS, 1, LANES), w_x.astype(BF16), b_x.reshape(2, RG_BLOCKS, 1, LANES), lam)


def _store_token_major(ref, x, row0=0):
    ntok, d = x.shape
    parts = d // LANES
    for c in range(parts):
        ref[pl.ds(row0 * parts + c, ntok, stride=parts), :] = x[:, c * LANES:(c + 1) * LANES]


def _load_token_major(ref, ntok, parts, row0=0):
    return jnp.concatenate([ref[pl.ds(row0 * parts + c, ntok, stride=parts), :] for c in range(parts)], axis=1)


def _mixout_kernel(ya_ref, yb_ref, sga_ref, sgb_ref, x_ref, gate_ref, scale_ref, shift_ref, gain_ref,
                   wpa_ref, wpb_ref, wo_ref, wr_ref, x1_ref, h2_ref, eid_ref, ew_ref, cnt_ref, running):
    y_a = jnp.dot(ya_ref[...], wpa_ref[...], preferred_element_type=F32)
    y_b = jnp.dot(yb_ref[...], wpb_ref[...], preferred_element_type=F32)
    merged = sga_ref[...].astype(F32) * y_a + sgb_ref[...].astype(F32) * y_b
    mix = jnp.dot(merged.astype(BF16), wo_ref[...], preferred_element_type=F32)
    x1 = x_ref[...] + gate_ref[0] * mix
    x1_ref[...] = x1
    y = x1 * lax.rsqrt(jnp.mean(x1 * x1, axis=-1, keepdims=True) + EPS) * gain_ref[...]
    h2 = y * (1.0 + scale_ref[0]) + shift_ref[0]
    _store_token_major(h2_ref, h2)

    h_hi = h2.astype(BF16)
    h_lo = (h2 - h_hi.astype(F32)).astype(BF16)
    p_hi = jnp.dot(h_hi, wr_ref[...], preferred_element_type=F32)
    p_lo = jnp.dot(h_lo, wr_ref[...], preferred_element_type=F32)
    logits = (p_hi[:, :LANES] + p_hi[:, LANES:]) + (p_lo[:, :LANES] + p_lo[:, LANES:])
    lane = lax.broadcasted_iota(jnp.int32, logits.shape, 1)
    neg = -jnp.inf
    gl = jnp.where(lane < N_GROUPS, logits, neg)
    gmax = jnp.max(gl, axis=-1, keepdims=True)
    g_sel = jnp.min(jnp.where(gl == gmax, lane, LANES), axis=-1, keepdims=True)
    p_group = 1.0 / jnp.sum(jnp.exp(gl - gmax), axis=-1, keepdims=True)
    lo = N_GROUPS + g_sel * EXPERTS_PER_GROUP
    el = jnp.where((lane >= lo) & (lane < lo + EXPERTS_PER_GROUP), logits, neg)
    v1 = jnp.max(el, axis=-1, keepdims=True)
    i1 = jnp.min(jnp.where(el == v1, lane, LANES), axis=-1, keepdims=True)
    el2 = jnp.where(lane == i1, neg, el)
    v2 = jnp.max(el2, axis=-1, keepdims=True)
    i2 = jnp.min(jnp.where(el2 == v2, lane, LANES), axis=-1, keepdims=True)
    e21 = jnp.exp(v2 - v1)
    w1 = p_group / (1.0 + e21)
    w2 = p_group * e21 / (1.0 + e21)
    ew_ref[...] = jnp.where(lane == 0, w1, jnp.where(lane == 1, w2, 0.0))

    @pl.when(pl.program_id(0) == 0)
    def _():
        running[...] = jnp.zeros(running.shape, F32)

    e1, e2 = i1 - N_GROUPS, i2 - N_GROUPS
    tm = logits.shape[0]
    before = (lax.broadcasted_iota(jnp.int32, (tm, tm), 0)
              > lax.broadcasted_iota(jnp.int32, (tm, tm), 1))
    before = jnp.where(before, 1.0, 0.0).astype(BF16)
    oh1 = jnp.where(lane == e1, 1.0, 0.0)
    oh2 = jnp.where(lane == e2, 1.0, 0.0)
    pre1 = jnp.dot(before, oh1.astype(BF16), preferred_element_type=F32)
    pre2 = jnp.dot(before, oh2.astype(BF16), preferred_element_type=F32)
    cnt1 = jnp.sum(oh1, axis=0, keepdims=True)
    base = running[...]
    r1 = jnp.sum(oh1 * (pre1 + base), axis=-1, keepdims=True).astype(jnp.int32)
    r2 = jnp.sum(oh2 * (pre2 + base + cnt1), axis=-1, keepdims=True).astype(jnp.int32)
    total = base + cnt1 + jnp.sum(oh2, axis=0, keepdims=True)
    running[...] = total
    cnt_ref[...] = total.astype(jnp.int32)
    eid_ref[...] = jnp.where(lane == 0, e1, jnp.where(lane == 1, e2,
                             jnp.where(lane == 2, r1, jnp.where(lane == 3, r2, 0))))


def _mixout(ya, yb, acts, x2, gate1, scale2, shift2, gain2, wpa, wpb, wo, w_route, seq):
    n, d = x2.shape
    bsz = n // seq
    tm = 256
    tpb = seq // tm
    half = ya.shape[1]
    ga_blk = 0
    per_batch = pl.BlockSpec((1, 1, d), lambda i: (i // tpb, 0, 0))
    const = lambda shape: pl.BlockSpec(shape, lambda i: (0, 0), pipeline_mode=pl.Buffered(1))
    return pl.pallas_call(
        _mixout_kernel,
        grid=(n // tm,),
        in_specs=[pl.BlockSpec((tm, half), lambda i: (i, 0)),
                  pl.BlockSpec((tm, half), lambda i: (i, 0)),
                  pl.BlockSpec((tm, d), lambda i: (i, ga_blk)),
                  pl.BlockSpec((tm, d), lambda i: (i, ga_blk + 1)),
                  pl.BlockSpec((tm, d), lambda i: (i, 0)),
                  per_batch, per_batch, per_batch,
                  const((1, d)), const((half, d)), const((half, d)), const((d, d)), const((d, 2 * LANES))],
        out_specs=[pl.BlockSpec((tm, d), lambda i: (i, 0)),
                   pl.BlockSpec((tm * (d // LANES), LANES), lambda i: (i, 0)),
                   pl.BlockSpec((tm, LANES), lambda i: (i, 0)),
                   pl.BlockSpec((tm, LANES), lambda i: (i, 0)),
                   pl.BlockSpec((1, LANES), lambda i: (0, 0))],
        out_shape=[jax.ShapeDtypeStruct((n, d), F32),
                   jax.ShapeDtypeStruct((n * (d // LANES), LANES), F32),
                   jax.ShapeDtypeStruct((n, LANES), jnp.int32),
                   jax.ShapeDtypeStruct((n, LANES), F32),
                   jax.ShapeDtypeStruct((1, LANES), jnp.int32)],
        scratch_shapes=[pltpu.VMEM((1, LANES), F32)],
        compiler_params=_params("arbitrary"),
        name="mixout_route",
    )(ya, yb, acts, acts, x2, gate1.reshape(bsz, 1, d), scale2.reshape(bsz, 1, d),
      shift2.reshape(bsz, 1, d), gain2.reshape(1, d), wpa, wpb, wo, w_route)


def _start_token_gather(idx_ref, base, ntok, parts, src_hbm, dst, sem):
    for r in range(ntok):
        t0 = pl.multiple_of(idx_ref[base + r] * parts, parts)
        pltpu.make_async_copy(src_hbm.at[pl.ds(t0, parts), :], dst.at[pl.ds(r * parts, parts), :], sem).start()


def _wait_token_gather(ntok, parts, src_hbm, dst, sem):
    pltpu.make_async_copy(src_hbm.at[pl.ds(0, ntok * parts), :], dst.at[pl.ds(0, ntok * parts), :], sem).wait()


def _expert_kernel(be_ref, nv_ref, first_ref, next_ref, dest_ref, nused_ref,
                   h_hbm, wg_hbm, wu_hbm, wd_hbm, o_ref,
                   xbuf, xsems, row_tok, stage_g, stage_u, stage_d, wg_bf, wu_bf, wd_bf, wsems, *, layer):
    b = pl.program_id(0)
    nb = pl.num_programs(0)
    n_used = nused_ref[0]
    unroll = 8
    cast_rows = 256

    def weight_copies(e):
        return (pltpu.make_async_copy(wg_hbm.at[layer, e], stage_g, wsems.at[0]),
                pltpu.make_async_copy(wu_hbm.at[layer, e], stage_u, wsems.at[1]),
                pltpu.make_async_copy(wd_hbm.at[layer, e], stage_d, wsems.at[2]))

    parts = xbuf.shape[1] // MOE_BLOCK
    nslot = xbuf.shape[0]
    ahead = nslot - 1

    def gather(blk, slot):
        _start_token_gather(row_tok, blk * MOE_BLOCK, MOE_BLOCK, parts, h_hbm, xbuf.at[slot], xsems.at[slot])

    def wait_gather(slot):
        _wait_token_gather(MOE_BLOCK, parts, h_hbm, xbuf.at[slot], xsems.at[slot])

    @pl.when(b == 0)
    def _():
        for cp in weight_copies(be_ref[0]):
            cp.start()

        def clear(i, carry):
            for u in range(unroll):
                row_tok[i * unroll + u] = 0
            return carry

        lax.fori_loop(0, row_tok.shape[0] // unroll, clear, 0)

        def invert(i, carry):
            for u in range(unroll):
                row_tok[dest_ref[i * unroll + u]] = i * (unroll // TOP_K) + u // TOP_K
            return carry

        lax.fori_loop(0, dest_ref.shape[0] // unroll, invert, 0)
        for q in range(ahead):
            gather(q, q)

    @pl.when(b < n_used + ahead)
    def _():
        wait_gather(b % nslot)

    @pl.when(b < n_used)
    def _():
        slot = b % nslot

        @pl.when(first_ref[b] == 1)
        def _():
            for cp in weight_copies(0):
                cp.wait()
            for stage, dst in ((stage_g, wg_bf), (stage_u, wu_bf), (stage_d, wd_bf)):
                def cast(i, carry, stage=stage, dst=dst):
                    r0 = pl.multiple_of(i * cast_rows, cast_rows)
                    dst[pl.ds(r0, cast_rows), :] = stage[pl.ds(r0, cast_rows), :].astype(BF16)
                    return carry

                lax.fori_loop(0, stage.shape[0] // cast_rows, cast, 0)

            @pl.when(next_ref[b] >= 0)
            def _():
                for cp in weight_copies(next_ref[b]):
                    cp.start()

        valid = lax.broadcasted_iota(jnp.int32, (MOE_BLOCK, 1), 0) < nv_ref[b]
        xb = jnp.where(valid, _load_token_major(xbuf.at[slot], MOE_BLOCK, parts), 0.0).astype(BF16)
        gather(b + ahead, (b + ahead) % nslot)
        gate = jnp.dot(xb, wg_bf[...], preferred_element_type=F32)
        up = jnp.dot(xb, wu_bf[...], preferred_element_type=F32)
        hid = (_silu(gate) * up).astype(BF16)
        _store_token_major(o_ref, jnp.dot(hid, wd_bf[...], preferred_element_type=F32))

    @pl.when(b >= n_used)
    def _():
        o_ref[...] = jnp.zeros(o_ref.shape, o_ref.dtype)

    @pl.when(b == nb - 1)
    def _():
        for q in range(ahead):
            @pl.when(nb + q < n_used + ahead)
            def _(q=q):
                wait_gather((nb + q) % nslot)


def _experts(h2_tm, wg, wu, wd, layer, plan):
    p = plan["rows"]
    nb = p // MOE_BLOCK
    d, de = wg.shape[-2:]
    parts = d // LANES
    hbm = pl.BlockSpec(memory_space=pl.ANY)
    grid_spec = pltpu.PrefetchScalarGridSpec(
        num_scalar_prefetch=6,
        grid=(nb,),
        in_specs=[hbm, hbm, hbm, hbm],
        out_specs=pl.BlockSpec((MOE_BLOCK * parts, LANES), lambda b, *_: (b, 0)),
        scratch_shapes=[pltpu.VMEM((EXPERT_X_SLOTS, MOE_BLOCK * parts, LANES), F32),
                        pltpu.SemaphoreType.DMA((EXPERT_X_SLOTS,)),
                        pltpu.SMEM((p + (EXPERT_X_SLOTS - 1) * MOE_BLOCK,), jnp.int32),
                        pltpu.VMEM((d, de), F32), pltpu.VMEM((d, de), F32), pltpu.VMEM((de, d), F32),
                        pltpu.VMEM((d, de), BF16), pltpu.VMEM((d, de), BF16), pltpu.VMEM((de, d), BF16),
                        pltpu.SemaphoreType.DMA((3,))],
    )
    return pl.pallas_call(
        functools.partial(_expert_kernel, layer=layer),
        grid_spec=grid_spec,
        out_shape=jax.ShapeDtypeStruct((p * parts, LANES), F32),
        compiler_params=_params("arbitrary"),
        name="experts",
    )(plan["block_expert"], plan["n_valid"], plan["first"], plan["next_expert"], plan["dest"],
      plan["n_used"], h2_tm, wg, wu, wd)


COMBINE_TILE = 64
COMBINE_BATCHES = 4


def _combine_kernel(pos_ref, y_hbm, x1_ref, ew_ref, gate_ref, gain_ref, o_ref, *scratch):
    i = pl.program_id(0)
    tm = COMBINE_TILE
    nbuf = COMBINE_BATCHES
    rows = TOP_K * tm
    bufs, sems = scratch[:nbuf], scratch[nbuf]

    parts = x1_ref.shape[1] // LANES

    def gather(tile, k):
        _start_token_gather(pos_ref, tile * rows, rows, parts, y_hbm, bufs[k], sems.at[k])

    def wait(k):
        _wait_token_gather(rows, parts, y_hbm, bufs[k], sems.at[k])

    def finish(k):
        sl = slice(k * tm, (k + 1) * tm)
        ew = ew_ref[sl, :]
        moe = ew[:, 0:1] * _load_token_major(bufs[k], tm, parts)
        for j in range(1, TOP_K):
            moe = moe + ew[:, j:j + 1] * _load_token_major(bufs[k], tm, parts, row0=j * tm)
        x2 = x1_ref[sl, :] + gate_ref[0] * moe
        o_ref[sl, :] = x2 * lax.rsqrt(jnp.mean(x2 * x2, axis=-1, keepdims=True) + EPS) * gain_ref[...]

    ahead = nbuf - 1

    @pl.when(i == 0)
    def _():
        for q in range(ahead):
            gather(q, q)

    for q in range(nbuf):
        wait(q)
        gather(nbuf * i + q + ahead, (q + ahead) % nbuf)
        finish(q)

    @pl.when(i == pl.num_programs(0) - 1)
    def _():
        for q in range(ahead):
            wait(q)


def _combine(ys, pos_tiles, x1, ew, gate2, final_gain, seq):
    n, d = x1.shape
    bsz = n // seq
    tm = COMBINE_BATCHES * COMBINE_TILE
    tpb = seq // tm
    grid_spec = pltpu.PrefetchScalarGridSpec(
        num_scalar_prefetch=1,
        grid=(n // tm,),
        in_specs=[pl.BlockSpec(memory_space=pl.ANY),
                  pl.BlockSpec((tm, d), lambda i, pos: (i, 0)),
                  pl.BlockSpec((tm, LANES), lambda i, pos: (i, 0)),
                  pl.BlockSpec((1, 1, d), lambda i, pos: (i // tpb, 0, 0)),
                  pl.BlockSpec((1, d), lambda i, pos: (0, 0))],
        out_specs=pl.BlockSpec((tm, d), lambda i, pos: (i, 0)),
        scratch_shapes=[pltpu.VMEM((TOP_K * COMBINE_TILE * (d // LANES), LANES), F32)] * COMBINE_BATCHES
        + [pltpu.SemaphoreType.DMA((COMBINE_BATCHES,))],
    )
    return pl.pallas_call(
        _combine_kernel,
        grid_spec=grid_spec,
        out_shape=jax.ShapeDtypeStruct((n, d), F32),
        compiler_params=_params("arbitrary"),
        name="combine_norm",
    )(pos_tiles, ys, x1, ew, gate2.reshape(bsz, 1, d), final_gain.reshape(1, d))


def _dispatch_plan(route, counts):
    n = route.shape[0]
    a = n * TOP_K
    eid = route[:, :TOP_K]
    rank = route[:, TOP_K:2 * TOP_K]
    cnt = counts[0, :N_EXPERTS]
    padded = (cnt + MOE_BLOCK - 1) // MOE_BLOCK * MOE_BLOCK
    pad_ends = jnp.cumsum(padded)
    pad_starts = pad_ends - padded
    dest = (pad_starts[eid] + rank).astype(jnp.int32)
    p = ((a + MOE_BLOCK - 1) // MOE_BLOCK + N_EXPERTS) * MOE_BLOCK
    nb = p // MOE_BLOCK
    n_used = (pad_ends[-1] // MOE_BLOCK).astype(jnp.int32)
    blk = jnp.arange(nb, dtype=jnp.int32)
    blk_start = jnp.minimum(blk, n_used - 1) * MOE_BLOCK
    block_expert = jnp.minimum(jnp.sum((pad_ends[None, :] <= blk_start[:, None]).astype(jnp.int32), axis=1),
                               N_EXPERTS - 1).astype(jnp.int32)
    n_valid = jnp.clip(pad_starts[block_expert] + cnt[block_expert] - blk_start, 0, MOE_BLOCK).astype(jnp.int32)
    first = ((blk_start == pad_starts[block_expert]) & (blk < n_used)).astype(jnp.int32)
    nxt_blk = pad_ends[block_expert] // MOE_BLOCK
    next_expert = jnp.where(nxt_blk < n_used, block_expert[jnp.minimum(nxt_blk, nb - 1)], -1).astype(jnp.int32)
    pos_tiles = dest.reshape(n // COMBINE_TILE, COMBINE_TILE, TOP_K).transpose(0, 2, 1).reshape(a)
    pos_tiles = jnp.concatenate(
        [pos_tiles, jnp.zeros(((COMBINE_BATCHES - 1) * TOP_K * COMBINE_TILE,), jnp.int32)])
    plan = dict(block_expert=block_expert, n_valid=n_valid, first=first, next_expert=next_expert,
                dest=dest.reshape(a), n_used=n_used.reshape(1), rows=p)
    return plan, pos_tiles


def kernel(x, c, w_ada, b_ada, norm1, w_in, hg_lb, hg_norm, rg_conv_w, rg_conv_b, rg_w_a, rg_b_a,
           rg_w_x, rg_b_x, rg_lambda, w_proj_a, w_proj_b, w_out, norm2, w_group, w_router, w_gate,
           w_up, w_down, final_norm):
    bsz, seq, d = x.shape
    depth = w_ada.shape[0]
    n = bsz * seq
    x2 = x.reshape(n, d)
    lb_table = hg_lb.reshape(hg_lb.shape[0], -1)
    for layer in range(depth):
        mod = _adaln(c, w_ada[layer], b_ada[layer])
        shift1, scale1, gate1, shift2, scale2, gate2 = jnp.split(mod, 6, axis=-1)
        logf, acts = _inproj(x2, norm1[layer], scale1, shift1, w_in[layer].astype(BF16), lb_table, layer, seq)
        acts3 = acts.reshape(bsz, seq, acts.shape[1])
        ya = _hgrn(acts3, logf.reshape(bsz, seq, logf.shape[1]), hg_norm[layer])
        yb = _rglru(acts3, rg_conv_w[layer], rg_conv_b[layer], rg_w_a[layer], rg_b_a[layer],
                    rg_w_x[layer], rg_b_x[layer], rg_lambda[layer])
        w_route = jnp.concatenate(
            [w_group[layer], w_router[layer].reshape(d, N_EXPERTS),
             jnp.zeros((d, LANES - N_GROUPS - N_EXPERTS), F32)], axis=1)
        w_route_hi = w_route.astype(BF16)
        w_route = jnp.concatenate([w_route_hi, (w_route - w_route_hi.astype(F32)).astype(BF16)], axis=1)
        x1, h2, route, ew, counts = _mixout(
            ya.reshape(n, -1), yb.reshape(n, -1), acts, x2, gate1, scale2, shift2, norm2[layer],
            w_proj_a[layer].astype(BF16), w_proj_b[layer].astype(BF16), w_out[layer].astype(BF16), w_route, seq)
        plan, pos_tiles = _dispatch_plan(route, counts)
        ys = _experts(h2, w_gate, w_up, w_down, layer, plan)
        if layer + 1 < depth:
            raise NotImplementedError("the fused combine + final norm assumes a single layer")
        x2 = _combine(ys, pos_tiles, x1, ew, gate2, final_norm, seq)
    return x2.reshape(bsz, seq, d)
```

```python
import functools

import jax
import jax.numpy as jnp
import numpy as np
from jax import lax
from jax.experimental import pallas as pl
from jax.experimental.pallas import tpu as pltpu

EPS = 1e-6
LANES = 128
SUBLANES = 8
HG_HEADS = 8
RG_BLOCKS = 8
RG_CONV = 4
RG_C = 8.0
N_GROUPS = 4
EXPERTS_PER_GROUP = 8
N_EXPERTS = N_GROUPS * EXPERTS_PER_GROUP
TOP_K = 2
HG_CHUNK = 128
MOE_BLOCK = 128
EXPERT_X_SLOTS = 6
VMEM_LIMIT = 56 * 1024 * 1024
ACT_Q, ACT_V, ACT_G, ACT_RX, ACT_GY = 32, 40, 48, 56, 64

F32 = jnp.float32
BF16 = jnp.bfloat16


def _sigmoid(x):
    return 1.0 / (1.0 + jnp.exp(-x))


def _silu(x):
    return x * _sigmoid(x)


def _gelu_tanh(x):
    return 0.5 * x * (1.0 + jnp.tanh(0.7978845608028654 * (x + 0.044715 * (x * x * x))))


def _params(*sem):
    return pltpu.CompilerParams(dimension_semantics=sem, vmem_limit_bytes=VMEM_LIMIT)


def _adaln_kernel(ct_ref, w_ref, b_ref, o_ref, *, kc):
    kdim, nb = ct_ref.shape
    tn = w_ref.shape[1]

    def body(i, accs):
        k0 = pl.multiple_of(i * kc, kc)
        w = w_ref[pl.ds(k0, kc), :]
        cs = _silu(ct_ref[pl.ds(k0, kc), :])
        return tuple(acc + jnp.sum(w * cs[:, b:b + 1], axis=0, keepdims=True)
                     for b, acc in enumerate(accs))

    accs = lax.fori_loop(0, kdim // kc, body, tuple(jnp.zeros((1, tn), F32) for _ in range(nb)))
    for b, acc in enumerate(accs):
        o_ref[b:b + 1, :] = acc + b_ref[...]


def _adaln(c, w, bias):
    bsz, d = c.shape
    n = w.shape[1]
    tn = 1024
    return pl.pallas_call(
        functools.partial(_adaln_kernel, kc=256),
        grid=(n // tn,),
        in_specs=[pl.BlockSpec((d, bsz), lambda j: (0, 0)),
                  pl.BlockSpec((d, tn), lambda j: (0, j)),
                  pl.BlockSpec((1, tn), lambda j: (0, j))],
        out_specs=pl.BlockSpec((bsz, tn), lambda j: (0, j)),
        out_shape=jax.ShapeDtypeStruct((bsz, n), F32),
        compiler_params=_params("parallel"),
        name="adaln",
    )(c.T, w, bias.reshape(1, n))


def _norm_mod_to_scratch(x_ref, gain_ref, scale_ref, shift_ref, h_scr):
    rows = 64

    def body(i, carry):
        r0 = pl.multiple_of(i * rows, rows)
        x = x_ref[pl.ds(r0, rows), :]
        y = x * lax.rsqrt(jnp.mean(x * x, axis=-1, keepdims=True) + EPS) * gain_ref[...]
        h_scr[pl.ds(r0, rows), :] = (y * (1.0 + scale_ref[0]) + shift_ref[0]).astype(BF16)
        return carry

    lax.fori_loop(0, x_ref.shape[0] // rows, body, 0)


INPROJ_ROWS = 128
INPROJ_COLS = 256


def _inproj_logf_kernel(x_ref, gain_ref, scale_ref, shift_ref, w_ref, lbt_ref, o_ref, h_scr, *, layer):
    @pl.when(pl.program_id(1) == 0)
    def _():
        _norm_mod_to_scratch(x_ref, gain_ref, scale_ref, shift_ref, h_scr)

    t = lbt_ref[...]
    e = jnp.exp(t - jnp.max(t, axis=0, keepdims=True))
    lb = jnp.sum(e[:layer + 1], axis=0, keepdims=True) / jnp.sum(e, axis=0, keepdims=True)
    for r0 in range(0, h_scr.shape[0], INPROJ_ROWS):
        for c0 in range(0, w_ref.shape[1], INPROJ_COLS):
            z = jnp.dot(h_scr[r0:r0 + INPROJ_ROWS, :], w_ref[:, c0:c0 + INPROJ_COLS], preferred_element_type=F32)
            lbc = lb[:, c0:c0 + INPROJ_COLS]
            o_ref[r0:r0 + INPROJ_ROWS, c0:c0 + INPROJ_COLS] = jnp.log(lbc + (1.0 - lbc) * _sigmoid(z))


def _inproj_act_kernel(x_ref, gain_ref, scale_ref, shift_ref, w_ref, o_ref, h_scr):
    j = pl.program_id(1)

    @pl.when(j == 0)
    def _():
        _norm_mod_to_scratch(x_ref, gain_ref, scale_ref, shift_ref, h_scr)

    z = jnp.dot(h_scr[...], w_ref[...], preferred_element_type=F32)

    @pl.when((j == 4) | (j == 6))
    def _():
        o_ref[...] = _silu(z).astype(BF16)

    @pl.when((j < 4) | (j == 5) | (j == 7))
    def _():
        o_ref[...] = z.astype(BF16)

    @pl.when(j == 8)
    def _():
        o_ref[...] = _gelu_tanh(z).astype(BF16)


def _inproj(x2, gain, scale, shift, w_bf, lb_table, layer, seq):
    n, d = x2.shape
    bsz = n // seq
    tn = 1024
    gain2 = gain.reshape(1, d)
    scale3 = scale.reshape(bsz, 1, d)
    shift3 = shift.reshape(bsz, 1, d)

    def common(tm):
        per_batch = pl.BlockSpec((1, 1, d), lambda i, j: (i // (seq // tm), 0, 0))
        return [pl.BlockSpec((tm, d), lambda i, j: (i, 0)), pl.BlockSpec((1, d), lambda i, j: (0, 0)),
                per_batch, per_batch]

    n_lb = lb_table.shape[0]
    tm = 1024
    logf = pl.pallas_call(
        functools.partial(_inproj_logf_kernel, layer=layer),
        grid=(n // tm, 2),
        in_specs=common(tm) + [pl.BlockSpec((d, tn), lambda i, j: (0, j + 2)),
                               pl.BlockSpec((n_lb, tn), lambda i, j: (0, j))],
        out_specs=pl.BlockSpec((tm, tn), lambda i, j: (i, j)),
        out_shape=jax.ShapeDtypeStruct((n, 2 * tn), F32),
        scratch_shapes=[pltpu.VMEM((tm, d), BF16)],
        compiler_params=_params("parallel", "arbitrary"),
        name="inproj_logf",
    )(x2, gain2, scale3, shift3, w_bf, lb_table)
    tm = 512
    acts = pl.pallas_call(
        _inproj_act_kernel,
        grid=(n // tm, 9),
        in_specs=common(tm) + [pl.BlockSpec(
            (d, tn), lambda i, j: (0, jnp.where(j < 4, j + 7, jnp.where(j < 6, j - 4, j - 2))))],
        out_specs=pl.BlockSpec((tm, tn), lambda i, j: (i, j)),
        out_shape=jax.ShapeDtypeStruct((n, 9 * tn), BF16),
        scratch_shapes=[pltpu.VMEM((tm, d), BF16)],
        compiler_params=_params("parallel", "arbitrary"),
        name="inproj_act",
    )(x2, gain2, scale3, shift3, w_bf)
    return logf, acts


def _hgrn_level_halves(chunk):
    return [chunk >> (i + 1) for i in range(chunk.bit_length() - 1)]


def _hgrn_kernel(q_ref, v_ref, g_ref, lff_ref, lfb_ref, gain_ref, o_ref,
                 oacc, qf_scr, kf_scr, qb_scr, kb_scr, *, chunk, group):
    seq = q_ref.shape[1]
    nchunks = seq // chunk
    halves = _hgrn_level_halves(chunk)
    nt = (((1,), (1,)), ((), ()))
    tn = (((0,), (0,)), ((), ()))

    row = lax.broadcasted_iota(jnp.int32, (chunk, LANES), 0)
    ti = lax.broadcasted_iota(jnp.int32, (chunk, chunk), 0)
    si = lax.broadcasted_iota(jnp.int32, (chunk, chunk), 1)
    txs = ti ^ si
    tri_f = jnp.where(si <= ti, 1.0, 0.0).astype(BF16)
    tri_b = jnp.where(si >= ti, 1.0, 0.0).astype(BF16)

    def cumulative(lf, tri):
        hi = lf.astype(BF16)
        mid = (lf - hi.astype(F32)).astype(BF16)
        r = jnp.dot(tri, jnp.concatenate([hi, mid], axis=1), preferred_element_type=F32)
        return r[:, :LANES] + r[:, LANES:]

    lid = functools.reduce(lambda a, b: a + b, [(txs >= h).astype(jnp.int32) for h in halves])

    def boundary_diff(cum, m, reverse):
        pieces = []
        for base in range(0, chunk, 2 * m):
            bnd = base + (m if reverse else m - 1)
            pieces.append(jnp.broadcast_to(cum[bnd:bnd + 1, :], (2 * m, LANES)))
        return cum - jnp.concatenate(pieces, axis=0)

    def intra(c):
        r0 = pl.multiple_of(c * chunk, chunk)
        q = q_ref[0, pl.ds(r0, chunk), :].astype(F32)
        v = v_ref[0, pl.ds(r0, chunk), :]
        lff = lff_ref[0, pl.ds(r0, chunk), :]
        lfb = lfb_ref[0, pl.ds(r0, chunk), :]
        cum_f = cumulative(lff, tri_f)
        cum_b = cumulative(lfb, tri_b)
        ff = jnp.exp(lff)
        fb = jnp.exp(lfb)
        kf = 1.0 - ff
        kb = 1.0 - fb

        p0 = lax.dot_general(q.astype(BF16), (kf + kb).astype(BF16), nt, preferred_element_type=F32)
        scores = jnp.where(lid == 0, p0, 0.0)
        for m in halves:
            later = (row & m) != 0
            if 2 * m >= SUBLANES:
                d_f = boundary_diff(cum_f, m, False)
                d_b = boundary_diff(cum_b, m, True)
                x = q * jnp.exp(jnp.where(later, d_f, d_b))
                y = jnp.where(later, kb, kf) * jnp.exp(-jnp.where(later, d_b, d_f))
            elif m == 2:
                r4 = row & 3
                prev = lambda a: pltpu.roll(a, 1, 0)
                nxt = lambda a: pltpu.roll(a, chunk - 1, 0)
                x = q * jnp.where(r4 == 0, fb * nxt(fb), jnp.where(r4 == 1, fb,
                                  jnp.where(r4 == 2, ff, ff * prev(ff))))
                y = jnp.where(r4 == 0, kf * nxt(ff), jnp.where(r4 == 1, kf,
                              jnp.where(r4 == 2, kb, kb * prev(fb))))
            else:
                x = q * jnp.where(later, ff, fb)
                y = jnp.where(later, kb, kf)
            pm = lax.dot_general(x.astype(BF16), y.astype(BF16), nt, preferred_element_type=F32)
            scores = jnp.where(lid == m.bit_length(), pm, scores)
        oacc[pl.ds(r0, chunk), :] = jnp.dot(scores.astype(BF16), v, preferred_element_type=F32)
        tot_f = cum_f[chunk - 1:chunk, :]
        tot_b = cum_b[0:1, :]
        qf_scr[pl.ds(r0, chunk), :] = (q * jnp.exp(cum_f)).astype(BF16)
        kf_scr[pl.ds(r0, chunk), :] = (kf * jnp.exp(tot_f - cum_f)).astype(BF16)
        qb_scr[pl.ds(r0, chunk), :] = (q * jnp.exp(cum_b)).astype(BF16)
        kb_scr[pl.ds(r0, chunk), :] = (kb * jnp.exp(tot_b - cum_b)).astype(BF16)

    def intra_body(i, carry):
        for u in range(group):
            intra(i * group + u)
        return carry

    lax.fori_loop(0, nchunks // group, intra_body, 0)

    def apply_state(c, st, q_scr, k_scr, lf_ref):
        r0 = pl.multiple_of(c * chunk, chunk)
        o = oacc[pl.ds(r0, chunk), :] + lax.dot_general(
            q_scr[pl.ds(r0, chunk), :], st.astype(BF16), nt, preferred_element_type=F32)
        tot = jnp.sum(lf_ref[0, pl.ds(r0, chunk), :], axis=0, keepdims=True)
        st = st * jnp.exp(tot) + lax.dot_general(
            v_ref[0, pl.ds(r0, chunk), :], k_scr[pl.ds(r0, chunk), :], tn, preferred_element_type=F32)
        return r0, o, st

    def finish(r0, o):
        y = o * lax.rsqrt(jnp.mean(o * o, axis=-1, keepdims=True) + EPS) * gain_ref[0]
        o_ref[0, pl.ds(r0, chunk), :] = (y * g_ref[0, pl.ds(r0, chunk), :].astype(F32)).astype(BF16)

    def sweep(final):
        def body(i, states):
            st_f, st_b = states
            rf, of, st_f = apply_state(i, st_f, qf_scr, kf_scr, lff_ref)
            rb, ob, st_b = apply_state(nchunks - 1 - i, st_b, qb_scr, kb_scr, lfb_ref)
            if final:
                finish(rf, of)
                finish(rb, ob)
            else:
                oacc[pl.ds(rf, chunk), :] = of
                oacc[pl.ds(rb, chunk), :] = ob
            return st_f, st_b
        return body

    zero = jnp.zeros((LANES, LANES), F32)
    states = lax.fori_loop(0, nchunks // 2, sweep(False), (zero, zero), unroll=2)
    lax.fori_loop(nchunks // 2, nchunks, sweep(True), states, unroll=2)


def _hgrn(acts3, logf3, hg_norm):
    bsz, seq, _ = acts3.shape
    col = lambda off: pl.BlockSpec((1, seq, LANES), lambda b, h: (b, 0, off + h))
    return pl.pallas_call(
        functools.partial(_hgrn_kernel, chunk=HG_CHUNK, group=4),
        grid=(bsz, HG_HEADS),
        in_specs=[col(ACT_Q), col(ACT_V), col(ACT_G), col(0), col(HG_HEADS),
                  pl.BlockSpec((1, 1, LANES), lambda b, h: (h, 0, 0))],
        out_specs=pl.BlockSpec((1, seq, LANES), lambda b, h: (b, 0, h)),
        out_shape=jax.ShapeDtypeStruct((bsz, seq, HG_HEADS * LANES), BF16),
        scratch_shapes=[pltpu.VMEM((seq, LANES), F32)] + [pltpu.VMEM((seq, LANES), BF16)] * 4,
        compiler_params=_params("parallel", "parallel"),
        name="hgrn2",
    )(acts3, acts3, acts3, logf3, logf3, hg_norm.reshape(HG_HEADS, 1, LANES))


def _scan_tile(a, u, reverse):
    rows = a.shape[0]
    row = lax.broadcasted_iota(jnp.int32, a.shape, 0)
    d = 1
    while d < rows:
        if reverse:
            ok = row < rows - d
            shift = rows - d
        else:
            ok = row >= d
            shift = d
        a_s = jnp.where(ok, pltpu.roll(a, shift, 0), 1.0)
        u_s = jnp.where(ok, pltpu.roll(u, shift, 0), 0.0)
        u = u + a * u_s
        a = a * a_s
        d *= 2
    return a, u


def _rglru_kernel(rx_ref, gy_ref, cw_ref, cb_ref, wa_ref, ba_ref, wx_ref, bx_ref, lam_ref, o_ref,
                  xpad, hf_scr, ab_scr, ub_scr, *, tile):
    seq = rx_ref.shape[1]
    ntiles = seq // tile
    halo = 8

    xpad[0:halo, :] = jnp.zeros((halo, LANES), F32)
    xpad[halo + seq:halo + seq + halo, :] = jnp.zeros((halo, LANES), F32)
    xpad[halo:halo + seq, :] = rx_ref[0].astype(F32)

    lam = lam_ref[...]
    neg_c_softplus = -RG_C * (jnp.maximum(-lam, 0.0) + jnp.log(1.0 + jnp.exp(-jnp.abs(lam))))

    def gates(xcb, z):
        r = _sigmoid(jnp.dot(xcb, wa_ref[z, 0], preferred_element_type=F32) + ba_ref[z, 0])
        gi = _sigmoid(jnp.dot(xcb, wx_ref[z, 0], preferred_element_type=F32) + bx_ref[z, 0])
        a = jnp.exp(r * neg_c_softplus[z:z + 1, :])
        return a, jnp.sqrt(1.0 - a * a) * gi

    def fwd_body(i, carry):
        t0 = pl.multiple_of(i * tile, tile)
        ext = xpad[pl.ds(t0, tile + 2 * halo), :]
        n_ext = tile + 2 * halo
        tap = lambda off: pltpu.roll(ext, (-off) % n_ext, 0)[halo:halo + tile]
        xc = (cw_ref[0:1, :] * tap(-2) + cw_ref[1:2, :] * tap(-1) + cw_ref[2:3, :] * ext[halo:halo + tile]
              + cw_ref[3:4, :] * tap(1) + cb_ref[...])
        xcb = xc.astype(BF16)
        a_f, s_f = gates(xcb, 0)
        a_b, s_b = gates(xcb, 1)
        ap, hl = _scan_tile(a_f, s_f * xc, reverse=False)
        h = hl + ap * carry
        hf_scr[pl.ds(t0, tile), :] = h
        ab_scr[pl.ds(t0, tile), :] = a_b
        ub_scr[pl.ds(t0, tile), :] = s_b * xc
        return h[tile - 1:tile, :]

    lax.fori_loop(0, ntiles, fwd_body, jnp.zeros((1, LANES), F32))

    def bwd_body(i, carry):
        t0 = pl.multiple_of((ntiles - 1 - i) * tile, tile)
        ap, hl = _scan_tile(ab_scr[pl.ds(t0, tile), :], ub_scr[pl.ds(t0, tile), :], reverse=True)
        h = hl + ap * carry
        o_ref[0, pl.ds(t0, tile), :] = (
            (hf_scr[pl.ds(t0, tile), :] + h) * gy_ref[0, pl.ds(t0, tile), :].astype(F32)).astype(BF16)
        return h[0:1, :]

    lax.fori_loop(0, ntiles, bwd_body, jnp.zeros((1, LANES), F32))


def _rglru(acts3, conv_w, conv_b, w_a, b_a, w_x, b_x, lam):
    bsz, seq, _ = acts3.shape
    width = RG_BLOCKS * LANES
    tile = 256
    rx_off, gy_off = ACT_RX, ACT_GY
    col = lambda off: pl.BlockSpec((1, seq, LANES), lambda b, j: (b, 0, off + j))
    wspec = pl.BlockSpec((2, 1, LANES, LANES), lambda b, j: (0, j, 0, 0))
    bspec = pl.BlockSpec((2, 1, 1, LANES), lambda b, j: (0, j, 0, 0))
    return pl.pallas_call(
        functools.partial(_rglru_kernel, tile=tile),
        grid=(bsz, RG_BLOCKS),
        in_specs=[col(rx_off), col(gy_off),
                  pl.BlockSpec((RG_CONV, LANES), lambda b, j: (0, j)),
                  pl.BlockSpec((1, LANES), lambda b, j: (0, j)),
                  wspec, bspec, wspec, bspec,
                  pl.BlockSpec((2, LANES), lambda b, j: (0, j))],
        out_specs=pl.BlockSpec((1, seq, LANES), lambda b, j: (b, 0, j)),
        out_shape=jax.ShapeDtypeStruct((bsz, seq, width), BF16),
        scratch_shapes=[pltpu.VMEM((seq + 16, LANES), F32),
                        pltpu.VMEM((seq, LANES), F32),
                        pltpu.VMEM((seq, LANES), F32),
                        pltpu.VMEM((seq, LANES), F32)],
        compiler_params=_params("parallel", "parallel"),
        name="rglru",
    )(acts3, acts3, conv_w, conv_b.reshape(1, width), w_a.astype(BF16),
      b_a.reshape(2, RG_BLOCKS, 1, LANES), w_x.astype(BF16), b_x.reshape(2, RG_BLOCKS, 1, LANES), lam)


def _store_token_major(ref, x, row0=0):
    ntok, d = x.shape
    parts = d // LANES
    for c in range(parts):
        ref[pl.ds(row0 * parts + c, ntok, stride=parts), :] = x[:, c * LANES:(c + 1) * LANES]


def _load_token_major(ref, ntok, parts, row0=0):
    return jnp.concatenate([ref[pl.ds(row0 * parts + c, ntok, stride=parts), :] for c in range(parts)], axis=1)


def _mixout_kernel(ya_ref, yb_ref, ga_ref, gb_ref, x_ref, gate_ref, scale_ref, shift_ref, gain_ref,
                   wpa_ref, wpb_ref, wo_ref, wr_ref, x1_ref, h2_ref, eid_ref, ew_ref, cnt_ref, running):
    y_a = jnp.dot(ya_ref[...], wpa_ref[...], preferred_element_type=F32)
    y_b = jnp.dot(yb_ref[...], wpb_ref[...], preferred_element_type=F32)
    merged = _sigmoid(ga_ref[...].astype(F32)) * y_a + _sigmoid(gb_ref[...].astype(F32)) * y_b
    mix = jnp.dot(merged.astype(BF16), wo_ref[...], preferred_element_type=F32)
    x1 = x_ref[...] + gate_ref[0] * mix
    x1_ref[...] = x1
    y = x1 * lax.rsqrt(jnp.mean(x1 * x1, axis=-1, keepdims=True) + EPS) * gain_ref[...]
    h2 = y * (1.0 + scale_ref[0]) + shift_ref[0]
    _store_token_major(h2_ref, h2)

    h_hi = h2.astype(BF16)
    h_lo = (h2 - h_hi.astype(F32)).astype(BF16)
    p_hi = jnp.dot(h_hi, wr_ref[...], preferred_element_type=F32)
    p_lo = jnp.dot(h_lo, wr_ref[...], preferred_element_type=F32)
    logits = (p_hi[:, :LANES] + p_hi[:, LANES:]) + (p_lo[:, :LANES] + p_lo[:, LANES:])
    lane = lax.broadcasted_iota(jnp.int32, logits.shape, 1)
    neg = -jnp.inf
    gl = jnp.where(lane < N_GROUPS, logits, neg)
    gmax = jnp.max(gl, axis=-1, keepdims=True)
    g_sel = jnp.min(jnp.where(gl == gmax, lane, LANES), axis=-1, keepdims=True)
    p_group = 1.0 / jnp.sum(jnp.exp(gl - gmax), axis=-1, keepdims=True)
    lo = N_GROUPS + g_sel * EXPERTS_PER_GROUP
    el = jnp.where((lane >= lo) & (lane < lo + EXPERTS_PER_GROUP), logits, neg)
    v1 = jnp.max(el, axis=-1, keepdims=True)
    i1 = jnp.min(jnp.where(el == v1, lane, LANES), axis=-1, keepdims=True)
    el2 = jnp.where(lane == i1, neg, el)
    v2 = jnp.max(el2, axis=-1, keepdims=True)
    i2 = jnp.min(jnp.where(el2 == v2, lane, LANES), axis=-1, keepdims=True)
    e21 = jnp.exp(v2 - v1)
    w1 = p_group / (1.0 + e21)
    w2 = p_group * e21 / (1.0 + e21)
    ew_ref[...] = jnp.where(lane == 0, w1, jnp.where(lane == 1, w2, 0.0))

    @pl.when(pl.program_id(0) == 0)
    def _():
        running[...] = jnp.zeros(running.shape, F32)

    e1, e2 = i1 - N_GROUPS, i2 - N_GROUPS
    tm = logits.shape[0]
    before = (lax.broadcasted_iota(jnp.int32, (tm, tm), 0)
              > lax.broadcasted_iota(jnp.int32, (tm, tm), 1))
    before = jnp.where(before, 1.0, 0.0).astype(BF16)
    oh1 = jnp.where(lane == e1, 1.0, 0.0)
    oh2 = jnp.where(lane == e2, 1.0, 0.0)
    pre1 = jnp.dot(before, oh1.astype(BF16), preferred_element_type=F32)
    pre2 = jnp.dot(before, oh2.astype(BF16), preferred_element_type=F32)
    cnt1 = jnp.sum(oh1, axis=0, keepdims=True)
    base = running[...]
    r1 = jnp.sum(oh1 * (pre1 + base), axis=-1, keepdims=True).astype(jnp.int32)
    r2 = jnp.sum(oh2 * (pre2 + base + cnt1), axis=-1, keepdims=True).astype(jnp.int32)
    total = base + cnt1 + jnp.sum(oh2, axis=0, keepdims=True)
    running[...] = total
    cnt_ref[...] = total.astype(jnp.int32)
    eid_ref[...] = jnp.where(lane == 0, e1, jnp.where(lane == 1, e2,
                             jnp.where(lane == 2, r1, jnp.where(lane == 3, r2, 0))))


def _mixout(ya, yb, acts, x2, gate1, scale2, shift2, gain2, wpa, wpb, wo, w_route, seq):
    n, d = x2.shape
    bsz = n // seq
    tm = 256
    tpb = seq // tm
    half = ya.shape[1]
    ga_blk = 0
    per_batch = pl.BlockSpec((1, 1, d), lambda i: (i // tpb, 0, 0))
    const = lambda shape: pl.BlockSpec(shape, lambda i: (0, 0), pipeline_mode=pl.Buffered(1))
    return pl.pallas_call(
        _mixout_kernel,
        grid=(n // tm,),
        in_specs=[pl.BlockSpec((tm, half), lambda i: (i, 0)),
                  pl.BlockSpec((tm, half), lambda i: (i, 0)),
                  pl.BlockSpec((tm, d), lambda i: (i, ga_blk)),
                  pl.BlockSpec((tm, d), lambda i: (i, ga_blk + 1)),
                  pl.BlockSpec((tm, d), lambda i: (i, 0)),
                  per_batch, per_batch, per_batch,
                  const((1, d)), const((half, d)), const((half, d)), const((d, d)), const((d, 2 * LANES))],
        out_specs=[pl.BlockSpec((tm, d), lambda i: (i, 0)),
                   pl.BlockSpec((tm * (d // LANES), LANES), lambda i: (i, 0)),
                   pl.BlockSpec((tm, LANES), lambda i: (i, 0)),
                   pl.BlockSpec((tm, LANES), lambda i: (i, 0)),
                   pl.BlockSpec((1, LANES), lambda i: (0, 0))],
        out_shape=[jax.ShapeDtypeStruct((n, d), F32),
                   jax.ShapeDtypeStruct((n * (d // LANES), LANES), F32),
                   jax.ShapeDtypeStruct((n, LANES), jnp.int32),
                   jax.ShapeDtypeStruct((n, LANES), F32),
                   jax.ShapeDtypeStruct((1, LANES), jnp.int32)],
        scratch_shapes=[pltpu.VMEM((1, LANES), F32)],
        compiler_params=_params("arbitrary"),
        name="mixout_route",
    )(ya, yb, acts, acts, x2, gate1.reshape(bsz, 1, d), scale2.reshape(bsz, 1, d),
      shift2.reshape(bsz, 1, d), gain2.reshape(1, d), wpa, wpb, wo, w_route)


def _start_token_gather(idx_ref, base, ntok, parts, src_hbm, dst, sem):
    for r in range(ntok):
        t0 = pl.multiple_of(idx_ref[base + r] * parts, parts)
        pltpu.make_async_copy(src_hbm.at[pl.ds(t0, parts), :], dst.at[pl.ds(r * parts, parts), :], sem).start()


def _wait_token_gather(ntok, parts, src_hbm, dst, sem):
    pltpu.make_async_copy(src_hbm.at[pl.ds(0, ntok * parts), :], dst.at[pl.ds(0, ntok * parts), :], sem).wait()


def _expert_kernel(be_ref, nv_ref, first_ref, next_ref, dest_ref, nused_ref,
                   h_hbm, wg_hbm, wu_hbm, wd_hbm, o_ref,
                   xbuf, xsems, row_tok, stage_g, stage_u, stage_d, wg_bf, wu_bf, wd_bf, wsems, *, layer):
    b = pl.program_id(0)
    nb = pl.num_programs(0)
    n_used = nused_ref[0]
    unroll = 8
    cast_rows = 256

    def weight_copies(e):
        return (pltpu.make_async_copy(wg_hbm.at[layer, e], stage_g, wsems.at[0]),
                pltpu.make_async_copy(wu_hbm.at[layer, e], stage_u, wsems.at[1]),
                pltpu.make_async_copy(wd_hbm.at[layer, e], stage_d, wsems.at[2]))

    parts = xbuf.shape[1] // MOE_BLOCK
    nslot = xbuf.shape[0]
    ahead = nslot - 1

    def gather(blk, slot):
        _start_token_gather(row_tok, blk * MOE_BLOCK, MOE_BLOCK, parts, h_hbm, xbuf.at[slot], xsems.at[slot])

    def wait_gather(slot):
        _wait_token_gather(MOE_BLOCK, parts, h_hbm, xbuf.at[slot], xsems.at[slot])

    @pl.when(b == 0)
    def _():
        for cp in weight_copies(be_ref[0]):
            cp.start()

        def clear(i, carry):
            for u in range(unroll):
                row_tok[i * unroll + u] = 0
            return carry

        lax.fori_loop(0, row_tok.shape[0] // unroll, clear, 0)

        def invert(i, carry):
            for u in range(unroll):
                row_tok[dest_ref[i * unroll + u]] = i * (unroll // TOP_K) + u // TOP_K
            return carry

        lax.fori_loop(0, dest_ref.shape[0] // unroll, invert, 0)
        for q in range(ahead):
            gather(q, q)

    @pl.when(b < n_used + ahead)
    def _():
        wait_gather(b % nslot)

    @pl.when(b < n_used)
    def _():
        slot = b % nslot

        @pl.when(first_ref[b] == 1)
        def _():
            for m, (stage, dst) in enumerate(((stage_g, wg_bf), (stage_u, wu_bf), (stage_d, wd_bf))):
                weight_copies(0)[m].wait()

                def cast(i, carry, stage=stage, dst=dst):
                    r0 = pl.multiple_of(i * cast_rows, cast_rows)
                    dst[pl.ds(r0, cast_rows), :] = stage[pl.ds(r0, cast_rows), :].astype(BF16)
                    return carry

                lax.fori_loop(0, stage.shape[0] // cast_rows, cast, 0)

                @pl.when(next_ref[b] >= 0)
                def _(m=m):
                    weight_copies(next_ref[b])[m].start()

        valid = lax.broadcasted_iota(jnp.int32, (MOE_BLOCK, 1), 0) < nv_ref[b]
        xb = jnp.where(valid, _load_token_major(xbuf.at[slot], MOE_BLOCK, parts), 0.0).astype(BF16)
        gather(b + ahead, (b + ahead) % nslot)
        gate = jnp.dot(xb, wg_bf[...], preferred_element_type=F32)
        up = jnp.dot(xb, wu_bf[...], preferred_element_type=F32)
        hid = (_silu(gate) * up).astype(BF16)
        _store_token_major(o_ref, jnp.dot(hid, wd_bf[...], preferred_element_type=F32))

    @pl.when(b >= n_used)
    def _():
        o_ref[...] = jnp.zeros(o_ref.shape, o_ref.dtype)

    @pl.when(b == nb - 1)
    def _():
        for q in range(ahead):
            @pl.when(nb + q < n_used + ahead)
            def _(q=q):
                wait_gather((nb + q) % nslot)


def _experts(h2_tm, wg, wu, wd, layer, plan):
    p = plan["rows"]
    nb = p // MOE_BLOCK
    d, de = wg.shape[-2:]
    parts = d // LANES
    hbm = pl.BlockSpec(memory_space=pl.ANY)
    grid_spec = pltpu.PrefetchScalarGridSpec(
        num_scalar_prefetch=6,
        grid=(nb,),
        in_specs=[hbm, hbm, hbm, hbm],
        out_specs=pl.BlockSpec((MOE_BLOCK * parts, LANES), lambda b, *_: (b, 0)),
        scratch_shapes=[pltpu.VMEM((EXPERT_X_SLOTS, MOE_BLOCK * parts, LANES), F32),
                        pltpu.SemaphoreType.DMA((EXPERT_X_SLOTS,)),
                        pltpu.SMEM((p + (EXPERT_X_SLOTS - 1) * MOE_BLOCK,), jnp.int32),
                        pltpu.VMEM((d, de), F32), pltpu.VMEM((d, de), F32), pltpu.VMEM((de, d), F32),
                        pltpu.VMEM((d, de), BF16), pltpu.VMEM((d, de), BF16), pltpu.VMEM((de, d), BF16),
                        pltpu.SemaphoreType.DMA((3,))],
    )
    return pl.pallas_call(
        functools.partial(_expert_kernel, layer=layer),
        grid_spec=grid_spec,
        out_shape=jax.ShapeDtypeStruct((p * parts, LANES), F32),
        compiler_params=_params("arbitrary"),
        name="experts",
    )(plan["block_expert"], plan["n_valid"], plan["first"], plan["next_expert"], plan["dest"],
      plan["n_used"], h2_tm, wg, wu, wd)


COMBINE_TILE = 64
COMBINE_BATCHES = 8


def _combine_kernel(pos_ref, y_hbm, x1_ref, ew_ref, gate_ref, gain_ref, o_ref, *scratch):
    i = pl.program_id(0)
    tm = COMBINE_TILE
    nbuf = COMBINE_BATCHES
    rows = TOP_K * tm
    bufs, sems = scratch[:nbuf], scratch[nbuf]

    parts = x1_ref.shape[1] // LANES

    def gather(tile, k):
        _start_token_gather(pos_ref, tile * rows, rows, parts, y_hbm, bufs[k], sems.at[k])

    def wait(k):
        _wait_token_gather(rows, parts, y_hbm, bufs[k], sems.at[k])

    def finish(k):
        sl = slice(k * tm, (k + 1) * tm)
        ew = ew_ref[sl, :]
        moe = ew[:, 0:1] * _load_token_major(bufs[k], tm, parts)
        for j in range(1, TOP_K):
            moe = moe + ew[:, j:j + 1] * _load_token_major(bufs[k], tm, parts, row0=j * tm)
        x2 = x1_ref[sl, :] + gate_ref[0] * moe
        o_ref[sl, :] = x2 * lax.rsqrt(jnp.mean(x2 * x2, axis=-1, keepdims=True) + EPS) * gain_ref[...]

    ahead = nbuf - 1

    @pl.when(i == 0)
    def _():
        for q in range(ahead):
            gather(q, q)

    for q in range(nbuf):
        wait(q)
        gather(nbuf * i + q + ahead, (q + ahead) % nbuf)
        finish(q)

    @pl.when(i == pl.num_programs(0) - 1)
    def _():
        for q in range(ahead):
            wait(q)


def _combine(ys, pos_tiles, x1, ew, gate2, final_gain, seq):
    n, d = x1.shape
    bsz = n // seq
    tm = COMBINE_BATCHES * COMBINE_TILE
    tpb = seq // tm
    grid_spec = pltpu.PrefetchScalarGridSpec(
        num_scalar_prefetch=1,
        grid=(n // tm,),
        in_specs=[pl.BlockSpec(memory_space=pl.ANY),
                  pl.BlockSpec((tm, d), lambda i, pos: (i, 0)),
                  pl.BlockSpec((tm, LANES), lambda i, pos: (i, 0)),
                  pl.BlockSpec((1, 1, d), lambda i, pos: (i // tpb, 0, 0)),
                  pl.BlockSpec((1, d), lambda i, pos: (0, 0))],
        out_specs=pl.BlockSpec((tm, d), lambda i, pos: (i, 0)),
        scratch_shapes=[pltpu.VMEM((TOP_K * COMBINE_TILE * (d // LANES), LANES), F32)] * COMBINE_BATCHES
        + [pltpu.SemaphoreType.DMA((COMBINE_BATCHES,))],
    )
    return pl.pallas_call(
        _combine_kernel,
        grid_spec=grid_spec,
        out_shape=jax.ShapeDtypeStruct((n, d), F32),
        compiler_params=_params("arbitrary"),
        name="combine_norm",
    )(pos_tiles, ys, x1, ew, gate2.reshape(bsz, 1, d), final_gain.reshape(1, d))


def _dispatch_plan(route, counts):
    n = route.shape[0]
    a = n * TOP_K
    eid = route[:, :TOP_K]
    rank = route[:, TOP_K:2 * TOP_K]
    cnt = counts[0, :N_EXPERTS]
    padded = (cnt + MOE_BLOCK - 1) // MOE_BLOCK * MOE_BLOCK
    pad_ends = jnp.cumsum(padded)
    pad_starts = pad_ends - padded
    dest = (pad_starts[eid] + rank).astype(jnp.int32)
    p = ((a + MOE_BLOCK - 1) // MOE_BLOCK + N_EXPERTS) * MOE_BLOCK
    nb = p // MOE_BLOCK
    n_used = (pad_ends[-1] // MOE_BLOCK).astype(jnp.int32)
    blk = jnp.arange(nb, dtype=jnp.int32)
    blk_start = jnp.minimum(blk, n_used - 1) * MOE_BLOCK
    block_expert = jnp.minimum(jnp.sum((pad_ends[None, :] <= blk_start[:, None]).astype(jnp.int32), axis=1),
                               N_EXPERTS - 1).astype(jnp.int32)
    n_valid = jnp.clip(pad_starts[block_expert] + cnt[block_expert] - blk_start, 0, MOE_BLOCK).astype(jnp.int32)
    first = ((blk_start == pad_starts[block_expert]) & (blk < n_used)).astype(jnp.int32)
    nxt_blk = pad_ends[block_expert] // MOE_BLOCK
    next_expert = jnp.where(nxt_blk < n_used, block_expert[jnp.minimum(nxt_blk, nb - 1)], -1).astype(jnp.int32)
    pos_tiles = dest.reshape(n // COMBINE_TILE, COMBINE_TILE, TOP_K).transpose(0, 2, 1).reshape(a)
    pos_tiles = jnp.concatenate(
        [pos_tiles, jnp.zeros(((COMBINE_BATCHES - 1) * TOP_K * COMBINE_TILE,), jnp.int32)])
    plan = dict(block_expert=block_expert, n_valid=n_valid, first=first, next_expert=next_expert,
                dest=dest.reshape(a), n_used=n_used.reshape(1), rows=p)
    return plan, pos_tiles


def kernel(x, c, w_ada, b_ada, norm1, w_in, hg_lb, hg_norm, rg_conv_w, rg_conv_b, rg_w_a, rg_b_a,
           rg_w_x, rg_b_x, rg_lambda, w_proj_a, w_proj_b, w_out, norm2, w_group, w_router, w_gate,
           w_up, w_down, final_norm):
    bsz, seq, d = x.shape
    depth = w_ada.shape[0]
    n = bsz * seq
    x2 = x.reshape(n, d)
    lb_table = hg_lb.reshape(hg_lb.shape[0], -1)
    for layer in range(depth):
        mod = _adaln(c, w_ada[layer], b_ada[layer])
        shift1, scale1, gate1, shift2, scale2, gate2 = jnp.split(mod, 6, axis=-1)
        logf, acts = _inproj(x2, norm1[layer], scale1, shift1, w_in[layer].astype(BF16), lb_table, layer, seq)
        acts3 = acts.reshape(bsz, seq, acts.shape[1])
        ya = _hgrn(acts3, logf.reshape(bsz, seq, logf.shape[1]), hg_norm[layer])
        yb = _rglru(acts3, rg_conv_w[layer], rg_conv_b[layer], rg_w_a[layer], rg_b_a[layer],
                    rg_w_x[layer], rg_b_x[layer], rg_lambda[layer])
        w_route = jnp.concatenate(
            [w_group[layer], w_router[layer].reshape(d, N_EXPERTS),
             jnp.zeros((d, LANES - N_GROUPS - N_EXPERTS), F32)], axis=1)
        w_route_hi = w_route.astype(BF16)
        w_route = jnp.concatenate([w_route_hi, (w_route - w_route_hi.astype(F32)).astype(BF16)], axis=1)
        x1, h2, route, ew, counts = _mixout(
            ya.reshape(n, -1), yb.reshape(n, -1), acts, x2, gate1, scale2, shift2, norm2[layer],
            w_proj_a[layer].astype(BF16), w_proj_b[layer].astype(BF16), w_out[layer].astype(BF16), w_route, seq)
        plan, pos_tiles = _dispatch_plan(route, counts)
        ys = _experts(h2, w_gate, w_up, w_down, layer, plan)
        if layer + 1 < depth:
            raise NotImplementedError("the fused combine + final norm assumes a single layer")
        x2 = _combine(ys, pos_tiles, x1, ew, gate2, final_norm, seq)
    return x2.reshape(bsz, seq, d)
```

```python
import functools

import jax
import jax.numpy as jnp
import numpy as np
from jax import lax
from jax.experimental import pallas as pl
from jax.experimental.pallas import tpu as pltpu

EPS = 1e-6
LANES = 128
SUBLANES = 8
HG_HEADS = 8
RG_BLOCKS = 8
RG_CONV = 4
RG_C = 8.0
N_GROUPS = 4
EXPERTS_PER_GROUP = 8
N_EXPERTS = N_GROUPS * EXPERTS_PER_GROUP
TOP_K = 2
HG_CHUNK = 128
MOE_BLOCK = 128
EXPERT_X_SLOTS = 6
VMEM_LIMIT = 56 * 1024 * 1024
ACT_Q, ACT_V, ACT_G, ACT_RX, ACT_GY = 32, 40, 48, 56, 64

F32 = jnp.float32
BF16 = jnp.bfloat16


def _sigmoid(x):
    return 1.0 / (1.0 + jnp.exp(-x))


def _silu(x):
    return x * _sigmoid(x)


def _gelu_tanh(x):
    return 0.5 * x * (1.0 + jnp.tanh(0.7978845608028654 * (x + 0.044715 * (x * x * x))))


def _params(*sem):
    return pltpu.CompilerParams(dimension_semantics=sem, vmem_limit_bytes=VMEM_LIMIT)


def _adaln_kernel(ct_ref, w_ref, b_ref, o_ref, *, kc):
    kdim, nb = ct_ref.shape
    tn = w_ref.shape[1]

    def body(i, accs):
        k0 = pl.multiple_of(i * kc, kc)
        w = w_ref[pl.ds(k0, kc), :]
        cs = _silu(ct_ref[pl.ds(k0, kc), :])
        return tuple(acc + jnp.sum(w * cs[:, b:b + 1], axis=0, keepdims=True)
                     for b, acc in enumerate(accs))

    accs = lax.fori_loop(0, kdim // kc, body, tuple(jnp.zeros((1, tn), F32) for _ in range(nb)))
    for b, acc in enumerate(accs):
        o_ref[b:b + 1, :] = acc + b_ref[...]


def _adaln(c, w, bias):
    bsz, d = c.shape
    n = w.shape[1]
    tn = 1024
    return pl.pallas_call(
        functools.partial(_adaln_kernel, kc=256),
        grid=(n // tn,),
        in_specs=[pl.BlockSpec((d, bsz), lambda j: (0, 0)),
                  pl.BlockSpec((d, tn), lambda j: (0, j)),
                  pl.BlockSpec((1, tn), lambda j: (0, j))],
        out_specs=pl.BlockSpec((bsz, tn), lambda j: (0, j)),
        out_shape=jax.ShapeDtypeStruct((bsz, n), F32),
        compiler_params=_params("parallel"),
        name="adaln",
    )(c.T, w, bias.reshape(1, n))


def _norm_mod_to_scratch(x_ref, gain_ref, scale_ref, shift_ref, h_scr):
    rows = 64

    def body(i, carry):
        r0 = pl.multiple_of(i * rows, rows)
        x = x_ref[pl.ds(r0, rows), :]
        y = x * lax.rsqrt(jnp.mean(x * x, axis=-1, keepdims=True) + EPS) * gain_ref[...]
        h_scr[pl.ds(r0, rows), :] = (y * (1.0 + scale_ref[0]) + shift_ref[0]).astype(BF16)
        return carry

    lax.fori_loop(0, x_ref.shape[0] // rows, body, 0)


INPROJ_ROWS = 128
INPROJ_COLS = 256


def _inproj_logf_kernel(x_ref, gain_ref, scale_ref, shift_ref, w_ref, lbt_ref, o_ref, h_scr, *, layer):
    @pl.when(pl.program_id(1) == 0)
    def _():
        _norm_mod_to_scratch(x_ref, gain_ref, scale_ref, shift_ref, h_scr)

    t = lbt_ref[...]
    e = jnp.exp(t - jnp.max(t, axis=0, keepdims=True))
    lb = jnp.sum(e[:layer + 1], axis=0, keepdims=True) / jnp.sum(e, axis=0, keepdims=True)
    for r0 in range(0, h_scr.shape[0], INPROJ_ROWS):
        for c0 in range(0, w_ref.shape[1], INPROJ_COLS):
            z = jnp.dot(h_scr[r0:r0 + INPROJ_ROWS, :], w_ref[:, c0:c0 + INPROJ_COLS], preferred_element_type=F32)
            lbc = lb[:, c0:c0 + INPROJ_COLS]
            o_ref[r0:r0 + INPROJ_ROWS, c0:c0 + INPROJ_COLS] = jnp.log(lbc + (1.0 - lbc) * _sigmoid(z))


def _inproj_act_kernel(x_ref, gain_ref, scale_ref, shift_ref, w_ref, o_ref, h_scr):
    j = pl.program_id(1)

    @pl.when(j == 0)
    def _():
        _norm_mod_to_scratch(x_ref, gain_ref, scale_ref, shift_ref, h_scr)

    z = jnp.dot(h_scr[...], w_ref[...], preferred_element_type=F32)

    @pl.when((j == 4) | (j == 6))
    def _():
        o_ref[...] = _silu(z).astype(BF16)

    @pl.when((j < 4) | (j == 5) | (j == 7))
    def _():
        o_ref[...] = z.astype(BF16)

    @pl.when(j == 8)
    def _():
        o_ref[...] = _gelu_tanh(z).astype(BF16)


def _inproj(x2, gain, scale, shift, w_bf, lb_table, layer, seq):
    n, d = x2.shape
    bsz = n // seq
    tn = 1024
    gain2 = gain.reshape(1, d)
    scale3 = scale.reshape(bsz, 1, d)
    shift3 = shift.reshape(bsz, 1, d)

    def common(tm):
        per_batch = pl.BlockSpec((1, 1, d), lambda i, j: (i // (seq // tm), 0, 0))
        return [pl.BlockSpec((tm, d), lambda i, j: (i, 0)), pl.BlockSpec((1, d), lambda i, j: (0, 0)),
                per_batch, per_batch]

    n_lb = lb_table.shape[0]
    tm = 1024
    logf = pl.pallas_call(
        functools.partial(_inproj_logf_kernel, layer=layer),
        grid=(n // tm, 2),
        in_specs=common(tm) + [pl.BlockSpec((d, tn), lambda i, j: (0, j + 2)),
                               pl.BlockSpec((n_lb, tn), lambda i, j: (0, j))],
        out_specs=pl.BlockSpec((tm, tn), lambda i, j: (i, j)),
        out_shape=jax.ShapeDtypeStruct((n, 2 * tn), F32),
        scratch_shapes=[pltpu.VMEM((tm, d), BF16)],
        compiler_params=_params("parallel", "arbitrary"),
        name="inproj_logf",
    )(x2, gain2, scale3, shift3, w_bf, lb_table)
    tm = 512
    acts = pl.pallas_call(
        _inproj_act_kernel,
        grid=(n // tm, 9),
        in_specs=common(tm) + [pl.BlockSpec(
            (d, tn), lambda i, j: (0, jnp.where(j < 4, j + 7, jnp.where(j < 6, j - 4, j - 2))))],
        out_specs=pl.BlockSpec((tm, tn), lambda i, j: (i, j)),
        out_shape=jax.ShapeDtypeStruct((n, 9 * tn), BF16),
        scratch_shapes=[pltpu.VMEM((tm, d), BF16)],
        compiler_params=_params("parallel", "arbitrary"),
        name="inproj_act",
    )(x2, gain2, scale3, shift3, w_bf)
    return logf, acts


def _hgrn_level_halves(chunk):
    return [chunk >> (i + 1) for i in range(chunk.bit_length() - 1)]


def _hgrn_kernel(q_ref, v_ref, g_ref, lff_ref, lfb_ref, gain_ref, o_ref,
                 oacc, qf_scr, kf_scr, qb_scr, kb_scr, *, chunk, group):
    seq = q_ref.shape[1]
    nchunks = seq // chunk
    halves = _hgrn_level_halves(chunk)
    nt = (((1,), (1,)), ((), ()))
    tn = (((0,), (0,)), ((), ()))

    row = lax.broadcasted_iota(jnp.int32, (chunk, LANES), 0)
    ti = lax.broadcasted_iota(jnp.int32, (chunk, chunk), 0)
    si = lax.broadcasted_iota(jnp.int32, (chunk, chunk), 1)
    txs = ti ^ si
    tri_f = jnp.where(si <= ti, 1.0, 0.0).astype(BF16)
    tri_b = jnp.where(si >= ti, 1.0, 0.0).astype(BF16)

    def cumulative(lf, tri):
        hi = lf.astype(BF16)
        mid = (lf - hi.astype(F32)).astype(BF16)
        r = jnp.dot(tri, jnp.concatenate([hi, mid], axis=1), preferred_element_type=F32)
        return r[:, :LANES] + r[:, LANES:]

    lid = functools.reduce(lambda a, b: a + b, [(txs >= h).astype(jnp.int32) for h in halves])

    def boundary_diff(cum, m, reverse):
        pieces = []
        for base in range(0, chunk, 2 * m):
            bnd = base + (m if reverse else m - 1)
            pieces.append(jnp.broadcast_to(cum[bnd:bnd + 1, :], (2 * m, LANES)))
        return cum - jnp.concatenate(pieces, axis=0)

    def intra(c):
        r0 = pl.multiple_of(c * chunk, chunk)
        q = q_ref[0, pl.ds(r0, chunk), :].astype(F32)
        v = v_ref[0, pl.ds(r0, chunk), :]
        lff = lff_ref[0, pl.ds(r0, chunk), :]
        lfb = lfb_ref[0, pl.ds(r0, chunk), :]
        cum_f = cumulative(lff, tri_f)
        cum_b = cumulative(lfb, tri_b)
        ff = jnp.exp(lff)
        fb = jnp.exp(lfb)
        kf = 1.0 - ff
        kb = 1.0 - fb

        p0 = lax.dot_general(q.astype(BF16), (kf + kb).astype(BF16), nt, preferred_element_type=F32)
        scores = jnp.where(lid == 0, p0, 0.0)
        for m in halves:
            later = (row & m) != 0
            if 2 * m >= SUBLANES:
                d_f = boundary_diff(cum_f, m, False)
                d_b = boundary_diff(cum_b, m, True)
                x = q * jnp.exp(jnp.where(later, d_f, d_b))
                y = jnp.where(later, kb, kf) * jnp.exp(-jnp.where(later, d_b, d_f))
            elif m == 2:
                r4 = row & 3
                prev = lambda a: pltpu.roll(a, 1, 0)
                nxt = lambda a: pltpu.roll(a, chunk - 1, 0)
                x = q * jnp.where(r4 == 0, fb * nxt(fb), jnp.where(r4 == 1, fb,
                                  jnp.where(r4 == 2, ff, ff * prev(ff))))
                y = jnp.where(r4 == 0, kf * nxt(ff), jnp.where(r4 == 1, kf,
                              jnp.where(r4 == 2, kb, kb * prev(fb))))
            else:
                x = q * jnp.where(later, ff, fb)
                y = jnp.where(later, kb, kf)
            pm = lax.dot_general(x.astype(BF16), y.astype(BF16), nt, preferred_element_type=F32)
            scores = jnp.where(lid == m.bit_length(), pm, scores)
        oacc[pl.ds(r0, chunk), :] = jnp.dot(scores.astype(BF16), v, preferred_element_type=F32)
        tot_f = cum_f[chunk - 1:chunk, :]
        tot_b = cum_b[0:1, :]
        qf_scr[pl.ds(r0, chunk), :] = (q * jnp.exp(cum_f)).astype(BF16)
        kf_scr[pl.ds(r0, chunk), :] = (kf * jnp.exp(tot_f - cum_f)).astype(BF16)
        qb_scr[pl.ds(r0, chunk), :] = (q * jnp.exp(cum_b)).astype(BF16)
        kb_scr[pl.ds(r0, chunk), :] = (kb * jnp.exp(tot_b - cum_b)).astype(BF16)

    def intra_body(i, carry):
        for u in range(group):
            intra(i * group + u)
        return carry

    lax.fori_loop(0, nchunks // group, intra_body, 0)

    def apply_state(c, st, q_scr, k_scr, lf_ref):
        r0 = pl.multiple_of(c * chunk, chunk)
        o = oacc[pl.ds(r0, chunk), :] + lax.dot_general(
            q_scr[pl.ds(r0, chunk), :], st.astype(BF16), nt, preferred_element_type=F32)
        tot = jnp.sum(lf_ref[0, pl.ds(r0, chunk), :], axis=0, keepdims=True)
        st = st * jnp.exp(tot) + lax.dot_general(
            v_ref[0, pl.ds(r0, chunk), :], k_scr[pl.ds(r0, chunk), :], tn, preferred_element_type=F32)
        return r0, o, st

    def finish(r0, o):
        y = o * lax.rsqrt(jnp.mean(o * o, axis=-1, keepdims=True) + EPS) * gain_ref[0]
        o_ref[0, pl.ds(r0, chunk), :] = (y * g_ref[0, pl.ds(r0, chunk), :].astype(F32)).astype(BF16)

    def sweep(final):
        def body(i, states):
            st_f, st_b = states
            rf, of, st_f = apply_state(i, st_f, qf_scr, kf_scr, lff_ref)
            rb, ob, st_b = apply_state(nchunks - 1 - i, st_b, qb_scr, kb_scr, lfb_ref)
            if final:
                finish(rf, of)
                finish(rb, ob)
            else:
                oacc[pl.ds(rf, chunk), :] = of
                oacc[pl.ds(rb, chunk), :] = ob
            return st_f, st_b
        return body

    zero = jnp.zeros((LANES, LANES), F32)
    states = lax.fori_loop(0, nchunks // 2, sweep(False), (zero, zero), unroll=2)
    lax.fori_loop(nchunks // 2, nchunks, sweep(True), states, unroll=2)


def _hgrn(acts3, logf3, hg_norm):
    bsz, seq, _ = acts3.shape
    col = lambda off: pl.BlockSpec((1, seq, LANES), lambda b, h: (b, 0, off + h))
    return pl.pallas_call(
        functools.partial(_hgrn_kernel, chunk=HG_CHUNK, group=4),
        grid=(bsz, HG_HEADS),
        in_specs=[col(ACT_Q), col(ACT_V), col(ACT_G), col(0), col(HG_HEADS),
                  pl.BlockSpec((1, 1, LANES), lambda b, h: (h, 0, 0))],
        out_specs=pl.BlockSpec((1, seq, LANES), lambda b, h: (b, 0, h)),
        out_shape=jax.ShapeDtypeStruct((bsz, seq, HG_HEADS * LANES), BF16),
        scratch_shapes=[pltpu.VMEM((seq, LANES), F32)] + [pltpu.VMEM((seq, LANES), BF16)] * 4,
        compiler_params=_params("parallel", "parallel"),
        name="hgrn2",
    )(acts3, acts3, acts3, logf3, logf3, hg_norm.reshape(HG_HEADS, 1, LANES))


def _scan_tile(a, u, reverse):
    rows = a.shape[0]
    row = lax.broadcasted_iota(jnp.int32, a.shape, 0)
    d = 1
    while d < rows:
        if reverse:
            ok = row < rows - d
            shift = rows - d
        else:
            ok = row >= d
            shift = d
        a_s = jnp.where(ok, pltpu.roll(a, shift, 0), 1.0)
        u_s = jnp.where(ok, pltpu.roll(u, shift, 0), 0.0)
        u = u + a * u_s
        a = a * a_s
        d *= 2
    return a, u


def _rglru_kernel(rx_ref, gy_ref, cw_ref, cb_ref, wa_ref, ba_ref, wx_ref, bx_ref, lam_ref, o_ref,
                  xpad, hf_scr, ab_scr, ub_scr, *, tile):
    seq = rx_ref.shape[1]
    ntiles = seq // tile
    halo = 8

    xpad[0:halo, :] = jnp.zeros((halo, LANES), F32)
    xpad[halo + seq:halo + seq + halo, :] = jnp.zeros((halo, LANES), F32)
    xpad[halo:halo + seq, :] = rx_ref[0].astype(F32)

    lam = lam_ref[...]
    neg_c_softplus = -RG_C * (jnp.maximum(-lam, 0.0) + jnp.log(1.0 + jnp.exp(-jnp.abs(lam))))

    def gates(xcb, z):
        r = _sigmoid(jnp.dot(xcb, wa_ref[z, 0], preferred_element_type=F32) + ba_ref[z, 0])
        gi = _sigmoid(jnp.dot(xcb, wx_ref[z, 0], preferred_element_type=F32) + bx_ref[z, 0])
        a = jnp.exp(r * neg_c_softplus[z:z + 1, :])
        return a, jnp.sqrt(1.0 - a * a) * gi

    def fwd_body(i, carry):
        t0 = pl.multiple_of(i * tile, tile)
        ext = xpad[pl.ds(t0, tile + 2 * halo), :]
        n_ext = tile + 2 * halo
        tap = lambda off: pltpu.roll(ext, (-off) % n_ext, 0)[halo:halo + tile]
        xc = (cw_ref[0:1, :] * tap(-2) + cw_ref[1:2, :] * tap(-1) + cw_ref[2:3, :] * ext[halo:halo + tile]
              + cw_ref[3:4, :] * tap(1) + cb_ref[...])
        xcb = xc.astype(BF16)
        a_f, s_f = gates(xcb, 0)
        a_b, s_b = gates(xcb, 1)
        ap, hl = _scan_tile(a_f, s_f * xc, reverse=False)
        h = hl + ap * carry
        hf_scr[pl.ds(t0, tile), :] = h
        ab_scr[pl.ds(t0, tile), :] = a_b
        ub_scr[pl.ds(t0, tile), :] = s_b * xc
        return h[tile - 1:tile, :]

    lax.fori_loop(0, ntiles, fwd_body, jnp.zeros((1, LANES), F32))

    def bwd_body(i, carry):
        t0 = pl.multiple_of((ntiles - 1 - i) * tile, tile)
        ap, hl = _scan_tile(ab_scr[pl.ds(t0, tile), :], ub_scr[pl.ds(t0, tile), :], reverse=True)
        h = hl + ap * carry
        o_ref[0, pl.ds(t0, tile), :] = (
            (hf_scr[pl.ds(t0, tile), :] + h) * gy_ref[0, pl.ds(t0, tile), :].astype(F32)).astype(BF16)
        return h[0:1, :]

    lax.fori_loop(0, ntiles, bwd_body, jnp.zeros((1, LANES), F32))


def _rglru(acts3, conv_w, conv_b, w_a, b_a, w_x, b_x, lam):
    bsz, seq, _ = acts3.shape
    width = RG_BLOCKS * LANES
    tile = 256
    rx_off, gy_off = ACT_RX, ACT_GY
    col = lambda off: pl.BlockSpec((1, seq, LANES), lambda b, j: (b, 0, off + j))
    wspec = pl.BlockSpec((2, 1, LANES, LANES), lambda b, j: (0, j, 0, 0))
    bspec = pl.BlockSpec((2, 1, 1, LANES), lambda b, j: (0, j, 0, 0))
    return pl.pallas_call(
        functools.partial(_rglru_kernel, tile=tile),
        grid=(bsz, RG_BLOCKS),
        in_specs=[col(rx_off), col(gy_off),
                  pl.BlockSpec((RG_CONV, LANES), lambda b, j: (0, j)),
                  pl.BlockSpec((1, LANES), lambda b, j: (0, j)),
                  wspec, bspec, wspec, bspec,
                  pl.BlockSpec((2, LANES), lambda b, j: (0, j))],
        out_specs=pl.BlockSpec((1, seq, LANES), lambda b, j: (b, 0, j)),
        out_shape=jax.ShapeDtypeStruct((bsz, seq, width), BF16),
        scratch_shapes=[pltpu.VMEM((seq + 16, LANES), F32),
                        pltpu.VMEM((seq, LANES), F32),
                        pltpu.VMEM((seq, LANES), F32),
                        pltpu.VMEM((seq, LANES), F32)],
        compiler_params=_params("parallel", "parallel"),
        name="rglru",
    )(acts3, acts3, conv_w, conv_b.reshape(1, width), w_a.astype(BF16),
      b_a.reshape(2, RG_BLOCKS, 1, LANES), w_x.astype(BF16), b_x.reshape(2, RG_BLOCKS, 1, LANES), lam)


def _store_token_major(ref, x, row0=0):
    ntok, d = x.shape
    parts = d // LANES
    for c in range(parts):
        ref[pl.ds(row0 * parts + c, ntok, stride=parts), :] = x[:, c * LANES:(c + 1) * LANES]


def _load_token_major(ref, ntok, parts, row0=0):
    return jnp.concatenate([ref[pl.ds(row0 * parts + c, ntok, stride=parts), :] for c in range(parts)], axis=1)


def _mixout_kernel(ya_ref, yb_ref, ga_ref, gb_ref, x_ref, gate_ref, scale_ref, shift_ref, gain_ref,
                   wpa_ref, wpb_ref, wo_ref, wr_ref, x1_ref, h2_ref, route_ref, ew_ref, cnt_ref, running):
    y_a = jnp.dot(ya_ref[...], wpa_ref[...], preferred_element_type=F32)
    y_b = jnp.dot(yb_ref[...], wpb_ref[...], preferred_element_type=F32)
    merged = _sigmoid(ga_ref[...].astype(F32)) * y_a + _sigmoid(gb_ref[...].astype(F32)) * y_b
    mix = jnp.dot(merged.astype(BF16), wo_ref[...], preferred_element_type=F32)
    x1 = x_ref[...] + gate_ref[0] * mix
    x1_ref[...] = x1
    y = x1 * lax.rsqrt(jnp.mean(x1 * x1, axis=-1, keepdims=True) + EPS) * gain_ref[...]
    h2 = y * (1.0 + scale_ref[0]) + shift_ref[0]
    _store_token_major(h2_ref, h2)

    h_hi = h2.astype(BF16)
    h_lo = (h2 - h_hi.astype(F32)).astype(BF16)
    p_hi = jnp.dot(h_hi, wr_ref[...], preferred_element_type=F32)
    p_lo = jnp.dot(h_lo, wr_ref[...], preferred_element_type=F32)
    logits = (p_hi[:, :LANES] + p_hi[:, LANES:]) + (p_lo[:, :LANES] + p_lo[:, LANES:])
    lane = lax.broadcasted_iota(jnp.int32, logits.shape, 1)
    neg = -jnp.inf
    gl = jnp.where(lane < N_GROUPS, logits, neg)
    gmax = jnp.max(gl, axis=-1, keepdims=True)
    g_sel = jnp.min(jnp.where(gl == gmax, lane, LANES), axis=-1, keepdims=True)
    p_group = 1.0 / jnp.sum(jnp.exp(gl - gmax), axis=-1, keepdims=True)
    lo = N_GROUPS + g_sel * EXPERTS_PER_GROUP
    el = jnp.where((lane >= lo) & (lane < lo + EXPERTS_PER_GROUP), logits, neg)
    v1 = jnp.max(el, axis=-1, keepdims=True)
    i1 = jnp.min(jnp.where(el == v1, lane, LANES), axis=-1, keepdims=True)
    el2 = jnp.where(lane == i1, neg, el)
    v2 = jnp.max(el2, axis=-1, keepdims=True)
    i2 = jnp.min(jnp.where(el2 == v2, lane, LANES), axis=-1, keepdims=True)
    e21 = jnp.exp(v2 - v1)
    w1 = p_group / (1.0 + e21)
    w2 = p_group * e21 / (1.0 + e21)
    ew_ref[...] = jnp.where(lane == 0, w1, jnp.where(lane == 1, w2, 0.0))

    @pl.when(pl.program_id(0) == 0)
    def _():
        running[...] = jnp.zeros(running.shape, F32)

    e1, e2 = i1 - N_GROUPS, i2 - N_GROUPS
    tm = logits.shape[0]
    before = (lax.broadcasted_iota(jnp.int32, (tm, tm), 0)
              > lax.broadcasted_iota(jnp.int32, (tm, tm), 1))
    before = jnp.where(before, 1.0, 0.0).astype(BF16)
    oh1 = jnp.where(lane == e1, 1.0, 0.0)
    oh2 = jnp.where(lane == e2, 1.0, 0.0)
    pre1 = jnp.dot(before, oh1.astype(BF16), preferred_element_type=F32)
    pre2 = jnp.dot(before, oh2.astype(BF16), preferred_element_type=F32)
    cnt1 = jnp.sum(oh1, axis=0, keepdims=True)
    base = running[...]
    r1 = jnp.sum(oh1 * (pre1 + base), axis=-1, keepdims=True).astype(jnp.int32)
    r2 = jnp.sum(oh2 * (pre2 + base + cnt1), axis=-1, keepdims=True).astype(jnp.int32)
    total = base + cnt1 + jnp.sum(oh2, axis=0, keepdims=True)
    running[...] = total
    cnt_ref[...] = total.astype(jnp.int32)
    cols = jnp.where(lane == 0, e1, jnp.where(lane == 1, e2,
                     jnp.where(lane == 2, r1 >> 7, jnp.where(lane == 3, r1 & 127,
                               jnp.where(lane == 4, r2 >> 7, jnp.where(lane == 5, r2 & 127, 0))))))
    srow = lax.broadcasted_iota(jnp.int32, (SUBLANES, LANES), 0)
    slane = lax.broadcasted_iota(jnp.int32, (SUBLANES, LANES), 1)
    sel = jnp.where((srow < 2) & (slane == srow), 1.0,
                    jnp.where((srow >= 2) & (srow < 4) & (slane == 2 * srow - 2), 128.0,
                              jnp.where((srow >= 2) & (srow < 4) & (slane == 2 * srow - 1), 1.0, 0.0)))
    route_t = lax.dot_general(sel.astype(BF16), cols.astype(F32).astype(BF16), (((1,), (1,)), ((), ())),
                              preferred_element_type=F32)
    route_ref[...] = route_t.astype(jnp.int32)


def _mixout(ya, yb, acts, x2, gate1, scale2, shift2, gain2, wpa, wpb, wo, w_route, seq):
    n, d = x2.shape
    bsz = n // seq
    tm = 256
    tpb = seq // tm
    half = ya.shape[1]
    ga_blk = 0
    per_batch = pl.BlockSpec((1, 1, d), lambda i: (i // tpb, 0, 0))
    const = lambda shape: pl.BlockSpec(shape, lambda i: (0, 0), pipeline_mode=pl.Buffered(1))
    return pl.pallas_call(
        _mixout_kernel,
        grid=(n // tm,),
        in_specs=[pl.BlockSpec((tm, half), lambda i: (i, 0)),
                  pl.BlockSpec((tm, half), lambda i: (i, 0)),
                  pl.BlockSpec((tm, d), lambda i: (i, ga_blk)),
                  pl.BlockSpec((tm, d), lambda i: (i, ga_blk + 1)),
                  pl.BlockSpec((tm, d), lambda i: (i, 0)),
                  per_batch, per_batch, per_batch,
                  const((1, d)), const((half, d)), const((half, d)), const((d, d)), const((d, 2 * LANES))],
        out_specs=[pl.BlockSpec((tm, d), lambda i: (i, 0)),
                   pl.BlockSpec((tm * (d // LANES), LANES), lambda i: (i, 0)),
                   pl.BlockSpec((SUBLANES, tm), lambda i: (0, i)),
                   pl.BlockSpec((tm, LANES), lambda i: (i, 0)),
                   pl.BlockSpec((1, LANES), lambda i: (0, 0))],
        out_shape=[jax.ShapeDtypeStruct((n, d), F32),
                   jax.ShapeDtypeStruct((n * (d // LANES), LANES), F32),
                   jax.ShapeDtypeStruct((SUBLANES, n), jnp.int32),
                   jax.ShapeDtypeStruct((n, LANES), F32),
                   jax.ShapeDtypeStruct((1, LANES), jnp.int32)],
        scratch_shapes=[pltpu.VMEM((1, LANES), F32)],
        compiler_params=_params("arbitrary"),
        name="mixout_route",
    )(ya, yb, acts, acts, x2, gate1.reshape(bsz, 1, d), scale2.reshape(bsz, 1, d),
      shift2.reshape(bsz, 1, d), gain2.reshape(1, d), wpa, wpb, wo, w_route)


def _start_token_gather(idx_ref, base, ntok, parts, src_hbm, dst, sem):
    for r in range(ntok):
        t0 = pl.multiple_of(idx_ref[base + r] * parts, parts)
        pltpu.make_async_copy(src_hbm.at[pl.ds(t0, parts), :], dst.at[pl.ds(r * parts, parts), :], sem).start()


def _wait_token_gather(ntok, parts, src_hbm, dst, sem):
    pltpu.make_async_copy(src_hbm.at[pl.ds(0, ntok * parts), :], dst.at[pl.ds(0, ntok * parts), :], sem).wait()


def _expert_kernel(be_ref, nv_ref, first_ref, next_ref, dest_ref, nused_ref,
                   h_hbm, wg_hbm, wu_hbm, wd_hbm, o_ref,
                   xbuf, xsems, row_tok, stage_g, stage_u, stage_d, wg_bf, wu_bf, wd_bf, wsems, *, layer):
    b = pl.program_id(0)
    nb = pl.num_programs(0)
    n_used = nused_ref[0]
    unroll = 8
    cast_rows = 256

    def weight_copies(e):
        return (pltpu.make_async_copy(wg_hbm.at[layer, e], stage_g, wsems.at[0]),
                pltpu.make_async_copy(wu_hbm.at[layer, e], stage_u, wsems.at[1]),
                pltpu.make_async_copy(wd_hbm.at[layer, e], stage_d, wsems.at[2]))

    parts = xbuf.shape[1] // MOE_BLOCK
    nslot = xbuf.shape[0]
    ahead = nslot - 1

    def gather(blk, slot):
        _start_token_gather(row_tok, blk * MOE_BLOCK, MOE_BLOCK, parts, h_hbm, xbuf.at[slot], xsems.at[slot])

    def wait_gather(slot):
        _wait_token_gather(MOE_BLOCK, parts, h_hbm, xbuf.at[slot], xsems.at[slot])

    @pl.when(b == 0)
    def _():
        for cp in weight_copies(be_ref[0]):
            cp.start()

        def clear(i, carry):
            for u in range(unroll):
                row_tok[i * unroll + u] = 0
            return carry

        lax.fori_loop(0, row_tok.shape[0] // unroll, clear, 0)

        ntok = dest_ref.shape[0] // TOP_K
        for k in range(TOP_K):
            def invert(i, carry, k=k):
                for u in range(unroll):
                    row_tok[dest_ref[k * ntok + i * unroll + u]] = i * unroll + u
                return carry

            lax.fori_loop(0, ntok // unroll, invert, 0)
        for q in range(ahead):
            gather(q, q)

    @pl.when(b < n_used + ahead)
    def _():
        wait_gather(b % nslot)

    @pl.when(b < n_used)
    def _():
        slot = b % nslot

        @pl.when(first_ref[b] == 1)
        def _():
            for m, (stage, dst) in enumerate(((stage_g, wg_bf), (stage_u, wu_bf), (stage_d, wd_bf))):
                weight_copies(0)[m].wait()

                def cast(i, carry, stage=stage, dst=dst):
                    r0 = pl.multiple_of(i * cast_rows, cast_rows)
                    dst[pl.ds(r0, cast_rows), :] = stage[pl.ds(r0, cast_rows), :].astype(BF16)
                    return carry

                lax.fori_loop(0, stage.shape[0] // cast_rows, cast, 0)

                @pl.when(next_ref[b] >= 0)
                def _(m=m):
                    weight_copies(next_ref[b])[m].start()

        valid = lax.broadcasted_iota(jnp.int32, (MOE_BLOCK, 1), 0) < nv_ref[b]
        xb = jnp.where(valid, _load_token_major(xbuf.at[slot], MOE_BLOCK, parts), 0.0).astype(BF16)
        gather(b + ahead, (b + ahead) % nslot)
        gate = jnp.dot(xb, wg_bf[...], preferred_element_type=F32)
        up = jnp.dot(xb, wu_bf[...], preferred_element_type=F32)
        hid = (_silu(gate) * up).astype(BF16)
        _store_token_major(o_ref, jnp.dot(hid, wd_bf[...], preferred_element_type=F32))

    @pl.when(b >= n_used)
    def _():
        o_ref[...] = jnp.zeros(o_ref.shape, o_ref.dtype)

    @pl.when(b == nb - 1)
    def _():
        for q in range(ahead):
            @pl.when(nb + q < n_used + ahead)
            def _(q=q):
                wait_gather((nb + q) % nslot)


def _experts(h2_tm, wg, wu, wd, layer, plan):
    p = plan["rows"]
    nb = p // MOE_BLOCK
    d, de = wg.shape[-2:]
    parts = d // LANES
    hbm = pl.BlockSpec(memory_space=pl.ANY)
    grid_spec = pltpu.PrefetchScalarGridSpec(
        num_scalar_prefetch=6,
        grid=(nb,),
        in_specs=[hbm, hbm, hbm, hbm],
        out_specs=pl.BlockSpec((MOE_BLOCK * parts, LANES), lambda b, *_: (b, 0)),
        scratch_shapes=[pltpu.VMEM((EXPERT_X_SLOTS, MOE_BLOCK * parts, LANES), F32),
                        pltpu.SemaphoreType.DMA((EXPERT_X_SLOTS,)),
                        pltpu.SMEM((p + (EXPERT_X_SLOTS - 1) * MOE_BLOCK,), jnp.int32),
                        pltpu.VMEM((d, de), F32), pltpu.VMEM((d, de), F32), pltpu.VMEM((de, d), F32),
                        pltpu.VMEM((d, de), BF16), pltpu.VMEM((d, de), BF16), pltpu.VMEM((de, d), BF16),
                        pltpu.SemaphoreType.DMA((3,))],
    )
    return pl.pallas_call(
        functools.partial(_expert_kernel, layer=layer),
        grid_spec=grid_spec,
        out_shape=jax.ShapeDtypeStruct((p * parts, LANES), F32),
        compiler_params=_params("arbitrary"),
        name="experts",
    )(plan["block_expert"], plan["n_valid"], plan["first"], plan["next_expert"], plan["dest"],
      plan["n_used"], h2_tm, wg, wu, wd)


COMBINE_TILE = 64
COMBINE_BATCHES = 8


def _combine_kernel(pos_ref, y_hbm, x1_ref, ew_ref, gate_ref, gain_ref, o_ref, *scratch):
    i = pl.program_id(0)
    tm = COMBINE_TILE
    nbuf = COMBINE_BATCHES
    rows = TOP_K * tm
    bufs, sems = scratch[:nbuf], scratch[nbuf]

    parts = x1_ref.shape[1] // LANES

    def gather(tile, k):
        _start_token_gather(pos_ref, tile * rows, rows, parts, y_hbm, bufs[k], sems.at[k])

    def wait(k):
        _wait_token_gather(rows, parts, y_hbm, bufs[k], sems.at[k])

    def finish(k):
        sl = slice(k * tm, (k + 1) * tm)
        ew = ew_ref[sl, :]
        moe = ew[:, 0:1] * _load_token_major(bufs[k], tm, parts)
        for j in range(1, TOP_K):
            moe = moe + ew[:, j:j + 1] * _load_token_major(bufs[k], tm, parts, row0=j * tm)
        x2 = x1_ref[sl, :] + gate_ref[0] * moe
        o_ref[sl, :] = x2 * lax.rsqrt(jnp.mean(x2 * x2, axis=-1, keepdims=True) + EPS) * gain_ref[...]

    ahead = nbuf - 1

    @pl.when(i == 0)
    def _():
        for q in range(ahead):
            gather(q, q)

    for q in range(nbuf):
        wait(q)
        gather(nbuf * i + q + ahead, (q + ahead) % nbuf)
        finish(q)

    @pl.when(i == pl.num_programs(0) - 1)
    def _():
        for q in range(ahead):
            wait(q)


def _combine(ys, pos_tiles, x1, ew, gate2, final_gain, seq):
    n, d = x1.shape
    bsz = n // seq
    tm = COMBINE_BATCHES * COMBINE_TILE
    tpb = seq // tm
    grid_spec = pltpu.PrefetchScalarGridSpec(
        num_scalar_prefetch=1,
        grid=(n // tm,),
        in_specs=[pl.BlockSpec(memory_space=pl.ANY),
                  pl.BlockSpec((tm, d), lambda i, pos: (i, 0)),
                  pl.BlockSpec((tm, LANES), lambda i, pos: (i, 0)),
                  pl.BlockSpec((1, 1, d), lambda i, pos: (i // tpb, 0, 0)),
                  pl.BlockSpec((1, d), lambda i, pos: (0, 0))],
        out_specs=pl.BlockSpec((tm, d), lambda i, pos: (i, 0)),
        scratch_shapes=[pltpu.VMEM((TOP_K * COMBINE_TILE * (d // LANES), LANES), F32)] * COMBINE_BATCHES
        + [pltpu.SemaphoreType.DMA((COMBINE_BATCHES,))],
    )
    return pl.pallas_call(
        _combine_kernel,
        grid_spec=grid_spec,
        out_shape=jax.ShapeDtypeStruct((n, d), F32),
        compiler_params=_params("arbitrary"),
        name="combine_norm",
    )(pos_tiles, ys, x1, ew, gate2.reshape(bsz, 1, d), final_gain.reshape(1, d))


def _dispatch_plan(route, counts):
    n = route.shape[1]
    a = n * TOP_K
    eid = route[:TOP_K]
    rank = route[TOP_K:2 * TOP_K]
    cnt = counts[0, :N_EXPERTS]
    padded = (cnt + MOE_BLOCK - 1) // MOE_BLOCK * MOE_BLOCK
    pad_ends = jnp.cumsum(padded)
    pad_starts = pad_ends - padded
    experts = jnp.arange(N_EXPERTS, dtype=jnp.int32)
    dest = (jnp.sum(jnp.where(eid[..., None] == experts, pad_starts, 0), axis=-1) + rank).astype(jnp.int32)
    p = ((a + MOE_BLOCK - 1) // MOE_BLOCK + N_EXPERTS) * MOE_BLOCK
    nb = p // MOE_BLOCK
    n_used = (pad_ends[-1] // MOE_BLOCK).astype(jnp.int32)
    blk = jnp.arange(nb, dtype=jnp.int32)
    blk_start = jnp.minimum(blk, n_used - 1) * MOE_BLOCK
    block_expert = jnp.minimum(jnp.sum((pad_ends[None, :] <= blk_start[:, None]).astype(jnp.int32), axis=1),
                               N_EXPERTS - 1).astype(jnp.int32)
    def lookup(table, idx):
        return jnp.sum(jnp.where(idx[..., None] == jnp.arange(table.shape[0], dtype=jnp.int32), table, 0), axis=-1)

    blk_pad_start = lookup(pad_starts, block_expert)
    n_valid = jnp.clip(blk_pad_start + lookup(cnt, block_expert) - blk_start, 0, MOE_BLOCK).astype(jnp.int32)
    first = ((blk_start == blk_pad_start) & (blk < n_used)).astype(jnp.int32)
    nxt_blk = lookup(pad_ends, block_expert) // MOE_BLOCK
    next_expert = jnp.where(nxt_blk < n_used, lookup(block_expert, jnp.minimum(nxt_blk, nb - 1)), -1).astype(jnp.int32)
    pos_tiles = dest.reshape(TOP_K, n // COMBINE_TILE, COMBINE_TILE).transpose(1, 0, 2).reshape(a)
    pos_tiles = jnp.concatenate(
        [pos_tiles, jnp.zeros(((COMBINE_BATCHES - 1) * TOP_K * COMBINE_TILE,), jnp.int32)])
    plan = dict(block_expert=block_expert, n_valid=n_valid, first=first, next_expert=next_expert,
                dest=dest.reshape(a), n_used=n_used.reshape(1), rows=p)
    return plan, pos_tiles


def kernel(x, c, w_ada, b_ada, norm1, w_in, hg_lb, hg_norm, rg_conv_w, rg_conv_b, rg_w_a, rg_b_a,
           rg_w_x, rg_b_x, rg_lambda, w_proj_a, w_proj_b, w_out, norm2, w_group, w_router, w_gate,
           w_up, w_down, final_norm):
    bsz, seq, d = x.shape
    depth = w_ada.shape[0]
    n = bsz * seq
    x2 = x.reshape(n, d)
    lb_table = hg_lb.reshape(hg_lb.shape[0], -1)
    for layer in range(depth):
        mod = _adaln(c, w_ada[layer], b_ada[layer])
        shift1, scale1, gate1, shift2, scale2, gate2 = jnp.split(mod, 6, axis=-1)
        logf, acts = _inproj(x2, norm1[layer], scale1, shift1, w_in[layer].astype(BF16), lb_table, layer, seq)
        acts3 = acts.reshape(bsz, seq, acts.shape[1])
        ya = _hgrn(acts3, logf.reshape(bsz, seq, logf.shape[1]), hg_norm[layer])
        yb = _rglru(acts3, rg_conv_w[layer], rg_conv_b[layer], rg_w_a[layer], rg_b_a[layer],
                    rg_w_x[layer], rg_b_x[layer], rg_lambda[layer])
        w_route = jnp.concatenate(
            [w_group[layer], w_router[layer].reshape(d, N_EXPERTS),
             jnp.zeros((d, LANES - N_GROUPS - N_EXPERTS), F32)], axis=1)
        w_route_hi = w_route.astype(BF16)
        w_route = jnp.concatenate([w_route_hi, (w_route - w_route_hi.astype(F32)).astype(BF16)], axis=1)
        x1, h2, route, ew, counts = _mixout(
            ya.reshape(n, -1), yb.reshape(n, -1), acts, x2, gate1, scale2, shift2, norm2[layer],
            w_proj_a[layer].astype(BF16), w_proj_b[layer].astype(BF16), w_out[layer].astype(BF16), w_route, seq)
        plan, pos_tiles = _dispatch_plan(route, counts)
        ys = _experts(h2, w_gate, w_up, w_down, layer, plan)
        if layer + 1 < depth:
            raise NotImplementedError("the fused combine + final norm assumes a single layer")
        x2 = _combine(ys, pos_tiles, x1, ew, gate2, final_norm, seq)
    return x2.reshape(bsz, seq, d)
```

```python
import functools

import jax
import jax.numpy as jnp
import numpy as np
from jax import lax
from jax.experimental import pallas as pl
from jax.experimental.pallas import tpu as pltpu

EPS = 1e-6
LANES = 128
SUBLANES = 8
HG_HEADS = 8
RG_BLOCKS = 8
RG_CONV = 4
RG_C = 8.0
N_GROUPS = 4
EXPERTS_PER_GROUP = 8
N_EXPERTS = N_GROUPS * EXPERTS_PER_GROUP
TOP_K = 2
HG_CHUNK = 128
MOE_BLOCK = 128
EXPERT_X_SLOTS = 6
VMEM_LIMIT = 56 * 1024 * 1024
ACT_Q, ACT_V, ACT_G, ACT_RX, ACT_GY = 32, 40, 48, 56, 64

F32 = jnp.float32
BF16 = jnp.bfloat16


def _sigmoid(x):
    return 1.0 / (1.0 + jnp.exp(-x))


def _silu(x):
    return x * _sigmoid(x)


def _gelu_tanh(x):
    return 0.5 * x * (1.0 + jnp.tanh(0.7978845608028654 * (x + 0.044715 * (x * x * x))))


def _params(*sem):
    return pltpu.CompilerParams(dimension_semantics=sem, vmem_limit_bytes=VMEM_LIMIT)


def _adaln_kernel(ct_ref, w_ref, b_ref, o_ref, *, kc):
    kdim, nb = ct_ref.shape
    tn = w_ref.shape[1]

    def body(i, accs):
        k0 = pl.multiple_of(i * kc, kc)
        w = w_ref[pl.ds(k0, kc), :]
        cs = _silu(ct_ref[pl.ds(k0, kc), :])
        return tuple(acc + jnp.sum(w * cs[:, b:b + 1], axis=0, keepdims=True)
                     for b, acc in enumerate(accs))

    accs = lax.fori_loop(0, kdim // kc, body, tuple(jnp.zeros((1, tn), F32) for _ in range(nb)))
    for b, acc in enumerate(accs):
        o_ref[b:b + 1, :] = acc + b_ref[...]


def _adaln(c, w, bias):
    bsz, d = c.shape
    n = w.shape[1]
    tn = 1024
    return pl.pallas_call(
        functools.partial(_adaln_kernel, kc=256),
        grid=(n // tn,),
        in_specs=[pl.BlockSpec((d, bsz), lambda j: (0, 0)),
                  pl.BlockSpec((d, tn), lambda j: (0, j)),
                  pl.BlockSpec((1, tn), lambda j: (0, j))],
        out_specs=pl.BlockSpec((bsz, tn), lambda j: (0, j)),
        out_shape=jax.ShapeDtypeStruct((bsz, n), F32),
        compiler_params=_params("parallel"),
        name="adaln",
    )(c.T, w, bias.reshape(1, n))


def _norm_mod_kernel(x_ref, gain_ref, scale_ref, shift_ref, h_ref):
    rows = 64

    def body(i, carry):
        r0 = pl.multiple_of(i * rows, rows)
        x = x_ref[pl.ds(r0, rows), :]
        y = x * lax.rsqrt(jnp.mean(x * x, axis=-1, keepdims=True) + EPS) * gain_ref[...]
        h_ref[pl.ds(r0, rows), :] = (y * (1.0 + scale_ref[0]) + shift_ref[0]).astype(BF16)
        return carry

    lax.fori_loop(0, x_ref.shape[0] // rows, body, 0)


def _cast_weight_tile(w_ref, w_scr):
    rows = 256

    @pl.when(pl.program_id(1) == 0)
    def _():
        def body(i, carry):
            r0 = pl.multiple_of(i * rows, rows)
            w_scr[pl.ds(r0, rows), :] = w_ref[pl.ds(r0, rows), :].astype(BF16)
            return carry

        lax.fori_loop(0, w_ref.shape[0] // rows, body, 0)


def _inproj_logf_kernel(h_ref, w_ref, lbt_ref, o_ref, w_scr, *, layer):
    _cast_weight_tile(w_ref, w_scr)
    t = lbt_ref[...]
    e = jnp.exp(t - jnp.max(t, axis=0, keepdims=True))
    lb = jnp.sum(e[:layer + 1], axis=0, keepdims=True) / jnp.sum(e, axis=0, keepdims=True)
    z = jnp.dot(h_ref[...], w_scr[...], preferred_element_type=F32)
    o_ref[...] = jnp.log(lb + (1.0 - lb) * _sigmoid(z))


def _inproj_act_kernel(h_ref, w_ref, o_ref, w_scr):
    j = pl.program_id(0)
    _cast_weight_tile(w_ref, w_scr)
    z = jnp.dot(h_ref[...], w_scr[...], preferred_element_type=F32)

    @pl.when((j == 4) | (j == 6))
    def _():
        o_ref[...] = _silu(z).astype(BF16)

    @pl.when((j < 4) | (j == 5) | (j == 7))
    def _():
        o_ref[...] = z.astype(BF16)

    @pl.when(j == 8)
    def _():
        o_ref[...] = _gelu_tanh(z).astype(BF16)


def _inproj(x2, gain, scale, shift, w_in, lb_table, layer, seq):
    n, d = x2.shape
    bsz = n // seq
    tn = 1024
    tm = 512
    per_batch = pl.BlockSpec((1, 1, d), lambda i: (i // (seq // tm), 0, 0))
    h = pl.pallas_call(
        _norm_mod_kernel,
        grid=(n // tm,),
        in_specs=[pl.BlockSpec((tm, d), lambda i: (i, 0)), pl.BlockSpec((1, d), lambda i: (0, 0)),
                  per_batch, per_batch],
        out_specs=pl.BlockSpec((tm, d), lambda i: (i, 0)),
        out_shape=jax.ShapeDtypeStruct((n, d), BF16),
        compiler_params=_params("parallel"),
        name="norm_mod",
    )(x2, gain.reshape(1, d), scale.reshape(bsz, 1, d), shift.reshape(bsz, 1, d))

    n_lb = lb_table.shape[0]
    tm = 1024
    logf = pl.pallas_call(
        functools.partial(_inproj_logf_kernel, layer=layer),
        grid=(2, n // tm),
        in_specs=[pl.BlockSpec((tm, d), lambda j, i: (i, 0)),
                  pl.BlockSpec((None, d, tn), lambda j, i: (layer, 0, j + 2)),
                  pl.BlockSpec((n_lb, tn), lambda j, i: (0, j))],
        out_specs=pl.BlockSpec((tm, tn), lambda j, i: (i, j)),
        out_shape=jax.ShapeDtypeStruct((n, 2 * tn), F32),
        scratch_shapes=[pltpu.VMEM((d, tn), BF16)],
        compiler_params=_params("parallel", "arbitrary"),
        name="inproj_logf",
    )(h, w_in, lb_table)
    tm = 512
    acts = pl.pallas_call(
        _inproj_act_kernel,
        grid=(9, n // tm),
        in_specs=[pl.BlockSpec((tm, d), lambda j, i: (i, 0)),
                  pl.BlockSpec((None, d, tn),
                               lambda j, i: (layer, 0, jnp.where(j < 4, j + 7, jnp.where(j < 6, j - 4, j - 2))))],
        out_specs=pl.BlockSpec((tm, tn), lambda j, i: (i, j)),
        out_shape=jax.ShapeDtypeStruct((n, 9 * tn), BF16),
        scratch_shapes=[pltpu.VMEM((d, tn), BF16)],
        compiler_params=_params("parallel", "arbitrary"),
        name="inproj_act",
    )(h, w_in)
    return logf, acts


def _hgrn_level_halves(chunk):
    return [chunk >> (i + 1) for i in range(chunk.bit_length() - 1)]


def _hgrn_kernel(q_ref, v_ref, g_ref, lff_ref, lfb_ref, gain_ref, o_ref,
                 oacc, qf_scr, kf_scr, qb_scr, kb_scr, *, chunk, group):
    seq = q_ref.shape[1]
    nchunks = seq // chunk
    halves = _hgrn_level_halves(chunk)
    nt = (((1,), (1,)), ((), ()))
    tn = (((0,), (0,)), ((), ()))

    row = lax.broadcasted_iota(jnp.int32, (chunk, LANES), 0)
    ti = lax.broadcasted_iota(jnp.int32, (chunk, chunk), 0)
    si = lax.broadcasted_iota(jnp.int32, (chunk, chunk), 1)
    txs = ti ^ si
    tri_f = jnp.where(si <= ti, 1.0, 0.0).astype(BF16)
    tri_b = jnp.where(si >= ti, 1.0, 0.0).astype(BF16)

    def cumulative(lf, tri):
        hi = lf.astype(BF16)
        mid = (lf - hi.astype(F32)).astype(BF16)
        r = jnp.dot(tri, jnp.concatenate([hi, mid], axis=1), preferred_element_type=F32)
        return r[:, :LANES] + r[:, LANES:]

    lid = functools.reduce(lambda a, b: a + b, [(txs >= h).astype(jnp.int32) for h in halves])

    def boundary_diff(cum, m, reverse):
        pieces = []
        for base in range(0, chunk, 2 * m):
            bnd = base + (m if reverse else m - 1)
            pieces.append(jnp.broadcast_to(cum[bnd:bnd + 1, :], (2 * m, LANES)))
        return cum - jnp.concatenate(pieces, axis=0)

    def intra(c):
        r0 = pl.multiple_of(c * chunk, chunk)
        q = q_ref[0, pl.ds(r0, chunk), :].astype(F32)
        v = v_ref[0, pl.ds(r0, chunk), :]
        lff = lff_ref[0, pl.ds(r0, chunk), :]
        lfb = lfb_ref[0, pl.ds(r0, chunk), :]
        cum_f = cumulative(lff, tri_f)
        cum_b = cumulative(lfb, tri_b)
        ff = jnp.exp(lff)
        fb = jnp.exp(lfb)
        kf = 1.0 - ff
        kb = 1.0 - fb

        p0 = lax.dot_general(q.astype(BF16), (kf + kb).astype(BF16), nt, preferred_element_type=F32)
        scores = jnp.where(lid == 0, p0, 0.0)
        for m in halves:
            later = (row & m) != 0
            if 2 * m >= SUBLANES:
                d_f = boundary_diff(cum_f, m, False)
                d_b = boundary_diff(cum_b, m, True)
                x = q * jnp.exp(jnp.where(later, d_f, d_b))
                y = jnp.where(later, kb, kf) * jnp.exp(-jnp.where(later, d_b, d_f))
            elif m == 2:
                r4 = row & 3
                prev = lambda a: pltpu.roll(a, 1, 0)
                nxt = lambda a: pltpu.roll(a, chunk - 1, 0)
                x = q * jnp.where(r4 == 0, fb * nxt(fb), jnp.where(r4 == 1, fb,
                                  jnp.where(r4 == 2, ff, ff * prev(ff))))
                y = jnp.where(r4 == 0, kf * nxt(ff), jnp.where(r4 == 1, kf,
                              jnp.where(r4 == 2, kb, kb * prev(fb))))
            else:
                x = q * jnp.where(later, ff, fb)
                y = jnp.where(later, kb, kf)
            pm = lax.dot_general(x.astype(BF16), y.astype(BF16), nt, preferred_element_type=F32)
            scores = jnp.where(lid == m.bit_length(), pm, scores)
        oacc[pl.ds(r0, chunk), :] = jnp.dot(scores.astype(BF16), v, preferred_element_type=F32)
        tot_f = cum_f[chunk - 1:chunk, :]
        tot_b = cum_b[0:1, :]
        qf_scr[pl.ds(r0, chunk), :] = (q * jnp.exp(cum_f)).astype(BF16)
        kf_scr[pl.ds(r0, chunk), :] = (kf * jnp.exp(tot_f - cum_f)).astype(BF16)
        qb_scr[pl.ds(r0, chunk), :] = (q * jnp.exp(cum_b)).astype(BF16)
        kb_scr[pl.ds(r0, chunk), :] = (kb * jnp.exp(tot_b - cum_b)).astype(BF16)

    def intra_body(i, carry):
        for u in range(group):
            intra(i * group + u)
        return carry

    lax.fori_loop(0, nchunks // group, intra_body, 0)

    def apply_state(c, st, q_scr, k_scr, lf_ref):
        r0 = pl.multiple_of(c * chunk, chunk)
        o = oacc[pl.ds(r0, chunk), :] + lax.dot_general(
            q_scr[pl.ds(r0, chunk), :], st.astype(BF16), nt, preferred_element_type=F32)
        tot = jnp.sum(lf_ref[0, pl.ds(r0, chunk), :], axis=0, keepdims=True)
        st = st * jnp.exp(tot) + lax.dot_general(
            v_ref[0, pl.ds(r0, chunk), :], k_scr[pl.ds(r0, chunk), :], tn, preferred_element_type=F32)
        return r0, o, st

    def finish(r0, o):
        y = o * lax.rsqrt(jnp.mean(o * o, axis=-1, keepdims=True) + EPS) * gain_ref[0]
        o_ref[0, pl.ds(r0, chunk), :] = (y * g_ref[0, pl.ds(r0, chunk), :].astype(F32)).astype(BF16)

    def sweep(final):
        def body(i, states):
            st_f, st_b = states
            rf, of, st_f = apply_state(i, st_f, qf_scr, kf_scr, lff_ref)
            rb, ob, st_b = apply_state(nchunks - 1 - i, st_b, qb_scr, kb_scr, lfb_ref)
            if final:
                finish(rf, of)
                finish(rb, ob)
            else:
                oacc[pl.ds(rf, chunk), :] = of
                oacc[pl.ds(rb, chunk), :] = ob
            return st_f, st_b
        return body

    zero = jnp.zeros((LANES, LANES), F32)
    states = lax.fori_loop(0, nchunks // 2, sweep(False), (zero, zero), unroll=2)
    lax.fori_loop(nchunks // 2, nchunks, sweep(True), states, unroll=2)


def _hgrn(acts3, logf3, hg_norm):
    bsz, seq, _ = acts3.shape
    col = lambda off: pl.BlockSpec((1, seq, LANES), lambda b, h: (b, 0, off + h))
    return pl.pallas_call(
        functools.partial(_hgrn_kernel, chunk=HG_CHUNK, group=4),
        grid=(bsz, HG_HEADS),
        in_specs=[col(ACT_Q), col(ACT_V), col(ACT_G), col(0), col(HG_HEADS),
                  pl.BlockSpec((1, 1, LANES), lambda b, h: (h, 0, 0))],
        out_specs=pl.BlockSpec((1, seq, LANES), lambda b, h: (b, 0, h)),
        out_shape=jax.ShapeDtypeStruct((bsz, seq, HG_HEADS * LANES), BF16),
        scratch_shapes=[pltpu.VMEM((seq, LANES), F32)] + [pltpu.VMEM((seq, LANES), BF16)] * 4,
        compiler_params=_params("parallel", "parallel"),
        name="hgrn2",
    )(acts3, acts3, acts3, logf3, logf3, hg_norm.reshape(HG_HEADS, 1, LANES))


def _scan_tile(a, u, reverse):
    rows = a.shape[0]
    row = lax.broadcasted_iota(jnp.int32, a.shape, 0)
    d = 1
    while d < rows:
        if reverse:
            ok = row < rows - d
            shift = rows - d
        else:
            ok = row >= d
            shift = d
        a_s = jnp.where(ok, pltpu.roll(a, shift, 0), 1.0)
        u_s = jnp.where(ok, pltpu.roll(u, shift, 0), 0.0)
        u = u + a * u_s
        a = a * a_s
        d *= 2
    return a, u


def _rglru_kernel(rx_ref, gy_ref, cw_ref, cb_ref, wa_ref, ba_ref, wx_ref, bx_ref, lam_ref, o_ref,
                  xpad, hf_scr, ab_scr, ub_scr, *, tile):
    seq = rx_ref.shape[1]
    ntiles = seq // tile
    halo = 8

    xpad[0:halo, :] = jnp.zeros((halo, LANES), F32)
    xpad[halo + seq:halo + seq + halo, :] = jnp.zeros((halo, LANES), F32)
    xpad[halo:halo + seq, :] = rx_ref[0].astype(F32)

    lam = lam_ref[...]
    neg_c_softplus = -RG_C * (jnp.maximum(-lam, 0.0) + jnp.log(1.0 + jnp.exp(-jnp.abs(lam))))

    def gates(xcb, z):
        r = _sigmoid(jnp.dot(xcb, wa_ref[z, 0], preferred_element_type=F32) + ba_ref[z, 0])
        gi = _sigmoid(jnp.dot(xcb, wx_ref[z, 0], preferred_element_type=F32) + bx_ref[z, 0])
        a = jnp.exp(r * neg_c_softplus[z:z + 1, :])
        return a, jnp.sqrt(1.0 - a * a) * gi

    def fwd_body(i, carry):
        t0 = pl.multiple_of(i * tile, tile)
        ext = xpad[pl.ds(t0, tile + 2 * halo), :]
        n_ext = tile + 2 * halo
        tap = lambda off: pltpu.roll(ext, (-off) % n_ext, 0)[halo:halo + tile]
        xc = (cw_ref[0:1, :] * tap(-2) + cw_ref[1:2, :] * tap(-1) + cw_ref[2:3, :] * ext[halo:halo + tile]
              + cw_ref[3:4, :] * tap(1) + cb_ref[...])
        xcb = xc.astype(BF16)
        a_f, s_f = gates(xcb, 0)
        a_b, s_b = gates(xcb, 1)
        ap, hl = _scan_tile(a_f, s_f * xc, reverse=False)
        h = hl + ap * carry
        hf_scr[pl.ds(t0, tile), :] = h
        ab_scr[pl.ds(t0, tile), :] = a_b
        ub_scr[pl.ds(t0, tile), :] = s_b * xc
        return h[tile - 1:tile, :]

    lax.fori_loop(0, ntiles, fwd_body, jnp.zeros((1, LANES), F32))

    def bwd_body(i, carry):
        t0 = pl.multiple_of((ntiles - 1 - i) * tile, tile)
        ap, hl = _scan_tile(ab_scr[pl.ds(t0, tile), :], ub_scr[pl.ds(t0, tile), :], reverse=True)
        h = hl + ap * carry
        o_ref[0, pl.ds(t0, tile), :] = (
            (hf_scr[pl.ds(t0, tile), :] + h) * gy_ref[0, pl.ds(t0, tile), :].astype(F32)).astype(BF16)
        return h[0:1, :]

    lax.fori_loop(0, ntiles, bwd_body, jnp.zeros((1, LANES), F32))


def _rglru(acts3, conv_w, conv_b, w_a, b_a, w_x, b_x, lam):
    bsz, seq, _ = acts3.shape
    width = RG_BLOCKS * LANES
    tile = 256
    rx_off, gy_off = ACT_RX, ACT_GY
    col = lambda off: pl.BlockSpec((1, seq, LANES), lambda b, j: (b, 0, off + j))
    wspec = pl.BlockSpec((2, 1, LANES, LANES), lambda b, j: (0, j, 0, 0))
    bspec = pl.BlockSpec((2, 1, 1, LANES), lambda b, j: (0, j, 0, 0))
    return pl.pallas_call(
        functools.partial(_rglru_kernel, tile=tile),
        grid=(bsz, RG_BLOCKS),
        in_specs=[col(rx_off), col(gy_off),
                  pl.BlockSpec((RG_CONV, LANES), lambda b, j: (0, j)),
                  pl.BlockSpec((1, LANES), lambda b, j: (0, j)),
                  wspec, bspec, wspec, bspec,
                  pl.BlockSpec((2, LANES), lambda b, j: (0, j))],
        out_specs=pl.BlockSpec((1, seq, LANES), lambda b, j: (b, 0, j)),
        out_shape=jax.ShapeDtypeStruct((bsz, seq, width), BF16),
        scratch_shapes=[pltpu.VMEM((seq + 16, LANES), F32),
                        pltpu.VMEM((seq, LANES), F32),
                        pltpu.VMEM((seq, LANES), F32),
                        pltpu.VMEM((seq, LANES), F32)],
        compiler_params=_params("parallel", "parallel"),
        name="rglru",
    )(acts3, acts3, conv_w, conv_b.reshape(1, width), w_a.astype(BF16),
      b_a.reshape(2, RG_BLOCKS, 1, LANES), w_x.astype(BF16), b_x.reshape(2, RG_BLOCKS, 1, LANES), lam)


def _store_token_major(ref, x, row0=0):
    ntok, d = x.shape
    parts = d // LANES
    for c in range(parts):
        ref[pl.ds(row0 * parts + c, ntok, stride=parts), :] = x[:, c * LANES:(c + 1) * LANES]


def _load_token_major(ref, ntok, parts, row0=0):
    return jnp.concatenate([ref[pl.ds(row0 * parts + c, ntok, stride=parts), :] for c in range(parts)], axis=1)


def _mixout_kernel(ya_ref, yb_ref, ga_ref, gb_ref, x_ref, gate_ref, scale_ref, shift_ref, gain_ref,
                   wpa_ref, wpb_ref, wo_ref, wr_ref, x1_ref, h2_ref, route_ref, ew_ref, cnt_ref, running):
    y_a = jnp.dot(ya_ref[...], wpa_ref[...], preferred_element_type=F32)
    y_b = jnp.dot(yb_ref[...], wpb_ref[...], preferred_element_type=F32)
    merged = _sigmoid(ga_ref[...].astype(F32)) * y_a + _sigmoid(gb_ref[...].astype(F32)) * y_b
    mix = jnp.dot(merged.astype(BF16), wo_ref[...], preferred_element_type=F32)
    x1 = x_ref[...] + gate_ref[0] * mix
    x1_ref[...] = x1
    y = x1 * lax.rsqrt(jnp.mean(x1 * x1, axis=-1, keepdims=True) + EPS) * gain_ref[...]
    h2 = y * (1.0 + scale_ref[0]) + shift_ref[0]
    _store_token_major(h2_ref, h2)

    h_hi = h2.astype(BF16)
    h_lo = (h2 - h_hi.astype(F32)).astype(BF16)
    p_hi = jnp.dot(h_hi, wr_ref[...], preferred_element_type=F32)
    p_lo = jnp.dot(h_lo, wr_ref[...], preferred_element_type=F32)
    logits = (p_hi[:, :LANES] + p_hi[:, LANES:]) + (p_lo[:, :LANES] + p_lo[:, LANES:])
    lane = lax.broadcasted_iota(jnp.int32, logits.shape, 1)
    neg = -jnp.inf
    gl = jnp.where(lane < N_GROUPS, logits, neg)
    gmax = jnp.max(gl, axis=-1, keepdims=True)
    g_sel = jnp.min(jnp.where(gl == gmax, lane, LANES), axis=-1, keepdims=True)
    p_group = 1.0 / jnp.sum(jnp.exp(gl - gmax), axis=-1, keepdims=True)
    lo = N_GROUPS + g_sel * EXPERTS_PER_GROUP
    el = jnp.where((lane >= lo) & (lane < lo + EXPERTS_PER_GROUP), logits, neg)
    v1 = jnp.max(el, axis=-1, keepdims=True)
    i1 = jnp.min(jnp.where(el == v1, lane, LANES), axis=-1, keepdims=True)
    el2 = jnp.where(lane == i1, neg, el)
    v2 = jnp.max(el2, axis=-1, keepdims=True)
    i2 = jnp.min(jnp.where(el2 == v2, lane, LANES), axis=-1, keepdims=True)
    e21 = jnp.exp(v2 - v1)
    w1 = p_group / (1.0 + e21)
    w2 = p_group * e21 / (1.0 + e21)
    ew_ref[...] = jnp.where(lane == 0, w1, jnp.where(lane == 1, w2, 0.0))

    @pl.when(pl.program_id(0) == 0)
    def _():
        running[...] = jnp.zeros(running.shape, F32)

    e1, e2 = i1 - N_GROUPS, i2 - N_GROUPS
    tm = logits.shape[0]
    before = (lax.broadcasted_iota(jnp.int32, (tm, tm), 0)
              > lax.broadcasted_iota(jnp.int32, (tm, tm), 1))
    before = jnp.where(before, 1.0, 0.0).astype(BF16)
    oh1 = jnp.where(lane == e1, 1.0, 0.0)
    oh2 = jnp.where(lane == e2, 1.0, 0.0)
    pre1 = jnp.dot(before, oh1.astype(BF16), preferred_element_type=F32)
    pre2 = jnp.dot(before, oh2.astype(BF16), preferred_element_type=F32)
    cnt1 = jnp.sum(oh1, axis=0, keepdims=True)
    base = running[...]
    r1 = jnp.sum(oh1 * (pre1 + base), axis=-1, keepdims=True).astype(jnp.int32)
    r2 = jnp.sum(oh2 * (pre2 + base + cnt1), axis=-1, keepdims=True).astype(jnp.int32)
    total = base + cnt1 + jnp.sum(oh2, axis=0, keepdims=True)
    running[...] = total
    cnt_ref[...] = total.astype(jnp.int32)
    cols = jnp.where(lane == 0, e1, jnp.where(lane == 1, e2,
                     jnp.where(lane == 2, r1 >> 7, jnp.where(lane == 3, r1 & 127,
                               jnp.where(lane == 4, r2 >> 7, jnp.where(lane == 5, r2 & 127, 0))))))
    srow = lax.broadcasted_iota(jnp.int32, (SUBLANES, LANES), 0)
    slane = lax.broadcasted_iota(jnp.int32, (SUBLANES, LANES), 1)
    sel = jnp.where((srow < 2) & (slane == srow), 1.0,
                    jnp.where((srow >= 2) & (srow < 4) & (slane == 2 * srow - 2), 128.0,
                              jnp.where((srow >= 2) & (srow < 4) & (slane == 2 * srow - 1), 1.0, 0.0)))
    route_t = lax.dot_general(sel.astype(BF16), cols.astype(F32).astype(BF16), (((1,), (1,)), ((), ())),
                              preferred_element_type=F32)
    route_ref[...] = route_t.astype(jnp.int32)


def _mixout(ya, yb, acts, x2, gate1, scale2, shift2, gain2, wpa, wpb, wo, w_route, seq):
    n, d = x2.shape
    bsz = n // seq
    tm = 256
    tpb = seq // tm
    half = ya.shape[1]
    ga_blk = 0
    per_batch = pl.BlockSpec((1, 1, d), lambda i: (i // tpb, 0, 0))
    const = lambda shape: pl.BlockSpec(shape, lambda i: (0, 0), pipeline_mode=pl.Buffered(1))
    return pl.pallas_call(
        _mixout_kernel,
        grid=(n // tm,),
        in_specs=[pl.BlockSpec((tm, half), lambda i: (i, 0)),
                  pl.BlockSpec((tm, half), lambda i: (i, 0)),
                  pl.BlockSpec((tm, d), lambda i: (i, ga_blk)),
                  pl.BlockSpec((tm, d), lambda i: (i, ga_blk + 1)),
                  pl.BlockSpec((tm, d), lambda i: (i, 0)),
                  per_batch, per_batch, per_batch,
                  const((1, d)), const((half, d)), const((half, d)), const((d, d)), const((d, 2 * LANES))],
        out_specs=[pl.BlockSpec((tm, d), lambda i: (i, 0)),
                   pl.BlockSpec((tm * (d // LANES), LANES), lambda i: (i, 0)),
                   pl.BlockSpec((SUBLANES, tm), lambda i: (0, i)),
                   pl.BlockSpec((tm, LANES), lambda i: (i, 0)),
                   pl.BlockSpec((1, LANES), lambda i: (0, 0))],
        out_shape=[jax.ShapeDtypeStruct((n, d), F32),
                   jax.ShapeDtypeStruct((n * (d // LANES), LANES), F32),
                   jax.ShapeDtypeStruct((SUBLANES, n), jnp.int32),
                   jax.ShapeDtypeStruct((n, LANES), F32),
                   jax.ShapeDtypeStruct((1, LANES), jnp.int32)],
        scratch_shapes=[pltpu.VMEM((1, LANES), F32)],
        compiler_params=_params("arbitrary"),
        name="mixout_route",
    )(ya, yb, acts, acts, x2, gate1.reshape(bsz, 1, d), scale2.reshape(bsz, 1, d),
      shift2.reshape(bsz, 1, d), gain2.reshape(1, d), wpa, wpb, wo, w_route)


def _start_token_gather(idx_ref, base, ntok, parts, src_hbm, dst, sem):
    for r in range(ntok):
        t0 = pl.multiple_of(idx_ref[base + r] * parts, parts)
        pltpu.make_async_copy(src_hbm.at[pl.ds(t0, parts), :], dst.at[pl.ds(r * parts, parts), :], sem).start()


def _wait_token_gather(ntok, parts, src_hbm, dst, sem):
    pltpu.make_async_copy(src_hbm.at[pl.ds(0, ntok * parts), :], dst.at[pl.ds(0, ntok * parts), :], sem).wait()


def _expert_kernel(be_ref, nv_ref, first_ref, next_ref, dest_ref, nused_ref,
                   h_hbm, wg_hbm, wu_hbm, wd_hbm, o_ref,
                   xbuf, xsems, row_tok, stage_g, stage_u, stage_d, wg_bf, wu_bf, wd_bf, wsems, *, layer):
    b = pl.program_id(0)
    nb = pl.num_programs(0)
    n_used = nused_ref[0]
    unroll = 8
    cast_rows = 256

    def weight_copies(e):
        return (pltpu.make_async_copy(wg_hbm.at[layer, e], stage_g, wsems.at[0]),
                pltpu.make_async_copy(wu_hbm.at[layer, e], stage_u, wsems.at[1]),
                pltpu.make_async_copy(wd_hbm.at[layer, e], stage_d, wsems.at[2]))

    parts = xbuf.shape[1] // MOE_BLOCK
    nslot = xbuf.shape[0]
    ahead = nslot - 1

    def gather(blk, slot):
        _start_token_gather(row_tok, blk * MOE_BLOCK, MOE_BLOCK, parts, h_hbm, xbuf.at[slot], xsems.at[slot])

    def wait_gather(slot):
        _wait_token_gather(MOE_BLOCK, parts, h_hbm, xbuf.at[slot], xsems.at[slot])

    @pl.when(b == 0)
    def _():
        for cp in weight_copies(be_ref[0]):
            cp.start()

        def clear(i, carry):
            for u in range(unroll):
                row_tok[i * unroll + u] = 0
            return carry

        lax.fori_loop(0, row_tok.shape[0] // unroll, clear, 0)

        ntok = dest_ref.shape[0] // TOP_K
        for k in range(TOP_K):
            def invert(i, carry, k=k):
                for u in range(unroll):
                    row_tok[dest_ref[k * ntok + i * unroll + u]] = i * unroll + u
                return carry

            lax.fori_loop(0, ntok // unroll, invert, 0)
        for q in range(ahead):
            gather(q, q)

    @pl.when(b < n_used + ahead)
    def _():
        wait_gather(b % nslot)

    @pl.when(b < n_used)
    def _():
        slot = b % nslot

        @pl.when(first_ref[b] == 1)
        def _():
            for m, (stage, dst) in enumerate(((stage_g, wg_bf), (stage_u, wu_bf), (stage_d, wd_bf))):
                weight_copies(0)[m].wait()

                def cast(i, carry, stage=stage, dst=dst):
                    r0 = pl.multiple_of(i * cast_rows, cast_rows)
                    dst[pl.ds(r0, cast_rows), :] = stage[pl.ds(r0, cast_rows), :].astype(BF16)
                    return carry

                lax.fori_loop(0, stage.shape[0] // cast_rows, cast, 0)

                @pl.when(next_ref[b] >= 0)
                def _(m=m):
                    weight_copies(next_ref[b])[m].start()

        valid = lax.broadcasted_iota(jnp.int32, (MOE_BLOCK, 1), 0) < nv_ref[b]
        xb = jnp.where(valid, _load_token_major(xbuf.at[slot], MOE_BLOCK, parts), 0.0).astype(BF16)
        gather(b + ahead, (b + ahead) % nslot)
        gate = jnp.dot(xb, wg_bf[...], preferred_element_type=F32)
        up = jnp.dot(xb, wu_bf[...], preferred_element_type=F32)
        hid = (_silu(gate) * up).astype(BF16)
        _store_token_major(o_ref, jnp.dot(hid, wd_bf[...], preferred_element_type=F32))

    @pl.when(b >= n_used)
    def _():
        o_ref[...] = jnp.zeros(o_ref.shape, o_ref.dtype)

    @pl.when(b == nb - 1)
    def _():
        for q in range(ahead):
            @pl.when(nb + q < n_used + ahead)
            def _(q=q):
                wait_gather((nb + q) % nslot)


def _experts(h2_tm, wg, wu, wd, layer, plan):
    p = plan["rows"]
    nb = p // MOE_BLOCK
    d, de = wg.shape[-2:]
    parts = d // LANES
    hbm = pl.BlockSpec(memory_space=pl.ANY)
    grid_spec = pltpu.PrefetchScalarGridSpec(
        num_scalar_prefetch=6,
        grid=(nb,),
        in_specs=[hbm, hbm, hbm, hbm],
        out_specs=pl.BlockSpec((MOE_BLOCK * parts, LANES), lambda b, *_: (b, 0)),
        scratch_shapes=[pltpu.VMEM((EXPERT_X_SLOTS, MOE_BLOCK * parts, LANES), F32),
                        pltpu.SemaphoreType.DMA((EXPERT_X_SLOTS,)),
                        pltpu.SMEM((p + (EXPERT_X_SLOTS - 1) * MOE_BLOCK,), jnp.int32),
                        pltpu.VMEM((d, de), F32), pltpu.VMEM((d, de), F32), pltpu.VMEM((de, d), F32),
                        pltpu.VMEM((d, de), BF16), pltpu.VMEM((d, de), BF16), pltpu.VMEM((de, d), BF16),
                        pltpu.SemaphoreType.DMA((3,))],
    )
    return pl.pallas_call(
        functools.partial(_expert_kernel, layer=layer),
        grid_spec=grid_spec,
        out_shape=jax.ShapeDtypeStruct((p * parts, LANES), F32),
        compiler_params=_params("arbitrary"),
        name="experts",
    )(plan["block_expert"], plan["n_valid"], plan["first"], plan["next_expert"], plan["dest"],
      plan["n_used"], h2_tm, wg, wu, wd)


COMBINE_TILE = 64
COMBINE_BATCHES = 8


def _combine_kernel(pos_ref, y_hbm, x1_ref, ew_ref, gate_ref, gain_ref, o_ref, *scratch):
    i = pl.program_id(0)
    tm = COMBINE_TILE
    nbuf = COMBINE_BATCHES
    rows = TOP_K * tm
    bufs, sems = scratch[:nbuf], scratch[nbuf]

    parts = x1_ref.shape[1] // LANES

    def gather(tile, k):
        _start_token_gather(pos_ref, tile * rows, rows, parts, y_hbm, bufs[k], sems.at[k])

    def wait(k):
        _wait_token_gather(rows, parts, y_hbm, bufs[k], sems.at[k])

    def finish(k):
        sl = slice(k * tm, (k + 1) * tm)
        ew = ew_ref[sl, :]
        moe = ew[:, 0:1] * _load_token_major(bufs[k], tm, parts)
        for j in range(1, TOP_K):
            moe = moe + ew[:, j:j + 1] * _load_token_major(bufs[k], tm, parts, row0=j * tm)
        x2 = x1_ref[sl, :] + gate_ref[0] * moe
        o_ref[sl, :] = x2 * lax.rsqrt(jnp.mean(x2 * x2, axis=-1, keepdims=True) + EPS) * gain_ref[...]

    ahead = nbuf - 1

    @pl.when(i == 0)
    def _():
        for q in range(ahead):
            gather(q, q)

    for q in range(nbuf):
        wait(q)
        gather(nbuf * i + q + ahead, (q + ahead) % nbuf)
        finish(q)

    @pl.when(i == pl.num_programs(0) - 1)
    def _():
        for q in range(ahead):
            wait(q)


def _combine(ys, pos_tiles, x1, ew, gate2, final_gain, seq):
    n, d = x1.shape
    bsz = n // seq
    tm = COMBINE_BATCHES * COMBINE_TILE
    tpb = seq // tm
    grid_spec = pltpu.PrefetchScalarGridSpec(
        num_scalar_prefetch=1,
        grid=(n // tm,),
        in_specs=[pl.BlockSpec(memory_space=pl.ANY),
                  pl.BlockSpec((tm, d), lambda i, pos: (i, 0)),
                  pl.BlockSpec((tm, LANES), lambda i, pos: (i, 0)),
                  pl.BlockSpec((1, 1, d), lambda i, pos: (i // tpb, 0, 0)),
                  pl.BlockSpec((1, d), lambda i, pos: (0, 0))],
        out_specs=pl.BlockSpec((tm, d), lambda i, pos: (i, 0)),
        scratch_shapes=[pltpu.VMEM((TOP_K * COMBINE_TILE * (d // LANES), LANES), F32)] * COMBINE_BATCHES
        + [pltpu.SemaphoreType.DMA((COMBINE_BATCHES,))],
    )
    return pl.pallas_call(
        _combine_kernel,
        grid_spec=grid_spec,
        out_shape=jax.ShapeDtypeStruct((n, d), F32),
        compiler_params=_params("arbitrary"),
        name="combine_norm",
    )(pos_tiles, ys, x1, ew, gate2.reshape(bsz, 1, d), final_gain.reshape(1, d))


def _dispatch_plan(route, counts):
    n = route.shape[1]
    a = n * TOP_K
    eid = route[:TOP_K]
    rank = route[TOP_K:2 * TOP_K]
    cnt = counts[0, :N_EXPERTS]
    padded = (cnt + MOE_BLOCK - 1) // MOE_BLOCK * MOE_BLOCK
    pad_ends = jnp.cumsum(padded)
    pad_starts = pad_ends - padded
    experts = jnp.arange(N_EXPERTS, dtype=jnp.int32)
    dest = (jnp.sum(jnp.where(eid[..., None] == experts, pad_starts, 0), axis=-1) + rank).astype(jnp.int32)
    p = ((a + MOE_BLOCK - 1) // MOE_BLOCK + N_EXPERTS) * MOE_BLOCK
    nb = p // MOE_BLOCK
    n_used = (pad_ends[-1] // MOE_BLOCK).astype(jnp.int32)
    blk = jnp.arange(nb, dtype=jnp.int32)
    blk_start = jnp.minimum(blk, n_used - 1) * MOE_BLOCK
    block_expert = jnp.minimum(jnp.sum((pad_ends[None, :] <= blk_start[:, None]).astype(jnp.int32), axis=1),
                               N_EXPERTS - 1).astype(jnp.int32)
    def lookup(table, idx):
        return jnp.sum(jnp.where(idx[..., None] == jnp.arange(table.shape[0], dtype=jnp.int32), table, 0), axis=-1)

    blk_pad_start = lookup(pad_starts, block_expert)
    n_valid = jnp.clip(blk_pad_start + lookup(cnt, block_expert) - blk_start, 0, MOE_BLOCK).astype(jnp.int32)
    first = ((blk_start == blk_pad_start) & (blk < n_used)).astype(jnp.int32)
    nxt_blk = lookup(pad_ends, block_expert) // MOE_BLOCK
    next_expert = jnp.where(nxt_blk < n_used, lookup(block_expert, jnp.minimum(nxt_blk, nb - 1)), -1).astype(jnp.int32)
    pos_tiles = dest.reshape(TOP_K, n // COMBINE_TILE, COMBINE_TILE).transpose(1, 0, 2).reshape(a)
    pos_tiles = jnp.concatenate(
        [pos_tiles, jnp.zeros(((COMBINE_BATCHES - 1) * TOP_K * COMBINE_TILE,), jnp.int32)])
    plan = dict(block_expert=block_expert, n_valid=n_valid, first=first, next_expert=next_expert,
                dest=dest.reshape(a), n_used=n_used.reshape(1), rows=p)
    return plan, pos_tiles


def kernel(x, c, w_ada, b_ada, norm1, w_in, hg_lb, hg_norm, rg_conv_w, rg_conv_b, rg_w_a, rg_b_a,
           rg_w_x, rg_b_x, rg_lambda, w_proj_a, w_proj_b, w_out, norm2, w_group, w_router, w_gate,
           w_up, w_down, final_norm):
    bsz, seq, d = x.shape
    depth = w_ada.shape[0]
    n = bsz * seq
    x2 = x.reshape(n, d)
    lb_table = hg_lb.reshape(hg_lb.shape[0], -1)
    for layer in range(depth):
        mod = _adaln(c, w_ada[layer], b_ada[layer])
        shift1, scale1, gate1, shift2, scale2, gate2 = jnp.split(mod, 6, axis=-1)
        logf, acts = _inproj(x2, norm1[layer], scale1, shift1, w_in, lb_table, layer, seq)
        acts3 = acts.reshape(bsz, seq, acts.shape[1])
        ya = _hgrn(acts3, logf.reshape(bsz, seq, logf.shape[1]), hg_norm[layer])
        yb = _rglru(acts3, rg_conv_w[layer], rg_conv_b[layer], rg_w_a[layer], rg_b_a[layer],
                    rg_w_x[layer], rg_b_x[layer], rg_lambda[layer])
        w_route = jnp.concatenate(
            [w_group[layer], w_router[layer].reshape(d, N_EXPERTS),
             jnp.zeros((d, LANES - N_GROUPS - N_EXPERTS), F32)], axis=1)
        w_route_hi = w_route.astype(BF16)
        w_route = jnp.concatenate([w_route_hi, (w_route - w_route_hi.astype(F32)).astype(BF16)], axis=1)
        x1, h2, route, ew, counts = _mixout(
            ya.reshape(n, -1), yb.reshape(n, -1), acts, x2, gate1, scale2, shift2, norm2[layer],
            w_proj_a[layer].astype(BF16), w_proj_b[layer].astype(BF16), w_out[layer].astype(BF16), w_route, seq)
        plan, pos_tiles = _dispatch_plan(route, counts)
        ys = _experts(h2, w_gate, w_up, w_down, layer, plan)
        if layer + 1 < depth:
            raise NotImplementedError("the fused combine + final norm assumes a single layer")
        x2 = _combine(ys, pos_tiles, x1, ew, gate2, final_norm, seq)
    return x2.reshape(bsz, seq, d)
```

```python
import functools

import jax
import jax.numpy as jnp
import numpy as np
from jax import lax
from jax.experimental import pallas as pl
from jax.experimental.pallas import tpu as pltpu

EPS = 1e-6
LANES = 128
SUBLANES = 8
HG_HEADS = 8
RG_BLOCKS = 8
RG_CONV = 4
RG_C = 8.0
N_GROUPS = 4
EXPERTS_PER_GROUP = 8
N_EXPERTS = N_GROUPS * EXPERTS_PER_GROUP
TOP_K = 2
HG_CHUNK = 128
MOE_BLOCK = 128
EXPERT_X_SLOTS = 6
VMEM_LIMIT = 56 * 1024 * 1024
ACT_Q, ACT_V, ACT_G, ACT_RX, ACT_GY = 32, 40, 48, 56, 64

F32 = jnp.float32
BF16 = jnp.bfloat16


def _sigmoid(x):
    return 1.0 / (1.0 + jnp.exp(-x))


def _silu(x):
    return x * _sigmoid(x)


def _gelu_tanh(x):
    return 0.5 * x * (1.0 + jnp.tanh(0.7978845608028654 * (x + 0.044715 * (x * x * x))))


def _params(*sem):
    return pltpu.CompilerParams(dimension_semantics=sem, vmem_limit_bytes=VMEM_LIMIT)


def _adaln_kernel(ct_ref, w_ref, b_ref, o_ref, *, kc):
    kdim, nb = ct_ref.shape
    tn = w_ref.shape[1]

    def body(i, accs):
        k0 = pl.multiple_of(i * kc, kc)
        w = w_ref[pl.ds(k0, kc), :]
        cs = _silu(ct_ref[pl.ds(k0, kc), :])
        return tuple(acc + jnp.sum(w * cs[:, b:b + 1], axis=0, keepdims=True)
                     for b, acc in enumerate(accs))

    accs = lax.fori_loop(0, kdim // kc, body, tuple(jnp.zeros((1, tn), F32) for _ in range(nb)))
    for b, acc in enumerate(accs):
        o_ref[b:b + 1, :] = acc + b_ref[...]


def _adaln(c, w, bias):
    bsz, d = c.shape
    n = w.shape[1]
    tn = 1024
    return pl.pallas_call(
        functools.partial(_adaln_kernel, kc=256),
        grid=(n // tn,),
        in_specs=[pl.BlockSpec((d, bsz), lambda j: (0, 0)),
                  pl.BlockSpec((d, tn), lambda j: (0, j)),
                  pl.BlockSpec((1, tn), lambda j: (0, j))],
        out_specs=pl.BlockSpec((bsz, tn), lambda j: (0, j)),
        out_shape=jax.ShapeDtypeStruct((bsz, n), F32),
        compiler_params=_params("parallel"),
        name="adaln",
    )(c.T, w, bias.reshape(1, n))


def _norm_mod_kernel(x_ref, gain_ref, scale_ref, shift_ref, h_ref):
    rows = 64

    def body(i, carry):
        r0 = pl.multiple_of(i * rows, rows)
        x = x_ref[pl.ds(r0, rows), :]
        y = x * lax.rsqrt(jnp.mean(x * x, axis=-1, keepdims=True) + EPS) * gain_ref[...]
        h_ref[pl.ds(r0, rows), :] = (y * (1.0 + scale_ref[0]) + shift_ref[0]).astype(BF16)
        return carry

    lax.fori_loop(0, x_ref.shape[0] // rows, body, 0)


def _cast_weight_tile(w_ref, w_scr):
    rows = 256

    @pl.when(pl.program_id(1) == 0)
    def _():
        def body(i, carry):
            r0 = pl.multiple_of(i * rows, rows)
            w_scr[pl.ds(r0, rows), :] = w_ref[pl.ds(r0, rows), :].astype(BF16)
            return carry

        lax.fori_loop(0, w_ref.shape[0] // rows, body, 0)


def _inproj_logf_kernel(h_ref, w_ref, lbt_ref, o_ref, w_scr, *, layer):
    _cast_weight_tile(w_ref, w_scr)
    t = lbt_ref[...]
    e = jnp.exp(t - jnp.max(t, axis=0, keepdims=True))
    lb = jnp.sum(e[:layer + 1], axis=0, keepdims=True) / jnp.sum(e, axis=0, keepdims=True)
    z = jnp.dot(h_ref[...], w_scr[...], preferred_element_type=F32)
    o_ref[...] = jnp.log(lb + (1.0 - lb) * _sigmoid(z))


def _inproj_act_kernel(h_ref, w_ref, o_ref, w_scr):
    j = pl.program_id(0)
    _cast_weight_tile(w_ref, w_scr)
    z = jnp.dot(h_ref[...], w_scr[...], preferred_element_type=F32)

    @pl.when((j == 4) | (j == 6))
    def _():
        o_ref[...] = _silu(z).astype(BF16)

    @pl.when((j < 4) | (j == 5) | (j == 7))
    def _():
        o_ref[...] = z.astype(BF16)

    @pl.when(j == 8)
    def _():
        o_ref[...] = _gelu_tanh(z).astype(BF16)


def _inproj(x2, gain, scale, shift, w_in, lb_table, layer, seq):
    n, d = x2.shape
    bsz = n // seq
    tn = 1024
    tm = 512
    per_batch = pl.BlockSpec((1, 1, d), lambda i: (i // (seq // tm), 0, 0))
    h = pl.pallas_call(
        _norm_mod_kernel,
        grid=(n // tm,),
        in_specs=[pl.BlockSpec((tm, d), lambda i: (i, 0)), pl.BlockSpec((1, d), lambda i: (0, 0)),
                  per_batch, per_batch],
        out_specs=pl.BlockSpec((tm, d), lambda i: (i, 0)),
        out_shape=jax.ShapeDtypeStruct((n, d), BF16),
        compiler_params=_params("parallel"),
        name="norm_mod",
    )(x2, gain.reshape(1, d), scale.reshape(bsz, 1, d), shift.reshape(bsz, 1, d))

    n_lb = lb_table.shape[0]
    tm = 1024
    logf = pl.pallas_call(
        functools.partial(_inproj_logf_kernel, layer=layer),
        grid=(2, n // tm),
        in_specs=[pl.BlockSpec((tm, d), lambda j, i: (i, 0)),
                  pl.BlockSpec((None, d, tn), lambda j, i: (layer, 0, j + 2)),
                  pl.BlockSpec((n_lb, tn), lambda j, i: (0, j))],
        out_specs=pl.BlockSpec((tm, tn), lambda j, i: (i, j)),
        out_shape=jax.ShapeDtypeStruct((n, 2 * tn), F32),
        scratch_shapes=[pltpu.VMEM((d, tn), BF16)],
        compiler_params=_params("parallel", "arbitrary"),
        name="inproj_logf",
    )(h, w_in, lb_table)
    tm = 512
    acts = pl.pallas_call(
        _inproj_act_kernel,
        grid=(9, n // tm),
        in_specs=[pl.BlockSpec((tm, d), lambda j, i: (i, 0)),
                  pl.BlockSpec((None, d, tn),
                               lambda j, i: (layer, 0, jnp.where(j < 4, j + 7, jnp.where(j < 6, j - 4, j - 2))))],
        out_specs=pl.BlockSpec((tm, tn), lambda j, i: (i, j)),
        out_shape=jax.ShapeDtypeStruct((n, 9 * tn), BF16),
        scratch_shapes=[pltpu.VMEM((d, tn), BF16)],
        compiler_params=_params("parallel", "arbitrary"),
        name="inproj_act",
    )(h, w_in)
    return logf, acts


def _hgrn_level_halves(chunk):
    return [chunk >> (i + 1) for i in range(chunk.bit_length() - 1)]


def _hgrn_kernel(q_ref, v_ref, g_ref, lff_ref, lfb_ref, gain_ref, o_ref,
                 oacc, qf_scr, kf_scr, qb_scr, kb_scr, *, chunk, group):
    seq = q_ref.shape[1]
    nchunks = seq // chunk
    halves = _hgrn_level_halves(chunk)
    nt = (((1,), (1,)), ((), ()))
    tn = (((0,), (0,)), ((), ()))

    row = lax.broadcasted_iota(jnp.int32, (chunk, LANES), 0)
    ti = lax.broadcasted_iota(jnp.int32, (chunk, chunk), 0)
    si = lax.broadcasted_iota(jnp.int32, (chunk, chunk), 1)
    txs = ti ^ si
    tri_f = jnp.where(si <= ti, 1.0, 0.0).astype(BF16)
    tri_b = jnp.where(si >= ti, 1.0, 0.0).astype(BF16)

    def cumulative(lf, tri):
        hi = lf.astype(BF16)
        mid = (lf - hi.astype(F32)).astype(BF16)
        r = jnp.dot(tri, jnp.concatenate([hi, mid], axis=1), preferred_element_type=F32)
        return r[:, :LANES] + r[:, LANES:]

    lid = functools.reduce(lambda a, b: a + b, [(txs >= h).astype(jnp.int32) for h in halves])

    def boundary_diff(cum, m, reverse):
        pieces = []
        for base in range(0, chunk, 2 * m):
            bnd = base + (m if reverse else m - 1)
            pieces.append(jnp.broadcast_to(cum[bnd:bnd + 1, :], (2 * m, LANES)))
        return cum - jnp.concatenate(pieces, axis=0)

    def intra(c):
        r0 = pl.multiple_of(c * chunk, chunk)
        q = q_ref[0, pl.ds(r0, chunk), :].astype(F32)
        v = v_ref[0, pl.ds(r0, chunk), :]
        lff = lff_ref[0, pl.ds(r0, chunk), :]
        lfb = lfb_ref[0, pl.ds(r0, chunk), :]
        cum_f = cumulative(lff, tri_f)
        cum_b = cumulative(lfb, tri_b)
        ff = jnp.exp(lff)
        fb = jnp.exp(lfb)
        kf = 1.0 - ff
        kb = 1.0 - fb

        p0 = lax.dot_general(q.astype(BF16), (kf + kb).astype(BF16), nt, preferred_element_type=F32)
        scores = jnp.where(lid == 0, p0, 0.0)
        for m in halves:
            later = (row & m) != 0
            if 2 * m >= SUBLANES:
                d_f = boundary_diff(cum_f, m, False)
                d_b = boundary_diff(cum_b, m, True)
                x = q * jnp.exp(jnp.where(later, d_f, d_b))
                y = jnp.where(later, kb, kf) * jnp.exp(-jnp.where(later, d_b, d_f))
            elif m == 2:
                r4 = row & 3
                prev = lambda a: pltpu.roll(a, 1, 0)
                nxt = lambda a: pltpu.roll(a, chunk - 1, 0)
                x = q * jnp.where(r4 == 0, fb * nxt(fb), jnp.where(r4 == 1, fb,
                                  jnp.where(r4 == 2, ff, ff * prev(ff))))
                y = jnp.where(r4 == 0, kf * nxt(ff), jnp.where(r4 == 1, kf,
                              jnp.where(r4 == 2, kb, kb * prev(fb))))
            else:
                x = q * jnp.where(later, ff, fb)
                y = jnp.where(later, kb, kf)
            pm = lax.dot_general(x.astype(BF16), y.astype(BF16), nt, preferred_element_type=F32)
            scores = jnp.where(lid == m.bit_length(), pm, scores)
        oacc[pl.ds(r0, chunk), :] = jnp.dot(scores.astype(BF16), v, preferred_element_type=F32)
        tot_f = cum_f[chunk - 1:chunk, :]
        tot_b = cum_b[0:1, :]
        qf_scr[pl.ds(r0, chunk), :] = (q * jnp.exp(cum_f)).astype(BF16)
        kf_scr[pl.ds(r0, chunk), :] = (kf * jnp.exp(tot_f - cum_f)).astype(BF16)
        qb_scr[pl.ds(r0, chunk), :] = (q * jnp.exp(cum_b)).astype(BF16)
        kb_scr[pl.ds(r0, chunk), :] = (kb * jnp.exp(tot_b - cum_b)).astype(BF16)

    def intra_body(i, carry):
        for u in range(group):
            intra(i * group + u)
        return carry

    lax.fori_loop(0, nchunks // group, intra_body, 0)

    def apply_state(c, st, q_scr, k_scr, lf_ref):
        r0 = pl.multiple_of(c * chunk, chunk)
        o = oacc[pl.ds(r0, chunk), :] + lax.dot_general(
            q_scr[pl.ds(r0, chunk), :], st.astype(BF16), nt, preferred_element_type=F32)
        tot = jnp.sum(lf_ref[0, pl.ds(r0, chunk), :], axis=0, keepdims=True)
        st = st * jnp.exp(tot) + lax.dot_general(
            v_ref[0, pl.ds(r0, chunk), :], k_scr[pl.ds(r0, chunk), :], tn, preferred_element_type=F32)
        return r0, o, st

    def finish(r0, o):
        y = o * lax.rsqrt(jnp.mean(o * o, axis=-1, keepdims=True) + EPS) * gain_ref[0]
        o_ref[0, pl.ds(r0, chunk), :] = (y * g_ref[0, pl.ds(r0, chunk), :].astype(F32)).astype(BF16)

    def sweep(final):
        def body(i, states):
            st_f, st_b = states
            rf, of, st_f = apply_state(i, st_f, qf_scr, kf_scr, lff_ref)
            rb, ob, st_b = apply_state(nchunks - 1 - i, st_b, qb_scr, kb_scr, lfb_ref)
            if final:
                finish(rf, of)
                finish(rb, ob)
            else:
                oacc[pl.ds(rf, chunk), :] = of
                oacc[pl.ds(rb, chunk), :] = ob
            return st_f, st_b
        return body

    zero = jnp.zeros((LANES, LANES), F32)
    states = lax.fori_loop(0, nchunks // 2, sweep(False), (zero, zero), unroll=8)
    lax.fori_loop(nchunks // 2, nchunks, sweep(True), states, unroll=8)


def _hgrn(acts3, logf3, hg_norm):
    bsz, seq, _ = acts3.shape
    col = lambda off: pl.BlockSpec((1, seq, LANES), lambda b, h: (b, 0, off + h))
    return pl.pallas_call(
        functools.partial(_hgrn_kernel, chunk=HG_CHUNK, group=4),
        grid=(bsz, HG_HEADS),
        in_specs=[col(ACT_Q), col(ACT_V), col(ACT_G), col(0), col(HG_HEADS),
                  pl.BlockSpec((1, 1, LANES), lambda b, h: (h, 0, 0))],
        out_specs=pl.BlockSpec((1, seq, LANES), lambda b, h: (b, 0, h)),
        out_shape=jax.ShapeDtypeStruct((bsz, seq, HG_HEADS * LANES), BF16),
        scratch_shapes=[pltpu.VMEM((seq, LANES), F32)] + [pltpu.VMEM((seq, LANES), BF16)] * 4,
        compiler_params=_params("parallel", "parallel"),
        name="hgrn2",
    )(acts3, acts3, acts3, logf3, logf3, hg_norm.reshape(HG_HEADS, 1, LANES))


RG_SCAN_SUB = 32


def _scan_tile(a, u, carry, reverse):
    rows = a.shape[0]
    sub = RG_SCAN_SUB
    pos = lax.broadcasted_iota(jnp.int32, a.shape, 0) & (sub - 1)
    d = 1
    while d < sub:
        if reverse:
            ok = pos < sub - d
            shift = rows - d
        else:
            ok = pos >= d
            shift = d
        a_s = jnp.where(ok, pltpu.roll(a, shift, 0), 1.0)
        u_s = jnp.where(ok, pltpu.roll(u, shift, 0), 0.0)
        u = u + a * u_s
        a = a * a_s
        d *= 2
    spans = list(range(rows // sub))
    pieces = [None] * len(spans)
    for k in (reversed(spans) if reverse else spans):
        hk = u[k * sub:(k + 1) * sub] + a[k * sub:(k + 1) * sub] * carry
        carry = hk[0:1] if reverse else hk[sub - 1:sub]
        pieces[k] = hk
    return jnp.concatenate(pieces, axis=0), carry


def _rglru_kernel(rx_ref, gy_ref, cw_ref, cb_ref, wa_ref, ba_ref, wx_ref, bx_ref, lam_ref, o_ref,
                  xpad, hf_scr, ab_scr, ub_scr, *, tile):
    seq = rx_ref.shape[1]
    ntiles = seq // tile
    halo = 8

    xpad[0:halo, :] = jnp.zeros((halo, LANES), F32)
    xpad[halo + seq:halo + seq + halo, :] = jnp.zeros((halo, LANES), F32)
    xpad[halo:halo + seq, :] = rx_ref[0].astype(F32)

    lam = lam_ref[...]
    neg_c_softplus = -RG_C * (jnp.maximum(-lam, 0.0) + jnp.log(1.0 + jnp.exp(-jnp.abs(lam))))

    def gates(xcb, z):
        r = _sigmoid(jnp.dot(xcb, wa_ref[z, 0], preferred_element_type=F32) + ba_ref[z, 0])
        gi = _sigmoid(jnp.dot(xcb, wx_ref[z, 0], preferred_element_type=F32) + bx_ref[z, 0])
        a = jnp.exp(r * neg_c_softplus[z:z + 1, :])
        return a, jnp.sqrt(1.0 - a * a) * gi

    def fwd_body(i, carry):
        t0 = pl.multiple_of(i * tile, tile)
        ext = xpad[pl.ds(t0, tile + 2 * halo), :]
        n_ext = tile + 2 * halo
        tap = lambda off: pltpu.roll(ext, (-off) % n_ext, 0)[halo:halo + tile]
        xc = (cw_ref[0:1, :] * tap(-2) + cw_ref[1:2, :] * tap(-1) + cw_ref[2:3, :] * ext[halo:halo + tile]
              + cw_ref[3:4, :] * tap(1) + cb_ref[...])
        xcb = xc.astype(BF16)
        a_f, s_f = gates(xcb, 0)
        a_b, s_b = gates(xcb, 1)
        h, carry = _scan_tile(a_f, s_f * xc, carry, reverse=False)
        hf_scr[pl.ds(t0, tile), :] = h
        ab_scr[pl.ds(t0, tile), :] = a_b
        ub_scr[pl.ds(t0, tile), :] = s_b * xc
        return carry

    lax.fori_loop(0, ntiles, fwd_body, jnp.zeros((1, LANES), F32))

    def bwd_body(i, carry):
        t0 = pl.multiple_of((ntiles - 1 - i) * tile, tile)
        h, carry = _scan_tile(ab_scr[pl.ds(t0, tile), :], ub_scr[pl.ds(t0, tile), :], carry, reverse=True)
        o_ref[0, pl.ds(t0, tile), :] = (
            (hf_scr[pl.ds(t0, tile), :] + h) * gy_ref[0, pl.ds(t0, tile), :].astype(F32)).astype(BF16)
        return carry

    lax.fori_loop(0, ntiles, bwd_body, jnp.zeros((1, LANES), F32))


def _rglru(acts3, conv_w, conv_b, w_a, b_a, w_x, b_x, lam):
    bsz, seq, _ = acts3.shape
    width = RG_BLOCKS * LANES
    tile = 256
    rx_off, gy_off = ACT_RX, ACT_GY
    col = lambda off: pl.BlockSpec((1, seq, LANES), lambda b, j: (b, 0, off + j))
    wspec = pl.BlockSpec((2, 1, LANES, LANES), lambda b, j: (0, j, 0, 0))
    bspec = pl.BlockSpec((2, 1, 1, LANES), lambda b, j: (0, j, 0, 0))
    return pl.pallas_call(
        functools.partial(_rglru_kernel, tile=tile),
        grid=(bsz, RG_BLOCKS),
        in_specs=[col(rx_off), col(gy_off),
                  pl.BlockSpec((RG_CONV, LANES), lambda b, j: (0, j)),
                  pl.BlockSpec((1, LANES), lambda b, j: (0, j)),
                  wspec, bspec, wspec, bspec,
                  pl.BlockSpec((2, LANES), lambda b, j: (0, j))],
        out_specs=pl.BlockSpec((1, seq, LANES), lambda b, j: (b, 0, j)),
        out_shape=jax.ShapeDtypeStruct((bsz, seq, width), BF16),
        scratch_shapes=[pltpu.VMEM((seq + 16, LANES), F32),
                        pltpu.VMEM((seq, LANES), F32),
                        pltpu.VMEM((seq, LANES), F32),
                        pltpu.VMEM((seq, LANES), F32)],
        compiler_params=_params("parallel", "parallel"),
        name="rglru",
    )(acts3, acts3, conv_w, conv_b.reshape(1, width), w_a.astype(BF16),
      b_a.reshape(2, RG_BLOCKS, 1, LANES), w_x.astype(BF16), b_x.reshape(2, RG_BLOCKS, 1, LANES), lam)


def _store_token_major(ref, x, row0=0):
    ntok, d = x.shape
    parts = d // LANES
    for c in range(parts):
        ref[pl.ds(row0 * parts + c, ntok, stride=parts), :] = x[:, c * LANES:(c + 1) * LANES]


def _load_token_major(ref, ntok, parts, row0=0):
    return jnp.concatenate([ref[pl.ds(row0 * parts + c, ntok, stride=parts), :] for c in range(parts)], axis=1)


def _mixout_kernel(ya_ref, yb_ref, ga_ref, gb_ref, x_ref, gate_ref, scale_ref, shift_ref, gain_ref,
                   wpa_ref, wpb_ref, wo_ref, wr_ref, x1_ref, h2_ref, route_ref, ew_ref, cnt_ref, running):
    y_a = jnp.dot(ya_ref[...], wpa_ref[...], preferred_element_type=F32)
    y_b = jnp.dot(yb_ref[...], wpb_ref[...], preferred_element_type=F32)
    merged = _sigmoid(ga_ref[...].astype(F32)) * y_a + _sigmoid(gb_ref[...].astype(F32)) * y_b
    mix = jnp.dot(merged.astype(BF16), wo_ref[...], preferred_element_type=F32)
    x1 = x_ref[...] + gate_ref[0] * mix
    x1_ref[...] = x1
    y = x1 * lax.rsqrt(jnp.mean(x1 * x1, axis=-1, keepdims=True) + EPS) * gain_ref[...]
    h2 = y * (1.0 + scale_ref[0]) + shift_ref[0]
    _store_token_major(h2_ref, h2)

    h_hi = h2.astype(BF16)
    h_lo = (h2 - h_hi.astype(F32)).astype(BF16)
    p_hi = jnp.dot(h_hi, wr_ref[...], preferred_element_type=F32)
    p_lo = jnp.dot(h_lo, wr_ref[...], preferred_element_type=F32)
    logits = (p_hi[:, :LANES] + p_hi[:, LANES:]) + (p_lo[:, :LANES] + p_lo[:, LANES:])
    lane = lax.broadcasted_iota(jnp.int32, logits.shape, 1)
    neg = -jnp.inf
    gl = jnp.where(lane < N_GROUPS, logits, neg)
    gmax = jnp.max(gl, axis=-1, keepdims=True)
    g_sel = jnp.min(jnp.where(gl == gmax, lane, LANES), axis=-1, keepdims=True)
    p_group = 1.0 / jnp.sum(jnp.exp(gl - gmax), axis=-1, keepdims=True)
    lo = N_GROUPS + g_sel * EXPERTS_PER_GROUP
    el = jnp.where((lane >= lo) & (lane < lo + EXPERTS_PER_GROUP), logits, neg)
    v1 = jnp.max(el, axis=-1, keepdims=True)
    i1 = jnp.min(jnp.where(el == v1, lane, LANES), axis=-1, keepdims=True)
    el2 = jnp.where(lane == i1, neg, el)
    v2 = jnp.max(el2, axis=-1, keepdims=True)
    i2 = jnp.min(jnp.where(el2 == v2, lane, LANES), axis=-1, keepdims=True)
    e21 = jnp.exp(v2 - v1)
    w1 = p_group / (1.0 + e21)
    w2 = p_group * e21 / (1.0 + e21)
    ew_ref[...] = jnp.where(lane == 0, w1, jnp.where(lane == 1, w2, 0.0))

    @pl.when(pl.program_id(0) == 0)
    def _():
        running[...] = jnp.zeros(running.shape, F32)

    e1, e2 = i1 - N_GROUPS, i2 - N_GROUPS
    tm = logits.shape[0]
    before = (lax.broadcasted_iota(jnp.int32, (tm, tm), 0)
              > lax.broadcasted_iota(jnp.int32, (tm, tm), 1))
    before = jnp.where(before, 1.0, 0.0).astype(BF16)
    oh1 = jnp.where(lane == e1, 1.0, 0.0)
    oh2 = jnp.where(lane == e2, 1.0, 0.0)
    pre1 = jnp.dot(before, oh1.astype(BF16), preferred_element_type=F32)
    pre2 = jnp.dot(before, oh2.astype(BF16), preferred_element_type=F32)
    cnt1 = jnp.sum(oh1, axis=0, keepdims=True)
    base = running[...]
    r1 = jnp.sum(oh1 * (pre1 + base), axis=-1, keepdims=True).astype(jnp.int32)
    r2 = jnp.sum(oh2 * (pre2 + base + cnt1), axis=-1, keepdims=True).astype(jnp.int32)
    total = base + cnt1 + jnp.sum(oh2, axis=0, keepdims=True)
    running[...] = total
    cnt_ref[...] = total.astype(jnp.int32)
    cols = jnp.where(lane == 0, e1, jnp.where(lane == 1, e2,
                     jnp.where(lane == 2, r1 >> 7, jnp.where(lane == 3, r1 & 127,
                               jnp.where(lane == 4, r2 >> 7, jnp.where(lane == 5, r2 & 127, 0))))))
    srow = lax.broadcasted_iota(jnp.int32, (SUBLANES, LANES), 0)
    slane = lax.broadcasted_iota(jnp.int32, (SUBLANES, LANES), 1)
    sel = jnp.where((srow < 2) & (slane == srow), 1.0,
                    jnp.where((srow >= 2) & (srow < 4) & (slane == 2 * srow - 2), 128.0,
                              jnp.where((srow >= 2) & (srow < 4) & (slane == 2 * srow - 1), 1.0, 0.0)))
    route_t = lax.dot_general(sel.astype(BF16), cols.astype(F32).astype(BF16), (((1,), (1,)), ((), ())),
                              preferred_element_type=F32)
    route_ref[...] = route_t.astype(jnp.int32)


def _mixout(ya, yb, acts, x2, gate1, scale2, shift2, gain2, wpa, wpb, wo, w_route, seq):
    n, d = x2.shape
    bsz = n // seq
    tm = 256
    tpb = seq // tm
    half = ya.shape[1]
    ga_blk = 0
    per_batch = pl.BlockSpec((1, 1, d), lambda i: (i // tpb, 0, 0))
    const = lambda shape: pl.BlockSpec(shape, lambda i: (0, 0), pipeline_mode=pl.Buffered(1))
    return pl.pallas_call(
        _mixout_kernel,
        grid=(n // tm,),
        in_specs=[pl.BlockSpec((tm, half), lambda i: (i, 0)),
                  pl.BlockSpec((tm, half), lambda i: (i, 0)),
                  pl.BlockSpec((tm, d), lambda i: (i, ga_blk)),
                  pl.BlockSpec((tm, d), lambda i: (i, ga_blk + 1)),
                  pl.BlockSpec((tm, d), lambda i: (i, 0)),
                  per_batch, per_batch, per_batch,
                  const((1, d)), const((half, d)), const((half, d)), const((d, d)), const((d, 2 * LANES))],
        out_specs=[pl.BlockSpec((tm, d), lambda i: (i, 0)),
                   pl.BlockSpec((tm * (d // LANES), LANES), lambda i: (i, 0)),
                   pl.BlockSpec((SUBLANES, tm), lambda i: (0, i)),
                   pl.BlockSpec((tm, LANES), lambda i: (i, 0)),
                   pl.BlockSpec((1, LANES), lambda i: (0, 0))],
        out_shape=[jax.ShapeDtypeStruct((n, d), F32),
                   jax.ShapeDtypeStruct((n * (d // LANES), LANES), F32),
                   jax.ShapeDtypeStruct((SUBLANES, n), jnp.int32),
                   jax.ShapeDtypeStruct((n, LANES), F32),
                   jax.ShapeDtypeStruct((1, LANES), jnp.int32)],
        scratch_shapes=[pltpu.VMEM((1, LANES), F32)],
        compiler_params=_params("arbitrary"),
        name="mixout_route",
    )(ya, yb, acts, acts, x2, gate1.reshape(bsz, 1, d), scale2.reshape(bsz, 1, d),
      shift2.reshape(bsz, 1, d), gain2.reshape(1, d), wpa, wpb, wo, w_route)


def _start_token_gather(idx_ref, base, ntok, parts, src_hbm, dst, sem):
    for r in range(ntok):
        t0 = pl.multiple_of(idx_ref[base + r] * parts, parts)
        pltpu.make_async_copy(src_hbm.at[pl.ds(t0, parts), :], dst.at[pl.ds(r * parts, parts), :], sem).start()


def _wait_token_gather(ntok, parts, src_hbm, dst, sem):
    pltpu.make_async_copy(src_hbm.at[pl.ds(0, ntok * parts), :], dst.at[pl.ds(0, ntok * parts), :], sem).wait()


def _expert_kernel(be_ref, nv_ref, first_ref, next_ref, dest_ref, nused_ref,
                   h_hbm, wg_hbm, wu_hbm, wd_hbm, o_ref,
                   xbuf, xsems, row_tok, stage_g, stage_u, stage_d, wg_bf, wu_bf, wd_bf, wsems, *, layer):
    b = pl.program_id(0)
    nb = pl.num_programs(0)
    n_used = nused_ref[0]
    unroll = 8
    cast_rows = 256

    def weight_copies(e):
        return (pltpu.make_async_copy(wg_hbm.at[layer, e], stage_g, wsems.at[0]),
                pltpu.make_async_copy(wu_hbm.at[layer, e], stage_u, wsems.at[1]),
                pltpu.make_async_copy(wd_hbm.at[layer, e], stage_d, wsems.at[2]))

    parts = xbuf.shape[1] // MOE_BLOCK
    nslot = xbuf.shape[0]
    ahead = nslot - 1

    def gather(blk, slot):
        _start_token_gather(row_tok, blk * MOE_BLOCK, MOE_BLOCK, parts, h_hbm, xbuf.at[slot], xsems.at[slot])

    def wait_gather(slot):
        _wait_token_gather(MOE_BLOCK, parts, h_hbm, xbuf.at[slot], xsems.at[slot])

    @pl.when(b == 0)
    def _():
        for cp in weight_copies(be_ref[0]):
            cp.start()

        def clear(i, carry):
            for u in range(unroll):
                row_tok[i * unroll + u] = 0
            return carry

        lax.fori_loop(0, row_tok.shape[0] // unroll, clear, 0)

        ntok = dest_ref.shape[0] // TOP_K
        for k in range(TOP_K):
            def invert(i, carry, k=k):
                for u in range(unroll):
                    row_tok[dest_ref[k * ntok + i * unroll + u]] = i * unroll + u
                return carry

            lax.fori_loop(0, ntok // unroll, invert, 0)
        for q in range(ahead):
            gather(q, q)

    @pl.when(b < n_used + ahead)
    def _():
        wait_gather(b % nslot)

    @pl.when(b < n_used)
    def _():
        slot = b % nslot

        @pl.when(first_ref[b] == 1)
        def _():
            for m, (stage, dst) in enumerate(((stage_g, wg_bf), (stage_u, wu_bf), (stage_d, wd_bf))):
                weight_copies(0)[m].wait()

                def cast(i, carry, stage=stage, dst=dst):
                    r0 = pl.multiple_of(i * cast_rows, cast_rows)
                    dst[pl.ds(r0, cast_rows), :] = stage[pl.ds(r0, cast_rows), :].astype(BF16)
                    return carry

                lax.fori_loop(0, stage.shape[0] // cast_rows, cast, 0)

                @pl.when(next_ref[b] >= 0)
                def _(m=m):
                    weight_copies(next_ref[b])[m].start()

        valid = lax.broadcasted_iota(jnp.int32, (MOE_BLOCK, 1), 0) < nv_ref[b]
        xb = jnp.where(valid, _load_token_major(xbuf.at[slot], MOE_BLOCK, parts), 0.0).astype(BF16)
        gather(b + ahead, (b + ahead) % nslot)
        gate = jnp.dot(xb, wg_bf[...], preferred_element_type=F32)
        up = jnp.dot(xb, wu_bf[...], preferred_element_type=F32)
        hid = (_silu(gate) * up).astype(BF16)
        _store_token_major(o_ref, jnp.dot(hid, wd_bf[...], preferred_element_type=F32))

    @pl.when(b >= n_used)
    def _():
        o_ref[...] = jnp.zeros(o_ref.shape, o_ref.dtype)

    @pl.when(b == nb - 1)
    def _():
        for q in range(ahead):
            @pl.when(nb + q < n_used + ahead)
            def _(q=q):
                wait_gather((nb + q) % nslot)


def _experts(h2_tm, wg, wu, wd, layer, plan):
    p = plan["rows"]
    nb = p // MOE_BLOCK
    d, de = wg.shape[-2:]
    parts = d // LANES
    hbm = pl.BlockSpec(memory_space=pl.ANY)
    grid_spec = pltpu.PrefetchScalarGridSpec(
        num_scalar_prefetch=6,
        grid=(nb,),
        in_specs=[hbm, hbm, hbm, hbm],
        out_specs=pl.BlockSpec((MOE_BLOCK * parts, LANES), lambda b, *_: (b, 0)),
        scratch_shapes=[pltpu.VMEM((EXPERT_X_SLOTS, MOE_BLOCK * parts, LANES), F32),
                        pltpu.SemaphoreType.DMA((EXPERT_X_SLOTS,)),
                        pltpu.SMEM((p + (EXPERT_X_SLOTS - 1) * MOE_BLOCK,), jnp.int32),
                        pltpu.VMEM((d, de), F32), pltpu.VMEM((d, de), F32), pltpu.VMEM((de, d), F32),
                        pltpu.VMEM((d, de), BF16), pltpu.VMEM((d, de), BF16), pltpu.VMEM((de, d), BF16),
                        pltpu.SemaphoreType.DMA((3,))],
    )
    return pl.pallas_call(
        functools.partial(_expert_kernel, layer=layer),
        grid_spec=grid_spec,
        out_shape=jax.ShapeDtypeStruct((p * parts, LANES), F32),
        compiler_params=_params("arbitrary"),
        name="experts",
    )(plan["block_expert"], plan["n_valid"], plan["first"], plan["next_expert"], plan["dest"],
      plan["n_used"], h2_tm, wg, wu, wd)


COMBINE_TILE = 64
COMBINE_BATCHES = 8


def _combine_kernel(pos_ref, y_hbm, x1_ref, ew_ref, gate_ref, gain_ref, o_ref, *scratch):
    i = pl.program_id(0)
    tm = COMBINE_TILE
    nbuf = COMBINE_BATCHES
    rows = TOP_K * tm
    bufs, sems = scratch[:nbuf], scratch[nbuf]

    parts = x1_ref.shape[1] // LANES

    def gather(tile, k):
        _start_token_gather(pos_ref, tile * rows, rows, parts, y_hbm, bufs[k], sems.at[k])

    def wait(k):
        _wait_token_gather(rows, parts, y_hbm, bufs[k], sems.at[k])

    def finish(k):
        sl = slice(k * tm, (k + 1) * tm)
        ew = ew_ref[sl, :]
        moe = ew[:, 0:1] * _load_token_major(bufs[k], tm, parts)
        for j in range(1, TOP_K):
            moe = moe + ew[:, j:j + 1] * _load_token_major(bufs[k], tm, parts, row0=j * tm)
        x2 = x1_ref[sl, :] + gate_ref[0] * moe
        o_ref[sl, :] = x2 * lax.rsqrt(jnp.mean(x2 * x2, axis=-1, keepdims=True) + EPS) * gain_ref[...]

    ahead = nbuf - 1

    @pl.when(i == 0)
    def _():
        for q in range(ahead):
            gather(q, q)

    for q in range(nbuf):
        wait(q)
        gather(nbuf * i + q + ahead, (q + ahead) % nbuf)
        finish(q)

    @pl.when(i == pl.num_programs(0) - 1)
    def _():
        for q in range(ahead):
            wait(q)


def _combine(ys, pos_tiles, x1, ew, gate2, final_gain, seq):
    n, d = x1.shape
    bsz = n // seq
    tm = COMBINE_BATCHES * COMBINE_TILE
    tpb = seq // tm
    grid_spec = pltpu.PrefetchScalarGridSpec(
        num_scalar_prefetch=1,
        grid=(n // tm,),
        in_specs=[pl.BlockSpec(memory_space=pl.ANY),
                  pl.BlockSpec((tm, d), lambda i, pos: (i, 0)),
                  pl.BlockSpec((tm, LANES), lambda i, pos: (i, 0)),
                  pl.BlockSpec((1, 1, d), lambda i, pos: (i // tpb, 0, 0)),
                  pl.BlockSpec((1, d), lambda i, pos: (0, 0))],
        out_specs=pl.BlockSpec((tm, d), lambda i, pos: (i, 0)),
        scratch_shapes=[pltpu.VMEM((TOP_K * COMBINE_TILE * (d // LANES), LANES), F32)] * COMBINE_BATCHES
        + [pltpu.SemaphoreType.DMA((COMBINE_BATCHES,))],
    )
    return pl.pallas_call(
        _combine_kernel,
        grid_spec=grid_spec,
        out_shape=jax.ShapeDtypeStruct((n, d), F32),
        compiler_params=_params("arbitrary"),
        name="combine_norm",
    )(pos_tiles, ys, x1, ew, gate2.reshape(bsz, 1, d), final_gain.reshape(1, d))


def _dispatch_plan(route, counts):
    n = route.shape[1]
    a = n * TOP_K
    eid = route[:TOP_K]
    rank = route[TOP_K:2 * TOP_K]
    cnt = counts[0, :N_EXPERTS]
    padded = (cnt + MOE_BLOCK - 1) // MOE_BLOCK * MOE_BLOCK
    pad_ends = jnp.cumsum(padded)
    pad_starts = pad_ends - padded
    experts = jnp.arange(N_EXPERTS, dtype=jnp.int32)
    dest = (jnp.sum(jnp.where(eid[..., None] == experts, pad_starts, 0), axis=-1) + rank).astype(jnp.int32)
    p = ((a + MOE_BLOCK - 1) // MOE_BLOCK + N_EXPERTS) * MOE_BLOCK
    nb = p // MOE_BLOCK
    n_used = (pad_ends[-1] // MOE_BLOCK).astype(jnp.int32)
    blk = jnp.arange(nb, dtype=jnp.int32)
    blk_start = jnp.minimum(blk, n_used - 1) * MOE_BLOCK
    block_expert = jnp.minimum(jnp.sum((pad_ends[None, :] <= blk_start[:, None]).astype(jnp.int32), axis=1),
                               N_EXPERTS - 1).astype(jnp.int32)
    def lookup(table, idx):
        return jnp.sum(jnp.where(idx[..., None] == jnp.arange(table.shape[0], dtype=jnp.int32), table, 0), axis=-1)

    blk_pad_start = lookup(pad_starts, block_expert)
    n_valid = jnp.clip(blk_pad_start + lookup(cnt, block_expert) - blk_start, 0, MOE_BLOCK).astype(jnp.int32)
    first = ((blk_start == blk_pad_start) & (blk < n_used)).astype(jnp.int32)
    nxt_blk = lookup(pad_ends, block_expert) // MOE_BLOCK
    next_expert = jnp.where(nxt_blk < n_used, lookup(block_expert, jnp.minimum(nxt_blk, nb - 1)), -1).astype(jnp.int32)
    pos_tiles = dest.reshape(TOP_K, n // COMBINE_TILE, COMBINE_TILE).transpose(1, 0, 2).reshape(a)
    pos_tiles = jnp.concatenate(
        [pos_tiles, jnp.zeros(((COMBINE_BATCHES - 1) * TOP_K * COMBINE_TILE,), jnp.int32)])
    plan = dict(block_expert=block_expert, n_valid=n_valid, first=first, next_expert=next_expert,
                dest=dest.reshape(a), n_used=n_used.reshape(1), rows=p)
    return plan, pos_tiles


def kernel(x, c, w_ada, b_ada, norm1, w_in, hg_lb, hg_norm, rg_conv_w, rg_conv_b, rg_w_a, rg_b_a,
           rg_w_x, rg_b_x, rg_lambda, w_proj_a, w_proj_b, w_out, norm2, w_group, w_router, w_gate,
           w_up, w_down, final_norm):
    bsz, seq, d = x.shape
    depth = w_ada.shape[0]
    n = bsz * seq
    x2 = x.reshape(n, d)
    lb_table = hg_lb.reshape(hg_lb.shape[0], -1)
    for layer in range(depth):
        mod = _adaln(c, w_ada[layer], b_ada[layer])
        shift1, scale1, gate1, shift2, scale2, gate2 = jnp.split(mod, 6, axis=-1)
        logf, acts = _inproj(x2, norm1[layer], scale1, shift1, w_in, lb_table, layer, seq)
        acts3 = acts.reshape(bsz, seq, acts.shape[1])
        ya = _hgrn(acts3, logf.reshape(bsz, seq, logf.shape[1]), hg_norm[layer])
        yb = _rglru(acts3, rg_conv_w[layer], rg_conv_b[layer], rg_w_a[layer], rg_b_a[layer],
                    rg_w_x[layer], rg_b_x[layer], rg_lambda[layer])
        w_route = jnp.concatenate(
            [w_group[layer], w_router[layer].reshape(d, N_EXPERTS),
             jnp.zeros((d, LANES - N_GROUPS - N_EXPERTS), F32)], axis=1)
        w_route_hi = w_route.astype(BF16)
        w_route = jnp.concatenate([w_route_hi, (w_route - w_route_hi.astype(F32)).astype(BF16)], axis=1)
        x1, h2, route, ew, counts = _mixout(
            ya.reshape(n, -1), yb.reshape(n, -1), acts, x2, gate1, scale2, shift2, norm2[layer],
            w_proj_a[layer].astype(BF16), w_proj_b[layer].astype(BF16), w_out[layer].astype(BF16), w_route, seq)
        plan, pos_tiles = _dispatch_plan(route, counts)
        ys = _experts(h2, w_gate, w_up, w_down, layer, plan)
        if layer + 1 < depth:
            raise NotImplementedError("the fused combine + final norm assumes a single layer")
        x2 = _combine(ys, pos_tiles, x1, ew, gate2, final_norm, seq)
    return x2.reshape(bsz, seq, d)
```

```python
import functools

import jax
import jax.numpy as jnp
import numpy as np
from jax import lax
from jax.experimental import pallas as pl
from jax.experimental.pallas import tpu as pltpu

EPS = 1e-6
LANES = 128
SUBLANES = 8
HG_HEADS = 8
RG_BLOCKS = 8
RG_CONV = 4
RG_C = 8.0
N_GROUPS = 4
EXPERTS_PER_GROUP = 8
N_EXPERTS = N_GROUPS * EXPERTS_PER_GROUP
TOP_K = 2
HG_CHUNK = 128
MOE_BLOCK = 128
EXPERT_X_SLOTS = 6
VMEM_LIMIT = 56 * 1024 * 1024
ACT_Q, ACT_V, ACT_G, ACT_RX, ACT_GY = 32, 40, 48, 56, 64

F32 = jnp.float32
BF16 = jnp.bfloat16


def _sigmoid(x):
    return 1.0 / (1.0 + jnp.exp(-x))


def _silu(x):
    return x * _sigmoid(x)


def _gelu_tanh(x):
    return 0.5 * x * (1.0 + jnp.tanh(0.7978845608028654 * (x + 0.044715 * (x * x * x))))


def _params(*sem):
    return pltpu.CompilerParams(dimension_semantics=sem, vmem_limit_bytes=VMEM_LIMIT)


def _adaln_kernel(ct_ref, w_ref, b_ref, o_ref, *, kc):
    kdim, nb = ct_ref.shape
    tn = w_ref.shape[1]

    def body(i, accs):
        k0 = pl.multiple_of(i * kc, kc)
        w = w_ref[pl.ds(k0, kc), :]
        cs = _silu(ct_ref[pl.ds(k0, kc), :])
        return tuple(acc + jnp.sum(w * cs[:, b:b + 1], axis=0, keepdims=True)
                     for b, acc in enumerate(accs))

    accs = lax.fori_loop(0, kdim // kc, body, tuple(jnp.zeros((1, tn), F32) for _ in range(nb)))
    for b, acc in enumerate(accs):
        o_ref[b:b + 1, :] = acc + b_ref[...]


def _adaln(c, w, bias):
    bsz, d = c.shape
    n = w.shape[1]
    tn = 1024
    return pl.pallas_call(
        functools.partial(_adaln_kernel, kc=256),
        grid=(n // tn,),
        in_specs=[pl.BlockSpec((d, bsz), lambda j: (0, 0)),
                  pl.BlockSpec((d, tn), lambda j: (0, j)),
                  pl.BlockSpec((1, tn), lambda j: (0, j))],
        out_specs=pl.BlockSpec((bsz, tn), lambda j: (0, j)),
        out_shape=jax.ShapeDtypeStruct((bsz, n), F32),
        compiler_params=_params("parallel"),
        name="adaln",
    )(c.T, w, bias.reshape(1, n))


def _norm_mod_kernel(x_ref, gain_ref, scale_ref, shift_ref, h_ref):
    rows = 64

    def body(i, carry):
        r0 = pl.multiple_of(i * rows, rows)
        x = x_ref[pl.ds(r0, rows), :]
        y = x * lax.rsqrt(jnp.mean(x * x, axis=-1, keepdims=True) + EPS) * gain_ref[...]
        h_ref[pl.ds(r0, rows), :] = (y * (1.0 + scale_ref[0]) + shift_ref[0]).astype(BF16)
        return carry

    lax.fori_loop(0, x_ref.shape[0] // rows, body, 0)


def _cast_weight_tile(w_ref, w_scr):
    rows = 256

    @pl.when(pl.program_id(1) == 0)
    def _():
        def body(i, carry):
            r0 = pl.multiple_of(i * rows, rows)
            w_scr[pl.ds(r0, rows), :] = w_ref[pl.ds(r0, rows), :].astype(BF16)
            return carry

        lax.fori_loop(0, w_ref.shape[0] // rows, body, 0)


def _inproj_logf_kernel(h_ref, w_ref, lbt_ref, o_ref, w_scr, *, layer):
    _cast_weight_tile(w_ref, w_scr)
    t = lbt_ref[...]
    e = jnp.exp(t - jnp.max(t, axis=0, keepdims=True))
    lb = jnp.sum(e[:layer + 1], axis=0, keepdims=True) / jnp.sum(e, axis=0, keepdims=True)
    z = jnp.dot(h_ref[...], w_scr[...], preferred_element_type=F32)
    o_ref[...] = jnp.log(lb + (1.0 - lb) * _sigmoid(z))


def _inproj_act_kernel(h_ref, w_ref, o_ref, w_scr):
    j = pl.program_id(0)
    _cast_weight_tile(w_ref, w_scr)
    z = jnp.dot(h_ref[...], w_scr[...], preferred_element_type=F32)

    @pl.when((j == 4) | (j == 6))
    def _():
        o_ref[...] = _silu(z).astype(BF16)

    @pl.when((j < 4) | (j == 5) | (j == 7))
    def _():
        o_ref[...] = z.astype(BF16)

    @pl.when(j == 8)
    def _():
        o_ref[...] = _gelu_tanh(z).astype(BF16)


def _inproj(x2, gain, scale, shift, w_in, lb_table, layer, seq):
    n, d = x2.shape
    bsz = n // seq
    tn = 1024
    tm = 512
    per_batch = pl.BlockSpec((1, 1, d), lambda i: (i // (seq // tm), 0, 0))
    h = pl.pallas_call(
        _norm_mod_kernel,
        grid=(n // tm,),
        in_specs=[pl.BlockSpec((tm, d), lambda i: (i, 0)), pl.BlockSpec((1, d), lambda i: (0, 0)),
                  per_batch, per_batch],
        out_specs=pl.BlockSpec((tm, d), lambda i: (i, 0)),
        out_shape=jax.ShapeDtypeStruct((n, d), BF16),
        compiler_params=_params("parallel"),
        name="norm_mod",
    )(x2, gain.reshape(1, d), scale.reshape(bsz, 1, d), shift.reshape(bsz, 1, d))

    n_lb = lb_table.shape[0]
    tm = 1024
    logf = pl.pallas_call(
        functools.partial(_inproj_logf_kernel, layer=layer),
        grid=(2, n // tm),
        in_specs=[pl.BlockSpec((tm, d), lambda j, i: (i, 0)),
                  pl.BlockSpec((None, d, tn), lambda j, i: (layer, 0, j + 2)),
                  pl.BlockSpec((n_lb, tn), lambda j, i: (0, j))],
        out_specs=pl.BlockSpec((tm, tn), lambda j, i: (i, j)),
        out_shape=jax.ShapeDtypeStruct((n, 2 * tn), F32),
        scratch_shapes=[pltpu.VMEM((d, tn), BF16)],
        compiler_params=_params("parallel", "arbitrary"),
        name="inproj_logf",
    )(h, w_in, lb_table)
    acts = pl.pallas_call(
        _inproj_act_kernel,
        grid=(9, n // tm),
        in_specs=[pl.BlockSpec((tm, d), lambda j, i: (i, 0)),
                  pl.BlockSpec((None, d, tn),
                               lambda j, i: (layer, 0, jnp.where(j < 4, j + 7, jnp.where(j < 6, j - 4, j - 2))))],
        out_specs=pl.BlockSpec((tm, tn), lambda j, i: (i, j)),
        out_shape=jax.ShapeDtypeStruct((n, 9 * tn), BF16),
        scratch_shapes=[pltpu.VMEM((d, tn), BF16)],
        compiler_params=_params("parallel", "arbitrary"),
        name="inproj_act",
    )(h, w_in)
    return logf, acts


def _hgrn_level_halves(chunk):
    return [chunk >> (i + 1) for i in range(chunk.bit_length() - 1)]


def _hgrn_kernel(q_ref, v_ref, g_ref, lff_ref, lfb_ref, gain_ref, o_ref,
                 oacc, qf_scr, kf_scr, qb_scr, kb_scr, *, chunk, group):
    seq = q_ref.shape[1]
    nchunks = seq // chunk
    halves = _hgrn_level_halves(chunk)
    nt = (((1,), (1,)), ((), ()))
    tn = (((0,), (0,)), ((), ()))

    row = lax.broadcasted_iota(jnp.int32, (chunk, LANES), 0)
    ti = lax.broadcasted_iota(jnp.int32, (chunk, chunk), 0)
    si = lax.broadcasted_iota(jnp.int32, (chunk, chunk), 1)
    txs = ti ^ si
    tri_f = jnp.where(si <= ti, 1.0, 0.0).astype(BF16)
    tri_b = jnp.where(si >= ti, 1.0, 0.0).astype(BF16)

    def cumulative(lf, tri):
        hi = lf.astype(BF16)
        mid = (lf - hi.astype(F32)).astype(BF16)
        r = jnp.dot(tri, jnp.concatenate([hi, mid], axis=1), preferred_element_type=F32)
        return r[:, :LANES] + r[:, LANES:]

    lid = functools.reduce(lambda a, b: a + b, [(txs >= h).astype(jnp.int32) for h in halves])

    def boundary_diff(cum, m, reverse):
        pieces = []
        for base in range(0, chunk, 2 * m):
            bnd = base + (m if reverse else m - 1)
            pieces.append(jnp.broadcast_to(cum[bnd:bnd + 1, :], (2 * m, LANES)))
        return cum - jnp.concatenate(pieces, axis=0)

    def intra(c):
        r0 = pl.multiple_of(c * chunk, chunk)
        q = q_ref[0, pl.ds(r0, chunk), :].astype(F32)
        v = v_ref[0, pl.ds(r0, chunk), :]
        lff = lff_ref[0, pl.ds(r0, chunk), :]
        lfb = lfb_ref[0, pl.ds(r0, chunk), :]
        cum_f = cumulative(lff, tri_f)
        cum_b = cumulative(lfb, tri_b)
        ff = jnp.exp(lff)
        fb = jnp.exp(lfb)
        kf = 1.0 - ff
        kb = 1.0 - fb

        p0 = lax.dot_general(q.astype(BF16), (kf + kb).astype(BF16), nt, preferred_element_type=F32)
        scores = jnp.where(lid == 0, p0, 0.0)
        for m in halves:
            later = (row & m) != 0
            if 2 * m >= SUBLANES:
                d_f = boundary_diff(cum_f, m, False)
                d_b = boundary_diff(cum_b, m, True)
                x = q * jnp.exp(jnp.where(later, d_f, d_b))
                y = jnp.where(later, kb, kf) * jnp.exp(-jnp.where(later, d_b, d_f))
            elif m == 2:
                r4 = row & 3
                prev = lambda a: pltpu.roll(a, 1, 0)
                nxt = lambda a: pltpu.roll(a, chunk - 1, 0)
                x = q * jnp.where(r4 == 0, fb * nxt(fb), jnp.where(r4 == 1, fb,
                                  jnp.where(r4 == 2, ff, ff * prev(ff))))
                y = jnp.where(r4 == 0, kf * nxt(ff), jnp.where(r4 == 1, kf,
                              jnp.where(r4 == 2, kb, kb * prev(fb))))
            else:
                x = q * jnp.where(later, ff, fb)
                y = jnp.where(later, kb, kf)
            pm = lax.dot_general(x.astype(BF16), y.astype(BF16), nt, preferred_element_type=F32)
            scores = jnp.where(lid == m.bit_length(), pm, scores)
        oacc[pl.ds(r0, chunk), :] = jnp.dot(scores.astype(BF16), v, preferred_element_type=F32)
        tot_f = cum_f[chunk - 1:chunk, :]
        tot_b = cum_b[0:1, :]
        qf_scr[pl.ds(r0, chunk), :] = (q * jnp.exp(cum_f)).astype(BF16)
        kf_scr[pl.ds(r0, chunk), :] = (kf * jnp.exp(tot_f - cum_f)).astype(BF16)
        qb_scr[pl.ds(r0, chunk), :] = (q * jnp.exp(cum_b)).astype(BF16)
        kb_scr[pl.ds(r0, chunk), :] = (kb * jnp.exp(tot_b - cum_b)).astype(BF16)

    def intra_body(i, carry):
        for u in range(group):
            intra(i * group + u)
        return carry

    lax.fori_loop(0, nchunks // group, intra_body, 0)

    def apply_state(c, st, q_scr, k_scr, lf_ref):
        r0 = pl.multiple_of(c * chunk, chunk)
        o = oacc[pl.ds(r0, chunk), :] + lax.dot_general(
            q_scr[pl.ds(r0, chunk), :], st.astype(BF16), nt, preferred_element_type=F32)
        tot = jnp.sum(lf_ref[0, pl.ds(r0, chunk), :], axis=0, keepdims=True)
        st = st * jnp.exp(tot) + lax.dot_general(
            v_ref[0, pl.ds(r0, chunk), :], k_scr[pl.ds(r0, chunk), :], tn, preferred_element_type=F32)
        return r0, o, st

    def finish(r0, o):
        y = o * lax.rsqrt(jnp.mean(o * o, axis=-1, keepdims=True) + EPS) * gain_ref[0]
        o_ref[0, pl.ds(r0, chunk), :] = (y * g_ref[0, pl.ds(r0, chunk), :].astype(F32)).astype(BF16)

    def sweep(final):
        def body(i, states):
            st_f, st_b = states
            rf, of, st_f = apply_state(i, st_f, qf_scr, kf_scr, lff_ref)
            rb, ob, st_b = apply_state(nchunks - 1 - i, st_b, qb_scr, kb_scr, lfb_ref)
            if final:
                finish(rf, of)
                finish(rb, ob)
            else:
                oacc[pl.ds(rf, chunk), :] = of
                oacc[pl.ds(rb, chunk), :] = ob
            return st_f, st_b
        return body

    zero = jnp.zeros((LANES, LANES), F32)
    states = lax.fori_loop(0, nchunks // 2, sweep(False), (zero, zero), unroll=8)
    lax.fori_loop(nchunks // 2, nchunks, sweep(True), states, unroll=8)


def _hgrn(acts3, logf3, hg_norm):
    bsz, seq, _ = acts3.shape
    col = lambda off: pl.BlockSpec((1, seq, LANES), lambda b, h: (b, 0, off + h))
    return pl.pallas_call(
        functools.partial(_hgrn_kernel, chunk=HG_CHUNK, group=8),
        grid=(bsz, HG_HEADS),
        in_specs=[col(ACT_Q), col(ACT_V), col(ACT_G), col(0), col(HG_HEADS),
                  pl.BlockSpec((1, 1, LANES), lambda b, h: (h, 0, 0))],
        out_specs=pl.BlockSpec((1, seq, LANES), lambda b, h: (b, 0, h)),
        out_shape=jax.ShapeDtypeStruct((bsz, seq, HG_HEADS * LANES), BF16),
        scratch_shapes=[pltpu.VMEM((seq, LANES), F32)] + [pltpu.VMEM((seq, LANES), BF16)] * 4,
        compiler_params=_params("parallel", "parallel"),
        name="hgrn2",
    )(acts3, acts3, acts3, logf3, logf3, hg_norm.reshape(HG_HEADS, 1, LANES))


RG_SCAN_SUB = 32


def _scan_tile(a, u, carry, reverse):
    rows = a.shape[0]
    sub = RG_SCAN_SUB
    pos = lax.broadcasted_iota(jnp.int32, a.shape, 0) & (sub - 1)
    d = 1
    while d < sub:
        if reverse:
            ok = pos < sub - d
            shift = rows - d
        else:
            ok = pos >= d
            shift = d
        a_s = jnp.where(ok, pltpu.roll(a, shift, 0), 1.0)
        u_s = jnp.where(ok, pltpu.roll(u, shift, 0), 0.0)
        u = u + a * u_s
        a = a * a_s
        d *= 2
    spans = list(range(rows // sub))
    pieces = [None] * len(spans)
    for k in (reversed(spans) if reverse else spans):
        hk = u[k * sub:(k + 1) * sub] + a[k * sub:(k + 1) * sub] * carry
        carry = hk[0:1] if reverse else hk[sub - 1:sub]
        pieces[k] = hk
    return jnp.concatenate(pieces, axis=0), carry


def _rglru_kernel(rx_ref, gy_ref, cw_ref, cb_ref, wa_ref, ba_ref, wx_ref, bx_ref, lam_ref, o_ref,
                  xpad, hf_scr, ab_scr, ub_scr, *, tile):
    seq = rx_ref.shape[1]
    ntiles = seq // tile
    halo = 8

    xpad[0:halo, :] = jnp.zeros((halo, LANES), F32)
    xpad[halo + seq:halo + seq + halo, :] = jnp.zeros((halo, LANES), F32)
    xpad[halo:halo + seq, :] = rx_ref[0].astype(F32)

    lam = lam_ref[...]
    neg_c_softplus = -RG_C * (jnp.maximum(-lam, 0.0) + jnp.log(1.0 + jnp.exp(-jnp.abs(lam))))

    def gates(xcb, z):
        r = _sigmoid(jnp.dot(xcb, wa_ref[z, 0], preferred_element_type=F32) + ba_ref[z, 0])
        gi = _sigmoid(jnp.dot(xcb, wx_ref[z, 0], preferred_element_type=F32) + bx_ref[z, 0])
        a = jnp.exp(r * neg_c_softplus[z:z + 1, :])
        return a, jnp.sqrt(1.0 - a * a) * gi

    def fwd_body(i, carry):
        t0 = pl.multiple_of(i * tile, tile)
        ext = xpad[pl.ds(t0, tile + 2 * halo), :]
        n_ext = tile + 2 * halo
        tap = lambda off: pltpu.roll(ext, (-off) % n_ext, 0)[halo:halo + tile]
        xc = (cw_ref[0:1, :] * tap(-2) + cw_ref[1:2, :] * tap(-1) + cw_ref[2:3, :] * ext[halo:halo + tile]
              + cw_ref[3:4, :] * tap(1) + cb_ref[...])
        xcb = xc.astype(BF16)
        a_f, s_f = gates(xcb, 0)
        a_b, s_b = gates(xcb, 1)
        h, carry = _scan_tile(a_f, s_f * xc, carry, reverse=False)
        hf_scr[pl.ds(t0, tile), :] = h
        ab_scr[pl.ds(t0, tile), :] = a_b
        ub_scr[pl.ds(t0, tile), :] = s_b * xc
        return carry

    lax.fori_loop(0, ntiles, fwd_body, jnp.zeros((1, LANES), F32))

    def bwd_body(i, carry):
        t0 = pl.multiple_of((ntiles - 1 - i) * tile, tile)
        h, carry = _scan_tile(ab_scr[pl.ds(t0, tile), :], ub_scr[pl.ds(t0, tile), :], carry, reverse=True)
        o_ref[0, pl.ds(t0, tile), :] = (
            (hf_scr[pl.ds(t0, tile), :] + h) * gy_ref[0, pl.ds(t0, tile), :].astype(F32)).astype(BF16)
        return carry

    lax.fori_loop(0, ntiles, bwd_body, jnp.zeros((1, LANES), F32))


def _rglru(acts3, conv_w, conv_b, w_a, b_a, w_x, b_x, lam):
    bsz, seq, _ = acts3.shape
    width = RG_BLOCKS * LANES
    tile = 256
    rx_off, gy_off = ACT_RX, ACT_GY
    col = lambda off: pl.BlockSpec((1, seq, LANES), lambda b, j: (b, 0, off + j))
    wspec = pl.BlockSpec((2, 1, LANES, LANES), lambda b, j: (0, j, 0, 0))
    bspec = pl.BlockSpec((2, 1, 1, LANES), lambda b, j: (0, j, 0, 0))
    return pl.pallas_call(
        functools.partial(_rglru_kernel, tile=tile),
        grid=(bsz, RG_BLOCKS),
        in_specs=[col(rx_off), col(gy_off),
                  pl.BlockSpec((RG_CONV, LANES), lambda b, j: (0, j)),
                  pl.BlockSpec((1, LANES), lambda b, j: (0, j)),
                  wspec, bspec, wspec, bspec,
                  pl.BlockSpec((2, LANES), lambda b, j: (0, j))],
        out_specs=pl.BlockSpec((1, seq, LANES), lambda b, j: (b, 0, j)),
        out_shape=jax.ShapeDtypeStruct((bsz, seq, width), BF16),
        scratch_shapes=[pltpu.VMEM((seq + 16, LANES), F32),
                        pltpu.VMEM((seq, LANES), F32),
                        pltpu.VMEM((seq, LANES), F32),
                        pltpu.VMEM((seq, LANES), F32)],
        compiler_params=_params("parallel", "parallel"),
        name="rglru",
    )(acts3, acts3, conv_w, conv_b.reshape(1, width), w_a.astype(BF16),
      b_a.reshape(2, RG_BLOCKS, 1, LANES), w_x.astype(BF16), b_x.reshape(2, RG_BLOCKS, 1, LANES), lam)


def _store_token_major(ref, x, row0=0):
    ntok, d = x.shape
    parts = d // LANES
    for c in range(parts):
        ref[pl.ds(row0 * parts + c, ntok, stride=parts), :] = x[:, c * LANES:(c + 1) * LANES]


def _load_token_major(ref, ntok, parts, row0=0):
    return jnp.concatenate([ref[pl.ds(row0 * parts + c, ntok, stride=parts), :] for c in range(parts)], axis=1)


def _mixout_kernel(ya_ref, yb_ref, ga_ref, gb_ref, x_ref, gate_ref, scale_ref, shift_ref, gain_ref,
                   wpa_ref, wpb_ref, wo_ref, wr_ref, x1_ref, h2_ref, route_ref, ew_ref, cnt_ref, running):
    y_a = jnp.dot(ya_ref[...], wpa_ref[...], preferred_element_type=F32)
    y_b = jnp.dot(yb_ref[...], wpb_ref[...], preferred_element_type=F32)
    merged = _sigmoid(ga_ref[...].astype(F32)) * y_a + _sigmoid(gb_ref[...].astype(F32)) * y_b
    mix = jnp.dot(merged.astype(BF16), wo_ref[...], preferred_element_type=F32)
    x1 = x_ref[...] + gate_ref[0] * mix
    x1_ref[...] = x1
    y = x1 * lax.rsqrt(jnp.mean(x1 * x1, axis=-1, keepdims=True) + EPS) * gain_ref[...]
    h2 = y * (1.0 + scale_ref[0]) + shift_ref[0]
    _store_token_major(h2_ref, h2)

    h_hi = h2.astype(BF16)
    h_lo = (h2 - h_hi.astype(F32)).astype(BF16)
    p_hi = jnp.dot(h_hi, wr_ref[...], preferred_element_type=F32)
    p_lo = jnp.dot(h_lo, wr_ref[...], preferred_element_type=F32)
    logits = (p_hi[:, :LANES] + p_hi[:, LANES:]) + (p_lo[:, :LANES] + p_lo[:, LANES:])
    lane = lax.broadcasted_iota(jnp.int32, logits.shape, 1)
    neg = -jnp.inf
    gl = jnp.where(lane < N_GROUPS, logits, neg)
    gmax = jnp.max(gl, axis=-1, keepdims=True)
    g_sel = jnp.min(jnp.where(gl == gmax, lane, LANES), axis=-1, keepdims=True)
    p_group = 1.0 / jnp.sum(jnp.exp(gl - gmax), axis=-1, keepdims=True)
    lo = N_GROUPS + g_sel * EXPERTS_PER_GROUP
    el = jnp.where((lane >= lo) & (lane < lo + EXPERTS_PER_GROUP), logits, neg)
    v1 = jnp.max(el, axis=-1, keepdims=True)
    i1 = jnp.min(jnp.where(el == v1, lane, LANES), axis=-1, keepdims=True)
    el2 = jnp.where(lane == i1, neg, el)
    v2 = jnp.max(el2, axis=-1, keepdims=True)
    i2 = jnp.min(jnp.where(el2 == v2, lane, LANES), axis=-1, keepdims=True)
    e21 = jnp.exp(v2 - v1)
    w1 = p_group / (1.0 + e21)
    w2 = p_group * e21 / (1.0 + e21)
    ew_ref[...] = jnp.where(lane == 0, w1, jnp.where(lane == 1, w2, 0.0))

    @pl.when(pl.program_id(0) == 0)
    def _():
        running[...] = jnp.zeros(running.shape, F32)

    e1, e2 = i1 - N_GROUPS, i2 - N_GROUPS
    tm = logits.shape[0]
    before = (lax.broadcasted_iota(jnp.int32, (tm, tm), 0)
              > lax.broadcasted_iota(jnp.int32, (tm, tm), 1))
    before = jnp.where(before, 1.0, 0.0).astype(BF16)
    oh1 = jnp.where(lane == e1, 1.0, 0.0)
    oh2 = jnp.where(lane == e2, 1.0, 0.0)
    pre1 = jnp.dot(before, oh1.astype(BF16), preferred_element_type=F32)
    pre2 = jnp.dot(before, oh2.astype(BF16), preferred_element_type=F32)
    cnt1 = jnp.sum(oh1, axis=0, keepdims=True)
    base = running[...]
    r1 = jnp.sum(oh1 * (pre1 + base), axis=-1, keepdims=True).astype(jnp.int32)
    r2 = jnp.sum(oh2 * (pre2 + base + cnt1), axis=-1, keepdims=True).astype(jnp.int32)
    total = base + cnt1 + jnp.sum(oh2, axis=0, keepdims=True)
    running[...] = total
    cnt_ref[...] = total.astype(jnp.int32)
    cols = jnp.where(lane == 0, e1, jnp.where(lane == 1, e2,
                     jnp.where(lane == 2, r1 >> 7, jnp.where(lane == 3, r1 & 127,
                               jnp.where(lane == 4, r2 >> 7, jnp.where(lane == 5, r2 & 127, 0))))))
    srow = lax.broadcasted_iota(jnp.int32, (SUBLANES, LANES), 0)
    slane = lax.broadcasted_iota(jnp.int32, (SUBLANES, LANES), 1)
    sel = jnp.where((srow < 2) & (slane == srow), 1.0,
                    jnp.where((srow >= 2) & (srow < 4) & (slane == 2 * srow - 2), 128.0,
                              jnp.where((srow >= 2) & (srow < 4) & (slane == 2 * srow - 1), 1.0, 0.0)))
    route_t = lax.dot_general(sel.astype(BF16), cols.astype(F32).astype(BF16), (((1,), (1,)), ((), ())),
                              preferred_element_type=F32)
    route_ref[...] = route_t.astype(jnp.int32)


def _mixout(ya, yb, acts, x2, gate1, scale2, shift2, gain2, wpa, wpb, wo, w_route, seq):
    n, d = x2.shape
    bsz = n // seq
    tm = 256
    tpb = seq // tm
    half = ya.shape[1]
    ga_blk = 0
    per_batch = pl.BlockSpec((1, 1, d), lambda i: (i // tpb, 0, 0))
    const = lambda shape: pl.BlockSpec(shape, lambda i: (0, 0), pipeline_mode=pl.Buffered(1))
    return pl.pallas_call(
        _mixout_kernel,
        grid=(n // tm,),
        in_specs=[pl.BlockSpec((tm, half), lambda i: (i, 0)),
                  pl.BlockSpec((tm, half), lambda i: (i, 0)),
                  pl.BlockSpec((tm, d), lambda i: (i, ga_blk)),
                  pl.BlockSpec((tm, d), lambda i: (i, ga_blk + 1)),
                  pl.BlockSpec((tm, d), lambda i: (i, 0)),
                  per_batch, per_batch, per_batch,
                  const((1, d)), const((half, d)), const((half, d)), const((d, d)), const((d, 2 * LANES))],
        out_specs=[pl.BlockSpec((tm, d), lambda i: (i, 0)),
                   pl.BlockSpec((tm * (d // LANES), LANES), lambda i: (i, 0)),
                   pl.BlockSpec((SUBLANES, tm), lambda i: (0, i)),
                   pl.BlockSpec((tm, LANES), lambda i: (i, 0)),
                   pl.BlockSpec((1, LANES), lambda i: (0, 0))],
        out_shape=[jax.ShapeDtypeStruct((n, d), F32),
                   jax.ShapeDtypeStruct((n * (d // LANES), LANES), F32),
                   jax.ShapeDtypeStruct((SUBLANES, n), jnp.int32),
                   jax.ShapeDtypeStruct((n, LANES), F32),
                   jax.ShapeDtypeStruct((1, LANES), jnp.int32)],
        scratch_shapes=[pltpu.VMEM((1, LANES), F32)],
        compiler_params=_params("arbitrary"),
        name="mixout_route",
    )(ya, yb, acts, acts, x2, gate1.reshape(bsz, 1, d), scale2.reshape(bsz, 1, d),
      shift2.reshape(bsz, 1, d), gain2.reshape(1, d), wpa, wpb, wo, w_route)


def _start_token_gather(idx_ref, base, ntok, parts, src_hbm, dst, sem):
    for r in range(ntok):
        t0 = pl.multiple_of(idx_ref[base + r] * parts, parts)
        pltpu.make_async_copy(src_hbm.at[pl.ds(t0, parts), :], dst.at[pl.ds(r * parts, parts), :], sem).start()


def _wait_token_gather(ntok, parts, src_hbm, dst, sem):
    pltpu.make_async_copy(src_hbm.at[pl.ds(0, ntok * parts), :], dst.at[pl.ds(0, ntok * parts), :], sem).wait()


def _expert_kernel(be_ref, nv_ref, first_ref, next_ref, dest_ref, nused_ref,
                   h_hbm, wg_hbm, wu_hbm, wd_hbm, o_ref,
                   xbuf, xsems, row_tok, stage_g, stage_u, stage_d, wg_bf, wu_bf, wd_bf, wsems, *, layer):
    b = pl.program_id(0)
    nb = pl.num_programs(0)
    n_used = nused_ref[0]
    unroll = 8
    cast_rows = 256

    def weight_copies(e):
        return (pltpu.make_async_copy(wg_hbm.at[layer, e], stage_g, wsems.at[0]),
                pltpu.make_async_copy(wu_hbm.at[layer, e], stage_u, wsems.at[1]),
                pltpu.make_async_copy(wd_hbm.at[layer, e], stage_d, wsems.at[2]))

    parts = xbuf.shape[1] // MOE_BLOCK
    nslot = xbuf.shape[0]
    ahead = nslot - 1

    def gather(blk, slot):
        _start_token_gather(row_tok, blk * MOE_BLOCK, MOE_BLOCK, parts, h_hbm, xbuf.at[slot], xsems.at[slot])

    def wait_gather(slot):
        _wait_token_gather(MOE_BLOCK, parts, h_hbm, xbuf.at[slot], xsems.at[slot])

    @pl.when(b == 0)
    def _():
        for cp in weight_copies(be_ref[0]):
            cp.start()

        def clear(i, carry):
            for u in range(unroll):
                row_tok[i * unroll + u] = 0
            return carry

        lax.fori_loop(0, row_tok.shape[0] // unroll, clear, 0)

        ntok = dest_ref.shape[0] // TOP_K
        for k in range(TOP_K):
            def invert(i, carry, k=k):
                for u in range(unroll):
                    row_tok[dest_ref[k * ntok + i * unroll + u]] = i * unroll + u
                return carry

            lax.fori_loop(0, ntok // unroll, invert, 0)
        for q in range(ahead):
            gather(q, q)

    @pl.when(b < n_used + ahead)
    def _():
        wait_gather(b % nslot)

    @pl.when(b < n_used)
    def _():
        slot = b % nslot

        @pl.when(first_ref[b] == 1)
        def _():
            for m, (stage, dst) in enumerate(((stage_g, wg_bf), (stage_u, wu_bf), (stage_d, wd_bf))):
                weight_copies(0)[m].wait()

                def cast(i, carry, stage=stage, dst=dst):
                    r0 = pl.multiple_of(i * cast_rows, cast_rows)
                    dst[pl.ds(r0, cast_rows), :] = stage[pl.ds(r0, cast_rows), :].astype(BF16)
                    return carry

                lax.fori_loop(0, stage.shape[0] // cast_rows, cast, 0)

                @pl.when(next_ref[b] >= 0)
                def _(m=m):
                    weight_copies(next_ref[b])[m].start()

        valid = lax.broadcasted_iota(jnp.int32, (MOE_BLOCK, 1), 0) < nv_ref[b]
        xb = jnp.where(valid, _load_token_major(xbuf.at[slot], MOE_BLOCK, parts), 0.0).astype(BF16)
        gather(b + ahead, (b + ahead) % nslot)
        gate = jnp.dot(xb, wg_bf[...], preferred_element_type=F32)
        up = jnp.dot(xb, wu_bf[...], preferred_element_type=F32)
        hid = (_silu(gate) * up).astype(BF16)
        _store_token_major(o_ref, jnp.dot(hid, wd_bf[...], preferred_element_type=F32))

    @pl.when(b >= n_used)
    def _():
        o_ref[...] = jnp.zeros(o_ref.shape, o_ref.dtype)

    @pl.when(b == nb - 1)
    def _():
        for q in range(ahead):
            @pl.when(nb + q < n_used + ahead)
            def _(q=q):
                wait_gather((nb + q) % nslot)


def _experts(h2_tm, wg, wu, wd, layer, plan):
    p = plan["rows"]
    nb = p // MOE_BLOCK
    d, de = wg.shape[-2:]
    parts = d // LANES
    hbm = pl.BlockSpec(memory_space=pl.ANY)
    grid_spec = pltpu.PrefetchScalarGridSpec(
        num_scalar_prefetch=6,
        grid=(nb,),
        in_specs=[hbm, hbm, hbm, hbm],
        out_specs=pl.BlockSpec((MOE_BLOCK * parts, LANES), lambda b, *_: (b, 0)),
        scratch_shapes=[pltpu.VMEM((EXPERT_X_SLOTS, MOE_BLOCK * parts, LANES), F32),
                        pltpu.SemaphoreType.DMA((EXPERT_X_SLOTS,)),
                        pltpu.SMEM((p + (EXPERT_X_SLOTS - 1) * MOE_BLOCK,), jnp.int32),
                        pltpu.VMEM((d, de), F32), pltpu.VMEM((d, de), F32), pltpu.VMEM((de, d), F32),
                        pltpu.VMEM((d, de), BF16), pltpu.VMEM((d, de), BF16), pltpu.VMEM((de, d), BF16),
                        pltpu.SemaphoreType.DMA((3,))],
    )
    return pl.pallas_call(
        functools.partial(_expert_kernel, layer=layer),
        grid_spec=grid_spec,
        out_shape=jax.ShapeDtypeStruct((p * parts, LANES), F32),
        compiler_params=_params("arbitrary"),
        name="experts",
    )(plan["block_expert"], plan["n_valid"], plan["first"], plan["next_expert"], plan["dest"],
      plan["n_used"], h2_tm, wg, wu, wd)


COMBINE_TILE = 64
COMBINE_BATCHES = 8


def _combine_kernel(pos_ref, y_hbm, x1_ref, ew_ref, gate_ref, gain_ref, o_ref, *scratch):
    i = pl.program_id(0)
    tm = COMBINE_TILE
    nbuf = COMBINE_BATCHES
    rows = TOP_K * tm
    bufs, sems = scratch[:nbuf], scratch[nbuf]

    parts = x1_ref.shape[1] // LANES

    def gather(tile, k):
        _start_token_gather(pos_ref, tile * rows, rows, parts, y_hbm, bufs[k], sems.at[k])

    def wait(k):
        _wait_token_gather(rows, parts, y_hbm, bufs[k], sems.at[k])

    def finish(k):
        sl = slice(k * tm, (k + 1) * tm)
        ew = ew_ref[sl, :]
        moe = ew[:, 0:1] * _load_token_major(bufs[k], tm, parts)
        for j in range(1, TOP_K):
            moe = moe + ew[:, j:j + 1] * _load_token_major(bufs[k], tm, parts, row0=j * tm)
        x2 = x1_ref[sl, :] + gate_ref[0] * moe
        o_ref[sl, :] = x2 * lax.rsqrt(jnp.mean(x2 * x2, axis=-1, keepdims=True) + EPS) * gain_ref[...]

    ahead = nbuf - 1

    @pl.when(i == 0)
    def _():
        for q in range(ahead):
            gather(q, q)

    for q in range(nbuf):
        wait(q)
        gather(nbuf * i + q + ahead, (q + ahead) % nbuf)
        finish(q)

    @pl.when(i == pl.num_programs(0) - 1)
    def _():
        for q in range(ahead):
            wait(q)


def _combine(ys, pos_tiles, x1, ew, gate2, final_gain, seq):
    n, d = x1.shape
    bsz = n // seq
    tm = COMBINE_BATCHES * COMBINE_TILE
    tpb = seq // tm
    grid_spec = pltpu.PrefetchScalarGridSpec(
        num_scalar_prefetch=1,
        grid=(n // tm,),
        in_specs=[pl.BlockSpec(memory_space=pl.ANY),
                  pl.BlockSpec((tm, d), lambda i, pos: (i, 0)),
                  pl.BlockSpec((tm, LANES), lambda i, pos: (i, 0)),
                  pl.BlockSpec((1, 1, d), lambda i, pos: (i // tpb, 0, 0)),
                  pl.BlockSpec((1, d), lambda i, pos: (0, 0))],
        out_specs=pl.BlockSpec((tm, d), lambda i, pos: (i, 0)),
        scratch_shapes=[pltpu.VMEM((TOP_K * COMBINE_TILE * (d // LANES), LANES), F32)] * COMBINE_BATCHES
        + [pltpu.SemaphoreType.DMA((COMBINE_BATCHES,))],
    )
    return pl.pallas_call(
        _combine_kernel,
        grid_spec=grid_spec,
        out_shape=jax.ShapeDtypeStruct((n, d), F32),
        compiler_params=_params("arbitrary"),
        name="combine_norm",
    )(pos_tiles, ys, x1, ew, gate2.reshape(bsz, 1, d), final_gain.reshape(1, d))


def _dispatch_plan(route, counts):
    n = route.shape[1]
    a = n * TOP_K
    eid = route[:TOP_K]
    rank = route[TOP_K:2 * TOP_K]
    cnt = counts[0, :N_EXPERTS]
    padded = (cnt + MOE_BLOCK - 1) // MOE_BLOCK * MOE_BLOCK
    pad_ends = jnp.cumsum(padded)
    pad_starts = pad_ends - padded
    experts = jnp.arange(N_EXPERTS, dtype=jnp.int32)
    dest = (jnp.sum(jnp.where(eid[..., None] == experts, pad_starts, 0), axis=-1) + rank).astype(jnp.int32)
    p = ((a + MOE_BLOCK - 1) // MOE_BLOCK + N_EXPERTS) * MOE_BLOCK
    nb = p // MOE_BLOCK
    n_used = (pad_ends[-1] // MOE_BLOCK).astype(jnp.int32)
    blk = jnp.arange(nb, dtype=jnp.int32)
    blk_start = jnp.minimum(blk, n_used - 1) * MOE_BLOCK
    block_expert = jnp.minimum(jnp.sum((pad_ends[None, :] <= blk_start[:, None]).astype(jnp.int32), axis=1),
                               N_EXPERTS - 1).astype(jnp.int32)
    def lookup(table, idx):
        return jnp.sum(jnp.where(idx[..., None] == jnp.arange(table.shape[0], dtype=jnp.int32), table, 0), axis=-1)

    blk_pad_start = lookup(pad_starts, block_expert)
    n_valid = jnp.clip(blk_pad_start + lookup(cnt, block_expert) - blk_start, 0, MOE_BLOCK).astype(jnp.int32)
    first = ((blk_start == blk_pad_start) & (blk < n_used)).astype(jnp.int32)
    nxt_blk = lookup(pad_ends, block_expert) // MOE_BLOCK
    next_expert = jnp.where(nxt_blk < n_used, lookup(block_expert, jnp.minimum(nxt_blk, nb - 1)), -1).astype(jnp.int32)
    pos_tiles = dest.reshape(TOP_K, n // COMBINE_TILE, COMBINE_TILE).transpose(1, 0, 2).reshape(a)
    pos_tiles = jnp.concatenate(
        [pos_tiles, jnp.zeros(((COMBINE_BATCHES - 1) * TOP_K * COMBINE_TILE,), jnp.int32)])
    plan = dict(block_expert=block_expert, n_valid=n_valid, first=first, next_expert=next_expert,
                dest=dest.reshape(a), n_used=n_used.reshape(1), rows=p)
    return plan, pos_tiles


def kernel(x, c, w_ada, b_ada, norm1, w_in, hg_lb, hg_norm, rg_conv_w, rg_conv_b, rg_w_a, rg_b_a,
           rg_w_x, rg_b_x, rg_lambda, w_proj_a, w_proj_b, w_out, norm2, w_group, w_router, w_gate,
           w_up, w_down, final_norm):
    bsz, seq, d = x.shape
    depth = w_ada.shape[0]
    n = bsz * seq
    x2 = x.reshape(n, d)
    lb_table = hg_lb.reshape(hg_lb.shape[0], -1)
    for layer in range(depth):
        mod = _adaln(c, w_ada[layer], b_ada[layer])
        shift1, scale1, gate1, shift2, scale2, gate2 = jnp.split(mod, 6, axis=-1)
        logf, acts = _inproj(x2, norm1[layer], scale1, shift1, w_in, lb_table, layer, seq)
        acts3 = acts.reshape(bsz, seq, acts.shape[1])
        ya = _hgrn(acts3, logf.reshape(bsz, seq, logf.shape[1]), hg_norm[layer])
        yb = _rglru(acts3, rg_conv_w[layer], rg_conv_b[layer], rg_w_a[layer], rg_b_a[layer],
                    rg_w_x[layer], rg_b_x[layer], rg_lambda[layer])
        w_route = jnp.concatenate(
            [w_group[layer], w_router[layer].reshape(d, N_EXPERTS),
             jnp.zeros((d, LANES - N_GROUPS - N_EXPERTS), F32)], axis=1)
        w_route_hi = w_route.astype(BF16)
        w_route = jnp.concatenate([w_route_hi, (w_route - w_route_hi.astype(F32)).astype(BF16)], axis=1)
        x1, h2, route, ew, counts = _mixout(
            ya.reshape(n, -1), yb.reshape(n, -1), acts, x2, gate1, scale2, shift2, norm2[layer],
            w_proj_a[layer].astype(BF16), w_proj_b[layer].astype(BF16), w_out[layer].astype(BF16), w_route, seq)
        plan, pos_tiles = _dispatch_plan(route, counts)
        ys = _experts(h2, w_gate, w_up, w_down, layer, plan)
        if layer + 1 < depth:
            raise NotImplementedError("the fused combine + final norm assumes a single layer")
        x2 = _combine(ys, pos_tiles, x1, ew, gate2, final_norm, seq)
    return x2.reshape(bsz, seq, d)
```

```python
import functools

import jax
import jax.numpy as jnp
import numpy as np
from jax import lax
from jax.experimental import pallas as pl
from jax.experimental.pallas import tpu as pltpu

EPS = 1e-6
LANES = 128
SUBLANES = 8
HG_HEADS = 8
RG_BLOCKS = 8
RG_CONV = 4
RG_C = 8.0
N_GROUPS = 4
EXPERTS_PER_GROUP = 8
N_EXPERTS = N_GROUPS * EXPERTS_PER_GROUP
TOP_K = 2
HG_CHUNK = 128
MOE_BLOCK = 128
EXPERT_X_SLOTS = 6
VMEM_LIMIT = 56 * 1024 * 1024
ACT_Q, ACT_V, ACT_G, ACT_RX, ACT_GY = 32, 40, 48, 56, 64

F32 = jnp.float32
BF16 = jnp.bfloat16


def _sigmoid(x):
    return 1.0 / (1.0 + jnp.exp(-x))


def _silu(x):
    return x * _sigmoid(x)


def _gelu_tanh(x):
    return 0.5 * x * (1.0 + jnp.tanh(0.7978845608028654 * (x + 0.044715 * (x * x * x))))


def _params(*sem):
    return pltpu.CompilerParams(dimension_semantics=sem, vmem_limit_bytes=VMEM_LIMIT)


def _adaln_kernel(ct_ref, w_ref, b_ref, o_ref, *, kc):
    kdim, nb = ct_ref.shape
    tn = w_ref.shape[1]

    def body(i, accs):
        k0 = pl.multiple_of(i * kc, kc)
        w = w_ref[pl.ds(k0, kc), :]
        cs = _silu(ct_ref[pl.ds(k0, kc), :])
        return tuple(acc + jnp.sum(w * cs[:, b:b + 1], axis=0, keepdims=True)
                     for b, acc in enumerate(accs))

    accs = lax.fori_loop(0, kdim // kc, body, tuple(jnp.zeros((1, tn), F32) for _ in range(nb)))
    for b, acc in enumerate(accs):
        o_ref[b:b + 1, :] = acc + b_ref[...]


def _adaln(c, w, bias):
    bsz, d = c.shape
    n = w.shape[1]
    tn = 1024
    return pl.pallas_call(
        functools.partial(_adaln_kernel, kc=256),
        grid=(n // tn,),
        in_specs=[pl.BlockSpec((d, bsz), lambda j: (0, 0)),
                  pl.BlockSpec((d, tn), lambda j: (0, j)),
                  pl.BlockSpec((1, tn), lambda j: (0, j))],
        out_specs=pl.BlockSpec((bsz, tn), lambda j: (0, j)),
        out_shape=jax.ShapeDtypeStruct((bsz, n), F32),
        compiler_params=_params("parallel"),
        name="adaln",
    )(c.T, w, bias.reshape(1, n))


def _norm_mod_kernel(x_ref, gain_ref, scale_ref, shift_ref, h_ref):
    rows = 64

    def body(i, carry):
        r0 = pl.multiple_of(i * rows, rows)
        x = x_ref[pl.ds(r0, rows), :]
        y = x * lax.rsqrt(jnp.mean(x * x, axis=-1, keepdims=True) + EPS) * gain_ref[...]
        h_ref[pl.ds(r0, rows), :] = (y * (1.0 + scale_ref[0]) + shift_ref[0]).astype(BF16)
        return carry

    lax.fori_loop(0, x_ref.shape[0] // rows, body, 0)


def _cast_weight_tile(w_ref, w_scr):
    rows = 256

    @pl.when(pl.program_id(1) == 0)
    def _():
        def body(i, carry):
            r0 = pl.multiple_of(i * rows, rows)
            w_scr[pl.ds(r0, rows), :] = w_ref[pl.ds(r0, rows), :].astype(BF16)
            return carry

        lax.fori_loop(0, w_ref.shape[0] // rows, body, 0)


def _inproj_logf_kernel(h_ref, w_ref, lbt_ref, o_ref, w_scr, *, layer):
    _cast_weight_tile(w_ref, w_scr)
    t = lbt_ref[...]
    e = jnp.exp(t - jnp.max(t, axis=0, keepdims=True))
    lb = jnp.sum(e[:layer + 1], axis=0, keepdims=True) / jnp.sum(e, axis=0, keepdims=True)
    z = jnp.dot(h_ref[...], w_scr[...], preferred_element_type=F32)
    o_ref[...] = jnp.log(lb + (1.0 - lb) * _sigmoid(z))


def _inproj_act_kernel(h_ref, w_ref, o_ref, w_scr):
    j = pl.program_id(0)
    _cast_weight_tile(w_ref, w_scr)
    z = jnp.dot(h_ref[...], w_scr[...], preferred_element_type=F32)

    @pl.when((j == 4) | (j == 6))
    def _():
        o_ref[...] = _silu(z).astype(BF16)

    @pl.when((j < 4) | (j == 5) | (j == 7))
    def _():
        o_ref[...] = z.astype(BF16)

    @pl.when(j == 8)
    def _():
        o_ref[...] = _gelu_tanh(z).astype(BF16)


def _inproj(x2, gain, scale, shift, w_in, lb_table, layer, seq):
    n, d = x2.shape
    bsz = n // seq
    tn = 1024
    tm = 512
    per_batch = pl.BlockSpec((1, 1, d), lambda i: (i // (seq // tm), 0, 0))
    h = pl.pallas_call(
        _norm_mod_kernel,
        grid=(n // tm,),
        in_specs=[pl.BlockSpec((tm, d), lambda i: (i, 0)), pl.BlockSpec((1, d), lambda i: (0, 0)),
                  per_batch, per_batch],
        out_specs=pl.BlockSpec((tm, d), lambda i: (i, 0)),
        out_shape=jax.ShapeDtypeStruct((n, d), BF16),
        compiler_params=_params("parallel"),
        name="norm_mod",
    )(x2, gain.reshape(1, d), scale.reshape(bsz, 1, d), shift.reshape(bsz, 1, d))

    n_lb = lb_table.shape[0]
    tm = 1024
    logf = pl.pallas_call(
        functools.partial(_inproj_logf_kernel, layer=layer),
        grid=(2, n // tm),
        in_specs=[pl.BlockSpec((tm, d), lambda j, i: (i, 0)),
                  pl.BlockSpec((None, d, tn), lambda j, i: (layer, 0, j + 2)),
                  pl.BlockSpec((n_lb, tn), lambda j, i: (0, j))],
        out_specs=pl.BlockSpec((tm, tn), lambda j, i: (i, j)),
        out_shape=jax.ShapeDtypeStruct((n, 2 * tn), F32),
        scratch_shapes=[pltpu.VMEM((d, tn), BF16)],
        compiler_params=_params("parallel", "arbitrary"),
        name="inproj_logf",
    )(h, w_in, lb_table)
    acts = pl.pallas_call(
        _inproj_act_kernel,
        grid=(9, n // tm),
        in_specs=[pl.BlockSpec((tm, d), lambda j, i: (i, 0)),
                  pl.BlockSpec((None, d, tn),
                               lambda j, i: (layer, 0, jnp.where(j < 4, j + 7, jnp.where(j < 6, j - 4, j - 2))))],
        out_specs=pl.BlockSpec((tm, tn), lambda j, i: (i, j)),
        out_shape=jax.ShapeDtypeStruct((n, 9 * tn), BF16),
        scratch_shapes=[pltpu.VMEM((d, tn), BF16)],
        compiler_params=_params("parallel", "arbitrary"),
        name="inproj_act",
    )(h, w_in)
    return logf, acts


def _hgrn_level_halves(chunk):
    return [chunk >> (i + 1) for i in range(chunk.bit_length() - 1)]


def _hgrn_kernel(q_ref, v_ref, g_ref, lff_ref, lfb_ref, gain_ref, o_ref,
                 oacc, qf_scr, kf_scr, qb_scr, kb_scr, *, chunk, group):
    seq = q_ref.shape[1]
    nchunks = seq // chunk
    halves = _hgrn_level_halves(chunk)
    nt = (((1,), (1,)), ((), ()))
    tn = (((0,), (0,)), ((), ()))

    row = lax.broadcasted_iota(jnp.int32, (chunk, LANES), 0)
    ti = lax.broadcasted_iota(jnp.int32, (chunk, chunk), 0)
    si = lax.broadcasted_iota(jnp.int32, (chunk, chunk), 1)
    txs = ti ^ si
    tri_f = jnp.where(si <= ti, 1.0, 0.0).astype(BF16)
    tri_b = jnp.where(si >= ti, 1.0, 0.0).astype(BF16)

    def cumulative(lf, tri):
        hi = lf.astype(BF16)
        mid = (lf - hi.astype(F32)).astype(BF16)
        r = jnp.dot(tri, jnp.concatenate([hi, mid], axis=1), preferred_element_type=F32)
        return r[:, :LANES] + r[:, LANES:]

    lid = functools.reduce(lambda a, b: a + b, [(txs >= h).astype(jnp.int32) for h in halves])

    def boundary_diff(cum, m, reverse):
        pieces = []
        for base in range(0, chunk, 2 * m):
            bnd = base + (m if reverse else m - 1)
            pieces.append(jnp.broadcast_to(cum[bnd:bnd + 1, :], (2 * m, LANES)))
        return cum - jnp.concatenate(pieces, axis=0)

    def intra(c):
        r0 = pl.multiple_of(c * chunk, chunk)
        q = q_ref[0, pl.ds(r0, chunk), :].astype(F32)
        v = v_ref[0, pl.ds(r0, chunk), :]
        lff = lff_ref[0, pl.ds(r0, chunk), :]
        lfb = lfb_ref[0, pl.ds(r0, chunk), :]
        cum_f = cumulative(lff, tri_f)
        cum_b = cumulative(lfb, tri_b)
        ff = jnp.exp(lff)
        fb = jnp.exp(lfb)
        kf = 1.0 - ff
        kb = 1.0 - fb

        p0 = lax.dot_general(q.astype(BF16), (kf + kb).astype(BF16), nt, preferred_element_type=F32)
        scores = jnp.where(lid == 0, p0, 0.0)
        for m in halves:
            later = (row & m) != 0
            if 2 * m >= SUBLANES:
                d_f = boundary_diff(cum_f, m, False)
                d_b = boundary_diff(cum_b, m, True)
                x = q * jnp.exp(jnp.where(later, d_f, d_b))
                y = jnp.where(later, kb, kf) * jnp.exp(-jnp.where(later, d_b, d_f))
            elif m == 2:
                r4 = row & 3
                prev = lambda a: pltpu.roll(a, 1, 0)
                nxt = lambda a: pltpu.roll(a, chunk - 1, 0)
                x = q * jnp.where(r4 == 0, fb * nxt(fb), jnp.where(r4 == 1, fb,
                                  jnp.where(r4 == 2, ff, ff * prev(ff))))
                y = jnp.where(r4 == 0, kf * nxt(ff), jnp.where(r4 == 1, kf,
                              jnp.where(r4 == 2, kb, kb * prev(fb))))
            else:
                x = q * jnp.where(later, ff, fb)
                y = jnp.where(later, kb, kf)
            pm = lax.dot_general(x.astype(BF16), y.astype(BF16), nt, preferred_element_type=F32)
            scores = jnp.where(lid == m.bit_length(), pm, scores)
        oacc[pl.ds(r0, chunk), :] = jnp.dot(scores.astype(BF16), v, preferred_element_type=F32)
        tot_f = cum_f[chunk - 1:chunk, :]
        tot_b = cum_b[0:1, :]
        qf_scr[pl.ds(r0, chunk), :] = (q * jnp.exp(cum_f)).astype(BF16)
        kf_scr[pl.ds(r0, chunk), :] = (kf * jnp.exp(tot_f - cum_f)).astype(BF16)
        qb_scr[pl.ds(r0, chunk), :] = (q * jnp.exp(cum_b)).astype(BF16)
        kb_scr[pl.ds(r0, chunk), :] = (kb * jnp.exp(tot_b - cum_b)).astype(BF16)

    def intra_body(i, carry):
        for u in range(group):
            intra(i * group + u)
        return carry

    lax.fori_loop(0, nchunks // group, intra_body, 0)

    def apply_state(c, st, q_scr, k_scr, lf_ref):
        r0 = pl.multiple_of(c * chunk, chunk)
        o = oacc[pl.ds(r0, chunk), :] + lax.dot_general(
            q_scr[pl.ds(r0, chunk), :], st.astype(BF16), nt, preferred_element_type=F32)
        tot = jnp.sum(lf_ref[0, pl.ds(r0, chunk), :], axis=0, keepdims=True)
        st = st * jnp.exp(tot) + lax.dot_general(
            v_ref[0, pl.ds(r0, chunk), :], k_scr[pl.ds(r0, chunk), :], tn, preferred_element_type=F32)
        return r0, o, st

    def finish(r0, o):
        y = o * lax.rsqrt(jnp.mean(o * o, axis=-1, keepdims=True) + EPS) * gain_ref[0]
        o_ref[0, pl.ds(r0, chunk), :] = (y * g_ref[0, pl.ds(r0, chunk), :].astype(F32)).astype(BF16)

    def sweep(final):
        def body(i, states):
            st_f, st_b = states
            rf, of, st_f = apply_state(i, st_f, qf_scr, kf_scr, lff_ref)
            rb, ob, st_b = apply_state(nchunks - 1 - i, st_b, qb_scr, kb_scr, lfb_ref)
            if final:
                finish(rf, of)
                finish(rb, ob)
            else:
                oacc[pl.ds(rf, chunk), :] = of
                oacc[pl.ds(rb, chunk), :] = ob
            return st_f, st_b
        return body

    zero = jnp.zeros((LANES, LANES), F32)
    states = lax.fori_loop(0, nchunks // 2, sweep(False), (zero, zero), unroll=8)
    lax.fori_loop(nchunks // 2, nchunks, sweep(True), states, unroll=8)


def _hgrn(acts3, logf3, hg_norm):
    bsz, seq, _ = acts3.shape
    col = lambda off: pl.BlockSpec((1, seq, LANES), lambda b, h: (b, 0, off + h))
    return pl.pallas_call(
        functools.partial(_hgrn_kernel, chunk=HG_CHUNK, group=8),
        grid=(bsz, HG_HEADS),
        in_specs=[col(ACT_Q), col(ACT_V), col(ACT_G), col(0), col(HG_HEADS),
                  pl.BlockSpec((1, 1, LANES), lambda b, h: (h, 0, 0))],
        out_specs=pl.BlockSpec((1, seq, LANES), lambda b, h: (b, 0, h)),
        out_shape=jax.ShapeDtypeStruct((bsz, seq, HG_HEADS * LANES), BF16),
        scratch_shapes=[pltpu.VMEM((seq, LANES), F32)] + [pltpu.VMEM((seq, LANES), BF16)] * 4,
        compiler_params=_params("parallel", "parallel"),
        name="hgrn2",
    )(acts3, acts3, acts3, logf3, logf3, hg_norm.reshape(HG_HEADS, 1, LANES))


RG_SCAN_SUB = 32


def _scan_tile(a, u, carry, reverse):
    rows = a.shape[0]
    sub = RG_SCAN_SUB
    pos = lax.broadcasted_iota(jnp.int32, a.shape, 0) & (sub - 1)
    d = 1
    while d < sub:
        if reverse:
            ok = pos < sub - d
            shift = rows - d
        else:
            ok = pos >= d
            shift = d
        a_s = jnp.where(ok, pltpu.roll(a, shift, 0), 1.0)
        u_s = jnp.where(ok, pltpu.roll(u, shift, 0), 0.0)
        u = u + a * u_s
        a = a * a_s
        d *= 2
    spans = list(range(rows // sub))
    pieces = [None] * len(spans)
    for k in (reversed(spans) if reverse else spans):
        hk = u[k * sub:(k + 1) * sub] + a[k * sub:(k + 1) * sub] * carry
        carry = hk[0:1] if reverse else hk[sub - 1:sub]
        pieces[k] = hk
    return jnp.concatenate(pieces, axis=0), carry


def _rglru_kernel(rx_ref, gy_ref, cw_ref, cb_ref, wa_ref, ba_ref, wx_ref, bx_ref, lam_ref, o_ref,
                  xpad, hf_scr, ab_scr, ub_scr, *, tile):
    seq = rx_ref.shape[1]
    ntiles = seq // tile
    halo = 8

    xpad[0:halo, :] = jnp.zeros((halo, LANES), F32)
    xpad[halo + seq:halo + seq + halo, :] = jnp.zeros((halo, LANES), F32)
    xpad[halo:halo + seq, :] = rx_ref[0].astype(F32)

    lam = lam_ref[...]
    neg_c_softplus = -RG_C * (jnp.maximum(-lam, 0.0) + jnp.log(1.0 + jnp.exp(-jnp.abs(lam))))

    def gates(xcb, z):
        r = _sigmoid(jnp.dot(xcb, wa_ref[z, 0], preferred_element_type=F32) + ba_ref[z, 0])
        gi = _sigmoid(jnp.dot(xcb, wx_ref[z, 0], preferred_element_type=F32) + bx_ref[z, 0])
        a = jnp.exp(r * neg_c_softplus[z:z + 1, :])
        return a, jnp.sqrt(1.0 - a * a) * gi

    def fwd_body(i, carry):
        t0 = pl.multiple_of(i * tile, tile)
        ext = xpad[pl.ds(t0, tile + 2 * halo), :]
        n_ext = tile + 2 * halo
        tap = lambda off: pltpu.roll(ext, (-off) % n_ext, 0)[halo:halo + tile]
        xc = (cw_ref[0:1, :] * tap(-2) + cw_ref[1:2, :] * tap(-1) + cw_ref[2:3, :] * ext[halo:halo + tile]
              + cw_ref[3:4, :] * tap(1) + cb_ref[...])
        xcb = xc.astype(BF16)
        a_f, s_f = gates(xcb, 0)
        a_b, s_b = gates(xcb, 1)
        h, carry = _scan_tile(a_f, s_f * xc, carry, reverse=False)
        hf_scr[pl.ds(t0, tile), :] = h
        ab_scr[pl.ds(t0, tile), :] = a_b
        ub_scr[pl.ds(t0, tile), :] = s_b * xc
        return carry

    lax.fori_loop(0, ntiles, fwd_body, jnp.zeros((1, LANES), F32))

    def bwd_body(i, carry):
        t0 = pl.multiple_of((ntiles - 1 - i) * tile, tile)
        h, carry = _scan_tile(ab_scr[pl.ds(t0, tile), :], ub_scr[pl.ds(t0, tile), :], carry, reverse=True)
        o_ref[0, pl.ds(t0, tile), :] = (
            (hf_scr[pl.ds(t0, tile), :] + h) * gy_ref[0, pl.ds(t0, tile), :].astype(F32)).astype(BF16)
        return carry

    lax.fori_loop(0, ntiles, bwd_body, jnp.zeros((1, LANES), F32))


def _rglru(acts3, conv_w, conv_b, w_a, b_a, w_x, b_x, lam):
    bsz, seq, _ = acts3.shape
    width = RG_BLOCKS * LANES
    tile = 256
    rx_off, gy_off = ACT_RX, ACT_GY
    col = lambda off: pl.BlockSpec((1, seq, LANES), lambda b, j: (b, 0, off + j))
    wspec = pl.BlockSpec((2, 1, LANES, LANES), lambda b, j: (0, j, 0, 0))
    bspec = pl.BlockSpec((2, 1, 1, LANES), lambda b, j: (0, j, 0, 0))
    return pl.pallas_call(
        functools.partial(_rglru_kernel, tile=tile),
        grid=(bsz, RG_BLOCKS),
        in_specs=[col(rx_off), col(gy_off),
                  pl.BlockSpec((RG_CONV, LANES), lambda b, j: (0, j)),
                  pl.BlockSpec((1, LANES), lambda b, j: (0, j)),
                  wspec, bspec, wspec, bspec,
                  pl.BlockSpec((2, LANES), lambda b, j: (0, j))],
        out_specs=pl.BlockSpec((1, seq, LANES), lambda b, j: (b, 0, j)),
        out_shape=jax.ShapeDtypeStruct((bsz, seq, width), BF16),
        scratch_shapes=[pltpu.VMEM((seq + 16, LANES), F32),
                        pltpu.VMEM((seq, LANES), F32),
                        pltpu.VMEM((seq, LANES), F32),
                        pltpu.VMEM((seq, LANES), F32)],
        compiler_params=_params("parallel", "parallel"),
        name="rglru",
    )(acts3, acts3, conv_w, conv_b.reshape(1, width), w_a.astype(BF16),
      b_a.reshape(2, RG_BLOCKS, 1, LANES), w_x.astype(BF16), b_x.reshape(2, RG_BLOCKS, 1, LANES), lam)


def _store_token_major(ref, x, row0=0):
    ntok, d = x.shape
    parts = d // LANES
    for c in range(parts):
        ref[pl.ds(row0 * parts + c, ntok, stride=parts), :] = x[:, c * LANES:(c + 1) * LANES]


def _load_token_major(ref, ntok, parts, row0=0):
    return jnp.concatenate([ref[pl.ds(row0 * parts + c, ntok, stride=parts), :] for c in range(parts)], axis=1)


def _mixout_kernel(ya_ref, yb_ref, ga_ref, gb_ref, x_ref, gate_ref, scale_ref, shift_ref, gain_ref,
                   wpa_ref, wpb_ref, wo_ref, wr_ref, x1_ref, h2_ref, route_ref, ew_ref, cnt_ref, running):
    y_a = jnp.dot(ya_ref[...], wpa_ref[...], preferred_element_type=F32)
    y_b = jnp.dot(yb_ref[...], wpb_ref[...], preferred_element_type=F32)
    merged = _sigmoid(ga_ref[...].astype(F32)) * y_a + _sigmoid(gb_ref[...].astype(F32)) * y_b
    mix = jnp.dot(merged.astype(BF16), wo_ref[...], preferred_element_type=F32)
    x1 = x_ref[...] + gate_ref[0] * mix
    x1_ref[...] = x1
    y = x1 * lax.rsqrt(jnp.mean(x1 * x1, axis=-1, keepdims=True) + EPS) * gain_ref[...]
    h2 = y * (1.0 + scale_ref[0]) + shift_ref[0]
    _store_token_major(h2_ref, h2)

    h_hi = h2.astype(BF16)
    h_lo = (h2 - h_hi.astype(F32)).astype(BF16)
    p_hi = jnp.dot(h_hi, wr_ref[...], preferred_element_type=F32)
    p_lo = jnp.dot(h_lo, wr_ref[...], preferred_element_type=F32)
    logits = (p_hi[:, :LANES] + p_hi[:, LANES:]) + (p_lo[:, :LANES] + p_lo[:, LANES:])
    lane = lax.broadcasted_iota(jnp.int32, logits.shape, 1)
    neg = -jnp.inf
    gl = jnp.where(lane < N_GROUPS, logits, neg)
    gmax = jnp.max(gl, axis=-1, keepdims=True)
    g_sel = jnp.min(jnp.where(gl == gmax, lane, LANES), axis=-1, keepdims=True)
    p_group = 1.0 / jnp.sum(jnp.exp(gl - gmax), axis=-1, keepdims=True)
    lo = N_GROUPS + g_sel * EXPERTS_PER_GROUP
    el = jnp.where((lane >= lo) & (lane < lo + EXPERTS_PER_GROUP), logits, neg)
    v1 = jnp.max(el, axis=-1, keepdims=True)
    i1 = jnp.min(jnp.where(el == v1, lane, LANES), axis=-1, keepdims=True)
    el2 = jnp.where(lane == i1, neg, el)
    v2 = jnp.max(el2, axis=-1, keepdims=True)
    i2 = jnp.min(jnp.where(el2 == v2, lane, LANES), axis=-1, keepdims=True)
    e21 = jnp.exp(v2 - v1)
    w1 = p_group / (1.0 + e21)
    w2 = p_group * e21 / (1.0 + e21)
    ew_ref[...] = jnp.where(lane == 0, w1, jnp.where(lane == 1, w2, 0.0))

    @pl.when(pl.program_id(0) == 0)
    def _():
        running[...] = jnp.zeros(running.shape, F32)

    e1, e2 = i1 - N_GROUPS, i2 - N_GROUPS
    tm = logits.shape[0]
    before = (lax.broadcasted_iota(jnp.int32, (tm, tm), 0)
              > lax.broadcasted_iota(jnp.int32, (tm, tm), 1))
    before = jnp.where(before, 1.0, 0.0).astype(BF16)
    oh1 = jnp.where(lane == e1, 1.0, 0.0)
    oh2 = jnp.where(lane == e2, 1.0, 0.0)
    pre1 = jnp.dot(before, oh1.astype(BF16), preferred_element_type=F32)
    pre2 = jnp.dot(before, oh2.astype(BF16), preferred_element_type=F32)
    cnt1 = jnp.sum(oh1, axis=0, keepdims=True)
    base = running[...]
    r1 = jnp.sum(oh1 * (pre1 + base), axis=-1, keepdims=True).astype(jnp.int32)
    r2 = jnp.sum(oh2 * (pre2 + base + cnt1), axis=-1, keepdims=True).astype(jnp.int32)
    total = base + cnt1 + jnp.sum(oh2, axis=0, keepdims=True)
    running[...] = total
    cnt_ref[...] = total.astype(jnp.int32)
    cols = jnp.where(lane == 0, e1, jnp.where(lane == 1, e2,
                     jnp.where(lane == 2, r1 >> 7, jnp.where(lane == 3, r1 & 127,
                               jnp.where(lane == 4, r2 >> 7, jnp.where(lane == 5, r2 & 127, 0))))))
    srow = lax.broadcasted_iota(jnp.int32, (SUBLANES, LANES), 0)
    slane = lax.broadcasted_iota(jnp.int32, (SUBLANES, LANES), 1)
    sel = jnp.where((srow < 2) & (slane == srow), 1.0,
                    jnp.where((srow >= 2) & (srow < 4) & (slane == 2 * srow - 2), 128.0,
                              jnp.where((srow >= 2) & (srow < 4) & (slane == 2 * srow - 1), 1.0, 0.0)))
    route_t = lax.dot_general(sel.astype(BF16), cols.astype(F32).astype(BF16), (((1,), (1,)), ((), ())),
                              preferred_element_type=F32)
    route_ref[...] = route_t.astype(jnp.int32)


def _mixout(ya, yb, acts, x2, gate1, scale2, shift2, gain2, wpa, wpb, wo, w_route, seq):
    n, d = x2.shape
    bsz = n // seq
    tm = 256
    tpb = seq // tm
    half = ya.shape[1]
    ga_blk = 0
    per_batch = pl.BlockSpec((1, 1, d), lambda i: (i // tpb, 0, 0))
    const = lambda shape: pl.BlockSpec(shape, lambda i: (0, 0), pipeline_mode=pl.Buffered(1))
    return pl.pallas_call(
        _mixout_kernel,
        grid=(n // tm,),
        in_specs=[pl.BlockSpec((tm, half), lambda i: (i, 0)),
                  pl.BlockSpec((tm, half), lambda i: (i, 0)),
                  pl.BlockSpec((tm, d), lambda i: (i, ga_blk)),
                  pl.BlockSpec((tm, d), lambda i: (i, ga_blk + 1)),
                  pl.BlockSpec((tm, d), lambda i: (i, 0)),
                  per_batch, per_batch, per_batch,
                  const((1, d)), const((half, d)), const((half, d)), const((d, d)), const((d, 2 * LANES))],
        out_specs=[pl.BlockSpec((tm, d), lambda i: (i, 0)),
                   pl.BlockSpec((tm * (d // LANES), LANES), lambda i: (i, 0)),
                   pl.BlockSpec((SUBLANES, tm), lambda i: (0, i)),
                   pl.BlockSpec((tm, LANES), lambda i: (i, 0)),
                   pl.BlockSpec((1, LANES), lambda i: (0, 0))],
        out_shape=[jax.ShapeDtypeStruct((n, d), F32),
                   jax.ShapeDtypeStruct((n * (d // LANES), LANES), F32),
                   jax.ShapeDtypeStruct((SUBLANES, n), jnp.int32),
                   jax.ShapeDtypeStruct((n, LANES), F32),
                   jax.ShapeDtypeStruct((1, LANES), jnp.int32)],
        scratch_shapes=[pltpu.VMEM((1, LANES), F32)],
        compiler_params=_params("arbitrary"),
        name="mixout_route",
    )(ya, yb, acts, acts, x2, gate1.reshape(bsz, 1, d), scale2.reshape(bsz, 1, d),
      shift2.reshape(bsz, 1, d), gain2.reshape(1, d), wpa, wpb, wo, w_route)


def _start_token_gather(idx_ref, base, ntok, parts, src_hbm, dst, sem):
    for r in range(ntok):
        t0 = pl.multiple_of(idx_ref[base + r] * parts, parts)
        pltpu.make_async_copy(src_hbm.at[pl.ds(t0, parts), :], dst.at[pl.ds(r * parts, parts), :], sem).start()


def _wait_token_gather(ntok, parts, src_hbm, dst, sem):
    pltpu.make_async_copy(src_hbm.at[pl.ds(0, ntok * parts), :], dst.at[pl.ds(0, ntok * parts), :], sem).wait()


def _expert_kernel(be_ref, nv_ref, first_ref, next_ref, dest_ref, nused_ref,
                   h_hbm, wg_hbm, wu_hbm, wd_hbm, zeros_hbm, o_ref,
                   xbuf, xsems, row_tok, stage_g, stage_u, stage_d, wg_bf, wu_bf, wd_bf, wsems, *, layer):
    b = pl.program_id(0)
    nb = pl.num_programs(0)
    n_used = nused_ref[0]
    unroll = 8
    cast_rows = 256

    def weight_copies(e):
        return (pltpu.make_async_copy(wg_hbm.at[layer, e], stage_g, wsems.at[0]),
                pltpu.make_async_copy(wu_hbm.at[layer, e], stage_u, wsems.at[1]),
                pltpu.make_async_copy(wd_hbm.at[layer, e], stage_d, wsems.at[2]))

    parts = xbuf.shape[1] // MOE_BLOCK
    nslot = xbuf.shape[0]
    ahead = nslot - 1

    def gather(blk, slot):
        _start_token_gather(row_tok, blk * MOE_BLOCK, MOE_BLOCK, parts, h_hbm, xbuf.at[slot], xsems.at[slot])

    def wait_gather(slot):
        _wait_token_gather(MOE_BLOCK, parts, h_hbm, xbuf.at[slot], xsems.at[slot])

    @pl.when(b == 0)
    def _():
        for cp in weight_copies(be_ref[0]):
            cp.start()

        clear = pltpu.make_async_copy(zeros_hbm, row_tok, wsems.at[3])
        clear.start()
        clear.wait()

        ntok = dest_ref.shape[0] // TOP_K
        for k in range(TOP_K):
            def invert(i, carry, k=k):
                for u in range(unroll):
                    row_tok[dest_ref[k * ntok + i * unroll + u]] = i * unroll + u
                return carry

            lax.fori_loop(0, ntok // unroll, invert, 0)
        for q in range(ahead):
            gather(q, q)

    @pl.when(b < n_used + ahead)
    def _():
        wait_gather(b % nslot)

    @pl.when(b < n_used)
    def _():
        slot = b % nslot

        @pl.when(first_ref[b] == 1)
        def _():
            for m, (stage, dst) in enumerate(((stage_g, wg_bf), (stage_u, wu_bf), (stage_d, wd_bf))):
                weight_copies(0)[m].wait()

                def cast(i, carry, stage=stage, dst=dst):
                    r0 = pl.multiple_of(i * cast_rows, cast_rows)
                    dst[pl.ds(r0, cast_rows), :] = stage[pl.ds(r0, cast_rows), :].astype(BF16)
                    return carry

                lax.fori_loop(0, stage.shape[0] // cast_rows, cast, 0)

                @pl.when(next_ref[b] >= 0)
                def _(m=m):
                    weight_copies(next_ref[b])[m].start()

        valid = lax.broadcasted_iota(jnp.int32, (MOE_BLOCK, 1), 0) < nv_ref[b]
        xb = jnp.where(valid, _load_token_major(xbuf.at[slot], MOE_BLOCK, parts), 0.0).astype(BF16)
        gather(b + ahead, (b + ahead) % nslot)
        gate = jnp.dot(xb, wg_bf[...], preferred_element_type=F32)
        up = jnp.dot(xb, wu_bf[...], preferred_element_type=F32)
        hid = (_silu(gate) * up).astype(BF16)
        _store_token_major(o_ref, jnp.dot(hid, wd_bf[...], preferred_element_type=F32))

    @pl.when(b >= n_used)
    def _():
        o_ref[...] = jnp.zeros(o_ref.shape, o_ref.dtype)

    @pl.when(b == nb - 1)
    def _():
        for q in range(ahead):
            @pl.when(nb + q < n_used + ahead)
            def _(q=q):
                wait_gather((nb + q) % nslot)


def _experts(h2_tm, wg, wu, wd, layer, plan):
    p = plan["rows"]
    nb = p // MOE_BLOCK
    d, de = wg.shape[-2:]
    parts = d // LANES
    map_len = -(-(p + (EXPERT_X_SLOTS - 1) * MOE_BLOCK) // 1024) * 1024
    hbm = pl.BlockSpec(memory_space=pl.ANY)
    grid_spec = pltpu.PrefetchScalarGridSpec(
        num_scalar_prefetch=6,
        grid=(nb,),
        in_specs=[hbm, hbm, hbm, hbm, hbm],
        out_specs=pl.BlockSpec((MOE_BLOCK * parts, LANES), lambda b, *_: (b, 0)),
        scratch_shapes=[pltpu.VMEM((EXPERT_X_SLOTS, MOE_BLOCK * parts, LANES), F32),
                        pltpu.SemaphoreType.DMA((EXPERT_X_SLOTS,)),
                        pltpu.SMEM((map_len,), jnp.int32),
                        pltpu.VMEM((d, de), F32), pltpu.VMEM((d, de), F32), pltpu.VMEM((de, d), F32),
                        pltpu.VMEM((d, de), BF16), pltpu.VMEM((d, de), BF16), pltpu.VMEM((de, d), BF16),
                        pltpu.SemaphoreType.DMA((4,))],
    )
    return pl.pallas_call(
        functools.partial(_expert_kernel, layer=layer),
        grid_spec=grid_spec,
        out_shape=jax.ShapeDtypeStruct((p * parts, LANES), F32),
        compiler_params=_params("arbitrary"),
        name="experts",
    )(plan["block_expert"], plan["n_valid"], plan["first"], plan["next_expert"], plan["dest"],
      plan["n_used"], h2_tm, wg, wu, wd, jnp.zeros((map_len,), jnp.int32))


COMBINE_TILE = 64
COMBINE_BATCHES = 8


def _combine_kernel(pos_ref, y_hbm, x1_ref, ew_ref, gate_ref, gain_ref, o_ref, *scratch):
    i = pl.program_id(0)
    tm = COMBINE_TILE
    nbuf = COMBINE_BATCHES
    rows = TOP_K * tm
    bufs, sems = scratch[:nbuf], scratch[nbuf]

    parts = x1_ref.shape[1] // LANES

    def gather(tile, k):
        _start_token_gather(pos_ref, tile * rows, rows, parts, y_hbm, bufs[k], sems.at[k])

    def wait(k):
        _wait_token_gather(rows, parts, y_hbm, bufs[k], sems.at[k])

    def finish(k):
        sl = slice(k * tm, (k + 1) * tm)
        ew = ew_ref[sl, :]
        moe = ew[:, 0:1] * _load_token_major(bufs[k], tm, parts)
        for j in range(1, TOP_K):
            moe = moe + ew[:, j:j + 1] * _load_token_major(bufs[k], tm, parts, row0=j * tm)
        x2 = x1_ref[sl, :] + gate_ref[0] * moe
        o_ref[sl, :] = x2 * lax.rsqrt(jnp.mean(x2 * x2, axis=-1, keepdims=True) + EPS) * gain_ref[...]

    ahead = nbuf - 1

    @pl.when(i == 0)
    def _():
        for q in range(ahead):
            gather(q, q)

    for q in range(nbuf):
        wait(q)
        gather(nbuf * i + q + ahead, (q + ahead) % nbuf)
        finish(q)

    @pl.when(i == pl.num_programs(0) - 1)
    def _():
        for q in range(ahead):
            wait(q)


def _combine(ys, pos_tiles, x1, ew, gate2, final_gain, seq):
    n, d = x1.shape
    bsz = n // seq
    tm = COMBINE_BATCHES * COMBINE_TILE
    tpb = seq // tm
    grid_spec = pltpu.PrefetchScalarGridSpec(
        num_scalar_prefetch=1,
        grid=(n // tm,),
        in_specs=[pl.BlockSpec(memory_space=pl.ANY),
                  pl.BlockSpec((tm, d), lambda i, pos: (i, 0)),
                  pl.BlockSpec((tm, LANES), lambda i, pos: (i, 0)),
                  pl.BlockSpec((1, 1, d), lambda i, pos: (i // tpb, 0, 0)),
                  pl.BlockSpec((1, d), lambda i, pos: (0, 0))],
        out_specs=pl.BlockSpec((tm, d), lambda i, pos: (i, 0)),
        scratch_shapes=[pltpu.VMEM((TOP_K * COMBINE_TILE * (d // LANES), LANES), F32)] * COMBINE_BATCHES
        + [pltpu.SemaphoreType.DMA((COMBINE_BATCHES,))],
    )
    return pl.pallas_call(
        _combine_kernel,
        grid_spec=grid_spec,
        out_shape=jax.ShapeDtypeStruct((n, d), F32),
        compiler_params=_params("arbitrary"),
        name="combine_norm",
    )(pos_tiles, ys, x1, ew, gate2.reshape(bsz, 1, d), final_gain.reshape(1, d))


def _dispatch_plan(route, counts):
    n = route.shape[1]
    a = n * TOP_K
    eid = route[:TOP_K]
    rank = route[TOP_K:2 * TOP_K]
    cnt = counts[0, :N_EXPERTS]
    padded = (cnt + MOE_BLOCK - 1) // MOE_BLOCK * MOE_BLOCK
    pad_ends = jnp.cumsum(padded)
    pad_starts = pad_ends - padded
    experts = jnp.arange(N_EXPERTS, dtype=jnp.int32)
    dest = (jnp.sum(jnp.where(eid[..., None] == experts, pad_starts, 0), axis=-1) + rank).astype(jnp.int32)
    p = ((a + MOE_BLOCK - 1) // MOE_BLOCK + N_EXPERTS) * MOE_BLOCK
    nb = p // MOE_BLOCK
    n_used = (pad_ends[-1] // MOE_BLOCK).astype(jnp.int32)
    blk = jnp.arange(nb, dtype=jnp.int32)
    blk_start = jnp.minimum(blk, n_used - 1) * MOE_BLOCK
    block_expert = jnp.minimum(jnp.sum((pad_ends[None, :] <= blk_start[:, None]).astype(jnp.int32), axis=1),
                               N_EXPERTS - 1).astype(jnp.int32)
    def lookup(table, idx):
        return jnp.sum(jnp.where(idx[..., None] == jnp.arange(table.shape[0], dtype=jnp.int32), table, 0), axis=-1)

    blk_pad_start = lookup(pad_starts, block_expert)
    n_valid = jnp.clip(blk_pad_start + lookup(cnt, block_expert) - blk_start, 0, MOE_BLOCK).astype(jnp.int32)
    first = ((blk_start == blk_pad_start) & (blk < n_used)).astype(jnp.int32)
    nxt_blk = lookup(pad_ends, block_expert) // MOE_BLOCK
    next_expert = jnp.where(nxt_blk < n_used, lookup(block_expert, jnp.minimum(nxt_blk, nb - 1)), -1).astype(jnp.int32)
    pos_tiles = dest.reshape(TOP_K, n // COMBINE_TILE, COMBINE_TILE).transpose(1, 0, 2).reshape(a)
    pos_tiles = jnp.concatenate(
        [pos_tiles, jnp.zeros(((COMBINE_BATCHES - 1) * TOP_K * COMBINE_TILE,), jnp.int32)])
    plan = dict(block_expert=block_expert, n_valid=n_valid, first=first, next_expert=next_expert,
                dest=dest.reshape(a), n_used=n_used.reshape(1), rows=p)
    return plan, pos_tiles


def kernel(x, c, w_ada, b_ada, norm1, w_in, hg_lb, hg_norm, rg_conv_w, rg_conv_b, rg_w_a, rg_b_a,
           rg_w_x, rg_b_x, rg_lambda, w_proj_a, w_proj_b, w_out, norm2, w_group, w_router, w_gate,
           w_up, w_down, final_norm):
    bsz, seq, d = x.shape
    depth = w_ada.shape[0]
    n = bsz * seq
    x2 = x.reshape(n, d)
    lb_table = hg_lb.reshape(hg_lb.shape[0], -1)
    for layer in range(depth):
        mod = _adaln(c, w_ada[layer], b_ada[layer])
        shift1, scale1, gate1, shift2, scale2, gate2 = jnp.split(mod, 6, axis=-1)
        logf, acts = _inproj(x2, norm1[layer], scale1, shift1, w_in, lb_table, layer, seq)
        acts3 = acts.reshape(bsz, seq, acts.shape[1])
        ya = _hgrn(acts3, logf.reshape(bsz, seq, logf.shape[1]), hg_norm[layer])
        yb = _rglru(acts3, rg_conv_w[layer], rg_conv_b[layer], rg_w_a[layer], rg_b_a[layer],
                    rg_w_x[layer], rg_b_x[layer], rg_lambda[layer])
        w_route = jnp.concatenate(
            [w_group[layer], w_router[layer].reshape(d, N_EXPERTS),
             jnp.zeros((d, LANES - N_GROUPS - N_EXPERTS), F32)], axis=1)
        w_route_hi = w_route.astype(BF16)
        w_route = jnp.concatenate([w_route_hi, (w_route - w_route_hi.astype(F32)).astype(BF16)], axis=1)
        x1, h2, route, ew, counts = _mixout(
            ya.reshape(n, -1), yb.reshape(n, -1), acts, x2, gate1, scale2, shift2, norm2[layer],
            w_proj_a[layer].astype(BF16), w_proj_b[layer].astype(BF16), w_out[layer].astype(BF16), w_route, seq)
        plan, pos_tiles = _dispatch_plan(route, counts)
        ys = _experts(h2, w_gate, w_up, w_down, layer, plan)
        if layer + 1 < depth:
            raise NotImplementedError("the fused combine + final norm assumes a single layer")
        x2 = _combine(ys, pos_tiles, x1, ew, gate2, final_norm, seq)
    return x2.reshape(bsz, seq, d)
```

```python
import functools

import jax
import jax.numpy as jnp
import numpy as np
from jax import lax
from jax.experimental import pallas as pl
from jax.experimental.pallas import tpu as pltpu

EPS = 1e-6
LANES = 128
SUBLANES = 8
HG_HEADS = 8
RG_BLOCKS = 8
RG_CONV = 4
RG_C = 8.0
N_GROUPS = 4
EXPERTS_PER_GROUP = 8
N_EXPERTS = N_GROUPS * EXPERTS_PER_GROUP
TOP_K = 2
HG_CHUNK = 128
MOE_BLOCK = 128
EXPERT_X_SLOTS = 8
VMEM_LIMIT = 56 * 1024 * 1024
ACT_Q, ACT_V, ACT_G, ACT_RX, ACT_GY = 32, 40, 48, 56, 64

F32 = jnp.float32
BF16 = jnp.bfloat16


def _sigmoid(x):
    return 1.0 / (1.0 + jnp.exp(-x))


def _silu(x):
    return x * _sigmoid(x)


def _gelu_tanh(x):
    return 0.5 * x * (1.0 + jnp.tanh(0.7978845608028654 * (x + 0.044715 * (x * x * x))))


def _params(*sem):
    return pltpu.CompilerParams(dimension_semantics=sem, vmem_limit_bytes=VMEM_LIMIT)


def _adaln_kernel(ct_ref, w_ref, b_ref, o_ref, *, kc):
    kdim, nb = ct_ref.shape
    tn = w_ref.shape[1]

    def body(i, accs):
        k0 = pl.multiple_of(i * kc, kc)
        w = w_ref[pl.ds(k0, kc), :]
        cs = _silu(ct_ref[pl.ds(k0, kc), :])
        return tuple(acc + jnp.sum(w * cs[:, b:b + 1], axis=0, keepdims=True)
                     for b, acc in enumerate(accs))

    accs = lax.fori_loop(0, kdim // kc, body, tuple(jnp.zeros((1, tn), F32) for _ in range(nb)))
    for b, acc in enumerate(accs):
        o_ref[b:b + 1, :] = acc + b_ref[...]


def _adaln(c, w, bias):
    bsz, d = c.shape
    n = w.shape[1]
    tn = 1024
    return pl.pallas_call(
        functools.partial(_adaln_kernel, kc=256),
        grid=(n // tn,),
        in_specs=[pl.BlockSpec((d, bsz), lambda j: (0, 0)),
                  pl.BlockSpec((d, tn), lambda j: (0, j)),
                  pl.BlockSpec((1, tn), lambda j: (0, j))],
        out_specs=pl.BlockSpec((bsz, tn), lambda j: (0, j)),
        out_shape=jax.ShapeDtypeStruct((bsz, n), F32),
        compiler_params=_params("parallel"),
        name="adaln",
    )(c.T, w, bias.reshape(1, n))


def _norm_mod_kernel(x_ref, gain_ref, scale_ref, shift_ref, h_ref):
    rows = 64

    def body(i, carry):
        r0 = pl.multiple_of(i * rows, rows)
        x = x_ref[pl.ds(r0, rows), :]
        y = x * lax.rsqrt(jnp.mean(x * x, axis=-1, keepdims=True) + EPS) * gain_ref[...]
        h_ref[pl.ds(r0, rows), :] = (y * (1.0 + scale_ref[0]) + shift_ref[0]).astype(BF16)
        return carry

    lax.fori_loop(0, x_ref.shape[0] // rows, body, 0)


def _cast_weight_tile(w_ref, w_scr):
    rows = 256

    @pl.when(pl.program_id(1) == 0)
    def _():
        def body(i, carry):
            r0 = pl.multiple_of(i * rows, rows)
            w_scr[pl.ds(r0, rows), :] = w_ref[pl.ds(r0, rows), :].astype(BF16)
            return carry

        lax.fori_loop(0, w_ref.shape[0] // rows, body, 0)


def _inproj_logf_kernel(h_ref, w_ref, lbt_ref, o_ref, w_scr, *, layer):
    _cast_weight_tile(w_ref, w_scr)
    t = lbt_ref[...]
    e = jnp.exp(t - jnp.max(t, axis=0, keepdims=True))
    lb = jnp.sum(e[:layer + 1], axis=0, keepdims=True) / jnp.sum(e, axis=0, keepdims=True)
    z = jnp.dot(h_ref[...], w_scr[...], preferred_element_type=F32)
    o_ref[...] = jnp.log(lb + (1.0 - lb) * _sigmoid(z))


def _inproj_act_kernel(h_ref, w_ref, o_ref, w_scr):
    j = pl.program_id(0)
    _cast_weight_tile(w_ref, w_scr)
    z = jnp.dot(h_ref[...], w_scr[...], preferred_element_type=F32)

    @pl.when((j == 4) | (j == 6))
    def _():
        o_ref[...] = _silu(z).astype(BF16)

    @pl.when((j < 4) | (j == 5) | (j == 7))
    def _():
        o_ref[...] = z.astype(BF16)

    @pl.when(j == 8)
    def _():
        o_ref[...] = _gelu_tanh(z).astype(BF16)


def _inproj(x2, gain, scale, shift, w_in, lb_table, layer, seq):
    n, d = x2.shape
    bsz = n // seq
    tn = 1024
    tm = 512
    per_batch = pl.BlockSpec((1, 1, d), lambda i: (i // (seq // tm), 0, 0))
    h = pl.pallas_call(
        _norm_mod_kernel,
        grid=(n // tm,),
        in_specs=[pl.BlockSpec((tm, d), lambda i: (i, 0)), pl.BlockSpec((1, d), lambda i: (0, 0)),
                  per_batch, per_batch],
        out_specs=pl.BlockSpec((tm, d), lambda i: (i, 0)),
        out_shape=jax.ShapeDtypeStruct((n, d), BF16),
        compiler_params=_params("parallel"),
        name="norm_mod",
    )(x2, gain.reshape(1, d), scale.reshape(bsz, 1, d), shift.reshape(bsz, 1, d))

    n_lb = lb_table.shape[0]
    tm = 1024
    logf = pl.pallas_call(
        functools.partial(_inproj_logf_kernel, layer=layer),
        grid=(2, n // tm),
        in_specs=[pl.BlockSpec((tm, d), lambda j, i: (i, 0)),
                  pl.BlockSpec((None, d, tn), lambda j, i: (layer, 0, j + 2)),
                  pl.BlockSpec((n_lb, tn), lambda j, i: (0, j))],
        out_specs=pl.BlockSpec((tm, tn), lambda j, i: (i, j)),
        out_shape=jax.ShapeDtypeStruct((n, 2 * tn), F32),
        scratch_shapes=[pltpu.VMEM((d, tn), BF16)],
        compiler_params=_params("parallel", "arbitrary"),
        name="inproj_logf",
    )(h, w_in, lb_table)
    acts = pl.pallas_call(
        _inproj_act_kernel,
        grid=(9, n // tm),
        in_specs=[pl.BlockSpec((tm, d), lambda j, i: (i, 0)),
                  pl.BlockSpec((None, d, tn),
                               lambda j, i: (layer, 0, jnp.where(j < 4, j + 7, jnp.where(j < 6, j - 4, j - 2))))],
        out_specs=pl.BlockSpec((tm, tn), lambda j, i: (i, j)),
        out_shape=jax.ShapeDtypeStruct((n, 9 * tn), BF16),
        scratch_shapes=[pltpu.VMEM((d, tn), BF16)],
        compiler_params=_params("parallel", "arbitrary"),
        name="inproj_act",
    )(h, w_in)
    return logf, acts


def _hgrn_level_halves(chunk):
    return [chunk >> (i + 1) for i in range(chunk.bit_length() - 1)]


def _hgrn_kernel(q_ref, v_ref, g_ref, lff_ref, lfb_ref, gain_ref, o_ref,
                 oacc, qf_scr, kf_scr, qb_scr, kb_scr, *, chunk, group):
    seq = q_ref.shape[1]
    nchunks = seq // chunk
    halves = _hgrn_level_halves(chunk)
    nt = (((1,), (1,)), ((), ()))
    tn = (((0,), (0,)), ((), ()))

    row = lax.broadcasted_iota(jnp.int32, (chunk, LANES), 0)
    ti = lax.broadcasted_iota(jnp.int32, (chunk, chunk), 0)
    si = lax.broadcasted_iota(jnp.int32, (chunk, chunk), 1)
    txs = ti ^ si
    tri_f = jnp.where(si <= ti, 1.0, 0.0).astype(BF16)
    tri_b = jnp.where(si >= ti, 1.0, 0.0).astype(BF16)

    def cumulative(lf, tri):
        hi = lf.astype(BF16)
        mid = (lf - hi.astype(F32)).astype(BF16)
        r = jnp.dot(tri, jnp.concatenate([hi, mid], axis=1), preferred_element_type=F32)
        return r[:, :LANES] + r[:, LANES:]

    lid = functools.reduce(lambda a, b: a + b, [(txs >= h).astype(jnp.int32) for h in halves])

    def boundary_diff(cum, m, reverse):
        pieces = []
        for base in range(0, chunk, 2 * m):
            bnd = base + (m if reverse else m - 1)
            pieces.append(jnp.broadcast_to(cum[bnd:bnd + 1, :], (2 * m, LANES)))
        return cum - jnp.concatenate(pieces, axis=0)

    def intra(c):
        r0 = pl.multiple_of(c * chunk, chunk)
        q = q_ref[0, pl.ds(r0, chunk), :].astype(F32)
        v = v_ref[0, pl.ds(r0, chunk), :]
        lff = lff_ref[0, pl.ds(r0, chunk), :]
        lfb = lfb_ref[0, pl.ds(r0, chunk), :]
        cum_f = cumulative(lff, tri_f)
        cum_b = cumulative(lfb, tri_b)
        ff = jnp.exp(lff)
        fb = jnp.exp(lfb)
        kf = 1.0 - ff
        kb = 1.0 - fb

        p0 = lax.dot_general(q.astype(BF16), (kf + kb).astype(BF16), nt, preferred_element_type=F32)
        scores = jnp.where(lid == 0, p0, 0.0)
        for m in halves:
            later = (row & m) != 0
            if 2 * m >= SUBLANES:
                d_f = boundary_diff(cum_f, m, False)
                d_b = boundary_diff(cum_b, m, True)
                x = q * jnp.exp(jnp.where(later, d_f, d_b))
                y = jnp.where(later, kb, kf) * jnp.exp(-jnp.where(later, d_b, d_f))
            elif m == 2:
                r4 = row & 3
                prev = lambda a: pltpu.roll(a, 1, 0)
                nxt = lambda a: pltpu.roll(a, chunk - 1, 0)
                x = q * jnp.where(r4 == 0, fb * nxt(fb), jnp.where(r4 == 1, fb,
                                  jnp.where(r4 == 2, ff, ff * prev(ff))))
                y = jnp.where(r4 == 0, kf * nxt(ff), jnp.where(r4 == 1, kf,
                              jnp.where(r4 == 2, kb, kb * prev(fb))))
            else:
                x = q * jnp.where(later, ff, fb)
                y = jnp.where(later, kb, kf)
            pm = lax.dot_general(x.astype(BF16), y.astype(BF16), nt, preferred_element_type=F32)
            scores = jnp.where(lid == m.bit_length(), pm, scores)
        oacc[pl.ds(r0, chunk), :] = jnp.dot(scores.astype(BF16), v, preferred_element_type=F32)
        tot_f = cum_f[chunk - 1:chunk, :]
        tot_b = cum_b[0:1, :]
        qf_scr[pl.ds(r0, chunk), :] = (q * jnp.exp(cum_f)).astype(BF16)
        kf_scr[pl.ds(r0, chunk), :] = (kf * jnp.exp(tot_f - cum_f)).astype(BF16)
        qb_scr[pl.ds(r0, chunk), :] = (q * jnp.exp(cum_b)).astype(BF16)
        kb_scr[pl.ds(r0, chunk), :] = (kb * jnp.exp(tot_b - cum_b)).astype(BF16)

    def intra_body(i, carry):
        for u in range(group):
            intra(i * group + u)
        return carry

    lax.fori_loop(0, nchunks // group, intra_body, 0)

    def apply_state(c, st, q_scr, k_scr, lf_ref):
        r0 = pl.multiple_of(c * chunk, chunk)
        o = oacc[pl.ds(r0, chunk), :] + lax.dot_general(
            q_scr[pl.ds(r0, chunk), :], st.astype(BF16), nt, preferred_element_type=F32)
        tot = jnp.sum(lf_ref[0, pl.ds(r0, chunk), :], axis=0, keepdims=True)
        st = st * jnp.exp(tot) + lax.dot_general(
            v_ref[0, pl.ds(r0, chunk), :], k_scr[pl.ds(r0, chunk), :], tn, preferred_element_type=F32)
        return r0, o, st

    def finish(r0, o):
        y = o * lax.rsqrt(jnp.mean(o * o, axis=-1, keepdims=True) + EPS) * gain_ref[0]
        o_ref[0, pl.ds(r0, chunk), :] = (y * g_ref[0, pl.ds(r0, chunk), :].astype(F32)).astype(BF16)

    def sweep(final):
        def body(i, states):
            st_f, st_b = states
            rf, of, st_f = apply_state(i, st_f, qf_scr, kf_scr, lff_ref)
            rb, ob, st_b = apply_state(nchunks - 1 - i, st_b, qb_scr, kb_scr, lfb_ref)
            if final:
                finish(rf, of)
                finish(rb, ob)
            else:
                oacc[pl.ds(rf, chunk), :] = of
                oacc[pl.ds(rb, chunk), :] = ob
            return st_f, st_b
        return body

    zero = jnp.zeros((LANES, LANES), F32)
    states = lax.fori_loop(0, nchunks // 2, sweep(False), (zero, zero), unroll=8)
    lax.fori_loop(nchunks // 2, nchunks, sweep(True), states, unroll=8)


def _hgrn(acts3, logf3, hg_norm):
    bsz, seq, _ = acts3.shape
    col = lambda off: pl.BlockSpec((1, seq, LANES), lambda b, h: (b, 0, off + h))
    return pl.pallas_call(
        functools.partial(_hgrn_kernel, chunk=HG_CHUNK, group=8),
        grid=(bsz, HG_HEADS),
        in_specs=[col(ACT_Q), col(ACT_V), col(ACT_G), col(0), col(HG_HEADS),
                  pl.BlockSpec((1, 1, LANES), lambda b, h: (h, 0, 0))],
        out_specs=pl.BlockSpec((1, seq, LANES), lambda b, h: (b, 0, h)),
        out_shape=jax.ShapeDtypeStruct((bsz, seq, HG_HEADS * LANES), BF16),
        scratch_shapes=[pltpu.VMEM((seq, LANES), F32)] + [pltpu.VMEM((seq, LANES), BF16)] * 4,
        compiler_params=_params("parallel", "parallel"),
        name="hgrn2",
    )(acts3, acts3, acts3, logf3, logf3, hg_norm.reshape(HG_HEADS, 1, LANES))


RG_SCAN_SUB = 32


def _scan_tile(a, u, carry, reverse):
    rows = a.shape[0]
    sub = RG_SCAN_SUB
    pos = lax.broadcasted_iota(jnp.int32, a.shape, 0) & (sub - 1)
    d = 1
    while d < sub:
        if reverse:
            ok = pos < sub - d
            shift = rows - d
        else:
            ok = pos >= d
            shift = d
        a_s = jnp.where(ok, pltpu.roll(a, shift, 0), 1.0)
        u_s = jnp.where(ok, pltpu.roll(u, shift, 0), 0.0)
        u = u + a * u_s
        a = a * a_s
        d *= 2
    spans = list(range(rows // sub))
    pieces = [None] * len(spans)
    for k in (reversed(spans) if reverse else spans):
        hk = u[k * sub:(k + 1) * sub] + a[k * sub:(k + 1) * sub] * carry
        carry = hk[0:1] if reverse else hk[sub - 1:sub]
        pieces[k] = hk
    return jnp.concatenate(pieces, axis=0), carry


def _rglru_kernel(rx_ref, gy_ref, cw_ref, cb_ref, wa_ref, ba_ref, wx_ref, bx_ref, lam_ref, o_ref,
                  xpad, hf_scr, ab_scr, ub_scr, *, tile):
    seq = rx_ref.shape[1]
    ntiles = seq // tile
    halo = 8

    xpad[0:halo, :] = jnp.zeros((halo, LANES), F32)
    xpad[halo + seq:halo + seq + halo, :] = jnp.zeros((halo, LANES), F32)
    xpad[halo:halo + seq, :] = rx_ref[0].astype(F32)

    lam = lam_ref[...]
    neg_c_softplus = -RG_C * (jnp.maximum(-lam, 0.0) + jnp.log(1.0 + jnp.exp(-jnp.abs(lam))))

    def gates(xcb, z):
        r = _sigmoid(jnp.dot(xcb, wa_ref[z, 0], preferred_element_type=F32) + ba_ref[z, 0])
        gi = _sigmoid(jnp.dot(xcb, wx_ref[z, 0], preferred_element_type=F32) + bx_ref[z, 0])
        a = jnp.exp(r * neg_c_softplus[z:z + 1, :])
        y = 1.0 - a * a
        return a, jnp.where(y > 0.0, y * lax.rsqrt(y), 0.0) * gi

    def fwd_body(i, carry):
        t0 = pl.multiple_of(i * tile, tile)
        ext = xpad[pl.ds(t0, tile + 2 * halo), :]
        n_ext = tile + 2 * halo
        tap = lambda off: pltpu.roll(ext, (-off) % n_ext, 0)[halo:halo + tile]
        xc = (cw_ref[0:1, :] * tap(-2) + cw_ref[1:2, :] * tap(-1) + cw_ref[2:3, :] * ext[halo:halo + tile]
              + cw_ref[3:4, :] * tap(1) + cb_ref[...])
        xcb = xc.astype(BF16)
        a_f, s_f = gates(xcb, 0)
        a_b, s_b = gates(xcb, 1)
        h, carry = _scan_tile(a_f, s_f * xc, carry, reverse=False)
        hf_scr[pl.ds(t0, tile), :] = h
        ab_scr[pl.ds(t0, tile), :] = a_b
        ub_scr[pl.ds(t0, tile), :] = s_b * xc
        return carry

    lax.fori_loop(0, ntiles, fwd_body, jnp.zeros((1, LANES), F32))

    def bwd_body(i, carry):
        t0 = pl.multiple_of((ntiles - 1 - i) * tile, tile)
        h, carry = _scan_tile(ab_scr[pl.ds(t0, tile), :], ub_scr[pl.ds(t0, tile), :], carry, reverse=True)
        o_ref[0, pl.ds(t0, tile), :] = (
            (hf_scr[pl.ds(t0, tile), :] + h) * gy_ref[0, pl.ds(t0, tile), :].astype(F32)).astype(BF16)
        return carry

    lax.fori_loop(0, ntiles, bwd_body, jnp.zeros((1, LANES), F32))


def _rglru(acts3, conv_w, conv_b, w_a, b_a, w_x, b_x, lam):
    bsz, seq, _ = acts3.shape
    width = RG_BLOCKS * LANES
    tile = 256
    rx_off, gy_off = ACT_RX, ACT_GY
    col = lambda off: pl.BlockSpec((1, seq, LANES), lambda b, j: (b, 0, off + j))
    wspec = pl.BlockSpec((2, 1, LANES, LANES), lambda b, j: (0, j, 0, 0))
    bspec = pl.BlockSpec((2, 1, 1, LANES), lambda b, j: (0, j, 0, 0))
    return pl.pallas_call(
        functools.partial(_rglru_kernel, tile=tile),
        grid=(bsz, RG_BLOCKS),
        in_specs=[col(rx_off), col(gy_off),
                  pl.BlockSpec((RG_CONV, LANES), lambda b, j: (0, j)),
                  pl.BlockSpec((1, LANES), lambda b, j: (0, j)),
                  wspec, bspec, wspec, bspec,
                  pl.BlockSpec((2, LANES), lambda b, j: (0, j))],
        out_specs=pl.BlockSpec((1, seq, LANES), lambda b, j: (b, 0, j)),
        out_shape=jax.ShapeDtypeStruct((bsz, seq, width), BF16),
        scratch_shapes=[pltpu.VMEM((seq + 16, LANES), F32),
                        pltpu.VMEM((seq, LANES), F32),
                        pltpu.VMEM((seq, LANES), F32),
                        pltpu.VMEM((seq, LANES), F32)],
        compiler_params=_params("parallel", "parallel"),
        name="rglru",
    )(acts3, acts3, conv_w, conv_b.reshape(1, width), w_a.astype(BF16),
      b_a.reshape(2, RG_BLOCKS, 1, LANES), w_x.astype(BF16), b_x.reshape(2, RG_BLOCKS, 1, LANES), lam)


def _store_token_major(ref, x, row0=0):
    ntok, d = x.shape
    parts = d // LANES
    for c in range(parts):
        ref[pl.ds(row0 * parts + c, ntok, stride=parts), :] = x[:, c * LANES:(c + 1) * LANES]


def _load_token_major(ref, ntok, parts, row0=0):
    return jnp.concatenate([ref[pl.ds(row0 * parts + c, ntok, stride=parts), :] for c in range(parts)], axis=1)


def _mixout_kernel(ya_ref, yb_ref, ga_ref, gb_ref, x_ref, gate_ref, scale_ref, shift_ref, gain_ref,
                   wpa_ref, wpb_ref, wo_ref, wr_ref, x1_ref, h2_ref, route_ref, ew_ref, cnt_ref, running):
    y_a = jnp.dot(ya_ref[...], wpa_ref[...], preferred_element_type=F32)
    y_b = jnp.dot(yb_ref[...], wpb_ref[...], preferred_element_type=F32)
    merged = _sigmoid(ga_ref[...].astype(F32)) * y_a + _sigmoid(gb_ref[...].astype(F32)) * y_b
    mix = jnp.dot(merged.astype(BF16), wo_ref[...], preferred_element_type=F32)
    x1 = x_ref[...] + gate_ref[0] * mix
    x1_ref[...] = x1
    y = x1 * lax.rsqrt(jnp.mean(x1 * x1, axis=-1, keepdims=True) + EPS) * gain_ref[...]
    h2 = y * (1.0 + scale_ref[0]) + shift_ref[0]
    _store_token_major(h2_ref, h2)

    h_hi = h2.astype(BF16)
    h_lo = (h2 - h_hi.astype(F32)).astype(BF16)
    p_hi = jnp.dot(h_hi, wr_ref[...], preferred_element_type=F32)
    p_lo = jnp.dot(h_lo, wr_ref[...], preferred_element_type=F32)
    logits = (p_hi[:, :LANES] + p_hi[:, LANES:]) + (p_lo[:, :LANES] + p_lo[:, LANES:])
    lane = lax.broadcasted_iota(jnp.int32, logits.shape, 1)
    neg = -jnp.inf
    gl = jnp.where(lane < N_GROUPS, logits, neg)
    gmax = jnp.max(gl, axis=-1, keepdims=True)
    g_sel = jnp.min(jnp.where(gl == gmax, lane, LANES), axis=-1, keepdims=True)
    p_group = 1.0 / jnp.sum(jnp.exp(gl - gmax), axis=-1, keepdims=True)
    lo = N_GROUPS + g_sel * EXPERTS_PER_GROUP
    el = jnp.where((lane >= lo) & (lane < lo + EXPERTS_PER_GROUP), logits, neg)
    v1 = jnp.max(el, axis=-1, keepdims=True)
    i1 = jnp.min(jnp.where(el == v1, lane, LANES), axis=-1, keepdims=True)
    el2 = jnp.where(lane == i1, neg, el)
    v2 = jnp.max(el2, axis=-1, keepdims=True)
    i2 = jnp.min(jnp.where(el2 == v2, lane, LANES), axis=-1, keepdims=True)
    e21 = jnp.exp(v2 - v1)
    w1 = p_group / (1.0 + e21)
    w2 = p_group * e21 / (1.0 + e21)
    ew_ref[...] = jnp.where(lane == 0, w1, jnp.where(lane == 1, w2, 0.0))

    @pl.when(pl.program_id(0) == 0)
    def _():
        running[...] = jnp.zeros(running.shape, F32)

    e1, e2 = i1 - N_GROUPS, i2 - N_GROUPS
    tm = logits.shape[0]
    before = (lax.broadcasted_iota(jnp.int32, (tm, tm), 0)
              > lax.broadcasted_iota(jnp.int32, (tm, tm), 1))
    before = jnp.where(before, 1.0, 0.0).astype(BF16)
    oh1 = jnp.where(lane == e1, 1.0, 0.0)
    oh2 = jnp.where(lane == e2, 1.0, 0.0)
    pre1 = jnp.dot(before, oh1.astype(BF16), preferred_element_type=F32)
    pre2 = jnp.dot(before, oh2.astype(BF16), preferred_element_type=F32)
    cnt1 = jnp.sum(oh1, axis=0, keepdims=True)
    base = running[...]
    r1 = jnp.sum(oh1 * (pre1 + base), axis=-1, keepdims=True).astype(jnp.int32)
    r2 = jnp.sum(oh2 * (pre2 + base + cnt1), axis=-1, keepdims=True).astype(jnp.int32)
    total = base + cnt1 + jnp.sum(oh2, axis=0, keepdims=True)
    running[...] = total
    cnt_ref[...] = total.astype(jnp.int32)
    cols = jnp.where(lane == 0, e1, jnp.where(lane == 1, e2,
                     jnp.where(lane == 2, r1 >> 7, jnp.where(lane == 3, r1 & 127,
                               jnp.where(lane == 4, r2 >> 7, jnp.where(lane == 5, r2 & 127, 0))))))
    srow = lax.broadcasted_iota(jnp.int32, (SUBLANES, LANES), 0)
    slane = lax.broadcasted_iota(jnp.int32, (SUBLANES, LANES), 1)
    sel = jnp.where((srow < 2) & (slane == srow), 1.0,
                    jnp.where((srow >= 2) & (srow < 4) & (slane == 2 * srow - 2), 128.0,
                              jnp.where((srow >= 2) & (srow < 4) & (slane == 2 * srow - 1), 1.0, 0.0)))
    route_t = lax.dot_general(sel.astype(BF16), cols.astype(F32).astype(BF16), (((1,), (1,)), ((), ())),
                              preferred_element_type=F32)
    route_ref[...] = route_t.astype(jnp.int32)


def _mixout(ya, yb, acts, x2, gate1, scale2, shift2, gain2, wpa, wpb, wo, w_route, seq):
    n, d = x2.shape
    bsz = n // seq
    tm = 256
    tpb = seq // tm
    half = ya.shape[1]
    ga_blk = 0
    per_batch = pl.BlockSpec((1, 1, d), lambda i: (i // tpb, 0, 0))
    const = lambda shape: pl.BlockSpec(shape, lambda i: (0, 0), pipeline_mode=pl.Buffered(1))
    return pl.pallas_call(
        _mixout_kernel,
        grid=(n // tm,),
        in_specs=[pl.BlockSpec((tm, half), lambda i: (i, 0)),
                  pl.BlockSpec((tm, half), lambda i: (i, 0)),
                  pl.BlockSpec((tm, d), lambda i: (i, ga_blk)),
                  pl.BlockSpec((tm, d), lambda i: (i, ga_blk + 1)),
                  pl.BlockSpec((tm, d), lambda i: (i, 0)),
                  per_batch, per_batch, per_batch,
                  const((1, d)), const((half, d)), const((half, d)), const((d, d)), const((d, 2 * LANES))],
        out_specs=[pl.BlockSpec((tm, d), lambda i: (i, 0)),
                   pl.BlockSpec((tm * (d // LANES), LANES), lambda i: (i, 0)),
                   pl.BlockSpec((SUBLANES, tm), lambda i: (0, i)),
                   pl.BlockSpec((tm, LANES), lambda i: (i, 0)),
                   pl.BlockSpec((1, LANES), lambda i: (0, 0))],
        out_shape=[jax.ShapeDtypeStruct((n, d), F32),
                   jax.ShapeDtypeStruct((n * (d // LANES), LANES), F32),
                   jax.ShapeDtypeStruct((SUBLANES, n), jnp.int32),
                   jax.ShapeDtypeStruct((n, LANES), F32),
                   jax.ShapeDtypeStruct((1, LANES), jnp.int32)],
        scratch_shapes=[pltpu.VMEM((1, LANES), F32)],
        compiler_params=_params("arbitrary"),
        name="mixout_route",
    )(ya, yb, acts, acts, x2, gate1.reshape(bsz, 1, d), scale2.reshape(bsz, 1, d),
      shift2.reshape(bsz, 1, d), gain2.reshape(1, d), wpa, wpb, wo, w_route)


def _start_token_gather(idx_ref, base, ntok, parts, src_hbm, dst, sem):
    for r in range(ntok):
        t0 = pl.multiple_of(idx_ref[base + r] * parts, parts)
        pltpu.make_async_copy(src_hbm.at[pl.ds(t0, parts), :], dst.at[pl.ds(r * parts, parts), :], sem).start()


def _wait_token_gather(ntok, parts, src_hbm, dst, sem):
    pltpu.make_async_copy(src_hbm.at[pl.ds(0, ntok * parts), :], dst.at[pl.ds(0, ntok * parts), :], sem).wait()


def _expert_kernel(be_ref, nv_ref, first_ref, next_ref, dest_ref, nused_ref,
                   h_hbm, wg_hbm, wu_hbm, wd_hbm, zeros_hbm, o_ref,
                   xbuf, xsems, row_tok, stage_g, stage_u, stage_d, wg_bf, wu_bf, wd_bf, wsems, *, layer):
    b = pl.program_id(0)
    nb = pl.num_programs(0)
    n_used = nused_ref[0]
    unroll = 8
    cast_rows = 256

    def weight_copies(e):
        return (pltpu.make_async_copy(wg_hbm.at[layer, e], stage_g, wsems.at[0]),
                pltpu.make_async_copy(wu_hbm.at[layer, e], stage_u, wsems.at[1]),
                pltpu.make_async_copy(wd_hbm.at[layer, e], stage_d, wsems.at[2]))

    parts = xbuf.shape[1] // MOE_BLOCK
    nslot = xbuf.shape[0]
    ahead = nslot - 1

    def gather(blk, slot):
        _start_token_gather(row_tok, blk * MOE_BLOCK, MOE_BLOCK, parts, h_hbm, xbuf.at[slot], xsems.at[slot])

    def wait_gather(slot):
        _wait_token_gather(MOE_BLOCK, parts, h_hbm, xbuf.at[slot], xsems.at[slot])

    @pl.when(b == 0)
    def _():
        for cp in weight_copies(be_ref[0]):
            cp.start()

        clear = pltpu.make_async_copy(zeros_hbm, row_tok, wsems.at[3])
        clear.start()
        clear.wait()

        ntok = dest_ref.shape[0] // TOP_K
        for k in range(TOP_K):
            def invert(i, carry, k=k):
                for u in range(unroll):
                    row_tok[dest_ref[k * ntok + i * unroll + u]] = i * unroll + u
                return carry

            lax.fori_loop(0, ntok // unroll, invert, 0)
        for q in range(ahead):
            gather(q, q)

    @pl.when(b < n_used + ahead)
    def _():
        wait_gather(b % nslot)

    @pl.when(b < n_used)
    def _():
        slot = b % nslot

        @pl.when(first_ref[b] == 1)
        def _():
            for m, (stage, dst) in enumerate(((stage_g, wg_bf), (stage_u, wu_bf), (stage_d, wd_bf))):
                weight_copies(0)[m].wait()

                def cast(i, carry, stage=stage, dst=dst):
                    r0 = pl.multiple_of(i * cast_rows, cast_rows)
                    dst[pl.ds(r0, cast_rows), :] = stage[pl.ds(r0, cast_rows), :].astype(BF16)
                    return carry

                lax.fori_loop(0, stage.shape[0] // cast_rows, cast, 0)

                @pl.when(next_ref[b] >= 0)
                def _(m=m):
                    weight_copies(next_ref[b])[m].start()

        valid = lax.broadcasted_iota(jnp.int32, (MOE_BLOCK, 1), 0) < nv_ref[b]
        xb = jnp.where(valid, _load_token_major(xbuf.at[slot], MOE_BLOCK, parts), 0.0).astype(BF16)
        gather(b + ahead, (b + ahead) % nslot)
        gate = jnp.dot(xb, wg_bf[...], preferred_element_type=F32)
        up = jnp.dot(xb, wu_bf[...], preferred_element_type=F32)
        hid = (_silu(gate) * up).astype(BF16)
        _store_token_major(o_ref, jnp.dot(hid, wd_bf[...], preferred_element_type=F32))

    @pl.when(b >= n_used)
    def _():
        o_ref[...] = jnp.zeros(o_ref.shape, o_ref.dtype)

    @pl.when(b == nb - 1)
    def _():
        for q in range(ahead):
            @pl.when(nb + q < n_used + ahead)
            def _(q=q):
                wait_gather((nb + q) % nslot)


def _experts(h2_tm, wg, wu, wd, layer, plan):
    p = plan["rows"]
    nb = p // MOE_BLOCK
    d, de = wg.shape[-2:]
    parts = d // LANES
    map_len = -(-(p + (EXPERT_X_SLOTS - 1) * MOE_BLOCK) // 1024) * 1024
    hbm = pl.BlockSpec(memory_space=pl.ANY)
    grid_spec = pltpu.PrefetchScalarGridSpec(
        num_scalar_prefetch=6,
        grid=(nb,),
        in_specs=[hbm, hbm, hbm, hbm, hbm],
        out_specs=pl.BlockSpec((MOE_BLOCK * parts, LANES), lambda b, *_: (b, 0)),
        scratch_shapes=[pltpu.VMEM((EXPERT_X_SLOTS, MOE_BLOCK * parts, LANES), F32),
                        pltpu.SemaphoreType.DMA((EXPERT_X_SLOTS,)),
                        pltpu.SMEM((map_len,), jnp.int32),
                        pltpu.VMEM((d, de), F32), pltpu.VMEM((d, de), F32), pltpu.VMEM((de, d), F32),
                        pltpu.VMEM((d, de), BF16), pltpu.VMEM((d, de), BF16), pltpu.VMEM((de, d), BF16),
                        pltpu.SemaphoreType.DMA((4,))],
    )
    return pl.pallas_call(
        functools.partial(_expert_kernel, layer=layer),
        grid_spec=grid_spec,
        out_shape=jax.ShapeDtypeStruct((p * parts, LANES), F32),
        compiler_params=_params("arbitrary"),
        name="experts",
    )(plan["block_expert"], plan["n_valid"], plan["first"], plan["next_expert"], plan["dest"],
      plan["n_used"], h2_tm, wg, wu, wd, jnp.zeros((map_len,), jnp.int32))


COMBINE_TILE = 64
COMBINE_BATCHES = 8


def _combine_kernel(pos_ref, y_hbm, x1_ref, ew_ref, gate_ref, gain_ref, o_ref, *scratch):
    i = pl.program_id(0)
    tm = COMBINE_TILE
    nbuf = COMBINE_BATCHES
    rows = TOP_K * tm
    bufs, sems = scratch[:nbuf], scratch[nbuf]

    parts = x1_ref.shape[1] // LANES

    def gather(tile, k):
        _start_token_gather(pos_ref, tile * rows, rows, parts, y_hbm, bufs[k], sems.at[k])

    def wait(k):
        _wait_token_gather(rows, parts, y_hbm, bufs[k], sems.at[k])

    def finish(k):
        sl = slice(k * tm, (k + 1) * tm)
        ew = ew_ref[sl, :]
        moe = ew[:, 0:1] * _load_token_major(bufs[k], tm, parts)
        for j in range(1, TOP_K):
            moe = moe + ew[:, j:j + 1] * _load_token_major(bufs[k], tm, parts, row0=j * tm)
        x2 = x1_ref[sl, :] + gate_ref[0] * moe
        o_ref[sl, :] = x2 * lax.rsqrt(jnp.mean(x2 * x2, axis=-1, keepdims=True) + EPS) * gain_ref[...]

    ahead = nbuf - 1

    @pl.when(i == 0)
    def _():
        for q in range(ahead):
            gather(q, q)

    for q in range(nbuf):
        wait(q)
        gather(nbuf * i + q + ahead, (q + ahead) % nbuf)
        finish(q)

    @pl.when(i == pl.num_programs(0) - 1)
    def _():
        for q in range(ahead):
            wait(q)


def _combine(ys, pos_tiles, x1, ew, gate2, final_gain, seq):
    n, d = x1.shape
    bsz = n // seq
    tm = COMBINE_BATCHES * COMBINE_TILE
    tpb = seq // tm
    grid_spec = pltpu.PrefetchScalarGridSpec(
        num_scalar_prefetch=1,
        grid=(n // tm,),
        in_specs=[pl.BlockSpec(memory_space=pl.ANY),
                  pl.BlockSpec((tm, d), lambda i, pos: (i, 0)),
                  pl.BlockSpec((tm, LANES), lambda i, pos: (i, 0)),
                  pl.BlockSpec((1, 1, d), lambda i, pos: (i // tpb, 0, 0)),
                  pl.BlockSpec((1, d), lambda i, pos: (0, 0))],
        out_specs=pl.BlockSpec((tm, d), lambda i, pos: (i, 0)),
        scratch_shapes=[pltpu.VMEM((TOP_K * COMBINE_TILE * (d // LANES), LANES), F32)] * COMBINE_BATCHES
        + [pltpu.SemaphoreType.DMA((COMBINE_BATCHES,))],
    )
    return pl.pallas_call(
        _combine_kernel,
        grid_spec=grid_spec,
        out_shape=jax.ShapeDtypeStruct((n, d), F32),
        compiler_params=_params("arbitrary"),
        name="combine_norm",
    )(pos_tiles, ys, x1, ew, gate2.reshape(bsz, 1, d), final_gain.reshape(1, d))


def _dispatch_plan(route, counts):
    n = route.shape[1]
    a = n * TOP_K
    eid = route[:TOP_K]
    rank = route[TOP_K:2 * TOP_K]
    cnt = counts[0, :N_EXPERTS]
    padded = (cnt + MOE_BLOCK - 1) // MOE_BLOCK * MOE_BLOCK
    pad_ends = jnp.cumsum(padded)
    pad_starts = pad_ends - padded
    experts = jnp.arange(N_EXPERTS, dtype=jnp.int32)
    dest = (jnp.sum(jnp.where(eid[..., None] == experts, pad_starts, 0), axis=-1) + rank).astype(jnp.int32)
    p = ((a + MOE_BLOCK - 1) // MOE_BLOCK + N_EXPERTS) * MOE_BLOCK
    nb = p // MOE_BLOCK
    n_used = (pad_ends[-1] // MOE_BLOCK).astype(jnp.int32)
    blk = jnp.arange(nb, dtype=jnp.int32)
    blk_start = jnp.minimum(blk, n_used - 1) * MOE_BLOCK
    block_expert = jnp.minimum(jnp.sum((pad_ends[None, :] <= blk_start[:, None]).astype(jnp.int32), axis=1),
                               N_EXPERTS - 1).astype(jnp.int32)
    def lookup(table, idx):
        return jnp.sum(jnp.where(idx[..., None] == jnp.arange(table.shape[0], dtype=jnp.int32), table, 0), axis=-1)

    blk_pad_start = lookup(pad_starts, block_expert)
    n_valid = jnp.clip(blk_pad_start + lookup(cnt, block_expert) - blk_start, 0, MOE_BLOCK).astype(jnp.int32)
    first = ((blk_start == blk_pad_start) & (blk < n_used)).astype(jnp.int32)
    nxt_blk = lookup(pad_ends, block_expert) // MOE_BLOCK
    next_expert = jnp.where(nxt_blk < n_used, lookup(block_expert, jnp.minimum(nxt_blk, nb - 1)), -1).astype(jnp.int32)
    pos_tiles = dest.reshape(TOP_K, n // COMBINE_TILE, COMBINE_TILE).transpose(1, 0, 2).reshape(a)
    pos_tiles = jnp.concatenate(
        [pos_tiles, jnp.zeros(((COMBINE_BATCHES - 1) * TOP_K * COMBINE_TILE,), jnp.int32)])
    plan = dict(block_expert=block_expert, n_valid=n_valid, first=first, next_expert=next_expert,
                dest=dest.reshape(a), n_used=n_used.reshape(1), rows=p)
    return plan, pos_tiles


def kernel(x, c, w_ada, b_ada, norm1, w_in, hg_lb, hg_norm, rg_conv_w, rg_conv_b, rg_w_a, rg_b_a,
           rg_w_x, rg_b_x, rg_lambda, w_proj_a, w_proj_b, w_out, norm2, w_group, w_router, w_gate,
           w_up, w_down, final_norm):
    bsz, seq, d = x.shape
    depth = w_ada.shape[0]
    n = bsz * seq
    x2 = x.reshape(n, d)
    lb_table = hg_lb.reshape(hg_lb.shape[0], -1)
    for layer in range(depth):
        mod = _adaln(c, w_ada[layer], b_ada[layer])
        shift1, scale1, gate1, shift2, scale2, gate2 = jnp.split(mod, 6, axis=-1)
        logf, acts = _inproj(x2, norm1[layer], scale1, shift1, w_in, lb_table, layer, seq)
        acts3 = acts.reshape(bsz, seq, acts.shape[1])
        ya = _hgrn(acts3, logf.reshape(bsz, seq, logf.shape[1]), hg_norm[layer])
        yb = _rglru(acts3, rg_conv_w[layer], rg_conv_b[layer], rg_w_a[layer], rg_b_a[layer],
                    rg_w_x[layer], rg_b_x[layer], rg_lambda[layer])
        w_route = jnp.concatenate(
            [w_group[layer], w_router[layer].reshape(d, N_EXPERTS),
             jnp.zeros((d, LANES - N_GROUPS - N_EXPERTS), F32)], axis=1)
        w_route_hi = w_route.astype(BF16)
        w_route = jnp.concatenate([w_route_hi, (w_route - w_route_hi.astype(F32)).astype(BF16)], axis=1)
        x1, h2, route, ew, counts = _mixout(
            ya.reshape(n, -1), yb.reshape(n, -1), acts, x2, gate1, scale2, shift2, norm2[layer],
            w_proj_a[layer].astype(BF16), w_proj_b[layer].astype(BF16), w_out[layer].astype(BF16), w_route, seq)
        plan, pos_tiles = _dispatch_plan(route, counts)
        ys = _experts(h2, w_gate, w_up, w_down, layer, plan)
        if layer + 1 < depth:
            raise NotImplementedError("the fused combine + final norm assumes a single layer")
        x2 = _combine(ys, pos_tiles, x1, ew, gate2, final_norm, seq)
    return x2.reshape(bsz, seq, d)
```

```python
import functools

import jax
import jax.numpy as jnp
import numpy as np
from jax import lax
from jax.experimental import pallas as pl
from jax.experimental.pallas import tpu as pltpu

EPS = 1e-6
LANES = 128
SUBLANES = 8
HG_HEADS = 8
RG_BLOCKS = 8
RG_CONV = 4
RG_C = 8.0
N_GROUPS = 4
EXPERTS_PER_GROUP = 8
N_EXPERTS = N_GROUPS * EXPERTS_PER_GROUP
TOP_K = 2
HG_CHUNK = 128
MOE_BLOCK = 128
EXPERT_X_SLOTS = 8
VMEM_LIMIT = 56 * 1024 * 1024
ACT_Q, ACT_V, ACT_G, ACT_RX, ACT_GY = 32, 40, 48, 56, 64

F32 = jnp.float32
BF16 = jnp.bfloat16
LOG2_E = 1.4426950408889634


def _sigmoid(x):
    return 1.0 / (1.0 + jnp.exp2(x * -LOG2_E))


def _silu(x):
    return x * _sigmoid(x)


def _gelu_tanh(x):
    return 0.5 * x * (1.0 + jnp.tanh(0.7978845608028654 * (x + 0.044715 * (x * x * x))))


def _params(*sem):
    return pltpu.CompilerParams(dimension_semantics=sem, vmem_limit_bytes=VMEM_LIMIT)


def _adaln_kernel(ct_ref, w_ref, b_ref, o_ref, *, kc):
    kdim, nb = ct_ref.shape
    tn = w_ref.shape[1]

    def body(i, accs):
        k0 = pl.multiple_of(i * kc, kc)
        w = w_ref[pl.ds(k0, kc), :]
        cs = _silu(ct_ref[pl.ds(k0, kc), :])
        return tuple(acc + jnp.sum(w * cs[:, b:b + 1], axis=0, keepdims=True)
                     for b, acc in enumerate(accs))

    accs = lax.fori_loop(0, kdim // kc, body, tuple(jnp.zeros((1, tn), F32) for _ in range(nb)))
    for b, acc in enumerate(accs):
        o_ref[b:b + 1, :] = acc + b_ref[...]


def _adaln(c, w, bias):
    bsz, d = c.shape
    n = w.shape[1]
    tn = 1024
    return pl.pallas_call(
        functools.partial(_adaln_kernel, kc=256),
        grid=(n // tn,),
        in_specs=[pl.BlockSpec((d, bsz), lambda j: (0, 0)),
                  pl.BlockSpec((d, tn), lambda j: (0, j)),
                  pl.BlockSpec((1, tn), lambda j: (0, j))],
        out_specs=pl.BlockSpec((bsz, tn), lambda j: (0, j)),
        out_shape=jax.ShapeDtypeStruct((bsz, n), F32),
        compiler_params=_params("parallel"),
        name="adaln",
    )(c.T, w, bias.reshape(1, n))


def _norm_mod_kernel(x_ref, gain_ref, scale_ref, shift_ref, h_ref):
    rows = 64

    def body(i, carry):
        r0 = pl.multiple_of(i * rows, rows)
        x = x_ref[pl.ds(r0, rows), :]
        y = x * lax.rsqrt(jnp.mean(x * x, axis=-1, keepdims=True) + EPS) * gain_ref[...]
        h_ref[pl.ds(r0, rows), :] = (y * (1.0 + scale_ref[0]) + shift_ref[0]).astype(BF16)
        return carry

    lax.fori_loop(0, x_ref.shape[0] // rows, body, 0)


def _cast_weight_tile(w_ref, w_scr):
    rows = 256

    @pl.when(pl.program_id(1) == 0)
    def _():
        def body(i, carry):
            r0 = pl.multiple_of(i * rows, rows)
            w_scr[pl.ds(r0, rows), :] = w_ref[pl.ds(r0, rows), :].astype(BF16)
            return carry

        lax.fori_loop(0, w_ref.shape[0] // rows, body, 0)


def _inproj_logf_kernel(h_ref, w_ref, lbt_ref, o_ref, w_scr, *, layer):
    _cast_weight_tile(w_ref, w_scr)
    t = lbt_ref[...]
    e = jnp.exp(t - jnp.max(t, axis=0, keepdims=True))
    lb = jnp.sum(e[:layer + 1], axis=0, keepdims=True) / jnp.sum(e, axis=0, keepdims=True)
    z = jnp.dot(h_ref[...], w_scr[...], preferred_element_type=F32)
    o_ref[...] = jnp.log2(lb + (1.0 - lb) * _sigmoid(z))


def _inproj_act_kernel(h_ref, w_ref, o_ref, w_scr):
    j = pl.program_id(0)
    _cast_weight_tile(w_ref, w_scr)
    z = jnp.dot(h_ref[...], w_scr[...], preferred_element_type=F32)

    @pl.when((j == 4) | (j == 6))
    def _():
        o_ref[...] = _silu(z).astype(BF16)

    @pl.when((j < 4) | (j == 5) | (j == 7))
    def _():
        o_ref[...] = z.astype(BF16)

    @pl.when(j == 8)
    def _():
        o_ref[...] = _gelu_tanh(z).astype(BF16)


def _inproj(x2, gain, scale, shift, w_in, lb_table, layer, seq):
    n, d = x2.shape
    bsz = n // seq
    tn = 1024
    tm = 512
    per_batch = pl.BlockSpec((1, 1, d), lambda i: (i // (seq // tm), 0, 0))
    h = pl.pallas_call(
        _norm_mod_kernel,
        grid=(n // tm,),
        in_specs=[pl.BlockSpec((tm, d), lambda i: (i, 0)), pl.BlockSpec((1, d), lambda i: (0, 0)),
                  per_batch, per_batch],
        out_specs=pl.BlockSpec((tm, d), lambda i: (i, 0)),
        out_shape=jax.ShapeDtypeStruct((n, d), BF16),
        compiler_params=_params("parallel"),
        name="norm_mod",
    )(x2, gain.reshape(1, d), scale.reshape(bsz, 1, d), shift.reshape(bsz, 1, d))

    n_lb = lb_table.shape[0]
    tm = 1024
    logf = pl.pallas_call(
        functools.partial(_inproj_logf_kernel, layer=layer),
        grid=(2, n // tm),
        in_specs=[pl.BlockSpec((tm, d), lambda j, i: (i, 0)),
                  pl.BlockSpec((None, d, tn), lambda j, i: (layer, 0, j + 2)),
                  pl.BlockSpec((n_lb, tn), lambda j, i: (0, j))],
        out_specs=pl.BlockSpec((tm, tn), lambda j, i: (i, j)),
        out_shape=jax.ShapeDtypeStruct((n, 2 * tn), F32),
        scratch_shapes=[pltpu.VMEM((d, tn), BF16)],
        compiler_params=_params("parallel", "arbitrary"),
        name="inproj_logf",
    )(h, w_in, lb_table)
    acts = pl.pallas_call(
        _inproj_act_kernel,
        grid=(9, n // tm),
        in_specs=[pl.BlockSpec((tm, d), lambda j, i: (i, 0)),
                  pl.BlockSpec((None, d, tn),
                               lambda j, i: (layer, 0, jnp.where(j < 4, j + 7, jnp.where(j < 6, j - 4, j - 2))))],
        out_specs=pl.BlockSpec((tm, tn), lambda j, i: (i, j)),
        out_shape=jax.ShapeDtypeStruct((n, 9 * tn), BF16),
        scratch_shapes=[pltpu.VMEM((d, tn), BF16)],
        compiler_params=_params("parallel", "arbitrary"),
        name="inproj_act",
    )(h, w_in)
    return logf, acts


def _hgrn_level_halves(chunk):
    return [chunk >> (i + 1) for i in range(chunk.bit_length() - 1)]


def _hgrn_kernel(q_ref, v_ref, g_ref, lff_ref, lfb_ref, gain_ref, o_ref,
                 oacc, qf_scr, kf_scr, qb_scr, kb_scr, *, chunk, group):
    seq = q_ref.shape[1]
    nchunks = seq // chunk
    halves = _hgrn_level_halves(chunk)
    nt = (((1,), (1,)), ((), ()))
    tn = (((0,), (0,)), ((), ()))

    row = lax.broadcasted_iota(jnp.int32, (chunk, LANES), 0)
    ti = lax.broadcasted_iota(jnp.int32, (chunk, chunk), 0)
    si = lax.broadcasted_iota(jnp.int32, (chunk, chunk), 1)
    txs = ti ^ si
    tri_f = jnp.where(si <= ti, 1.0, 0.0).astype(BF16)
    tri_b = jnp.where(si >= ti, 1.0, 0.0).astype(BF16)

    def cumulative(lf, tri):
        hi = lf.astype(BF16)
        mid = (lf - hi.astype(F32)).astype(BF16)
        r = jnp.dot(tri, jnp.concatenate([hi, mid], axis=1), preferred_element_type=F32)
        return r[:, :LANES] + r[:, LANES:]

    lid = functools.reduce(lambda a, b: a + b, [(txs >= h).astype(jnp.int32) for h in halves])

    def boundary_diff(cum, m, reverse):
        pieces = []
        for base in range(0, chunk, 2 * m):
            bnd = base + (m if reverse else m - 1)
            pieces.append(jnp.broadcast_to(cum[bnd:bnd + 1, :], (2 * m, LANES)))
        return cum - jnp.concatenate(pieces, axis=0)

    def intra(c):
        r0 = pl.multiple_of(c * chunk, chunk)
        q = q_ref[0, pl.ds(r0, chunk), :].astype(F32)
        v = v_ref[0, pl.ds(r0, chunk), :]
        lff = lff_ref[0, pl.ds(r0, chunk), :]
        lfb = lfb_ref[0, pl.ds(r0, chunk), :]
        cum_f = cumulative(lff, tri_f)
        cum_b = cumulative(lfb, tri_b)
        ff = jnp.exp2(lff)
        fb = jnp.exp2(lfb)
        kf = 1.0 - ff
        kb = 1.0 - fb

        p0 = lax.dot_general(q.astype(BF16), (kf + kb).astype(BF16), nt, preferred_element_type=F32)
        scores = jnp.where(lid == 0, p0, 0.0)
        for m in halves:
            later = (row & m) != 0
            if 2 * m >= SUBLANES:
                d_f = boundary_diff(cum_f, m, False)
                d_b = boundary_diff(cum_b, m, True)
                x = q * jnp.exp2(jnp.where(later, d_f, d_b))
                y = jnp.where(later, kb, kf) * jnp.exp2(-jnp.where(later, d_b, d_f))
            elif m == 2:
                r4 = row & 3
                prev = lambda a: pltpu.roll(a, 1, 0)
                nxt = lambda a: pltpu.roll(a, chunk - 1, 0)
                x = q * jnp.where(r4 == 0, fb * nxt(fb), jnp.where(r4 == 1, fb,
                                  jnp.where(r4 == 2, ff, ff * prev(ff))))
                y = jnp.where(r4 == 0, kf * nxt(ff), jnp.where(r4 == 1, kf,
                              jnp.where(r4 == 2, kb, kb * prev(fb))))
            else:
                x = q * jnp.where(later, ff, fb)
                y = jnp.where(later, kb, kf)
            pm = lax.dot_general(x.astype(BF16), y.astype(BF16), nt, preferred_element_type=F32)
            scores = jnp.where(lid == m.bit_length(), pm, scores)
        oacc[pl.ds(r0, chunk), :] = jnp.dot(scores.astype(BF16), v, preferred_element_type=F32)
        tot_f = cum_f[chunk - 1:chunk, :]
        tot_b = cum_b[0:1, :]
        qf_scr[pl.ds(r0, chunk), :] = (q * jnp.exp2(cum_f)).astype(BF16)
        kf_scr[pl.ds(r0, chunk), :] = (kf * jnp.exp2(tot_f - cum_f)).astype(BF16)
        qb_scr[pl.ds(r0, chunk), :] = (q * jnp.exp2(cum_b)).astype(BF16)
        kb_scr[pl.ds(r0, chunk), :] = (kb * jnp.exp2(tot_b - cum_b)).astype(BF16)

    def intra_body(i, carry):
        for u in range(group):
            intra(i * group + u)
        return carry

    lax.fori_loop(0, nchunks // group, intra_body, 0)

    def apply_state(c, st, q_scr, k_scr, lf_ref):
        r0 = pl.multiple_of(c * chunk, chunk)
        o = oacc[pl.ds(r0, chunk), :] + lax.dot_general(
            q_scr[pl.ds(r0, chunk), :], st.astype(BF16), nt, preferred_element_type=F32)
        tot = jnp.sum(lf_ref[0, pl.ds(r0, chunk), :], axis=0, keepdims=True)
        st = st * jnp.exp2(tot) + lax.dot_general(
            v_ref[0, pl.ds(r0, chunk), :], k_scr[pl.ds(r0, chunk), :], tn, preferred_element_type=F32)
        return r0, o, st

    def finish(r0, o):
        y = o * lax.rsqrt(jnp.mean(o * o, axis=-1, keepdims=True) + EPS) * gain_ref[0]
        o_ref[0, pl.ds(r0, chunk), :] = (y * g_ref[0, pl.ds(r0, chunk), :].astype(F32)).astype(BF16)

    def sweep(final):
        def body(i, states):
            st_f, st_b = states
            rf, of, st_f = apply_state(i, st_f, qf_scr, kf_scr, lff_ref)
            rb, ob, st_b = apply_state(nchunks - 1 - i, st_b, qb_scr, kb_scr, lfb_ref)
            if final:
                finish(rf, of)
                finish(rb, ob)
            else:
                oacc[pl.ds(rf, chunk), :] = of
                oacc[pl.ds(rb, chunk), :] = ob
            return st_f, st_b
        return body

    zero = jnp.zeros((LANES, LANES), F32)
    states = lax.fori_loop(0, nchunks // 2, sweep(False), (zero, zero), unroll=8)
    lax.fori_loop(nchunks // 2, nchunks, sweep(True), states, unroll=8)


def _hgrn(acts3, logf3, hg_norm):
    bsz, seq, _ = acts3.shape
    col = lambda off: pl.BlockSpec((1, seq, LANES), lambda b, h: (b, 0, off + h))
    return pl.pallas_call(
        functools.partial(_hgrn_kernel, chunk=HG_CHUNK, group=8),
        grid=(bsz, HG_HEADS),
        in_specs=[col(ACT_Q), col(ACT_V), col(ACT_G), col(0), col(HG_HEADS),
                  pl.BlockSpec((1, 1, LANES), lambda b, h: (h, 0, 0))],
        out_specs=pl.BlockSpec((1, seq, LANES), lambda b, h: (b, 0, h)),
        out_shape=jax.ShapeDtypeStruct((bsz, seq, HG_HEADS * LANES), BF16),
        scratch_shapes=[pltpu.VMEM((seq, LANES), F32)] + [pltpu.VMEM((seq, LANES), BF16)] * 4,
        compiler_params=_params("parallel", "parallel"),
        name="hgrn2",
    )(acts3, acts3, acts3, logf3, logf3, hg_norm.reshape(HG_HEADS, 1, LANES))


RG_SCAN_SUB = 32


def _scan_tile(a, u, carry, reverse):
    rows = a.shape[0]
    sub = RG_SCAN_SUB
    pos = lax.broadcasted_iota(jnp.int32, a.shape, 0) & (sub - 1)
    d = 1
    while d < sub:
        if reverse:
            ok = pos < sub - d
            shift = rows - d
        else:
            ok = pos >= d
            shift = d
        a_s = jnp.where(ok, pltpu.roll(a, shift, 0), 1.0)
        u_s = jnp.where(ok, pltpu.roll(u, shift, 0), 0.0)
        u = u + a * u_s
        a = a * a_s
        d *= 2
    spans = list(range(rows // sub))
    pieces = [None] * len(spans)
    for k in (reversed(spans) if reverse else spans):
        hk = u[k * sub:(k + 1) * sub] + a[k * sub:(k + 1) * sub] * carry
        carry = hk[0:1] if reverse else hk[sub - 1:sub]
        pieces[k] = hk
    return jnp.concatenate(pieces, axis=0), carry


def _rglru_kernel(rx_ref, gy_ref, cw_ref, cb_ref, wa_ref, ba_ref, wx_ref, bx_ref, lam_ref, o_ref,
                  xpad, hf_scr, ab_scr, ub_scr, *, tile):
    seq = rx_ref.shape[1]
    ntiles = seq // tile
    halo = 8

    xpad[0:halo, :] = jnp.zeros((halo, LANES), F32)
    xpad[halo + seq:halo + seq + halo, :] = jnp.zeros((halo, LANES), F32)
    xpad[halo:halo + seq, :] = rx_ref[0].astype(F32)

    lam = lam_ref[...]
    neg_c_softplus = (-RG_C * LOG2_E) * (jnp.maximum(-lam, 0.0) + jnp.log(1.0 + jnp.exp(-jnp.abs(lam))))

    def gates(xcb, z):
        r = _sigmoid(jnp.dot(xcb, wa_ref[z, 0], preferred_element_type=F32) + ba_ref[z, 0])
        gi = _sigmoid(jnp.dot(xcb, wx_ref[z, 0], preferred_element_type=F32) + bx_ref[z, 0])
        a = jnp.exp2(r * neg_c_softplus[z:z + 1, :])
        y = 1.0 - a * a
        return a, jnp.where(y > 0.0, y * lax.rsqrt(y), 0.0) * gi

    def fwd_body(i, carry):
        t0 = pl.multiple_of(i * tile, tile)
        ext = xpad[pl.ds(t0, tile + 2 * halo), :]
        n_ext = tile + 2 * halo
        tap = lambda off: pltpu.roll(ext, (-off) % n_ext, 0)[halo:halo + tile]
        xc = (cw_ref[0:1, :] * tap(-2) + cw_ref[1:2, :] * tap(-1) + cw_ref[2:3, :] * ext[halo:halo + tile]
              + cw_ref[3:4, :] * tap(1) + cb_ref[...])
        xcb = xc.astype(BF16)
        a_f, s_f = gates(xcb, 0)
        a_b, s_b = gates(xcb, 1)
        h, carry = _scan_tile(a_f, s_f * xc, carry, reverse=False)
        hf_scr[pl.ds(t0, tile), :] = h
        ab_scr[pl.ds(t0, tile), :] = a_b
        ub_scr[pl.ds(t0, tile), :] = s_b * xc
        return carry

    lax.fori_loop(0, ntiles, fwd_body, jnp.zeros((1, LANES), F32))

    def bwd_body(i, carry):
        t0 = pl.multiple_of((ntiles - 1 - i) * tile, tile)
        h, carry = _scan_tile(ab_scr[pl.ds(t0, tile), :], ub_scr[pl.ds(t0, tile), :], carry, reverse=True)
        o_ref[0, pl.ds(t0, tile), :] = (
            (hf_scr[pl.ds(t0, tile), :] + h) * gy_ref[0, pl.ds(t0, tile), :].astype(F32)).astype(BF16)
        return carry

    lax.fori_loop(0, ntiles, bwd_body, jnp.zeros((1, LANES), F32))


def _rglru(acts3, conv_w, conv_b, w_a, b_a, w_x, b_x, lam):
    bsz, seq, _ = acts3.shape
    width = RG_BLOCKS * LANES
    tile = 256
    rx_off, gy_off = ACT_RX, ACT_GY
    col = lambda off: pl.BlockSpec((1, seq, LANES), lambda b, j: (b, 0, off + j))
    wspec = pl.BlockSpec((2, 1, LANES, LANES), lambda b, j: (0, j, 0, 0))
    bspec = pl.BlockSpec((2, 1, 1, LANES), lambda b, j: (0, j, 0, 0))
    return pl.pallas_call(
        functools.partial(_rglru_kernel, tile=tile),
        grid=(bsz, RG_BLOCKS),
        in_specs=[col(rx_off), col(gy_off),
                  pl.BlockSpec((RG_CONV, LANES), lambda b, j: (0, j)),
                  pl.BlockSpec((1, LANES), lambda b, j: (0, j)),
                  wspec, bspec, wspec, bspec,
                  pl.BlockSpec((2, LANES), lambda b, j: (0, j))],
        out_specs=pl.BlockSpec((1, seq, LANES), lambda b, j: (b, 0, j)),
        out_shape=jax.ShapeDtypeStruct((bsz, seq, width), BF16),
        scratch_shapes=[pltpu.VMEM((seq + 16, LANES), F32),
                        pltpu.VMEM((seq, LANES), F32),
                        pltpu.VMEM((seq, LANES), F32),
                        pltpu.VMEM((seq, LANES), F32)],
        compiler_params=_params("parallel", "parallel"),
        name="rglru",
    )(acts3, acts3, conv_w, conv_b.reshape(1, width), w_a.astype(BF16),
      b_a.reshape(2, RG_BLOCKS, 1, LANES), w_x.astype(BF16), b_x.reshape(2, RG_BLOCKS, 1, LANES), lam)


def _store_token_major(ref, x, row0=0):
    ntok, d = x.shape
    parts = d // LANES
    for c in range(parts):
        ref[pl.ds(row0 * parts + c, ntok, stride=parts), :] = x[:, c * LANES:(c + 1) * LANES]


def _load_token_major(ref, ntok, parts, row0=0):
    return jnp.concatenate([ref[pl.ds(row0 * parts + c, ntok, stride=parts), :] for c in range(parts)], axis=1)


def _mixout_kernel(ya_ref, yb_ref, ga_ref, gb_ref, x_ref, gate_ref, scale_ref, shift_ref, gain_ref,
                   wpa_ref, wpb_ref, wo_ref, wr_ref, x1_ref, h2_ref, route_ref, ew_ref, cnt_ref, running):
    y_a = jnp.dot(ya_ref[...], wpa_ref[...], preferred_element_type=F32)
    y_b = jnp.dot(yb_ref[...], wpb_ref[...], preferred_element_type=F32)
    merged = _sigmoid(ga_ref[...].astype(F32)) * y_a + _sigmoid(gb_ref[...].astype(F32)) * y_b
    mix = jnp.dot(merged.astype(BF16), wo_ref[...], preferred_element_type=F32)
    x1 = x_ref[...] + gate_ref[0] * mix
    x1_ref[...] = x1
    y = x1 * lax.rsqrt(jnp.mean(x1 * x1, axis=-1, keepdims=True) + EPS) * gain_ref[...]
    h2 = y * (1.0 + scale_ref[0]) + shift_ref[0]
    _store_token_major(h2_ref, h2)

    h_hi = h2.astype(BF16)
    h_lo = (h2 - h_hi.astype(F32)).astype(BF16)
    p_hi = jnp.dot(h_hi, wr_ref[...], preferred_element_type=F32)
    p_lo = jnp.dot(h_lo, wr_ref[...], preferred_element_type=F32)
    logits = (p_hi[:, :LANES] + p_hi[:, LANES:]) + (p_lo[:, :LANES] + p_lo[:, LANES:])
    lane = lax.broadcasted_iota(jnp.int32, logits.shape, 1)
    neg = -jnp.inf
    gl = jnp.where(lane < N_GROUPS, logits, neg)
    gmax = jnp.max(gl, axis=-1, keepdims=True)
    g_sel = jnp.min(jnp.where(gl == gmax, lane, LANES), axis=-1, keepdims=True)
    p_group = 1.0 / jnp.sum(jnp.exp(gl - gmax), axis=-1, keepdims=True)
    lo = N_GROUPS + g_sel * EXPERTS_PER_GROUP
    el = jnp.where((lane >= lo) & (lane < lo + EXPERTS_PER_GROUP), logits, neg)
    v1 = jnp.max(el, axis=-1, keepdims=True)
    i1 = jnp.min(jnp.where(el == v1, lane, LANES), axis=-1, keepdims=True)
    el2 = jnp.where(lane == i1, neg, el)
    v2 = jnp.max(el2, axis=-1, keepdims=True)
    i2 = jnp.min(jnp.where(el2 == v2, lane, LANES), axis=-1, keepdims=True)
    e21 = jnp.exp(v2 - v1)
    w1 = p_group / (1.0 + e21)
    w2 = p_group * e21 / (1.0 + e21)
    ew_ref[...] = jnp.where(lane == 0, w1, jnp.where(lane == 1, w2, 0.0))

    @pl.when(pl.program_id(0) == 0)
    def _():
        running[...] = jnp.zeros(running.shape, F32)

    e1, e2 = i1 - N_GROUPS, i2 - N_GROUPS
    tm = logits.shape[0]
    before = (lax.broadcasted_iota(jnp.int32, (tm, tm), 0)
              > lax.broadcasted_iota(jnp.int32, (tm, tm), 1))
    before = jnp.where(before, 1.0, 0.0).astype(BF16)
    oh1 = jnp.where(lane == e1, 1.0, 0.0)
    oh2 = jnp.where(lane == e2, 1.0, 0.0)
    pre1 = jnp.dot(before, oh1.astype(BF16), preferred_element_type=F32)
    pre2 = jnp.dot(before, oh2.astype(BF16), preferred_element_type=F32)
    cnt1 = jnp.sum(oh1, axis=0, keepdims=True)
    base = running[...]
    r1 = jnp.sum(oh1 * (pre1 + base), axis=-1, keepdims=True).astype(jnp.int32)
    r2 = jnp.sum(oh2 * (pre2 + base + cnt1), axis=-1, keepdims=True).astype(jnp.int32)
    total = base + cnt1 + jnp.sum(oh2, axis=0, keepdims=True)
    running[...] = total
    cnt_ref[...] = total.astype(jnp.int32)
    cols = jnp.where(lane == 0, e1, jnp.where(lane == 1, e2,
                     jnp.where(lane == 2, r1 >> 7, jnp.where(lane == 3, r1 & 127,
                               jnp.where(lane == 4, r2 >> 7, jnp.where(lane == 5, r2 & 127, 0))))))
    srow = lax.broadcasted_iota(jnp.int32, (SUBLANES, LANES), 0)
    slane = lax.broadcasted_iota(jnp.int32, (SUBLANES, LANES), 1)
    sel = jnp.where((srow < 2) & (slane == srow), 1.0,
                    jnp.where((srow >= 2) & (srow < 4) & (slane == 2 * srow - 2), 128.0,
                              jnp.where((srow >= 2) & (srow < 4) & (slane == 2 * srow - 1), 1.0, 0.0)))
    route_t = lax.dot_general(sel.astype(BF16), cols.astype(F32).astype(BF16), (((1,), (1,)), ((), ())),
                              preferred_element_type=F32)
    route_ref[...] = route_t.astype(jnp.int32)


def _mixout(ya, yb, acts, x2, gate1, scale2, shift2, gain2, wpa, wpb, wo, w_route, seq):
    n, d = x2.shape
    bsz = n // seq
    tm = 256
    tpb = seq // tm
    half = ya.shape[1]
    ga_blk = 0
    per_batch = pl.BlockSpec((1, 1, d), lambda i: (i // tpb, 0, 0))
    const = lambda shape: pl.BlockSpec(shape, lambda i: (0, 0), pipeline_mode=pl.Buffered(1))
    return pl.pallas_call(
        _mixout_kernel,
        grid=(n // tm,),
        in_specs=[pl.BlockSpec((tm, half), lambda i: (i, 0)),
                  pl.BlockSpec((tm, half), lambda i: (i, 0)),
                  pl.BlockSpec((tm, d), lambda i: (i, ga_blk)),
                  pl.BlockSpec((tm, d), lambda i: (i, ga_blk + 1)),
                  pl.BlockSpec((tm, d), lambda i: (i, 0)),
                  per_batch, per_batch, per_batch,
                  const((1, d)), const((half, d)), const((half, d)), const((d, d)), const((d, 2 * LANES))],
        out_specs=[pl.BlockSpec((tm, d), lambda i: (i, 0)),
                   pl.BlockSpec((tm * (d // LANES), LANES), lambda i: (i, 0)),
                   pl.BlockSpec((SUBLANES, tm), lambda i: (0, i)),
                   pl.BlockSpec((tm, LANES), lambda i: (i, 0)),
                   pl.BlockSpec((1, LANES), lambda i: (0, 0))],
        out_shape=[jax.ShapeDtypeStruct((n, d), F32),
                   jax.ShapeDtypeStruct((n * (d // LANES), LANES), F32),
                   jax.ShapeDtypeStruct((SUBLANES, n), jnp.int32),
                   jax.ShapeDtypeStruct((n, LANES), F32),
                   jax.ShapeDtypeStruct((1, LANES), jnp.int32)],
        scratch_shapes=[pltpu.VMEM((1, LANES), F32)],
        compiler_params=_params("arbitrary"),
        name="mixout_route",
    )(ya, yb, acts, acts, x2, gate1.reshape(bsz, 1, d), scale2.reshape(bsz, 1, d),
      shift2.reshape(bsz, 1, d), gain2.reshape(1, d), wpa, wpb, wo, w_route)


def _start_token_gather(idx_ref, base, ntok, parts, src_hbm, dst, sem):
    for r in range(ntok):
        t0 = pl.multiple_of(idx_ref[base + r] * parts, parts)
        pltpu.make_async_copy(src_hbm.at[pl.ds(t0, parts), :], dst.at[pl.ds(r * parts, parts), :], sem).start()


def _wait_token_gather(ntok, parts, src_hbm, dst, sem):
    pltpu.make_async_copy(src_hbm.at[pl.ds(0, ntok * parts), :], dst.at[pl.ds(0, ntok * parts), :], sem).wait()


def _expert_kernel(be_ref, nv_ref, first_ref, next_ref, dest_ref, nused_ref,
                   h_hbm, wg_hbm, wu_hbm, wd_hbm, zeros_hbm, o_ref,
                   xbuf, xsems, row_tok, stage_g, stage_u, stage_d, wg_bf, wu_bf, wd_bf, wsems, *, layer):
    b = pl.program_id(0)
    nb = pl.num_programs(0)
    n_used = nused_ref[0]
    unroll = 8
    cast_rows = 256

    def weight_copies(e):
        return (pltpu.make_async_copy(wg_hbm.at[layer, e], stage_g, wsems.at[0]),
                pltpu.make_async_copy(wu_hbm.at[layer, e], stage_u, wsems.at[1]),
                pltpu.make_async_copy(wd_hbm.at[layer, e], stage_d, wsems.at[2]))

    parts = xbuf.shape[1] // MOE_BLOCK
    nslot = xbuf.shape[0]
    ahead = nslot - 1

    def gather(blk, slot):
        _start_token_gather(row_tok, blk * MOE_BLOCK, MOE_BLOCK, parts, h_hbm, xbuf.at[slot], xsems.at[slot])

    def wait_gather(slot):
        _wait_token_gather(MOE_BLOCK, parts, h_hbm, xbuf.at[slot], xsems.at[slot])

    @pl.when(b == 0)
    def _():
        for cp in weight_copies(be_ref[0]):
            cp.start()

        clear = pltpu.make_async_copy(zeros_hbm, row_tok, wsems.at[3])
        clear.start()
        clear.wait()

        ntok = dest_ref.shape[0] // TOP_K
        for k in range(TOP_K):
            def invert(i, carry, k=k):
                for u in range(unroll):
                    row_tok[dest_ref[k * ntok + i * unroll + u]] = i * unroll + u
                return carry

            lax.fori_loop(0, ntok // unroll, invert, 0)
        for q in range(ahead):
            gather(q, q)

    @pl.when(b < n_used + ahead)
    def _():
        wait_gather(b % nslot)

    @pl.when(b < n_used)
    def _():
        slot = b % nslot

        @pl.when(first_ref[b] == 1)
        def _():
            for m, (stage, dst) in enumerate(((stage_g, wg_bf), (stage_u, wu_bf), (stage_d, wd_bf))):
                weight_copies(0)[m].wait()

                def cast(i, carry, stage=stage, dst=dst):
                    r0 = pl.multiple_of(i * cast_rows, cast_rows)
                    dst[pl.ds(r0, cast_rows), :] = stage[pl.ds(r0, cast_rows), :].astype(BF16)
                    return carry

                lax.fori_loop(0, stage.shape[0] // cast_rows, cast, 0)

                @pl.when(next_ref[b] >= 0)
                def _(m=m):
                    weight_copies(next_ref[b])[m].start()

        valid = lax.broadcasted_iota(jnp.int32, (MOE_BLOCK, 1), 0) < nv_ref[b]
        xb = jnp.where(valid, _load_token_major(xbuf.at[slot], MOE_BLOCK, parts), 0.0).astype(BF16)
        gather(b + ahead, (b + ahead) % nslot)
        gate = jnp.dot(xb, wg_bf[...], preferred_element_type=F32)
        up = jnp.dot(xb, wu_bf[...], preferred_element_type=F32)
        hid = (_silu(gate) * up).astype(BF16)
        _store_token_major(o_ref, jnp.dot(hid, wd_bf[...], preferred_element_type=F32))

    @pl.when(b >= n_used)
    def _():
        o_ref[...] = jnp.zeros(o_ref.shape, o_ref.dtype)

    @pl.when(b == nb - 1)
    def _():
        for q in range(ahead):
            @pl.when(nb + q < n_used + ahead)
            def _(q=q):
                wait_gather((nb + q) % nslot)


def _experts(h2_tm, wg, wu, wd, layer, plan):
    p = plan["rows"]
    nb = p // MOE_BLOCK
    d, de = wg.shape[-2:]
    parts = d // LANES
    map_len = -(-(p + (EXPERT_X_SLOTS - 1) * MOE_BLOCK) // 1024) * 1024
    hbm = pl.BlockSpec(memory_space=pl.ANY)
    grid_spec = pltpu.PrefetchScalarGridSpec(
        num_scalar_prefetch=6,
        grid=(nb,),
        in_specs=[hbm, hbm, hbm, hbm, hbm],
        out_specs=pl.BlockSpec((MOE_BLOCK * parts, LANES), lambda b, *_: (b, 0)),
        scratch_shapes=[pltpu.VMEM((EXPERT_X_SLOTS, MOE_BLOCK * parts, LANES), F32),
                        pltpu.SemaphoreType.DMA((EXPERT_X_SLOTS,)),
                        pltpu.SMEM((map_len,), jnp.int32),
                        pltpu.VMEM((d, de), F32), pltpu.VMEM((d, de), F32), pltpu.VMEM((de, d), F32),
                        pltpu.VMEM((d, de), BF16), pltpu.VMEM((d, de), BF16), pltpu.VMEM((de, d), BF16),
                        pltpu.SemaphoreType.DMA((4,))],
    )
    return pl.pallas_call(
        functools.partial(_expert_kernel, layer=layer),
        grid_spec=grid_spec,
        out_shape=jax.ShapeDtypeStruct((p * parts, LANES), F32),
        compiler_params=_params("arbitrary"),
        name="experts",
    )(plan["block_expert"], plan["n_valid"], plan["first"], plan["next_expert"], plan["dest"],
      plan["n_used"], h2_tm, wg, wu, wd, jnp.zeros((map_len,), jnp.int32))


COMBINE_TILE = 64
COMBINE_BATCHES = 8


def _combine_kernel(pos_ref, y_hbm, x1_ref, ew_ref, gate_ref, gain_ref, o_ref, *scratch):
    i = pl.program_id(0)
    tm = COMBINE_TILE
    nbuf = COMBINE_BATCHES
    rows = TOP_K * tm
    bufs, sems = scratch[:nbuf], scratch[nbuf]

    parts = x1_ref.shape[1] // LANES

    def gather(tile, k):
        _start_token_gather(pos_ref, tile * rows, rows, parts, y_hbm, bufs[k], sems.at[k])

    def wait(k):
        _wait_token_gather(rows, parts, y_hbm, bufs[k], sems.at[k])

    def finish(k):
        sl = slice(k * tm, (k + 1) * tm)
        ew = ew_ref[sl, :]
        moe = ew[:, 0:1] * _load_token_major(bufs[k], tm, parts)
        for j in range(1, TOP_K):
            moe = moe + ew[:, j:j + 1] * _load_token_major(bufs[k], tm, parts, row0=j * tm)
        x2 = x1_ref[sl, :] + gate_ref[0] * moe
        o_ref[sl, :] = x2 * lax.rsqrt(jnp.mean(x2 * x2, axis=-1, keepdims=True) + EPS) * gain_ref[...]

    ahead = nbuf - 1

    @pl.when(i == 0)
    def _():
        for q in range(ahead):
            gather(q, q)

    for q in range(nbuf):
        wait(q)
        gather(nbuf * i + q + ahead, (q + ahead) % nbuf)
        finish(q)

    @pl.when(i == pl.num_programs(0) - 1)
    def _():
        for q in range(ahead):
            wait(q)


def _combine(ys, pos_tiles, x1, ew, gate2, final_gain, seq):
    n, d = x1.shape
    bsz = n // seq
    tm = COMBINE_BATCHES * COMBINE_TILE
    tpb = seq // tm
    grid_spec = pltpu.PrefetchScalarGridSpec(
        num_scalar_prefetch=1,
        grid=(n // tm,),
        in_specs=[pl.BlockSpec(memory_space=pl.ANY),
                  pl.BlockSpec((tm, d), lambda i, pos: (i, 0)),
                  pl.BlockSpec((tm, LANES), lambda i, pos: (i, 0)),
                  pl.BlockSpec((1, 1, d), lambda i, pos: (i // tpb, 0, 0)),
                  pl.BlockSpec((1, d), lambda i, pos: (0, 0))],
        out_specs=pl.BlockSpec((tm, d), lambda i, pos: (i, 0)),
        scratch_shapes=[pltpu.VMEM((TOP_K * COMBINE_TILE * (d // LANES), LANES), F32)] * COMBINE_BATCHES
        + [pltpu.SemaphoreType.DMA((COMBINE_BATCHES,))],
    )
    return pl.pallas_call(
        _combine_kernel,
        grid_spec=grid_spec,
        out_shape=jax.ShapeDtypeStruct((n, d), F32),
        compiler_params=_params("arbitrary"),
        name="combine_norm",
    )(pos_tiles, ys, x1, ew, gate2.reshape(bsz, 1, d), final_gain.reshape(1, d))


def _dispatch_plan(route, counts):
    n = route.shape[1]
    a = n * TOP_K
    eid = route[:TOP_K]
    rank = route[TOP_K:2 * TOP_K]
    cnt = counts[0, :N_EXPERTS]
    padded = (cnt + MOE_BLOCK - 1) // MOE_BLOCK * MOE_BLOCK
    pad_ends = jnp.cumsum(padded)
    pad_starts = pad_ends - padded
    experts = jnp.arange(N_EXPERTS, dtype=jnp.int32)
    dest = (jnp.sum(jnp.where(eid[..., None] == experts, pad_starts, 0), axis=-1) + rank).astype(jnp.int32)
    p = ((a + MOE_BLOCK - 1) // MOE_BLOCK + N_EXPERTS) * MOE_BLOCK
    nb = p // MOE_BLOCK
    n_used = (pad_ends[-1] // MOE_BLOCK).astype(jnp.int32)
    blk = jnp.arange(nb, dtype=jnp.int32)
    blk_start = jnp.minimum(blk, n_used - 1) * MOE_BLOCK
    block_expert = jnp.minimum(jnp.sum((pad_ends[None, :] <= blk_start[:, None]).astype(jnp.int32), axis=1),
                               N_EXPERTS - 1).astype(jnp.int32)
    def lookup(table, idx):
        return jnp.sum(jnp.where(idx[..., None] == jnp.arange(table.shape[0], dtype=jnp.int32), table, 0), axis=-1)

    blk_pad_start = lookup(pad_starts, block_expert)
    n_valid = jnp.clip(blk_pad_start + lookup(cnt, block_expert) - blk_start, 0, MOE_BLOCK).astype(jnp.int32)
    first = ((blk_start == blk_pad_start) & (blk < n_used)).astype(jnp.int32)
    nxt_blk = lookup(pad_ends, block_expert) // MOE_BLOCK
    next_expert = jnp.where(nxt_blk < n_used, lookup(block_expert, jnp.minimum(nxt_blk, nb - 1)), -1).astype(jnp.int32)
    pos_tiles = dest.reshape(TOP_K, n // COMBINE_TILE, COMBINE_TILE).transpose(1, 0, 2).reshape(a)
    pos_tiles = jnp.concatenate(
        [pos_tiles, jnp.zeros(((COMBINE_BATCHES - 1) * TOP_K * COMBINE_TILE,), jnp.int32)])
    plan = dict(block_expert=block_expert, n_valid=n_valid, first=first, next_expert=next_expert,
                dest=dest.reshape(a), n_used=n_used.reshape(1), rows=p)
    return plan, pos_tiles


def kernel(x, c, w_ada, b_ada, norm1, w_in, hg_lb, hg_norm, rg_conv_w, rg_conv_b, rg_w_a, rg_b_a,
           rg_w_x, rg_b_x, rg_lambda, w_proj_a, w_proj_b, w_out, norm2, w_group, w_router, w_gate,
           w_up, w_down, final_norm):
    bsz, seq, d = x.shape
    depth = w_ada.shape[0]
    n = bsz * seq
    x2 = x.reshape(n, d)
    lb_table = hg_lb.reshape(hg_lb.shape[0], -1)
    for layer in range(depth):
        mod = _adaln(c, w_ada[layer], b_ada[layer])
        shift1, scale1, gate1, shift2, scale2, gate2 = jnp.split(mod, 6, axis=-1)
        logf, acts = _inproj(x2, norm1[layer], scale1, shift1, w_in, lb_table, layer, seq)
        acts3 = acts.reshape(bsz, seq, acts.shape[1])
        ya = _hgrn(acts3, logf.reshape(bsz, seq, logf.shape[1]), hg_norm[layer])
        yb = _rglru(acts3, rg_conv_w[layer], rg_conv_b[layer], rg_w_a[layer], rg_b_a[layer],
                    rg_w_x[layer], rg_b_x[layer], rg_lambda[layer])
        w_route = jnp.concatenate(
            [w_group[layer], w_router[layer].reshape(d, N_EXPERTS),
             jnp.zeros((d, LANES - N_GROUPS - N_EXPERTS), F32)], axis=1)
        w_route_hi = w_route.astype(BF16)
        w_route = jnp.concatenate([w_route_hi, (w_route - w_route_hi.astype(F32)).astype(BF16)], axis=1)
        x1, h2, route, ew, counts = _mixout(
            ya.reshape(n, -1), yb.reshape(n, -1), acts, x2, gate1, scale2, shift2, norm2[layer],
            w_proj_a[layer].astype(BF16), w_proj_b[layer].astype(BF16), w_out[layer].astype(BF16), w_route, seq)
        plan, pos_tiles = _dispatch_plan(route, counts)
        ys = _experts(h2, w_gate, w_up, w_down, layer, plan)
        if layer + 1 < depth:
            raise NotImplementedError("the fused combine + final norm assumes a single layer")
        x2 = _combine(ys, pos_tiles, x1, ew, gate2, final_norm, seq)
    return x2.reshape(bsz, seq, d)
```

```python
import functools

import jax
import jax.numpy as jnp
import numpy as np
from jax import lax
from jax.experimental import pallas as pl
from jax.experimental.pallas import tpu as pltpu

EPS = 1e-6
LANES = 128
SUBLANES = 8
HG_HEADS = 8
RG_BLOCKS = 8
RG_CONV = 4
RG_C = 8.0
N_GROUPS = 4
EXPERTS_PER_GROUP = 8
N_EXPERTS = N_GROUPS * EXPERTS_PER_GROUP
TOP_K = 2
HG_CHUNK = 128
MOE_BLOCK = 128
WEIGHT_PIECES = 2
EXPERT_X_SLOTS = 8
VMEM_LIMIT = 56 * 1024 * 1024
ACT_Q, ACT_V, ACT_G, ACT_RX, ACT_GY = 32, 40, 48, 56, 64

F32 = jnp.float32
BF16 = jnp.bfloat16
LOG2_E = 1.4426950408889634


def _sigmoid(x):
    return 1.0 / (1.0 + jnp.exp2(x * -LOG2_E))


def _silu(x):
    return x * _sigmoid(x)


def _gelu_tanh(x):
    return 0.5 * x * (1.0 + jnp.tanh(0.7978845608028654 * (x + 0.044715 * (x * x * x))))


def _params(*sem):
    return pltpu.CompilerParams(dimension_semantics=sem, vmem_limit_bytes=VMEM_LIMIT)


def _adaln_kernel(ct_ref, w_ref, b_ref, o_ref, *, kc):
    kdim, nb = ct_ref.shape
    tn = w_ref.shape[1]

    def body(i, accs):
        k0 = pl.multiple_of(i * kc, kc)
        w = w_ref[pl.ds(k0, kc), :]
        cs = _silu(ct_ref[pl.ds(k0, kc), :])
        return tuple(acc + jnp.sum(w * cs[:, b:b + 1], axis=0, keepdims=True)
                     for b, acc in enumerate(accs))

    accs = lax.fori_loop(0, kdim // kc, body, tuple(jnp.zeros((1, tn), F32) for _ in range(nb)))
    for b, acc in enumerate(accs):
        o_ref[b:b + 1, :] = acc + b_ref[...]


def _adaln(c, w, bias):
    bsz, d = c.shape
    n = w.shape[1]
    tn = 1024
    return pl.pallas_call(
        functools.partial(_adaln_kernel, kc=256),
        grid=(n // tn,),
        in_specs=[pl.BlockSpec((d, bsz), lambda j: (0, 0)),
                  pl.BlockSpec((d, tn), lambda j: (0, j)),
                  pl.BlockSpec((1, tn), lambda j: (0, j))],
        out_specs=pl.BlockSpec((bsz, tn), lambda j: (0, j)),
        out_shape=jax.ShapeDtypeStruct((bsz, n), F32),
        compiler_params=_params("parallel"),
        name="adaln",
    )(c.T, w, bias.reshape(1, n))


def _norm_mod_kernel(x_ref, gain_ref, scale_ref, shift_ref, h_ref):
    rows = 64

    def body(i, carry):
        r0 = pl.multiple_of(i * rows, rows)
        x = x_ref[pl.ds(r0, rows), :]
        y = x * lax.rsqrt(jnp.mean(x * x, axis=-1, keepdims=True) + EPS) * gain_ref[...]
        h_ref[pl.ds(r0, rows), :] = (y * (1.0 + scale_ref[0]) + shift_ref[0]).astype(BF16)
        return carry

    lax.fori_loop(0, x_ref.shape[0] // rows, body, 0)


def _cast_weight_tile(w_ref, w_scr):
    rows = 256

    @pl.when(pl.program_id(1) == 0)
    def _():
        def body(i, carry):
            r0 = pl.multiple_of(i * rows, rows)
            w_scr[pl.ds(r0, rows), :] = w_ref[pl.ds(r0, rows), :].astype(BF16)
            return carry

        lax.fori_loop(0, w_ref.shape[0] // rows, body, 0)


def _inproj_logf_kernel(h_ref, w_ref, lbt_ref, o_ref, w_scr, *, layer):
    _cast_weight_tile(w_ref, w_scr)
    t = lbt_ref[...]
    e = jnp.exp(t - jnp.max(t, axis=0, keepdims=True))
    lb = jnp.sum(e[:layer + 1], axis=0, keepdims=True) / jnp.sum(e, axis=0, keepdims=True)
    z = jnp.dot(h_ref[...], w_scr[...], preferred_element_type=F32)
    o_ref[...] = jnp.log2(lb + (1.0 - lb) * _sigmoid(z))


def _inproj_act_kernel(h_ref, w_ref, o_ref, w_scr):
    j = pl.program_id(0)
    _cast_weight_tile(w_ref, w_scr)
    z = jnp.dot(h_ref[...], w_scr[...], preferred_element_type=F32)

    @pl.when((j == 4) | (j == 6))
    def _():
        o_ref[...] = _silu(z).astype(BF16)

    @pl.when((j < 4) | (j == 5) | (j == 7))
    def _():
        o_ref[...] = z.astype(BF16)

    @pl.when(j == 8)
    def _():
        o_ref[...] = _gelu_tanh(z).astype(BF16)


def _inproj(x2, gain, scale, shift, w_in, lb_table, layer, seq):
    n, d = x2.shape
    bsz = n // seq
    tn = 1024
    tm = 512
    per_batch = pl.BlockSpec((1, 1, d), lambda i: (i // (seq // tm), 0, 0))
    h = pl.pallas_call(
        _norm_mod_kernel,
        grid=(n // tm,),
        in_specs=[pl.BlockSpec((tm, d), lambda i: (i, 0)), pl.BlockSpec((1, d), lambda i: (0, 0)),
                  per_batch, per_batch],
        out_specs=pl.BlockSpec((tm, d), lambda i: (i, 0)),
        out_shape=jax.ShapeDtypeStruct((n, d), BF16),
        compiler_params=_params("parallel"),
        name="norm_mod",
    )(x2, gain.reshape(1, d), scale.reshape(bsz, 1, d), shift.reshape(bsz, 1, d))

    n_lb = lb_table.shape[0]
    tm = 1024
    logf = pl.pallas_call(
        functools.partial(_inproj_logf_kernel, layer=layer),
        grid=(2, n // tm),
        in_specs=[pl.BlockSpec((tm, d), lambda j, i: (i, 0)),
                  pl.BlockSpec((None, d, tn), lambda j, i: (layer, 0, j + 2)),
                  pl.BlockSpec((n_lb, tn), lambda j, i: (0, j))],
        out_specs=pl.BlockSpec((tm, tn), lambda j, i: (i, j)),
        out_shape=jax.ShapeDtypeStruct((n, 2 * tn), F32),
        scratch_shapes=[pltpu.VMEM((d, tn), BF16)],
        compiler_params=_params("parallel", "arbitrary"),
        name="inproj_logf",
    )(h, w_in, lb_table)
    acts = pl.pallas_call(
        _inproj_act_kernel,
        grid=(9, n // tm),
        in_specs=[pl.BlockSpec((tm, d), lambda j, i: (i, 0)),
                  pl.BlockSpec((None, d, tn),
                               lambda j, i: (layer, 0, jnp.where(j < 4, j + 7, jnp.where(j < 6, j - 4, j - 2))))],
        out_specs=pl.BlockSpec((tm, tn), lambda j, i: (i, j)),
        out_shape=jax.ShapeDtypeStruct((n, 9 * tn), BF16),
        scratch_shapes=[pltpu.VMEM((d, tn), BF16)],
        compiler_params=_params("parallel", "arbitrary"),
        name="inproj_act",
    )(h, w_in)
    return logf, acts


def _hgrn_level_halves(chunk):
    return [chunk >> (i + 1) for i in range(chunk.bit_length() - 1)]


def _hgrn_kernel(q_ref, v_ref, g_ref, lff_ref, lfb_ref, gain_ref, o_ref,
                 oacc, qf_scr, kf_scr, qb_scr, kb_scr, *, chunk, group):
    seq = q_ref.shape[1]
    nchunks = seq // chunk
    halves = _hgrn_level_halves(chunk)
    nt = (((1,), (1,)), ((), ()))
    tn = (((0,), (0,)), ((), ()))

    row = lax.broadcasted_iota(jnp.int32, (chunk, LANES), 0)
    ti = lax.broadcasted_iota(jnp.int32, (chunk, chunk), 0)
    si = lax.broadcasted_iota(jnp.int32, (chunk, chunk), 1)
    txs = ti ^ si
    tri_f = jnp.where(si <= ti, 1.0, 0.0).astype(BF16)
    tri_b = jnp.where(si >= ti, 1.0, 0.0).astype(BF16)

    def cumulative(lf, tri):
        hi = lf.astype(BF16)
        mid = (lf - hi.astype(F32)).astype(BF16)
        r = jnp.dot(tri, jnp.concatenate([hi, mid], axis=1), preferred_element_type=F32)
        return r[:, :LANES] + r[:, LANES:]

    lid = functools.reduce(lambda a, b: a + b, [(txs >= h).astype(jnp.int32) for h in halves])

    def boundary_diff(cum, m, reverse):
        pieces = []
        for base in range(0, chunk, 2 * m):
            bnd = base + (m if reverse else m - 1)
            pieces.append(jnp.broadcast_to(cum[bnd:bnd + 1, :], (2 * m, LANES)))
        return cum - jnp.concatenate(pieces, axis=0)

    def intra(c):
        r0 = pl.multiple_of(c * chunk, chunk)
        q = q_ref[0, pl.ds(r0, chunk), :].astype(F32)
        v = v_ref[0, pl.ds(r0, chunk), :]
        lff = lff_ref[0, pl.ds(r0, chunk), :]
        lfb = lfb_ref[0, pl.ds(r0, chunk), :]
        cum_f = cumulative(lff, tri_f)
        cum_b = cumulative(lfb, tri_b)
        ff = jnp.exp2(lff)
        fb = jnp.exp2(lfb)
        kf = 1.0 - ff
        kb = 1.0 - fb

        p0 = lax.dot_general(q.astype(BF16), (kf + kb).astype(BF16), nt, preferred_element_type=F32)
        scores = jnp.where(lid == 0, p0, 0.0)
        for m in halves:
            later = (row & m) != 0
            if 2 * m >= SUBLANES:
                d_f = boundary_diff(cum_f, m, False)
                d_b = boundary_diff(cum_b, m, True)
                x = q * jnp.exp2(jnp.where(later, d_f, d_b))
                y = jnp.where(later, kb, kf) * jnp.exp2(-jnp.where(later, d_b, d_f))
            elif m == 2:
                r4 = row & 3
                prev = lambda a: pltpu.roll(a, 1, 0)
                nxt = lambda a: pltpu.roll(a, chunk - 1, 0)
                x = q * jnp.where(r4 == 0, fb * nxt(fb), jnp.where(r4 == 1, fb,
                                  jnp.where(r4 == 2, ff, ff * prev(ff))))
                y = jnp.where(r4 == 0, kf * nxt(ff), jnp.where(r4 == 1, kf,
                              jnp.where(r4 == 2, kb, kb * prev(fb))))
            else:
                x = q * jnp.where(later, ff, fb)
                y = jnp.where(later, kb, kf)
            pm = lax.dot_general(x.astype(BF16), y.astype(BF16), nt, preferred_element_type=F32)
            scores = jnp.where(lid == m.bit_length(), pm, scores)
        oacc[pl.ds(r0, chunk), :] = jnp.dot(scores.astype(BF16), v, preferred_element_type=F32)
        tot_f = cum_f[chunk - 1:chunk, :]
        tot_b = cum_b[0:1, :]
        qf_scr[pl.ds(r0, chunk), :] = (q * jnp.exp2(cum_f)).astype(BF16)
        kf_scr[pl.ds(r0, chunk), :] = (kf * jnp.exp2(tot_f - cum_f)).astype(BF16)
        qb_scr[pl.ds(r0, chunk), :] = (q * jnp.exp2(cum_b)).astype(BF16)
        kb_scr[pl.ds(r0, chunk), :] = (kb * jnp.exp2(tot_b - cum_b)).astype(BF16)

    def intra_body(i, carry):
        for u in range(group):
            intra(i * group + u)
        return carry

    lax.fori_loop(0, nchunks // group, intra_body, 0)

    def apply_state(c, st, q_scr, k_scr, lf_ref):
        r0 = pl.multiple_of(c * chunk, chunk)
        o = oacc[pl.ds(r0, chunk), :] + lax.dot_general(
            q_scr[pl.ds(r0, chunk), :], st.astype(BF16), nt, preferred_element_type=F32)
        tot = jnp.sum(lf_ref[0, pl.ds(r0, chunk), :], axis=0, keepdims=True)
        st = st * jnp.exp2(tot) + lax.dot_general(
            v_ref[0, pl.ds(r0, chunk), :], k_scr[pl.ds(r0, chunk), :], tn, preferred_element_type=F32)
        return r0, o, st

    def finish(r0, o):
        y = o * lax.rsqrt(jnp.mean(o * o, axis=-1, keepdims=True) + EPS) * gain_ref[0]
        o_ref[0, pl.ds(r0, chunk), :] = (y * g_ref[0, pl.ds(r0, chunk), :].astype(F32)).astype(BF16)

    def sweep(final):
        def body(i, states):
            st_f, st_b = states
            rf, of, st_f = apply_state(i, st_f, qf_scr, kf_scr, lff_ref)
            rb, ob, st_b = apply_state(nchunks - 1 - i, st_b, qb_scr, kb_scr, lfb_ref)
            if final:
                finish(rf, of)
                finish(rb, ob)
            else:
                oacc[pl.ds(rf, chunk), :] = of
                oacc[pl.ds(rb, chunk), :] = ob
            return st_f, st_b
        return body

    zero = jnp.zeros((LANES, LANES), F32)
    states = lax.fori_loop(0, nchunks // 2, sweep(False), (zero, zero), unroll=8)
    lax.fori_loop(nchunks // 2, nchunks, sweep(True), states, unroll=8)


def _hgrn(acts3, logf3, hg_norm):
    bsz, seq, _ = acts3.shape
    col = lambda off: pl.BlockSpec((1, seq, LANES), lambda b, h: (b, 0, off + h))
    return pl.pallas_call(
        functools.partial(_hgrn_kernel, chunk=HG_CHUNK, group=8),
        grid=(bsz, HG_HEADS),
        in_specs=[col(ACT_Q), col(ACT_V), col(ACT_G), col(0), col(HG_HEADS),
                  pl.BlockSpec((1, 1, LANES), lambda b, h: (h, 0, 0))],
        out_specs=pl.BlockSpec((1, seq, LANES), lambda b, h: (b, 0, h)),
        out_shape=jax.ShapeDtypeStruct((bsz, seq, HG_HEADS * LANES), BF16),
        scratch_shapes=[pltpu.VMEM((seq, LANES), F32)] + [pltpu.VMEM((seq, LANES), BF16)] * 4,
        compiler_params=_params("parallel", "parallel"),
        name="hgrn2",
    )(acts3, acts3, acts3, logf3, logf3, hg_norm.reshape(HG_HEADS, 1, LANES))


RG_SCAN_SUB = 32


def _scan_tile(a, u, carry, reverse):
    rows = a.shape[0]
    sub = RG_SCAN_SUB
    pos = lax.broadcasted_iota(jnp.int32, a.shape, 0) & (sub - 1)
    d = 1
    while d < sub:
        if reverse:
            ok = pos < sub - d
            shift = rows - d
        else:
            ok = pos >= d
            shift = d
        a_s = jnp.where(ok, pltpu.roll(a, shift, 0), 1.0)
        u_s = jnp.where(ok, pltpu.roll(u, shift, 0), 0.0)
        u = u + a * u_s
        a = a * a_s
        d *= 2
    spans = list(range(rows // sub))
    pieces = [None] * len(spans)
    for k in (reversed(spans) if reverse else spans):
        hk = u[k * sub:(k + 1) * sub] + a[k * sub:(k + 1) * sub] * carry
        carry = hk[0:1] if reverse else hk[sub - 1:sub]
        pieces[k] = hk
    return jnp.concatenate(pieces, axis=0), carry


def _rglru_kernel(rx_ref, gy_ref, cw_ref, cb_ref, wa_ref, ba_ref, wx_ref, bx_ref, lam_ref, o_ref,
                  xpad, hf_scr, ab_scr, ub_scr, *, tile):
    seq = rx_ref.shape[1]
    ntiles = seq // tile
    halo = 8

    xpad[0:halo, :] = jnp.zeros((halo, LANES), F32)
    xpad[halo + seq:halo + seq + halo, :] = jnp.zeros((halo, LANES), F32)
    xpad[halo:halo + seq, :] = rx_ref[0].astype(F32)

    lam = lam_ref[...]
    neg_c_softplus = (-RG_C * LOG2_E) * (jnp.maximum(-lam, 0.0) + jnp.log(1.0 + jnp.exp(-jnp.abs(lam))))

    def gates(xcb, z):
        r = _sigmoid(jnp.dot(xcb, wa_ref[z, 0], preferred_element_type=F32) + ba_ref[z, 0])
        gi = _sigmoid(jnp.dot(xcb, wx_ref[z, 0], preferred_element_type=F32) + bx_ref[z, 0])
        a = jnp.exp2(r * neg_c_softplus[z:z + 1, :])
        y = 1.0 - a * a
        return a, jnp.where(y > 0.0, y * lax.rsqrt(y), 0.0) * gi

    def fwd_body(i, carry):
        t0 = pl.multiple_of(i * tile, tile)
        ext = xpad[pl.ds(t0, tile + 2 * halo), :]
        n_ext = tile + 2 * halo
        tap = lambda off: pltpu.roll(ext, (-off) % n_ext, 0)[halo:halo + tile]
        xc = (cw_ref[0:1, :] * tap(-2) + cw_ref[1:2, :] * tap(-1) + cw_ref[2:3, :] * ext[halo:halo + tile]
              + cw_ref[3:4, :] * tap(1) + cb_ref[...])
        xcb = xc.astype(BF16)
        a_f, s_f = gates(xcb, 0)
        a_b, s_b = gates(xcb, 1)
        h, carry = _scan_tile(a_f, s_f * xc, carry, reverse=False)
        hf_scr[pl.ds(t0, tile), :] = h
        ab_scr[pl.ds(t0, tile), :] = a_b
        ub_scr[pl.ds(t0, tile), :] = s_b * xc
        return carry

    lax.fori_loop(0, ntiles, fwd_body, jnp.zeros((1, LANES), F32))

    def bwd_body(i, carry):
        t0 = pl.multiple_of((ntiles - 1 - i) * tile, tile)
        h, carry = _scan_tile(ab_scr[pl.ds(t0, tile), :], ub_scr[pl.ds(t0, tile), :], carry, reverse=True)
        o_ref[0, pl.ds(t0, tile), :] = (
            (hf_scr[pl.ds(t0, tile), :] + h) * gy_ref[0, pl.ds(t0, tile), :].astype(F32)).astype(BF16)
        return carry

    lax.fori_loop(0, ntiles, bwd_body, jnp.zeros((1, LANES), F32))


def _rglru(acts3, conv_w, conv_b, w_a, b_a, w_x, b_x, lam):
    bsz, seq, _ = acts3.shape
    width = RG_BLOCKS * LANES
    tile = 256
    rx_off, gy_off = ACT_RX, ACT_GY
    col = lambda off: pl.BlockSpec((1, seq, LANES), lambda b, j: (b, 0, off + j))
    wspec = pl.BlockSpec((2, 1, LANES, LANES), lambda b, j: (0, j, 0, 0))
    bspec = pl.BlockSpec((2, 1, 1, LANES), lambda b, j: (0, j, 0, 0))
    return pl.pallas_call(
        functools.partial(_rglru_kernel, tile=tile),
        grid=(bsz, RG_BLOCKS),
        in_specs=[col(rx_off), col(gy_off),
                  pl.BlockSpec((RG_CONV, LANES), lambda b, j: (0, j)),
                  pl.BlockSpec((1, LANES), lambda b, j: (0, j)),
                  wspec, bspec, wspec, bspec,
                  pl.BlockSpec((2, LANES), lambda b, j: (0, j))],
        out_specs=pl.BlockSpec((1, seq, LANES), lambda b, j: (b, 0, j)),
        out_shape=jax.ShapeDtypeStruct((bsz, seq, width), BF16),
        scratch_shapes=[pltpu.VMEM((seq + 16, LANES), F32),
                        pltpu.VMEM((seq, LANES), F32),
                        pltpu.VMEM((seq, LANES), F32),
                        pltpu.VMEM((seq, LANES), F32)],
        compiler_params=_params("parallel", "parallel"),
        name="rglru",
    )(acts3, acts3, conv_w, conv_b.reshape(1, width), w_a.astype(BF16),
      b_a.reshape(2, RG_BLOCKS, 1, LANES), w_x.astype(BF16), b_x.reshape(2, RG_BLOCKS, 1, LANES), lam)


def _store_token_major(ref, x, row0=0):
    ntok, d = x.shape
    parts = d // LANES
    for c in range(parts):
        ref[pl.ds(row0 * parts + c, ntok, stride=parts), :] = x[:, c * LANES:(c + 1) * LANES]


def _load_token_major(ref, ntok, parts, row0=0):
    return jnp.concatenate([ref[pl.ds(row0 * parts + c, ntok, stride=parts), :] for c in range(parts)], axis=1)


def _mixout_kernel(ya_ref, yb_ref, ga_ref, gb_ref, x_ref, gate_ref, scale_ref, shift_ref, gain_ref,
                   wpa_ref, wpb_ref, wo_ref, wr_ref, x1_ref, h2_ref, route_ref, ew_ref, cnt_ref, running):
    y_a = jnp.dot(ya_ref[...], wpa_ref[...], preferred_element_type=F32)
    y_b = jnp.dot(yb_ref[...], wpb_ref[...], preferred_element_type=F32)
    merged = _sigmoid(ga_ref[...].astype(F32)) * y_a + _sigmoid(gb_ref[...].astype(F32)) * y_b
    mix = jnp.dot(merged.astype(BF16), wo_ref[...], preferred_element_type=F32)
    x1 = x_ref[...] + gate_ref[0] * mix
    x1_ref[...] = x1
    y = x1 * lax.rsqrt(jnp.mean(x1 * x1, axis=-1, keepdims=True) + EPS) * gain_ref[...]
    h2 = y * (1.0 + scale_ref[0]) + shift_ref[0]
    _store_token_major(h2_ref, h2)

    h_hi = h2.astype(BF16)
    h_lo = (h2 - h_hi.astype(F32)).astype(BF16)
    p_hi = jnp.dot(h_hi, wr_ref[...], preferred_element_type=F32)
    p_lo = jnp.dot(h_lo, wr_ref[...], preferred_element_type=F32)
    logits = (p_hi[:, :LANES] + p_hi[:, LANES:]) + (p_lo[:, :LANES] + p_lo[:, LANES:])
    lane = lax.broadcasted_iota(jnp.int32, logits.shape, 1)
    neg = -jnp.inf
    gl = jnp.where(lane < N_GROUPS, logits, neg)
    gmax = jnp.max(gl, axis=-1, keepdims=True)
    g_sel = jnp.min(jnp.where(gl == gmax, lane, LANES), axis=-1, keepdims=True)
    p_group = 1.0 / jnp.sum(jnp.exp(gl - gmax), axis=-1, keepdims=True)
    lo = N_GROUPS + g_sel * EXPERTS_PER_GROUP
    el = jnp.where((lane >= lo) & (lane < lo + EXPERTS_PER_GROUP), logits, neg)
    v1 = jnp.max(el, axis=-1, keepdims=True)
    i1 = jnp.min(jnp.where(el == v1, lane, LANES), axis=-1, keepdims=True)
    el2 = jnp.where(lane == i1, neg, el)
    v2 = jnp.max(el2, axis=-1, keepdims=True)
    i2 = jnp.min(jnp.where(el2 == v2, lane, LANES), axis=-1, keepdims=True)
    e21 = jnp.exp(v2 - v1)
    w1 = p_group / (1.0 + e21)
    w2 = p_group * e21 / (1.0 + e21)
    ew_ref[...] = jnp.where(lane == 0, w1, jnp.where(lane == 1, w2, 0.0))

    @pl.when(pl.program_id(0) == 0)
    def _():
        running[...] = jnp.zeros(running.shape, F32)

    e1, e2 = i1 - N_GROUPS, i2 - N_GROUPS
    tm = logits.shape[0]
    before = (lax.broadcasted_iota(jnp.int32, (tm, tm), 0)
              > lax.broadcasted_iota(jnp.int32, (tm, tm), 1))
    before = jnp.where(before, 1.0, 0.0).astype(BF16)
    oh1 = jnp.where(lane == e1, 1.0, 0.0)
    oh2 = jnp.where(lane == e2, 1.0, 0.0)
    pre1 = jnp.dot(before, oh1.astype(BF16), preferred_element_type=F32)
    pre2 = jnp.dot(before, oh2.astype(BF16), preferred_element_type=F32)
    cnt1 = jnp.sum(oh1, axis=0, keepdims=True)
    base = running[...]
    r1 = jnp.sum(oh1 * (pre1 + base), axis=-1, keepdims=True).astype(jnp.int32)
    r2 = jnp.sum(oh2 * (pre2 + base + cnt1), axis=-1, keepdims=True).astype(jnp.int32)
    total = base + cnt1 + jnp.sum(oh2, axis=0, keepdims=True)
    running[...] = total
    cnt_ref[...] = total.astype(jnp.int32)
    cols = jnp.where(lane == 0, e1, jnp.where(lane == 1, e2,
                     jnp.where(lane == 2, r1 >> 7, jnp.where(lane == 3, r1 & 127,
                               jnp.where(lane == 4, r2 >> 7, jnp.where(lane == 5, r2 & 127, 0))))))
    srow = lax.broadcasted_iota(jnp.int32, (SUBLANES, LANES), 0)
    slane = lax.broadcasted_iota(jnp.int32, (SUBLANES, LANES), 1)
    sel = jnp.where((srow < 2) & (slane == srow), 1.0,
                    jnp.where((srow >= 2) & (srow < 4) & (slane == 2 * srow - 2), 128.0,
                              jnp.where((srow >= 2) & (srow < 4) & (slane == 2 * srow - 1), 1.0, 0.0)))
    route_t = lax.dot_general(sel.astype(BF16), cols.astype(F32).astype(BF16), (((1,), (1,)), ((), ())),
                              preferred_element_type=F32)
    route_ref[...] = route_t.astype(jnp.int32)


def _mixout(ya, yb, acts, x2, gate1, scale2, shift2, gain2, wpa, wpb, wo, w_route, seq):
    n, d = x2.shape
    bsz = n // seq
    tm = 256
    tpb = seq // tm
    half = ya.shape[1]
    ga_blk = 0
    per_batch = pl.BlockSpec((1, 1, d), lambda i: (i // tpb, 0, 0))
    const = lambda shape: pl.BlockSpec(shape, lambda i: (0, 0), pipeline_mode=pl.Buffered(1))
    return pl.pallas_call(
        _mixout_kernel,
        grid=(n // tm,),
        in_specs=[pl.BlockSpec((tm, half), lambda i: (i, 0)),
                  pl.BlockSpec((tm, half), lambda i: (i, 0)),
                  pl.BlockSpec((tm, d), lambda i: (i, ga_blk)),
                  pl.BlockSpec((tm, d), lambda i: (i, ga_blk + 1)),
                  pl.BlockSpec((tm, d), lambda i: (i, 0)),
                  per_batch, per_batch, per_batch,
                  const((1, d)), const((half, d)), const((half, d)), const((d, d)), const((d, 2 * LANES))],
        out_specs=[pl.BlockSpec((tm, d), lambda i: (i, 0)),
                   pl.BlockSpec((tm * (d // LANES), LANES), lambda i: (i, 0)),
                   pl.BlockSpec((SUBLANES, tm), lambda i: (0, i)),
                   pl.BlockSpec((tm, LANES), lambda i: (i, 0)),
                   pl.BlockSpec((1, LANES), lambda i: (0, 0))],
        out_shape=[jax.ShapeDtypeStruct((n, d), F32),
                   jax.ShapeDtypeStruct((n * (d // LANES), LANES), F32),
                   jax.ShapeDtypeStruct((SUBLANES, n), jnp.int32),
                   jax.ShapeDtypeStruct((n, LANES), F32),
                   jax.ShapeDtypeStruct((1, LANES), jnp.int32)],
        scratch_shapes=[pltpu.VMEM((1, LANES), F32)],
        compiler_params=_params("arbitrary"),
        name="mixout_route",
    )(ya, yb, acts, acts, x2, gate1.reshape(bsz, 1, d), scale2.reshape(bsz, 1, d),
      shift2.reshape(bsz, 1, d), gain2.reshape(1, d), wpa, wpb, wo, w_route)


def _start_token_gather(idx_ref, base, ntok, parts, src_hbm, dst, sem):
    for r in range(ntok):
        t0 = pl.multiple_of(idx_ref[base + r] * parts, parts)
        pltpu.make_async_copy(src_hbm.at[pl.ds(t0, parts), :], dst.at[pl.ds(r * parts, parts), :], sem).start()


def _wait_token_gather(ntok, parts, src_hbm, dst, sem):
    pltpu.make_async_copy(src_hbm.at[pl.ds(0, ntok * parts), :], dst.at[pl.ds(0, ntok * parts), :], sem).wait()


def _expert_kernel(be_ref, nv_ref, first_ref, next_ref, dest_ref, nused_ref,
                   h_hbm, wg_hbm, wu_hbm, wd_hbm, zeros_hbm, o_ref,
                   xbuf, xsems, row_tok, stage_g, stage_u, stage_d, wg_bf, wu_bf, wd_bf, wsems, *, layer):
    b = pl.program_id(0)
    nb = pl.num_programs(0)
    n_used = nused_ref[0]
    unroll = 8
    cast_rows = 256

    pieces = []
    for hbm, stage, dst in ((wg_hbm, stage_g, wg_bf), (wu_hbm, stage_u, wu_bf), (wd_hbm, stage_d, wd_bf)):
        rows = stage.shape[0] // WEIGHT_PIECES
        pieces += [(hbm, stage, dst, k * rows, rows) for k in range(WEIGHT_PIECES)]

    def weight_copies(e):
        return tuple(pltpu.make_async_copy(hbm.at[layer, e, pl.ds(r0, rows), :], stage.at[pl.ds(r0, rows), :],
                                           wsems.at[i])
                     for i, (hbm, stage, dst, r0, rows) in enumerate(pieces))

    parts = xbuf.shape[1] // MOE_BLOCK
    nslot = xbuf.shape[0]
    ahead = nslot - 1

    def gather(blk, slot):
        _start_token_gather(row_tok, blk * MOE_BLOCK, MOE_BLOCK, parts, h_hbm, xbuf.at[slot], xsems.at[slot])

    def wait_gather(slot):
        _wait_token_gather(MOE_BLOCK, parts, h_hbm, xbuf.at[slot], xsems.at[slot])

    @pl.when(b == 0)
    def _():
        for cp in weight_copies(be_ref[0]):
            cp.start()

        clear = pltpu.make_async_copy(zeros_hbm, row_tok, wsems.at[len(pieces)])
        clear.start()
        clear.wait()

        ntok = dest_ref.shape[0] // TOP_K
        for k in range(TOP_K):
            def invert(i, carry, k=k):
                for u in range(unroll):
                    row_tok[dest_ref[k * ntok + i * unroll + u]] = i * unroll + u
                return carry

            lax.fori_loop(0, ntok // unroll, invert, 0)
        for q in range(ahead):
            gather(q, q)

    @pl.when(b < n_used + ahead)
    def _():
        wait_gather(b % nslot)

    @pl.when(b < n_used)
    def _():
        slot = b % nslot

        @pl.when(first_ref[b] == 1)
        def _():
            for m, (hbm, stage, dst, p0, rows) in enumerate(pieces):
                weight_copies(0)[m].wait()

                def cast(i, carry, stage=stage, dst=dst, p0=p0):
                    r0 = pl.multiple_of(p0 + i * cast_rows, cast_rows)
                    dst[pl.ds(r0, cast_rows), :] = stage[pl.ds(r0, cast_rows), :].astype(BF16)
                    return carry

                lax.fori_loop(0, rows // cast_rows, cast, 0)

                @pl.when(next_ref[b] >= 0)
                def _(m=m):
                    weight_copies(next_ref[b])[m].start()

        valid = lax.broadcasted_iota(jnp.int32, (MOE_BLOCK, 1), 0) < nv_ref[b]
        xb = jnp.where(valid, _load_token_major(xbuf.at[slot], MOE_BLOCK, parts), 0.0).astype(BF16)
        gather(b + ahead, (b + ahead) % nslot)
        gate = jnp.dot(xb, wg_bf[...], preferred_element_type=F32)
        up = jnp.dot(xb, wu_bf[...], preferred_element_type=F32)
        hid = (_silu(gate) * up).astype(BF16)
        _store_token_major(o_ref, jnp.dot(hid, wd_bf[...], preferred_element_type=F32))

    @pl.when(b >= n_used)
    def _():
        o_ref[...] = jnp.zeros(o_ref.shape, o_ref.dtype)

    @pl.when(b == nb - 1)
    def _():
        for q in range(ahead):
            @pl.when(nb + q < n_used + ahead)
            def _(q=q):
                wait_gather((nb + q) % nslot)


def _experts(h2_tm, wg, wu, wd, layer, plan):
    p = plan["rows"]
    nb = p // MOE_BLOCK
    d, de = wg.shape[-2:]
    parts = d // LANES
    map_len = -(-(p + (EXPERT_X_SLOTS - 1) * MOE_BLOCK) // 1024) * 1024
    hbm = pl.BlockSpec(memory_space=pl.ANY)
    grid_spec = pltpu.PrefetchScalarGridSpec(
        num_scalar_prefetch=6,
        grid=(nb,),
        in_specs=[hbm, hbm, hbm, hbm, hbm],
        out_specs=pl.BlockSpec((MOE_BLOCK * parts, LANES), lambda b, *_: (b, 0)),
        scratch_shapes=[pltpu.VMEM((EXPERT_X_SLOTS, MOE_BLOCK * parts, LANES), F32),
                        pltpu.SemaphoreType.DMA((EXPERT_X_SLOTS,)),
                        pltpu.SMEM((map_len,), jnp.int32),
                        pltpu.VMEM((d, de), F32), pltpu.VMEM((d, de), F32), pltpu.VMEM((de, d), F32),
                        pltpu.VMEM((d, de), BF16), pltpu.VMEM((d, de), BF16), pltpu.VMEM((de, d), BF16),
                        pltpu.SemaphoreType.DMA((3 * WEIGHT_PIECES + 1,))],
    )
    return pl.pallas_call(
        functools.partial(_expert_kernel, layer=layer),
        grid_spec=grid_spec,
        out_shape=jax.ShapeDtypeStruct((p * parts, LANES), F32),
        compiler_params=_params("arbitrary"),
        name="experts",
    )(plan["block_expert"], plan["n_valid"], plan["first"], plan["next_expert"], plan["dest"],
      plan["n_used"], h2_tm, wg, wu, wd, jnp.zeros((map_len,), jnp.int32))


COMBINE_TILE = 64
COMBINE_BATCHES = 8


def _combine_kernel(pos_ref, y_hbm, x1_ref, ew_ref, gate_ref, gain_ref, o_ref, *scratch):
    i = pl.program_id(0)
    tm = COMBINE_TILE
    nbuf = COMBINE_BATCHES
    rows = TOP_K * tm
    bufs, sems = scratch[:nbuf], scratch[nbuf]

    parts = x1_ref.shape[1] // LANES

    def gather(tile, k):
        _start_token_gather(pos_ref, tile * rows, rows, parts, y_hbm, bufs[k], sems.at[k])

    def wait(k):
        _wait_token_gather(rows, parts, y_hbm, bufs[k], sems.at[k])

    def finish(k):
        sl = slice(k * tm, (k + 1) * tm)
        ew = ew_ref[sl, :]
        moe = ew[:, 0:1] * _load_token_major(bufs[k], tm, parts)
        for j in range(1, TOP_K):
            moe = moe + ew[:, j:j + 1] * _load_token_major(bufs[k], tm, parts, row0=j * tm)
        x2 = x1_ref[sl, :] + gate_ref[0] * moe
        o_ref[sl, :] = x2 * lax.rsqrt(jnp.mean(x2 * x2, axis=-1, keepdims=True) + EPS) * gain_ref[...]

    ahead = nbuf - 1

    @pl.when(i == 0)
    def _():
        for q in range(ahead):
            gather(q, q)

    for q in range(nbuf):
        wait(q)
        gather(nbuf * i + q + ahead, (q + ahead) % nbuf)
        finish(q)

    @pl.when(i == pl.num_programs(0) - 1)
    def _():
        for q in range(ahead):
            wait(q)


def _combine(ys, pos_tiles, x1, ew, gate2, final_gain, seq):
    n, d = x1.shape
    bsz = n // seq
    tm = COMBINE_BATCHES * COMBINE_TILE
    tpb = seq // tm
    grid_spec = pltpu.PrefetchScalarGridSpec(
        num_scalar_prefetch=1,
        grid=(n // tm,),
        in_specs=[pl.BlockSpec(memory_space=pl.ANY),
                  pl.BlockSpec((tm, d), lambda i, pos: (i, 0)),
                  pl.BlockSpec((tm, LANES), lambda i, pos: (i, 0)),
                  pl.BlockSpec((1, 1, d), lambda i, pos: (i // tpb, 0, 0)),
                  pl.BlockSpec((1, d), lambda i, pos: (0, 0))],
        out_specs=pl.BlockSpec((tm, d), lambda i, pos: (i, 0)),
        scratch_shapes=[pltpu.VMEM((TOP_K * COMBINE_TILE * (d // LANES), LANES), F32)] * COMBINE_BATCHES
        + [pltpu.SemaphoreType.DMA((COMBINE_BATCHES,))],
    )
    return pl.pallas_call(
        _combine_kernel,
        grid_spec=grid_spec,
        out_shape=jax.ShapeDtypeStruct((n, d), F32),
        compiler_params=_params("arbitrary"),
        name="combine_norm",
    )(pos_tiles, ys, x1, ew, gate2.reshape(bsz, 1, d), final_gain.reshape(1, d))


def _dispatch_plan(route, counts):
    n = route.shape[1]
    a = n * TOP_K
    eid = route[:TOP_K]
    rank = route[TOP_K:2 * TOP_K]
    cnt = counts[0, :N_EXPERTS]
    padded = (cnt + MOE_BLOCK - 1) // MOE_BLOCK * MOE_BLOCK
    pad_ends = jnp.cumsum(padded)
    pad_starts = pad_ends - padded
    experts = jnp.arange(N_EXPERTS, dtype=jnp.int32)
    dest = (jnp.sum(jnp.where(eid[..., None] == experts, pad_starts, 0), axis=-1) + rank).astype(jnp.int32)
    p = ((a + MOE_BLOCK - 1) // MOE_BLOCK + N_EXPERTS) * MOE_BLOCK
    nb = p // MOE_BLOCK
    n_used = (pad_ends[-1] // MOE_BLOCK).astype(jnp.int32)
    blk = jnp.arange(nb, dtype=jnp.int32)
    blk_start = jnp.minimum(blk, n_used - 1) * MOE_BLOCK
    block_expert = jnp.minimum(jnp.sum((pad_ends[None, :] <= blk_start[:, None]).astype(jnp.int32), axis=1),
                               N_EXPERTS - 1).astype(jnp.int32)
    def lookup(table, idx):
        return jnp.sum(jnp.where(idx[..., None] == jnp.arange(table.shape[0], dtype=jnp.int32), table, 0), axis=-1)

    blk_pad_start = lookup(pad_starts, block_expert)
    n_valid = jnp.clip(blk_pad_start + lookup(cnt, block_expert) - blk_start, 0, MOE_BLOCK).astype(jnp.int32)
    first = ((blk_start == blk_pad_start) & (blk < n_used)).astype(jnp.int32)
    nxt_blk = lookup(pad_ends, block_expert) // MOE_BLOCK
    next_expert = jnp.where(nxt_blk < n_used, lookup(block_expert, jnp.minimum(nxt_blk, nb - 1)), -1).astype(jnp.int32)
    pos_tiles = dest.reshape(TOP_K, n // COMBINE_TILE, COMBINE_TILE).transpose(1, 0, 2).reshape(a)
    pos_tiles = jnp.concatenate(
        [pos_tiles, jnp.zeros(((COMBINE_BATCHES - 1) * TOP_K * COMBINE_TILE,), jnp.int32)])
    plan = dict(block_expert=block_expert, n_valid=n_valid, first=first, next_expert=next_expert,
                dest=dest.reshape(a), n_used=n_used.reshape(1), rows=p)
    return plan, pos_tiles


def kernel(x, c, w_ada, b_ada, norm1, w_in, hg_lb, hg_norm, rg_conv_w, rg_conv_b, rg_w_a, rg_b_a,
           rg_w_x, rg_b_x, rg_lambda, w_proj_a, w_proj_b, w_out, norm2, w_group, w_router, w_gate,
           w_up, w_down, final_norm):
    bsz, seq, d = x.shape
    depth = w_ada.shape[0]
    n = bsz * seq
    x2 = x.reshape(n, d)
    lb_table = hg_lb.reshape(hg_lb.shape[0], -1)
    for layer in range(depth):
        mod = _adaln(c, w_ada[layer], b_ada[layer])
        shift1, scale1, gate1, shift2, scale2, gate2 = jnp.split(mod, 6, axis=-1)
        logf, acts = _inproj(x2, norm1[layer], scale1, shift1, w_in, lb_table, layer, seq)
        acts3 = acts.reshape(bsz, seq, acts.shape[1])
        ya = _hgrn(acts3, logf.reshape(bsz, seq, logf.shape[1]), hg_norm[layer])
        yb = _rglru(acts3, rg_conv_w[layer], rg_conv_b[layer], rg_w_a[layer], rg_b_a[layer],
                    rg_w_x[layer], rg_b_x[layer], rg_lambda[layer])
        w_route = jnp.concatenate(
            [w_group[layer], w_router[layer].reshape(d, N_EXPERTS),
             jnp.zeros((d, LANES - N_GROUPS - N_EXPERTS), F32)], axis=1)
        w_route_hi = w_route.astype(BF16)
        w_route = jnp.concatenate([w_route_hi, (w_route - w_route_hi.astype(F32)).astype(BF16)], axis=1)
        x1, h2, route, ew, counts = _mixout(
            ya.reshape(n, -1), yb.reshape(n, -1), acts, x2, gate1, scale2, shift2, norm2[layer],
            w_proj_a[layer].astype(BF16), w_proj_b[layer].astype(BF16), w_out[layer].astype(BF16), w_route, seq)
        plan, pos_tiles = _dispatch_plan(route, counts)
        ys = _experts(h2, w_gate, w_up, w_down, layer, plan)
        if layer + 1 < depth:
            raise NotImplementedError("the fused combine + final norm assumes a single layer")
        x2 = _combine(ys, pos_tiles, x1, ew, gate2, final_norm, seq)
    return x2.reshape(bsz, seq, d)
```

```python
import functools

import jax
import jax.numpy as jnp
from jax import lax
from jax.experimental import pallas as pl
from jax.experimental.pallas import tpu as pltpu

EPS = 1e-6
LANES = 128
SUBLANES = 8
HG_HEADS = 8
RG_BLOCKS = 8
RG_CONV = 4
RG_C = 8.0
N_GROUPS = 4
EXPERTS_PER_GROUP = 8
N_EXPERTS = N_GROUPS * EXPERTS_PER_GROUP
TOP_K = 2
HG_CHUNK = 128
MOE_BLOCK = 128
WEIGHT_PIECES = 2
EXPERT_X_SLOTS = 8
VMEM_LIMIT = 56 * 1024 * 1024
ACT_Q, ACT_V, ACT_G, ACT_RX, ACT_GY = 32, 40, 48, 56, 64

F32 = jnp.float32
BF16 = jnp.bfloat16
LOG2_E = 1.4426950408889634


def _sigmoid(x):
    return 1.0 / (1.0 + jnp.exp2(x * -LOG2_E))


def _silu(x):
    return x * _sigmoid(x)


def _gelu_tanh(x):
    return 0.5 * x * (1.0 + jnp.tanh(0.7978845608028654 * (x + 0.044715 * (x * x * x))))


def _params(*sem):
    return pltpu.CompilerParams(dimension_semantics=sem, vmem_limit_bytes=VMEM_LIMIT)


def _adaln_kernel(ct_ref, w_ref, b_ref, o_ref, *, kc):
    kdim, nb = ct_ref.shape
    tn = w_ref.shape[1]

    def body(i, accs):
        k0 = pl.multiple_of(i * kc, kc)
        w = w_ref[pl.ds(k0, kc), :]
        cs = _silu(ct_ref[pl.ds(k0, kc), :])
        return tuple(acc + jnp.sum(w * cs[:, b:b + 1], axis=0, keepdims=True)
                     for b, acc in enumerate(accs))

    accs = lax.fori_loop(0, kdim // kc, body, tuple(jnp.zeros((1, tn), F32) for _ in range(nb)))
    for b, acc in enumerate(accs):
        o_ref[b:b + 1, :] = acc + b_ref[...]


def _adaln(c, w, bias):
    bsz, d = c.shape
    n = w.shape[1]
    tn = 1024
    return pl.pallas_call(
        functools.partial(_adaln_kernel, kc=256),
        grid=(n // tn,),
        in_specs=[pl.BlockSpec((d, bsz), lambda j: (0, 0)),
                  pl.BlockSpec((d, tn), lambda j: (0, j)),
                  pl.BlockSpec((1, tn), lambda j: (0, j))],
        out_specs=pl.BlockSpec((bsz, tn), lambda j: (0, j)),
        out_shape=jax.ShapeDtypeStruct((bsz, n), F32),
        compiler_params=_params("parallel"),
        name="adaln",
    )(c.T, w, bias.reshape(1, n))


def _norm_mod_kernel(x_ref, gain_ref, scale_ref, shift_ref, h_ref):
    rows = 64

    def body(i, carry):
        r0 = pl.multiple_of(i * rows, rows)
        x = x_ref[pl.ds(r0, rows), :]
        y = x * lax.rsqrt(jnp.mean(x * x, axis=-1, keepdims=True) + EPS) * gain_ref[...]
        h_ref[pl.ds(r0, rows), :] = (y * (1.0 + scale_ref[0]) + shift_ref[0]).astype(BF16)
        return carry

    lax.fori_loop(0, x_ref.shape[0] // rows, body, 0)


def _cast_weight_tile(w_ref, w_scr):
    rows = 256

    @pl.when(pl.program_id(1) == 0)
    def _():
        def body(i, carry):
            r0 = pl.multiple_of(i * rows, rows)
            w_scr[pl.ds(r0, rows), :] = w_ref[pl.ds(r0, rows), :].astype(BF16)
            return carry

        lax.fori_loop(0, w_ref.shape[0] // rows, body, 0)


def _inproj_logf_kernel(h_ref, w_ref, lbt_ref, o_ref, w_scr, *, layer):
    _cast_weight_tile(w_ref, w_scr)
    t = lbt_ref[...]
    e = jnp.exp(t - jnp.max(t, axis=0, keepdims=True))
    lb = jnp.sum(e[:layer + 1], axis=0, keepdims=True) / jnp.sum(e, axis=0, keepdims=True)
    z = jnp.dot(h_ref[...], w_scr[...], preferred_element_type=F32)
    o_ref[...] = jnp.log2(lb + (1.0 - lb) * _sigmoid(z))


def _inproj_act_kernel(h_ref, w_ref, o_ref, w_scr):
    j = pl.program_id(0)
    _cast_weight_tile(w_ref, w_scr)
    z = jnp.dot(h_ref[...], w_scr[...], preferred_element_type=F32)

    @pl.when((j == 4) | (j == 6))
    def _():
        o_ref[...] = _silu(z).astype(BF16)

    @pl.when((j < 4) | (j == 5) | (j == 7))
    def _():
        o_ref[...] = z.astype(BF16)

    @pl.when(j == 8)
    def _():
        o_ref[...] = _gelu_tanh(z).astype(BF16)


def _inproj(x2, gain, scale, shift, w_in, lb_table, layer, seq):
    n, d = x2.shape
    bsz = n // seq
    tn = 1024
    tm = 1024
    per_batch = pl.BlockSpec((1, 1, d), lambda i: (i // (seq // tm), 0, 0))
    h = pl.pallas_call(
        _norm_mod_kernel,
        grid=(n // tm,),
        in_specs=[pl.BlockSpec((tm, d), lambda i: (i, 0)), pl.BlockSpec((1, d), lambda i: (0, 0)),
                  per_batch, per_batch],
        out_specs=pl.BlockSpec((tm, d), lambda i: (i, 0)),
        out_shape=jax.ShapeDtypeStruct((n, d), BF16),
        compiler_params=_params("parallel"),
        name="norm_mod",
    )(x2, gain.reshape(1, d), scale.reshape(bsz, 1, d), shift.reshape(bsz, 1, d))

    n_lb = lb_table.shape[0]
    logf = pl.pallas_call(
        functools.partial(_inproj_logf_kernel, layer=layer),
        grid=(2, n // tm),
        in_specs=[pl.BlockSpec((tm, d), lambda j, i: (i, 0)),
                  pl.BlockSpec((None, d, tn), lambda j, i: (layer, 0, j + 2)),
                  pl.BlockSpec((n_lb, tn), lambda j, i: (0, j))],
        out_specs=pl.BlockSpec((tm, tn), lambda j, i: (i, j)),
        out_shape=jax.ShapeDtypeStruct((n, 2 * tn), F32),
        scratch_shapes=[pltpu.VMEM((d, tn), BF16)],
        compiler_params=_params("parallel", "arbitrary"),
        name="inproj_logf",
    )(h, w_in, lb_table)
    acts = pl.pallas_call(
        _inproj_act_kernel,
        grid=(9, n // tm),
        in_specs=[pl.BlockSpec((tm, d), lambda j, i: (i, 0)),
                  pl.BlockSpec((None, d, tn),
                               lambda j, i: (layer, 0, jnp.where(j < 4, j + 7, jnp.where(j < 6, j - 4, j - 2))))],
        out_specs=pl.BlockSpec((tm, tn), lambda j, i: (i, j)),
        out_shape=jax.ShapeDtypeStruct((n, 9 * tn), BF16),
        scratch_shapes=[pltpu.VMEM((d, tn), BF16)],
        compiler_params=_params("parallel", "arbitrary"),
        name="inproj_act",
    )(h, w_in)
    return logf, acts


def _hgrn_level_halves(chunk):
    return [chunk >> (i + 1) for i in range(chunk.bit_length() - 1)]


def _hgrn_kernel(q_ref, v_ref, g_ref, lff_ref, lfb_ref, gain_ref, o_ref,
                 oacc, qf_scr, kf_scr, qb_scr, kb_scr, *, chunk, group):
    seq = q_ref.shape[1]
    nchunks = seq // chunk
    halves = _hgrn_level_halves(chunk)
    nt = (((1,), (1,)), ((), ()))
    tn = (((0,), (0,)), ((), ()))

    row = lax.broadcasted_iota(jnp.int32, (chunk, LANES), 0)
    ti = lax.broadcasted_iota(jnp.int32, (chunk, chunk), 0)
    si = lax.broadcasted_iota(jnp.int32, (chunk, chunk), 1)
    txs = ti ^ si
    tri_f = jnp.where(si <= ti, 1.0, 0.0).astype(BF16)
    tri_b = jnp.where(si >= ti, 1.0, 0.0).astype(BF16)

    def cumulative(lf, tri):
        hi = lf.astype(BF16)
        mid = (lf - hi.astype(F32)).astype(BF16)
        r = jnp.dot(tri, jnp.concatenate([hi, mid], axis=1), preferred_element_type=F32)
        return r[:, :LANES] + r[:, LANES:]

    lid = functools.reduce(lambda a, b: a + b, [(txs >= h).astype(jnp.int32) for h in halves])

    def boundary_diff(cum, m, reverse):
        pieces = []
        for base in range(0, chunk, 2 * m):
            bnd = base + (m if reverse else m - 1)
            pieces.append(jnp.broadcast_to(cum[bnd:bnd + 1, :], (2 * m, LANES)))
        return cum - jnp.concatenate(pieces, axis=0)

    def intra(c):
        r0 = pl.multiple_of(c * chunk, chunk)
        q = q_ref[0, pl.ds(r0, chunk), :].astype(F32)
        v = v_ref[0, pl.ds(r0, chunk), :]
        lff = lff_ref[0, pl.ds(r0, chunk), :]
        lfb = lfb_ref[0, pl.ds(r0, chunk), :]
        cum_f = cumulative(lff, tri_f)
        cum_b = cumulative(lfb, tri_b)
        ff = jnp.exp2(lff)
        fb = jnp.exp2(lfb)
        kf = 1.0 - ff
        kb = 1.0 - fb

        p0 = lax.dot_general(q.astype(BF16), (kf + kb).astype(BF16), nt, preferred_element_type=F32)
        scores = jnp.where(lid == 0, p0, 0.0)
        for m in halves:
            later = (row & m) != 0
            if 2 * m >= SUBLANES:
                d_f = boundary_diff(cum_f, m, False)
                d_b = boundary_diff(cum_b, m, True)
                x = q * jnp.exp2(jnp.where(later, d_f, d_b))
                y = jnp.where(later, kb, kf) * jnp.exp2(-jnp.where(later, d_b, d_f))
            elif m == 2:
                r4 = row & 3
                prev = lambda a: pltpu.roll(a, 1, 0)
                nxt = lambda a: pltpu.roll(a, chunk - 1, 0)
                x = q * jnp.where(r4 == 0, fb * nxt(fb), jnp.where(r4 == 1, fb,
                                  jnp.where(r4 == 2, ff, ff * prev(ff))))
                y = jnp.where(r4 == 0, kf * nxt(ff), jnp.where(r4 == 1, kf,
                              jnp.where(r4 == 2, kb, kb * prev(fb))))
            else:
                x = q * jnp.where(later, ff, fb)
                y = jnp.where(later, kb, kf)
            pm = lax.dot_general(x.astype(BF16), y.astype(BF16), nt, preferred_element_type=F32)
            scores = jnp.where(lid == m.bit_length(), pm, scores)
        oacc[pl.ds(r0, chunk), :] = jnp.dot(scores.astype(BF16), v, preferred_element_type=F32)
        tot_f = cum_f[chunk - 1:chunk, :]
        tot_b = cum_b[0:1, :]
        qf_scr[pl.ds(r0, chunk), :] = (q * jnp.exp2(cum_f)).astype(BF16)
        kf_scr[pl.ds(r0, chunk), :] = (kf * jnp.exp2(tot_f - cum_f)).astype(BF16)
        qb_scr[pl.ds(r0, chunk), :] = (q * jnp.exp2(cum_b)).astype(BF16)
        kb_scr[pl.ds(r0, chunk), :] = (kb * jnp.exp2(tot_b - cum_b)).astype(BF16)

    def intra_body(i, carry):
        for u in range(group):
            intra(i * group + u)
        return carry

    lax.fori_loop(0, nchunks // group, intra_body, 0)

    def apply_state(c, st, q_scr, k_scr, lf_ref):
        r0 = pl.multiple_of(c * chunk, chunk)
        o = oacc[pl.ds(r0, chunk), :] + lax.dot_general(
            q_scr[pl.ds(r0, chunk), :], st.astype(BF16), nt, preferred_element_type=F32)
        tot = jnp.sum(lf_ref[0, pl.ds(r0, chunk), :], axis=0, keepdims=True)
        st = st * jnp.exp2(tot) + lax.dot_general(
            v_ref[0, pl.ds(r0, chunk), :], k_scr[pl.ds(r0, chunk), :], tn, preferred_element_type=F32)
        return r0, o, st

    def finish(r0, o):
        y = o * lax.rsqrt(jnp.mean(o * o, axis=-1, keepdims=True) + EPS) * gain_ref[0]
        o_ref[0, pl.ds(r0, chunk), :] = (y * g_ref[0, pl.ds(r0, chunk), :].astype(F32)).astype(BF16)

    def sweep(final):
        def body(i, states):
            st_f, st_b = states
            rf, of, st_f = apply_state(i, st_f, qf_scr, kf_scr, lff_ref)
            rb, ob, st_b = apply_state(nchunks - 1 - i, st_b, qb_scr, kb_scr, lfb_ref)
            if final:
                finish(rf, of)
                finish(rb, ob)
            else:
                oacc[pl.ds(rf, chunk), :] = of
                oacc[pl.ds(rb, chunk), :] = ob
            return st_f, st_b
        return body

    zero = jnp.zeros((LANES, LANES), F32)
    states = lax.fori_loop(0, nchunks // 2, sweep(False), (zero, zero), unroll=8)
    lax.fori_loop(nchunks // 2, nchunks, sweep(True), states, unroll=8)


def _hgrn(acts3, logf3, hg_norm):
    bsz, seq, _ = acts3.shape
    col = lambda off: pl.BlockSpec((1, seq, LANES), lambda b, h: (b, 0, off + h))
    return pl.pallas_call(
        functools.partial(_hgrn_kernel, chunk=HG_CHUNK, group=8),
        grid=(bsz, HG_HEADS),
        in_specs=[col(ACT_Q), col(ACT_V), col(ACT_G), col(0), col(HG_HEADS),
                  pl.BlockSpec((1, 1, LANES), lambda b, h: (h, 0, 0))],
        out_specs=pl.BlockSpec((1, seq, LANES), lambda b, h: (b, 0, h)),
        out_shape=jax.ShapeDtypeStruct((bsz, seq, HG_HEADS * LANES), BF16),
        scratch_shapes=[pltpu.VMEM((seq, LANES), F32)] + [pltpu.VMEM((seq, LANES), BF16)] * 4,
        compiler_params=_params("parallel", "parallel"),
        name="hgrn2",
    )(acts3, acts3, acts3, logf3, logf3, hg_norm.reshape(HG_HEADS, 1, LANES))


RG_SCAN_SUB = 32


def _scan_tile(a, u, carry, reverse):
    rows = a.shape[0]
    sub = RG_SCAN_SUB
    pos = lax.broadcasted_iota(jnp.int32, a.shape, 0) & (sub - 1)
    d = 1
    while d < sub:
        if reverse:
            ok = pos < sub - d
            shift = rows - d
        else:
            ok = pos >= d
            shift = d
        a_s = jnp.where(ok, pltpu.roll(a, shift, 0), 1.0)
        u_s = jnp.where(ok, pltpu.roll(u, shift, 0), 0.0)
        u = u + a * u_s
        a = a * a_s
        d *= 2
    spans = list(range(rows // sub))
    pieces = [None] * len(spans)
    for k in (reversed(spans) if reverse else spans):
        hk = u[k * sub:(k + 1) * sub] + a[k * sub:(k + 1) * sub] * carry
        carry = hk[0:1] if reverse else hk[sub - 1:sub]
        pieces[k] = hk
    return jnp.concatenate(pieces, axis=0), carry


def _rglru_kernel(rx_ref, gy_ref, cw_ref, cb_ref, wa_ref, ba_ref, wx_ref, bx_ref, lam_ref, o_ref,
                  xpad, hf_scr, ab_scr, ub_scr, *, tile):
    seq = rx_ref.shape[1]
    ntiles = seq // tile
    halo = 8

    xpad[0:halo, :] = jnp.zeros((halo, LANES), F32)
    xpad[halo + seq:halo + seq + halo, :] = jnp.zeros((halo, LANES), F32)
    xpad[halo:halo + seq, :] = rx_ref[0].astype(F32)

    lam = lam_ref[...]
    neg_c_softplus = (-RG_C * LOG2_E) * (jnp.maximum(-lam, 0.0) + jnp.log(1.0 + jnp.exp(-jnp.abs(lam))))

    def gates(xcb, z):
        r = _sigmoid(jnp.dot(xcb, wa_ref[z, 0], preferred_element_type=F32) + ba_ref[z, 0])
        gi = _sigmoid(jnp.dot(xcb, wx_ref[z, 0], preferred_element_type=F32) + bx_ref[z, 0])
        a = jnp.exp2(r * neg_c_softplus[z:z + 1, :])
        y = 1.0 - a * a
        return a, jnp.where(y > 0.0, y * lax.rsqrt(y), 0.0) * gi

    def fwd_body(i, carry):
        t0 = pl.multiple_of(i * tile, tile)
        ext = xpad[pl.ds(t0, tile + 2 * halo), :]
        n_ext = tile + 2 * halo
        tap = lambda off: pltpu.roll(ext, (-off) % n_ext, 0)[halo:halo + tile]
        xc = (cw_ref[0:1, :] * tap(-2) + cw_ref[1:2, :] * tap(-1) + cw_ref[2:3, :] * ext[halo:halo + tile]
              + cw_ref[3:4, :] * tap(1) + cb_ref[...])
        xcb = xc.astype(BF16)
        a_f, s_f = gates(xcb, 0)
        a_b, s_b = gates(xcb, 1)
        h, carry = _scan_tile(a_f, s_f * xc, carry, reverse=False)
        hf_scr[pl.ds(t0, tile), :] = h
        ab_scr[pl.ds(t0, tile), :] = a_b
        ub_scr[pl.ds(t0, tile), :] = s_b * xc
        return carry

    lax.fori_loop(0, ntiles, fwd_body, jnp.zeros((1, LANES), F32))

    def bwd_body(i, carry):
        t0 = pl.multiple_of((ntiles - 1 - i) * tile, tile)
        h, carry = _scan_tile(ab_scr[pl.ds(t0, tile), :], ub_scr[pl.ds(t0, tile), :], carry, reverse=True)
        o_ref[0, pl.ds(t0, tile), :] = (
            (hf_scr[pl.ds(t0, tile), :] + h) * gy_ref[0, pl.ds(t0, tile), :].astype(F32)).astype(BF16)
        return carry

    lax.fori_loop(0, ntiles, bwd_body, jnp.zeros((1, LANES), F32))


def _rglru(acts3, conv_w, conv_b, w_a, b_a, w_x, b_x, lam):
    bsz, seq, _ = acts3.shape
    width = RG_BLOCKS * LANES
    tile = 256
    rx_off, gy_off = ACT_RX, ACT_GY
    col = lambda off: pl.BlockSpec((1, seq, LANES), lambda b, j: (b, 0, off + j))
    wspec = pl.BlockSpec((2, 1, LANES, LANES), lambda b, j: (0, j, 0, 0))
    bspec = pl.BlockSpec((2, 1, 1, LANES), lambda b, j: (0, j, 0, 0))
    return pl.pallas_call(
        functools.partial(_rglru_kernel, tile=tile),
        grid=(bsz, RG_BLOCKS),
        in_specs=[col(rx_off), col(gy_off),
                  pl.BlockSpec((RG_CONV, LANES), lambda b, j: (0, j)),
                  pl.BlockSpec((1, LANES), lambda b, j: (0, j)),
                  wspec, bspec, wspec, bspec,
                  pl.BlockSpec((2, LANES), lambda b, j: (0, j))],
        out_specs=pl.BlockSpec((1, seq, LANES), lambda b, j: (b, 0, j)),
        out_shape=jax.ShapeDtypeStruct((bsz, seq, width), BF16),
        scratch_shapes=[pltpu.VMEM((seq + 16, LANES), F32),
                        pltpu.VMEM((seq, LANES), F32),
                        pltpu.VMEM((seq, LANES), F32),
                        pltpu.VMEM((seq, LANES), F32)],
        compiler_params=_params("parallel", "parallel"),
        name="rglru",
    )(acts3, acts3, conv_w, conv_b.reshape(1, width), w_a.astype(BF16),
      b_a.reshape(2, RG_BLOCKS, 1, LANES), w_x.astype(BF16), b_x.reshape(2, RG_BLOCKS, 1, LANES), lam)


def _store_token_major(ref, x, row0=0):
    ntok, d = x.shape
    parts = d // LANES
    for c in range(parts):
        ref[pl.ds(row0 * parts + c, ntok, stride=parts), :] = x[:, c * LANES:(c + 1) * LANES]


def _load_token_major(ref, ntok, parts, row0=0):
    return jnp.concatenate([ref[pl.ds(row0 * parts + c, ntok, stride=parts), :] for c in range(parts)], axis=1)


def _mixout_kernel(ya_ref, yb_ref, ga_ref, gb_ref, x_ref, gate_ref, scale_ref, shift_ref, gain_ref,
                   wpa_ref, wpb_ref, wo_ref, wr_ref, x1_ref, h2_ref, route_ref, ew_ref, cnt_ref, running):
    y_a = jnp.dot(ya_ref[...], wpa_ref[...], preferred_element_type=F32)
    y_b = jnp.dot(yb_ref[...], wpb_ref[...], preferred_element_type=F32)
    merged = _sigmoid(ga_ref[...].astype(F32)) * y_a + _sigmoid(gb_ref[...].astype(F32)) * y_b
    mix = jnp.dot(merged.astype(BF16), wo_ref[...], preferred_element_type=F32)
    x1 = x_ref[...] + gate_ref[0] * mix
    x1_ref[...] = x1
    y = x1 * lax.rsqrt(jnp.mean(x1 * x1, axis=-1, keepdims=True) + EPS) * gain_ref[...]
    h2 = y * (1.0 + scale_ref[0]) + shift_ref[0]
    _store_token_major(h2_ref, h2)

    h_hi = h2.astype(BF16)
    h_lo = (h2 - h_hi.astype(F32)).astype(BF16)
    p_hi = jnp.dot(h_hi, wr_ref[...], preferred_element_type=F32)
    p_lo = jnp.dot(h_lo, wr_ref[...], preferred_element_type=F32)
    logits = (p_hi[:, :LANES] + p_hi[:, LANES:]) + (p_lo[:, :LANES] + p_lo[:, LANES:])
    lane = lax.broadcasted_iota(jnp.int32, logits.shape, 1)
    neg = -jnp.inf
    gl = jnp.where(lane < N_GROUPS, logits, neg)
    gmax = jnp.max(gl, axis=-1, keepdims=True)
    g_sel = jnp.min(jnp.where(gl == gmax, lane, LANES), axis=-1, keepdims=True)
    p_group = 1.0 / jnp.sum(jnp.exp(gl - gmax), axis=-1, keepdims=True)
    lo = N_GROUPS + g_sel * EXPERTS_PER_GROUP
    el = jnp.where((lane >= lo) & (lane < lo + EXPERTS_PER_GROUP), logits, neg)
    v1 = jnp.max(el, axis=-1, keepdims=True)
    i1 = jnp.min(jnp.where(el == v1, lane, LANES), axis=-1, keepdims=True)
    el2 = jnp.where(lane == i1, neg, el)
    v2 = jnp.max(el2, axis=-1, keepdims=True)
    i2 = jnp.min(jnp.where(el2 == v2, lane, LANES), axis=-1, keepdims=True)
    e21 = jnp.exp(v2 - v1)
    w1 = p_group / (1.0 + e21)
    w2 = p_group * e21 / (1.0 + e21)
    ew_ref[...] = jnp.where(lane == 0, w1, jnp.where(lane == 1, w2, 0.0))

    @pl.when(pl.program_id(0) == 0)
    def _():
        running[...] = jnp.zeros(running.shape, F32)

    e1, e2 = i1 - N_GROUPS, i2 - N_GROUPS
    tm = logits.shape[0]
    before = (lax.broadcasted_iota(jnp.int32, (tm, tm), 0)
              > lax.broadcasted_iota(jnp.int32, (tm, tm), 1))
    before = jnp.where(before, 1.0, 0.0).astype(BF16)
    oh1 = jnp.where(lane == e1, 1.0, 0.0)
    oh2 = jnp.where(lane == e2, 1.0, 0.0)
    pre1 = jnp.dot(before, oh1.astype(BF16), preferred_element_type=F32)
    pre2 = jnp.dot(before, oh2.astype(BF16), preferred_element_type=F32)
    cnt1 = jnp.sum(oh1, axis=0, keepdims=True)
    base = running[...]
    r1 = jnp.sum(oh1 * (pre1 + base), axis=-1, keepdims=True).astype(jnp.int32)
    r2 = jnp.sum(oh2 * (pre2 + base + cnt1), axis=-1, keepdims=True).astype(jnp.int32)
    total = base + cnt1 + jnp.sum(oh2, axis=0, keepdims=True)
    running[...] = total
    cnt_ref[...] = total.astype(jnp.int32)
    cols = jnp.where(lane == 0, e1, jnp.where(lane == 1, e2,
                     jnp.where(lane == 2, r1 >> 7, jnp.where(lane == 3, r1 & 127,
                               jnp.where(lane == 4, r2 >> 7, jnp.where(lane == 5, r2 & 127, 0))))))
    srow = lax.broadcasted_iota(jnp.int32, (SUBLANES, LANES), 0)
    slane = lax.broadcasted_iota(jnp.int32, (SUBLANES, LANES), 1)
    sel = jnp.where((srow < 2) & (slane == srow), 1.0,
                    jnp.where((srow >= 2) & (srow < 4) & (slane == 2 * srow - 2), 128.0,
                              jnp.where((srow >= 2) & (srow < 4) & (slane == 2 * srow - 1), 1.0, 0.0)))
    route_t = lax.dot_general(sel.astype(BF16), cols.astype(F32).astype(BF16), (((1,), (1,)), ((), ())),
                              preferred_element_type=F32)
    route_ref[...] = route_t.astype(jnp.int32)


def _mixout(ya, yb, acts, x2, gate1, scale2, shift2, gain2, wpa, wpb, wo, w_route, seq):
    n, d = x2.shape
    bsz = n // seq
    tm = 256
    tpb = seq // tm
    half = ya.shape[1]
    ga_blk = 0
    per_batch = pl.BlockSpec((1, 1, d), lambda i: (i // tpb, 0, 0))
    const = lambda shape: pl.BlockSpec(shape, lambda i: (0, 0), pipeline_mode=pl.Buffered(1))
    return pl.pallas_call(
        _mixout_kernel,
        grid=(n // tm,),
        in_specs=[pl.BlockSpec((tm, half), lambda i: (i, 0)),
                  pl.BlockSpec((tm, half), lambda i: (i, 0)),
                  pl.BlockSpec((tm, d), lambda i: (i, ga_blk)),
                  pl.BlockSpec((tm, d), lambda i: (i, ga_blk + 1)),
                  pl.BlockSpec((tm, d), lambda i: (i, 0)),
                  per_batch, per_batch, per_batch,
                  const((1, d)), const((half, d)), const((half, d)), const((d, d)), const((d, 2 * LANES))],
        out_specs=[pl.BlockSpec((tm, d), lambda i: (i, 0)),
                   pl.BlockSpec((tm * (d // LANES), LANES), lambda i: (i, 0)),
                   pl.BlockSpec((SUBLANES, tm), lambda i: (0, i)),
                   pl.BlockSpec((tm, LANES), lambda i: (i, 0)),
                   pl.BlockSpec((1, LANES), lambda i: (0, 0))],
        out_shape=[jax.ShapeDtypeStruct((n, d), F32),
                   jax.ShapeDtypeStruct((n * (d // LANES), LANES), F32),
                   jax.ShapeDtypeStruct((SUBLANES, n), jnp.int32),
                   jax.ShapeDtypeStruct((n, LANES), F32),
                   jax.ShapeDtypeStruct((1, LANES), jnp.int32)],
        scratch_shapes=[pltpu.VMEM((1, LANES), F32)],
        compiler_params=_params("arbitrary"),
        name="mixout_route",
    )(ya, yb, acts, acts, x2, gate1.reshape(bsz, 1, d), scale2.reshape(bsz, 1, d),
      shift2.reshape(bsz, 1, d), gain2.reshape(1, d), wpa, wpb, wo, w_route)


def _start_token_gather(idx_ref, base, ntok, parts, src_hbm, dst, sem):
    for r in range(ntok):
        t0 = pl.multiple_of(idx_ref[base + r] * parts, parts)
        pltpu.make_async_copy(src_hbm.at[pl.ds(t0, parts), :], dst.at[pl.ds(r * parts, parts), :], sem).start()


def _wait_token_gather(ntok, parts, src_hbm, dst, sem):
    pltpu.make_async_copy(src_hbm.at[pl.ds(0, ntok * parts), :], dst.at[pl.ds(0, ntok * parts), :], sem).wait()


def _expert_kernel(be_ref, nv_ref, first_ref, next_ref, dest_ref, nused_ref,
                   h_hbm, wg_hbm, wu_hbm, wd_hbm, zeros_hbm, o_ref,
                   xbuf, xsems, row_tok, stage_g, stage_u, stage_d, wg_bf, wu_bf, wd_bf, wsems, *, layer):
    b = pl.program_id(0)
    nb = pl.num_programs(0)
    n_used = nused_ref[0]
    unroll = 8
    cast_rows = 256

    pieces = []
    for hbm, stage, dst in ((wg_hbm, stage_g, wg_bf), (wu_hbm, stage_u, wu_bf), (wd_hbm, stage_d, wd_bf)):
        rows = stage.shape[0] // WEIGHT_PIECES
        pieces += [(hbm, stage, dst, k * rows, rows) for k in range(WEIGHT_PIECES)]

    def weight_copies(e):
        return tuple(pltpu.make_async_copy(hbm.at[layer, e, pl.ds(r0, rows), :], stage.at[pl.ds(r0, rows), :],
                                           wsems.at[i])
                     for i, (hbm, stage, dst, r0, rows) in enumerate(pieces))

    parts = xbuf.shape[1] // MOE_BLOCK
    nslot = xbuf.shape[0]
    ahead = nslot - 1

    def gather(blk, slot):
        _start_token_gather(row_tok, blk * MOE_BLOCK, MOE_BLOCK, parts, h_hbm, xbuf.at[slot], xsems.at[slot])

    def wait_gather(slot):
        _wait_token_gather(MOE_BLOCK, parts, h_hbm, xbuf.at[slot], xsems.at[slot])

    @pl.when(b == 0)
    def _():
        for cp in weight_copies(be_ref[0]):
            cp.start()

        clear = pltpu.make_async_copy(zeros_hbm, row_tok, wsems.at[len(pieces)])
        clear.start()
        clear.wait()

        ntok = dest_ref.shape[0] // TOP_K
        for k in range(TOP_K):
            def invert(i, carry, k=k):
                for u in range(unroll):
                    row_tok[dest_ref[k * ntok + i * unroll + u]] = i * unroll + u
                return carry

            lax.fori_loop(0, ntok // unroll, invert, 0)
        for q in range(ahead):
            gather(q, q)

    @pl.when(b < n_used + ahead)
    def _():
        wait_gather(b % nslot)

    @pl.when(b < n_used)
    def _():
        slot = b % nslot

        @pl.when(first_ref[b] == 1)
        def _():
            for m, (hbm, stage, dst, p0, rows) in enumerate(pieces):
                weight_copies(0)[m].wait()

                def cast(i, carry, stage=stage, dst=dst, p0=p0):
                    r0 = pl.multiple_of(p0 + i * cast_rows, cast_rows)
                    dst[pl.ds(r0, cast_rows), :] = stage[pl.ds(r0, cast_rows), :].astype(BF16)
                    return carry

                lax.fori_loop(0, rows // cast_rows, cast, 0)

                @pl.when(next_ref[b] >= 0)
                def _(m=m):
                    weight_copies(next_ref[b])[m].start()

        valid = lax.broadcasted_iota(jnp.int32, (MOE_BLOCK, 1), 0) < nv_ref[b]
        xb = jnp.where(valid, _load_token_major(xbuf.at[slot], MOE_BLOCK, parts), 0.0).astype(BF16)
        gather(b + ahead, (b + ahead) % nslot)
        gate = jnp.dot(xb, wg_bf[...], preferred_element_type=F32)
        up = jnp.dot(xb, wu_bf[...], preferred_element_type=F32)
        hid = (_silu(gate) * up).astype(BF16)
        _store_token_major(o_ref, jnp.dot(hid, wd_bf[...], preferred_element_type=F32))

    @pl.when(b >= n_used)
    def _():
        o_ref[...] = jnp.zeros(o_ref.shape, o_ref.dtype)

    @pl.when(b == nb - 1)
    def _():
        for q in range(ahead):
            @pl.when(nb + q < n_used + ahead)
            def _(q=q):
                wait_gather((nb + q) % nslot)


def _experts(h2_tm, wg, wu, wd, layer, plan):
    p = plan["rows"]
    nb = p // MOE_BLOCK
    d, de = wg.shape[-2:]
    parts = d // LANES
    map_len = -(-(p + (EXPERT_X_SLOTS - 1) * MOE_BLOCK) // 1024) * 1024
    hbm = pl.BlockSpec(memory_space=pl.ANY)
    grid_spec = pltpu.PrefetchScalarGridSpec(
        num_scalar_prefetch=6,
        grid=(nb,),
        in_specs=[hbm, hbm, hbm, hbm, hbm],
        out_specs=pl.BlockSpec((MOE_BLOCK * parts, LANES), lambda b, *_: (b, 0)),
        scratch_shapes=[pltpu.VMEM((EXPERT_X_SLOTS, MOE_BLOCK * parts, LANES), F32),
                        pltpu.SemaphoreType.DMA((EXPERT_X_SLOTS,)),
                        pltpu.SMEM((map_len,), jnp.int32),
                        pltpu.VMEM((d, de), F32), pltpu.VMEM((d, de), F32), pltpu.VMEM((de, d), F32),
                        pltpu.VMEM((d, de), BF16), pltpu.VMEM((d, de), BF16), pltpu.VMEM((de, d), BF16),
                        pltpu.SemaphoreType.DMA((3 * WEIGHT_PIECES + 1,))],
    )
    return pl.pallas_call(
        functools.partial(_expert_kernel, layer=layer),
        grid_spec=grid_spec,
        out_shape=jax.ShapeDtypeStruct((p * parts, LANES), F32),
        compiler_params=_params("arbitrary"),
        name="experts",
    )(plan["block_expert"], plan["n_valid"], plan["first"], plan["next_expert"], plan["dest"],
      plan["n_used"], h2_tm, wg, wu, wd, jnp.zeros((map_len,), jnp.int32))


COMBINE_TILE = 64
COMBINE_BATCHES = 8


def _combine_kernel(pos_ref, y_hbm, x1_ref, ew_ref, gate_ref, gain_ref, o_ref, *scratch):
    i = pl.program_id(0)
    tm = COMBINE_TILE
    nbuf = COMBINE_BATCHES
    rows = TOP_K * tm
    bufs, sems = scratch[:nbuf], scratch[nbuf]

    parts = x1_ref.shape[1] // LANES

    def gather(tile, k):
        _start_token_gather(pos_ref, tile * rows, rows, parts, y_hbm, bufs[k], sems.at[k])

    def wait(k):
        _wait_token_gather(rows, parts, y_hbm, bufs[k], sems.at[k])

    def finish(k):
        sl = slice(k * tm, (k + 1) * tm)
        ew = ew_ref[sl, :]
        moe = ew[:, 0:1] * _load_token_major(bufs[k], tm, parts)
        for j in range(1, TOP_K):
            moe = moe + ew[:, j:j + 1] * _load_token_major(bufs[k], tm, parts, row0=j * tm)
        x2 = x1_ref[sl, :] + gate_ref[0] * moe
        o_ref[sl, :] = x2 * lax.rsqrt(jnp.mean(x2 * x2, axis=-1, keepdims=True) + EPS) * gain_ref[...]

    ahead = nbuf - 1

    @pl.when(i == 0)
    def _():
        for q in range(ahead):
            gather(q, q)

    for q in range(nbuf):
        wait(q)
        gather(nbuf * i + q + ahead, (q + ahead) % nbuf)
        finish(q)

    @pl.when(i == pl.num_programs(0) - 1)
    def _():
        for q in range(ahead):
            wait(q)


def _combine(ys, pos_tiles, x1, ew, gate2, final_gain, seq):
    n, d = x1.shape
    bsz = n // seq
    tm = COMBINE_BATCHES * COMBINE_TILE
    tpb = seq // tm
    grid_spec = pltpu.PrefetchScalarGridSpec(
        num_scalar_prefetch=1,
        grid=(n // tm,),
        in_specs=[pl.BlockSpec(memory_space=pl.ANY),
                  pl.BlockSpec((tm, d), lambda i, pos: (i, 0)),
                  pl.BlockSpec((tm, LANES), lambda i, pos: (i, 0)),
                  pl.BlockSpec((1, 1, d), lambda i, pos: (i // tpb, 0, 0)),
                  pl.BlockSpec((1, d), lambda i, pos: (0, 0))],
        out_specs=pl.BlockSpec((tm, d), lambda i, pos: (i, 0)),
        scratch_shapes=[pltpu.VMEM((TOP_K * COMBINE_TILE * (d // LANES), LANES), F32)] * COMBINE_BATCHES
        + [pltpu.SemaphoreType.DMA((COMBINE_BATCHES,))],
    )
    return pl.pallas_call(
        _combine_kernel,
        grid_spec=grid_spec,
        out_shape=jax.ShapeDtypeStruct((n, d), F32),
        compiler_params=_params("arbitrary"),
        name="combine_norm",
    )(pos_tiles, ys, x1, ew, gate2.reshape(bsz, 1, d), final_gain.reshape(1, d))


def _dispatch_plan(route, counts):
    n = route.shape[1]
    a = n * TOP_K
    eid = route[:TOP_K]
    rank = route[TOP_K:2 * TOP_K]
    cnt = counts[0, :N_EXPERTS]
    padded = (cnt + MOE_BLOCK - 1) // MOE_BLOCK * MOE_BLOCK
    pad_ends = jnp.cumsum(padded)
    pad_starts = pad_ends - padded
    experts = jnp.arange(N_EXPERTS, dtype=jnp.int32)
    dest = (jnp.sum(jnp.where(eid[..., None] == experts, pad_starts, 0), axis=-1) + rank).astype(jnp.int32)
    p = ((a + MOE_BLOCK - 1) // MOE_BLOCK + N_EXPERTS) * MOE_BLOCK
    nb = p // MOE_BLOCK
    n_used = (pad_ends[-1] // MOE_BLOCK).astype(jnp.int32)
    blk = jnp.arange(nb, dtype=jnp.int32)
    blk_start = jnp.minimum(blk, n_used - 1) * MOE_BLOCK
    block_expert = jnp.minimum(jnp.sum((pad_ends[None, :] <= blk_start[:, None]).astype(jnp.int32), axis=1),
                               N_EXPERTS - 1).astype(jnp.int32)
    def lookup(table, idx):
        return jnp.sum(jnp.where(idx[..., None] == jnp.arange(table.shape[0], dtype=jnp.int32), table, 0), axis=-1)

    blk_pad_start = lookup(pad_starts, block_expert)
    n_valid = jnp.clip(blk_pad_start + lookup(cnt, block_expert) - blk_start, 0, MOE_BLOCK).astype(jnp.int32)
    first = ((blk_start == blk_pad_start) & (blk < n_used)).astype(jnp.int32)
    nxt_blk = lookup(pad_ends, block_expert) // MOE_BLOCK
    next_expert = jnp.where(nxt_blk < n_used, lookup(block_expert, jnp.minimum(nxt_blk, nb - 1)), -1).astype(jnp.int32)
    pos_tiles = dest.reshape(TOP_K, n // COMBINE_TILE, COMBINE_TILE).transpose(1, 0, 2).reshape(a)
    pos_tiles = jnp.concatenate(
        [pos_tiles, jnp.zeros(((COMBINE_BATCHES - 1) * TOP_K * COMBINE_TILE,), jnp.int32)])
    plan = dict(block_expert=block_expert, n_valid=n_valid, first=first, next_expert=next_expert,
                dest=dest.reshape(a), n_used=n_used.reshape(1), rows=p)
    return plan, pos_tiles


def kernel(x, c, w_ada, b_ada, norm1, w_in, hg_lb, hg_norm, rg_conv_w, rg_conv_b, rg_w_a, rg_b_a,
           rg_w_x, rg_b_x, rg_lambda, w_proj_a, w_proj_b, w_out, norm2, w_group, w_router, w_gate,
           w_up, w_down, final_norm):
    bsz, seq, d = x.shape
    depth = w_ada.shape[0]
    n = bsz * seq
    x2 = x.reshape(n, d)
    lb_table = hg_lb.reshape(hg_lb.shape[0], -1)
    for layer in range(depth):
        mod = _adaln(c, w_ada[layer], b_ada[layer])
        shift1, scale1, gate1, shift2, scale2, gate2 = jnp.split(mod, 6, axis=-1)
        logf, acts = _inproj(x2, norm1[layer], scale1, shift1, w_in, lb_table, layer, seq)
        acts3 = acts.reshape(bsz, seq, acts.shape[1])
        ya = _hgrn(acts3, logf.reshape(bsz, seq, logf.shape[1]), hg_norm[layer])
        yb = _rglru(acts3, rg_conv_w[layer], rg_conv_b[layer], rg_w_a[layer], rg_b_a[layer],
                    rg_w_x[layer], rg_b_x[layer], rg_lambda[layer])
        w_route = jnp.concatenate(
            [w_group[layer], w_router[layer].reshape(d, N_EXPERTS),
             jnp.zeros((d, LANES - N_GROUPS - N_EXPERTS), F32)], axis=1)
        w_route_hi = w_route.astype(BF16)
        w_route = jnp.concatenate([w_route_hi, (w_route - w_route_hi.astype(F32)).astype(BF16)], axis=1)
        x1, h2, route, ew, counts = _mixout(
            ya.reshape(n, -1), yb.reshape(n, -1), acts, x2, gate1, scale2, shift2, norm2[layer],
            w_proj_a[layer].astype(BF16), w_proj_b[layer].astype(BF16), w_out[layer].astype(BF16), w_route, seq)
        plan, pos_tiles = _dispatch_plan(route, counts)
        ys = _experts(h2, w_gate, w_up, w_down, layer, plan)
        if layer + 1 < depth:
            raise NotImplementedError("the fused combine + final norm assumes a single layer")
        x2 = _combine(ys, pos_tiles, x1, ew, gate2, final_norm, seq)
    return x2.reshape(bsz, seq, d)
```

```python
import functools

import jax
import jax.numpy as jnp
from jax import lax
from jax.experimental import pallas as pl
from jax.experimental.pallas import tpu as pltpu

EPS = 1e-6
LANES = 128
SUBLANES = 8
HG_HEADS = 8
RG_BLOCKS = 8
RG_CONV = 4
RG_C = 8.0
N_GROUPS = 4
EXPERTS_PER_GROUP = 8
N_EXPERTS = N_GROUPS * EXPERTS_PER_GROUP
TOP_K = 2
HG_CHUNK = 128
MOE_BLOCK = 128
WEIGHT_PIECES = 2
EXPERT_X_SLOTS = 8
VMEM_LIMIT = 56 * 1024 * 1024
ACT_Q, ACT_V, ACT_G, ACT_RX, ACT_GY = 32, 40, 48, 56, 64

F32 = jnp.float32
BF16 = jnp.bfloat16
LOG2_E = 1.4426950408889634


def _sigmoid(x):
    return 1.0 / (1.0 + jnp.exp2(x * -LOG2_E))


def _silu(x):
    return x * _sigmoid(x)


def _gelu_tanh(x):
    return 0.5 * x * (1.0 + jnp.tanh(0.7978845608028654 * (x + 0.044715 * (x * x * x))))


def _params(*sem):
    return pltpu.CompilerParams(dimension_semantics=sem, vmem_limit_bytes=VMEM_LIMIT)


def _adaln_kernel(ct_ref, w_ref, b_ref, o_ref, *, kc):
    kdim, nb = ct_ref.shape
    tn = w_ref.shape[1]

    def body(i, accs):
        k0 = pl.multiple_of(i * kc, kc)
        w = w_ref[pl.ds(k0, kc), :]
        cs = _silu(ct_ref[pl.ds(k0, kc), :])
        return tuple(acc + jnp.sum(w * cs[:, b:b + 1], axis=0, keepdims=True)
                     for b, acc in enumerate(accs))

    accs = lax.fori_loop(0, kdim // kc, body, tuple(jnp.zeros((1, tn), F32) for _ in range(nb)))
    for b, acc in enumerate(accs):
        o_ref[b:b + 1, :] = acc + b_ref[...]


def _adaln(c, w, bias):
    bsz, d = c.shape
    n = w.shape[1]
    tn = 1024
    return pl.pallas_call(
        functools.partial(_adaln_kernel, kc=256),
        grid=(n // tn,),
        in_specs=[pl.BlockSpec((d, bsz), lambda j: (0, 0)),
                  pl.BlockSpec((d, tn), lambda j: (0, j)),
                  pl.BlockSpec((1, tn), lambda j: (0, j))],
        out_specs=pl.BlockSpec((bsz, tn), lambda j: (0, j)),
        out_shape=jax.ShapeDtypeStruct((bsz, n), F32),
        compiler_params=_params("parallel"),
        name="adaln",
    )(c.T, w, bias.reshape(1, n))


def _norm_mod_kernel(x_ref, gain_ref, scale_ref, shift_ref, h_ref):
    rows = 64

    def body(i, carry):
        r0 = pl.multiple_of(i * rows, rows)
        x = x_ref[pl.ds(r0, rows), :]
        y = x * lax.rsqrt(jnp.mean(x * x, axis=-1, keepdims=True) + EPS) * gain_ref[...]
        h_ref[pl.ds(r0, rows), :] = (y * (1.0 + scale_ref[0]) + shift_ref[0]).astype(BF16)
        return carry

    lax.fori_loop(0, x_ref.shape[0] // rows, body, 0)


def _cast_weight_tile(w_ref, w_scr):
    rows = 256

    @pl.when(pl.program_id(1) == 0)
    def _():
        def body(i, carry):
            r0 = pl.multiple_of(i * rows, rows)
            w_scr[pl.ds(r0, rows), :] = w_ref[pl.ds(r0, rows), :].astype(BF16)
            return carry

        lax.fori_loop(0, w_ref.shape[0] // rows, body, 0)


def _inproj_logf_kernel(h_ref, w_ref, lbt_ref, o_ref, w_scr, *, layer):
    _cast_weight_tile(w_ref, w_scr)
    t = lbt_ref[...]
    e = jnp.exp(t - jnp.max(t, axis=0, keepdims=True))
    lb = jnp.sum(e[:layer + 1], axis=0, keepdims=True) / jnp.sum(e, axis=0, keepdims=True)
    z = jnp.dot(h_ref[...], w_scr[...], preferred_element_type=F32)
    o_ref[...] = jnp.log2(lb + (1.0 - lb) * _sigmoid(z))


def _inproj_act_kernel(h_ref, w_ref, o_ref, w_scr):
    j = pl.program_id(0)
    _cast_weight_tile(w_ref, w_scr)
    z = jnp.dot(h_ref[...], w_scr[...], preferred_element_type=F32)

    @pl.when((j == 4) | (j == 6))
    def _():
        o_ref[...] = _silu(z).astype(BF16)

    @pl.when((j < 4) | (j == 5) | (j == 7))
    def _():
        o_ref[...] = z.astype(BF16)

    @pl.when(j == 8)
    def _():
        o_ref[...] = _gelu_tanh(z).astype(BF16)


def _inproj(x2, gain, scale, shift, w_in, lb_table, layer, seq):
    n, d = x2.shape
    bsz = n // seq
    tn = 1024
    tm = 1024
    per_batch = pl.BlockSpec((1, 1, d), lambda i: (i // (seq // tm), 0, 0))
    h = pl.pallas_call(
        _norm_mod_kernel,
        grid=(n // tm,),
        in_specs=[pl.BlockSpec((tm, d), lambda i: (i, 0)), pl.BlockSpec((1, d), lambda i: (0, 0)),
                  per_batch, per_batch],
        out_specs=pl.BlockSpec((tm, d), lambda i: (i, 0)),
        out_shape=jax.ShapeDtypeStruct((n, d), BF16),
        compiler_params=_params("parallel"),
        name="norm_mod",
    )(x2, gain.reshape(1, d), scale.reshape(bsz, 1, d), shift.reshape(bsz, 1, d))

    n_lb = lb_table.shape[0]
    logf = pl.pallas_call(
        functools.partial(_inproj_logf_kernel, layer=layer),
        grid=(2, n // tm),
        in_specs=[pl.BlockSpec((tm, d), lambda j, i: (i, 0)),
                  pl.BlockSpec((None, d, tn), lambda j, i: (layer, 0, j + 2)),
                  pl.BlockSpec((n_lb, tn), lambda j, i: (0, j))],
        out_specs=pl.BlockSpec((tm, tn), lambda j, i: (i, j)),
        out_shape=jax.ShapeDtypeStruct((n, 2 * tn), F32),
        scratch_shapes=[pltpu.VMEM((d, tn), BF16)],
        compiler_params=_params("parallel", "arbitrary"),
        name="inproj_logf",
    )(h, w_in, lb_table)
    acts = pl.pallas_call(
        _inproj_act_kernel,
        grid=(9, n // tm),
        in_specs=[pl.BlockSpec((tm, d), lambda j, i: (i, 0)),
                  pl.BlockSpec((None, d, tn),
                               lambda j, i: (layer, 0, jnp.where(j < 4, j + 7, jnp.where(j < 6, j - 4, j - 2))))],
        out_specs=pl.BlockSpec((tm, tn), lambda j, i: (i, j)),
        out_shape=jax.ShapeDtypeStruct((n, 9 * tn), BF16),
        scratch_shapes=[pltpu.VMEM((d, tn), BF16)],
        compiler_params=_params("parallel", "arbitrary"),
        name="inproj_act",
    )(h, w_in)
    return logf, acts


def _hgrn_level_halves(chunk):
    return [chunk >> (i + 1) for i in range(chunk.bit_length() - 1)]


def _hgrn_kernel(q_ref, v_ref, g_ref, lff_ref, lfb_ref, gain_ref, o_ref,
                 oacc, qf_scr, kf_scr, qb_scr, kb_scr, *, chunk, group):
    seq = q_ref.shape[1]
    nchunks = seq // chunk
    halves = _hgrn_level_halves(chunk)
    nt = (((1,), (1,)), ((), ()))
    tn = (((0,), (0,)), ((), ()))

    row = lax.broadcasted_iota(jnp.int32, (chunk, LANES), 0)
    ti = lax.broadcasted_iota(jnp.int32, (chunk, chunk), 0)
    si = lax.broadcasted_iota(jnp.int32, (chunk, chunk), 1)
    txs = ti ^ si
    tri_f = jnp.where(si <= ti, 1.0, 0.0).astype(BF16)
    tri_b = jnp.where(si >= ti, 1.0, 0.0).astype(BF16)

    def cumulative(lf, tri):
        hi = lf.astype(BF16)
        mid = (lf - hi.astype(F32)).astype(BF16)
        r = jnp.dot(tri, jnp.concatenate([hi, mid], axis=1), preferred_element_type=F32)
        return r[:, :LANES] + r[:, LANES:]

    lid = functools.reduce(lambda a, b: a + b, [(txs >= h).astype(jnp.int32) for h in halves])

    def boundary_diff(cum, m, reverse):
        pieces = []
        for base in range(0, chunk, 2 * m):
            bnd = base + (m if reverse else m - 1)
            pieces.append(jnp.broadcast_to(cum[bnd:bnd + 1, :], (2 * m, LANES)))
        return cum - jnp.concatenate(pieces, axis=0)

    def intra(c):
        r0 = pl.multiple_of(c * chunk, chunk)
        q = q_ref[0, pl.ds(r0, chunk), :].astype(F32)
        v = v_ref[0, pl.ds(r0, chunk), :]
        lff = lff_ref[0, pl.ds(r0, chunk), :]
        lfb = lfb_ref[0, pl.ds(r0, chunk), :]
        cum_f = cumulative(lff, tri_f)
        cum_b = cumulative(lfb, tri_b)
        ff = jnp.exp2(lff)
        fb = jnp.exp2(lfb)
        kf = 1.0 - ff
        kb = 1.0 - fb

        p0 = lax.dot_general(q.astype(BF16), (kf + kb).astype(BF16), nt, preferred_element_type=F32)
        scores = jnp.where(lid == 0, p0, 0.0)
        for m in halves:
            later = (row & m) != 0
            if 2 * m >= SUBLANES:
                d_f = boundary_diff(cum_f, m, False)
                d_b = boundary_diff(cum_b, m, True)
                x = q * jnp.exp2(jnp.where(later, d_f, d_b))
                y = jnp.where(later, kb, kf) * jnp.exp2(-jnp.where(later, d_b, d_f))
            elif m == 2:
                r4 = row & 3
                prev = lambda a: pltpu.roll(a, 1, 0)
                nxt = lambda a: pltpu.roll(a, chunk - 1, 0)
                x = q * jnp.where(r4 == 0, fb * nxt(fb), jnp.where(r4 == 1, fb,
                                  jnp.where(r4 == 2, ff, ff * prev(ff))))
                y = jnp.where(r4 == 0, kf * nxt(ff), jnp.where(r4 == 1, kf,
                              jnp.where(r4 == 2, kb, kb * prev(fb))))
            else:
                x = q * jnp.where(later, ff, fb)
                y = jnp.where(later, kb, kf)
            pm = lax.dot_general(x.astype(BF16), y.astype(BF16), nt, preferred_element_type=F32)
            scores = jnp.where(lid == m.bit_length(), pm, scores)
        oacc[pl.ds(r0, chunk), :] = jnp.dot(scores.astype(BF16), v, preferred_element_type=F32)
        tot_f = cum_f[chunk - 1:chunk, :]
        tot_b = cum_b[0:1, :]
        qf_scr[pl.ds(r0, chunk), :] = (q * jnp.exp2(cum_f)).astype(BF16)
        kf_scr[pl.ds(r0, chunk), :] = (kf * jnp.exp2(tot_f - cum_f)).astype(BF16)
        qb_scr[pl.ds(r0, chunk), :] = (q * jnp.exp2(cum_b)).astype(BF16)
        kb_scr[pl.ds(r0, chunk), :] = (kb * jnp.exp2(tot_b - cum_b)).astype(BF16)

    def intra_body(i, carry):
        for u in range(group):
            intra(i * group + u)
        return carry

    lax.fori_loop(0, nchunks // group, intra_body, 0)

    def apply_state(c, st, q_scr, k_scr, lf_ref):
        r0 = pl.multiple_of(c * chunk, chunk)
        o = oacc[pl.ds(r0, chunk), :] + lax.dot_general(
            q_scr[pl.ds(r0, chunk), :], st.astype(BF16), nt, preferred_element_type=F32)
        tot = jnp.sum(lf_ref[0, pl.ds(r0, chunk), :], axis=0, keepdims=True)
        st = st * jnp.exp2(tot) + lax.dot_general(
            v_ref[0, pl.ds(r0, chunk), :], k_scr[pl.ds(r0, chunk), :], tn, preferred_element_type=F32)
        return r0, o, st

    def finish(r0, o):
        y = o * lax.rsqrt(jnp.mean(o * o, axis=-1, keepdims=True) + EPS) * gain_ref[0]
        o_ref[0, pl.ds(r0, chunk), :] = (y * g_ref[0, pl.ds(r0, chunk), :].astype(F32)).astype(BF16)

    def sweep(final):
        def body(i, states):
            st_f, st_b = states
            rf, of, st_f = apply_state(i, st_f, qf_scr, kf_scr, lff_ref)
            rb, ob, st_b = apply_state(nchunks - 1 - i, st_b, qb_scr, kb_scr, lfb_ref)
            if final:
                finish(rf, of)
                finish(rb, ob)
            else:
                oacc[pl.ds(rf, chunk), :] = of
                oacc[pl.ds(rb, chunk), :] = ob
            return st_f, st_b
        return body

    zero = jnp.zeros((LANES, LANES), F32)
    states = lax.fori_loop(0, nchunks // 2, sweep(False), (zero, zero), unroll=8)
    lax.fori_loop(nchunks // 2, nchunks, sweep(True), states, unroll=8)


def _hgrn(acts3, logf3, hg_norm):
    bsz, seq, _ = acts3.shape
    col = lambda off: pl.BlockSpec((1, seq, LANES), lambda b, h: (b, 0, off + h))
    return pl.pallas_call(
        functools.partial(_hgrn_kernel, chunk=HG_CHUNK, group=8),
        grid=(bsz, HG_HEADS),
        in_specs=[col(ACT_Q), col(ACT_V), col(ACT_G), col(0), col(HG_HEADS),
                  pl.BlockSpec((1, 1, LANES), lambda b, h: (h, 0, 0))],
        out_specs=pl.BlockSpec((1, seq, LANES), lambda b, h: (b, 0, h)),
        out_shape=jax.ShapeDtypeStruct((bsz, seq, HG_HEADS * LANES), BF16),
        scratch_shapes=[pltpu.VMEM((seq, LANES), F32)] + [pltpu.VMEM((seq, LANES), BF16)] * 4,
        compiler_params=_params("parallel", "parallel"),
        name="hgrn2",
    )(acts3, acts3, acts3, logf3, logf3, hg_norm.reshape(HG_HEADS, 1, LANES))


RG_SCAN_SUB = 32


def _scan_tile(a, u, carry, reverse):
    rows = a.shape[0]
    sub = RG_SCAN_SUB
    pos = lax.broadcasted_iota(jnp.int32, a.shape, 0) & (sub - 1)
    d = 1
    while d < sub:
        if reverse:
            ok = pos < sub - d
            shift = rows - d
        else:
            ok = pos >= d
            shift = d
        a_s = jnp.where(ok, pltpu.roll(a, shift, 0), 1.0)
        u_s = jnp.where(ok, pltpu.roll(u, shift, 0), 0.0)
        u = u + a * u_s
        a = a * a_s
        d *= 2
    spans = list(range(rows // sub))
    pieces = [None] * len(spans)
    for k in (reversed(spans) if reverse else spans):
        hk = u[k * sub:(k + 1) * sub] + a[k * sub:(k + 1) * sub] * carry
        carry = hk[0:1] if reverse else hk[sub - 1:sub]
        pieces[k] = hk
    return jnp.concatenate(pieces, axis=0), carry


def _rglru_kernel(rx_ref, gy_ref, cw_ref, cb_ref, wa_ref, ba_ref, wx_ref, bx_ref, lam_ref, o_ref,
                  xpad, hf_scr, ab_scr, ub_scr, *, tile):
    seq = rx_ref.shape[1]
    ntiles = seq // tile
    halo = 8

    xpad[0:halo, :] = jnp.zeros((halo, LANES), F32)
    xpad[halo + seq:halo + seq + halo, :] = jnp.zeros((halo, LANES), F32)
    xpad[halo:halo + seq, :] = rx_ref[0].astype(F32)

    lam = lam_ref[...]
    neg_c_softplus = (-RG_C * LOG2_E) * (jnp.maximum(-lam, 0.0) + jnp.log(1.0 + jnp.exp(-jnp.abs(lam))))

    def gates(xcb, z):
        r = _sigmoid(jnp.dot(xcb, wa_ref[z, 0], preferred_element_type=F32) + ba_ref[z, 0])
        gi = _sigmoid(jnp.dot(xcb, wx_ref[z, 0], preferred_element_type=F32) + bx_ref[z, 0])
        a = jnp.exp2(r * neg_c_softplus[z:z + 1, :])
        y = 1.0 - a * a
        return a, jnp.where(y > 0.0, y * lax.rsqrt(y), 0.0) * gi

    def fwd_body(i, carry):
        t0 = pl.multiple_of(i * tile, tile)
        ext = xpad[pl.ds(t0, tile + 2 * halo), :]
        n_ext = tile + 2 * halo
        tap = lambda off: pltpu.roll(ext, (-off) % n_ext, 0)[halo:halo + tile]
        xc = (cw_ref[0:1, :] * tap(-2) + cw_ref[1:2, :] * tap(-1) + cw_ref[2:3, :] * ext[halo:halo + tile]
              + cw_ref[3:4, :] * tap(1) + cb_ref[...])
        xcb = xc.astype(BF16)
        a_f, s_f = gates(xcb, 0)
        a_b, s_b = gates(xcb, 1)
        h, carry = _scan_tile(a_f, s_f * xc, carry, reverse=False)
        hf_scr[pl.ds(t0, tile), :] = h
        ab_scr[pl.ds(t0, tile), :] = a_b
        ub_scr[pl.ds(t0, tile), :] = s_b * xc
        return carry

    lax.fori_loop(0, ntiles, fwd_body, jnp.zeros((1, LANES), F32))

    def bwd_body(i, carry):
        t0 = pl.multiple_of((ntiles - 1 - i) * tile, tile)
        h, carry = _scan_tile(ab_scr[pl.ds(t0, tile), :], ub_scr[pl.ds(t0, tile), :], carry, reverse=True)
        o_ref[0, pl.ds(t0, tile), :] = (
            (hf_scr[pl.ds(t0, tile), :] + h) * gy_ref[0, pl.ds(t0, tile), :].astype(F32)).astype(BF16)
        return carry

    lax.fori_loop(0, ntiles, bwd_body, jnp.zeros((1, LANES), F32))


def _rglru(acts3, conv_w, conv_b, w_a, b_a, w_x, b_x, lam):
    bsz, seq, _ = acts3.shape
    width = RG_BLOCKS * LANES
    tile = 256
    rx_off, gy_off = ACT_RX, ACT_GY
    col = lambda off: pl.BlockSpec((1, seq, LANES), lambda b, j: (b, 0, off + j))
    wspec = pl.BlockSpec((2, 1, LANES, LANES), lambda b, j: (0, j, 0, 0))
    bspec = pl.BlockSpec((2, 1, 1, LANES), lambda b, j: (0, j, 0, 0))
    return pl.pallas_call(
        functools.partial(_rglru_kernel, tile=tile),
        grid=(bsz, RG_BLOCKS),
        in_specs=[col(rx_off), col(gy_off),
                  pl.BlockSpec((RG_CONV, LANES), lambda b, j: (0, j)),
                  pl.BlockSpec((1, LANES), lambda b, j: (0, j)),
                  wspec, bspec, wspec, bspec,
                  pl.BlockSpec((2, LANES), lambda b, j: (0, j))],
        out_specs=pl.BlockSpec((1, seq, LANES), lambda b, j: (b, 0, j)),
        out_shape=jax.ShapeDtypeStruct((bsz, seq, width), BF16),
        scratch_shapes=[pltpu.VMEM((seq + 16, LANES), F32),
                        pltpu.VMEM((seq, LANES), F32),
                        pltpu.VMEM((seq, LANES), F32),
                        pltpu.VMEM((seq, LANES), F32)],
        compiler_params=_params("parallel", "parallel"),
        name="rglru",
    )(acts3, acts3, conv_w, conv_b.reshape(1, width), w_a.astype(BF16),
      b_a.reshape(2, RG_BLOCKS, 1, LANES), w_x.astype(BF16), b_x.reshape(2, RG_BLOCKS, 1, LANES), lam)


def _store_token_major(ref, x, row0=0):
    ntok, d = x.shape
    parts = d // LANES
    for c in range(parts):
        ref[pl.ds(row0 * parts + c, ntok, stride=parts), :] = x[:, c * LANES:(c + 1) * LANES]


def _load_token_major(ref, ntok, parts, row0=0):
    return jnp.concatenate([ref[pl.ds(row0 * parts + c, ntok, stride=parts), :] for c in range(parts)], axis=1)


def _mixout_kernel(ya_ref, yb_ref, ga_ref, gb_ref, x_ref, gate_ref, scale_ref, shift_ref, gain_ref,
                   wpa_ref, wpb_ref, wo_ref, wr_ref, x1_ref, h2_ref, route_ref, ew_ref, cnt_ref, running):
    y_a = jnp.dot(ya_ref[...], wpa_ref[...], preferred_element_type=F32)
    y_b = jnp.dot(yb_ref[...], wpb_ref[...], preferred_element_type=F32)
    merged = _sigmoid(ga_ref[...].astype(F32)) * y_a + _sigmoid(gb_ref[...].astype(F32)) * y_b
    mix = jnp.dot(merged.astype(BF16), wo_ref[...], preferred_element_type=F32)
    x1 = x_ref[...] + gate_ref[0] * mix
    x1_ref[...] = x1
    y = x1 * lax.rsqrt(jnp.mean(x1 * x1, axis=-1, keepdims=True) + EPS) * gain_ref[...]
    h2 = y * (1.0 + scale_ref[0]) + shift_ref[0]
    _store_token_major(h2_ref, h2)

    h_hi = h2.astype(BF16)
    h_lo = (h2 - h_hi.astype(F32)).astype(BF16)
    p_hi = jnp.dot(h_hi, wr_ref[...], preferred_element_type=F32)
    p_lo = jnp.dot(h_lo, wr_ref[...], preferred_element_type=F32)
    logits = (p_hi[:, :LANES] + p_hi[:, LANES:]) + (p_lo[:, :LANES] + p_lo[:, LANES:])
    lane = lax.broadcasted_iota(jnp.int32, logits.shape, 1)
    neg = -jnp.inf
    gl = jnp.where(lane < N_GROUPS, logits, neg)
    gmax = jnp.max(gl, axis=-1, keepdims=True)
    g_sel = jnp.min(jnp.where(gl == gmax, lane, LANES), axis=-1, keepdims=True)
    p_group = 1.0 / jnp.sum(jnp.exp(gl - gmax), axis=-1, keepdims=True)
    lo = N_GROUPS + g_sel * EXPERTS_PER_GROUP
    el = jnp.where((lane >= lo) & (lane < lo + EXPERTS_PER_GROUP), logits, neg)
    v1 = jnp.max(el, axis=-1, keepdims=True)
    i1 = jnp.min(jnp.where(el == v1, lane, LANES), axis=-1, keepdims=True)
    el2 = jnp.where(lane == i1, neg, el)
    v2 = jnp.max(el2, axis=-1, keepdims=True)
    i2 = jnp.min(jnp.where(el2 == v2, lane, LANES), axis=-1, keepdims=True)
    e21 = jnp.exp(v2 - v1)
    w1 = p_group / (1.0 + e21)
    w2 = p_group * e21 / (1.0 + e21)
    ew_ref[...] = jnp.where(lane == 0, w1, jnp.where(lane == 1, w2, 0.0))

    @pl.when(pl.program_id(0) == 0)
    def _():
        running[...] = jnp.zeros(running.shape, F32)

    e1, e2 = i1 - N_GROUPS, i2 - N_GROUPS
    tm = logits.shape[0]
    before = (lax.broadcasted_iota(jnp.int32, (tm, tm), 0)
              > lax.broadcasted_iota(jnp.int32, (tm, tm), 1))
    before = jnp.where(before, 1.0, 0.0).astype(BF16)
    oh1 = jnp.where(lane == e1, 1.0, 0.0)
    oh2 = jnp.where(lane == e2, 1.0, 0.0)
    pre1 = jnp.dot(before, oh1.astype(BF16), preferred_element_type=F32)
    pre2 = jnp.dot(before, oh2.astype(BF16), preferred_element_type=F32)
    cnt1 = jnp.sum(oh1, axis=0, keepdims=True)
    base = running[...]
    r1 = jnp.sum(oh1 * (pre1 + base), axis=-1, keepdims=True).astype(jnp.int32)
    r2 = jnp.sum(oh2 * (pre2 + base + cnt1), axis=-1, keepdims=True).astype(jnp.int32)
    total = base + cnt1 + jnp.sum(oh2, axis=0, keepdims=True)
    running[...] = total
    cnt_ref[...] = total.astype(jnp.int32)
    cols = jnp.where(lane == 0, e1, jnp.where(lane == 1, e2,
                     jnp.where(lane == 2, r1 >> 7, jnp.where(lane == 3, r1 & 127,
                               jnp.where(lane == 4, r2 >> 7, jnp.where(lane == 5, r2 & 127, 0))))))
    srow = lax.broadcasted_iota(jnp.int32, (SUBLANES, LANES), 0)
    slane = lax.broadcasted_iota(jnp.int32, (SUBLANES, LANES), 1)
    sel = jnp.where((srow < 2) & (slane == srow), 1.0,
                    jnp.where((srow >= 2) & (srow < 4) & (slane == 2 * srow - 2), 128.0,
                              jnp.where((srow >= 2) & (srow < 4) & (slane == 2 * srow - 1), 1.0, 0.0)))
    route_t = lax.dot_general(sel.astype(BF16), cols.astype(F32).astype(BF16), (((1,), (1,)), ((), ())),
                              preferred_element_type=F32)
    route_ref[...] = route_t.astype(jnp.int32)


def _mixout(ya, yb, acts, x2, gate1, scale2, shift2, gain2, wpa, wpb, wo, w_route, seq):
    n, d = x2.shape
    bsz = n // seq
    tm = 256
    tpb = seq // tm
    half = ya.shape[1]
    ga_blk = 0
    per_batch = pl.BlockSpec((1, 1, d), lambda i: (i // tpb, 0, 0))
    const = lambda shape: pl.BlockSpec(shape, lambda i: (0, 0), pipeline_mode=pl.Buffered(1))
    return pl.pallas_call(
        _mixout_kernel,
        grid=(n // tm,),
        in_specs=[pl.BlockSpec((tm, half), lambda i: (i, 0)),
                  pl.BlockSpec((tm, half), lambda i: (i, 0)),
                  pl.BlockSpec((tm, d), lambda i: (i, ga_blk)),
                  pl.BlockSpec((tm, d), lambda i: (i, ga_blk + 1)),
                  pl.BlockSpec((tm, d), lambda i: (i, 0)),
                  per_batch, per_batch, per_batch,
                  const((1, d)), const((half, d)), const((half, d)), const((d, d)), const((d, 2 * LANES))],
        out_specs=[pl.BlockSpec((tm, d), lambda i: (i, 0)),
                   pl.BlockSpec((tm * (d // LANES), LANES), lambda i: (i, 0)),
                   pl.BlockSpec((SUBLANES, tm), lambda i: (0, i)),
                   pl.BlockSpec((tm, LANES), lambda i: (i, 0)),
                   pl.BlockSpec((1, LANES), lambda i: (0, 0))],
        out_shape=[jax.ShapeDtypeStruct((n, d), F32),
                   jax.ShapeDtypeStruct((n * (d // LANES), LANES), F32),
                   jax.ShapeDtypeStruct((SUBLANES, n), jnp.int32),
                   jax.ShapeDtypeStruct((n, LANES), F32),
                   jax.ShapeDtypeStruct((1, LANES), jnp.int32)],
        scratch_shapes=[pltpu.VMEM((1, LANES), F32)],
        compiler_params=_params("arbitrary"),
        name="mixout_route",
    )(ya, yb, acts, acts, x2, gate1.reshape(bsz, 1, d), scale2.reshape(bsz, 1, d),
      shift2.reshape(bsz, 1, d), gain2.reshape(1, d), wpa, wpb, wo, w_route)


def _start_token_gather(idx_ref, base, ntok, parts, src_hbm, dst, sem, alternate):
    for r in range(ntok):
        t0 = pl.multiple_of(idx_ref[base + r] * parts, parts)
        pltpu.make_async_copy(src_hbm.at[pl.ds(t0, parts), :], dst.at[pl.ds(r * parts, parts), :], sem).start(
            priority=r % 2 if alternate else 0)


def _wait_token_gather(ntok, parts, src_hbm, dst, sem):
    pltpu.make_async_copy(src_hbm.at[pl.ds(0, ntok * parts), :], dst.at[pl.ds(0, ntok * parts), :], sem).wait()


def _expert_kernel(be_ref, nv_ref, first_ref, next_ref, dest_ref, nused_ref,
                   h_hbm, wg_hbm, wu_hbm, wd_hbm, zeros_hbm, o_ref,
                   xbuf, xsems, row_tok, stage_g, stage_u, stage_d, wg_bf, wu_bf, wd_bf, wsems, *, layer):
    b = pl.program_id(0)
    nb = pl.num_programs(0)
    n_used = nused_ref[0]
    unroll = 8
    cast_rows = 256

    pieces = []
    for hbm, stage, dst in ((wg_hbm, stage_g, wg_bf), (wu_hbm, stage_u, wu_bf), (wd_hbm, stage_d, wd_bf)):
        rows = stage.shape[0] // WEIGHT_PIECES
        pieces += [(hbm, stage, dst, k * rows, rows) for k in range(WEIGHT_PIECES)]

    def weight_copies(e):
        return tuple(pltpu.make_async_copy(hbm.at[layer, e, pl.ds(r0, rows), :], stage.at[pl.ds(r0, rows), :],
                                           wsems.at[i])
                     for i, (hbm, stage, dst, r0, rows) in enumerate(pieces))

    parts = xbuf.shape[1] // MOE_BLOCK
    nslot = xbuf.shape[0]
    ahead = nslot - 1

    def gather(blk, slot):
        _start_token_gather(row_tok, blk * MOE_BLOCK, MOE_BLOCK, parts, h_hbm, xbuf.at[slot], xsems.at[slot],
                            alternate=False)

    def wait_gather(slot):
        _wait_token_gather(MOE_BLOCK, parts, h_hbm, xbuf.at[slot], xsems.at[slot])

    @pl.when(b == 0)
    def _():
        for cp in weight_copies(be_ref[0]):
            cp.start(priority=1)

        clear = pltpu.make_async_copy(zeros_hbm, row_tok, wsems.at[len(pieces)])
        clear.start()
        clear.wait()

        ntok = dest_ref.shape[0] // TOP_K
        for k in range(TOP_K):
            def invert(i, carry, k=k):
                for u in range(unroll):
                    row_tok[dest_ref[k * ntok + i * unroll + u]] = i * unroll + u
                return carry

            lax.fori_loop(0, ntok // unroll, invert, 0)
        for q in range(ahead):
            gather(q, q)

    @pl.when(b < n_used + ahead)
    def _():
        wait_gather(b % nslot)

    @pl.when(b < n_used)
    def _():
        slot = b % nslot

        @pl.when(first_ref[b] == 1)
        def _():
            for m, (hbm, stage, dst, p0, rows) in enumerate(pieces):
                weight_copies(0)[m].wait()

                def cast(i, carry, stage=stage, dst=dst, p0=p0):
                    r0 = pl.multiple_of(p0 + i * cast_rows, cast_rows)
                    dst[pl.ds(r0, cast_rows), :] = stage[pl.ds(r0, cast_rows), :].astype(BF16)
                    return carry

                lax.fori_loop(0, rows // cast_rows, cast, 0)

                @pl.when(next_ref[b] >= 0)
                def _(m=m):
                    weight_copies(next_ref[b])[m].start(priority=1)

        valid = lax.broadcasted_iota(jnp.int32, (MOE_BLOCK, 1), 0) < nv_ref[b]
        xb = jnp.where(valid, _load_token_major(xbuf.at[slot], MOE_BLOCK, parts), 0.0).astype(BF16)
        gather(b + ahead, (b + ahead) % nslot)
        gate = jnp.dot(xb, wg_bf[...], preferred_element_type=F32)
        up = jnp.dot(xb, wu_bf[...], preferred_element_type=F32)
        hid = (_silu(gate) * up).astype(BF16)
        _store_token_major(o_ref, jnp.dot(hid, wd_bf[...], preferred_element_type=F32))

    @pl.when(b >= n_used)
    def _():
        o_ref[...] = jnp.zeros(o_ref.shape, o_ref.dtype)

    @pl.when(b == nb - 1)
    def _():
        for q in range(ahead):
            @pl.when(nb + q < n_used + ahead)
            def _(q=q):
                wait_gather((nb + q) % nslot)


def _experts(h2_tm, wg, wu, wd, layer, plan):
    p = plan["rows"]
    nb = p // MOE_BLOCK
    d, de = wg.shape[-2:]
    parts = d // LANES
    map_len = -(-(p + (EXPERT_X_SLOTS - 1) * MOE_BLOCK) // 1024) * 1024
    hbm = pl.BlockSpec(memory_space=pl.ANY)
    grid_spec = pltpu.PrefetchScalarGridSpec(
        num_scalar_prefetch=6,
        grid=(nb,),
        in_specs=[hbm, hbm, hbm, hbm, hbm],
        out_specs=pl.BlockSpec((MOE_BLOCK * parts, LANES), lambda b, *_: (b, 0)),
        scratch_shapes=[pltpu.VMEM((EXPERT_X_SLOTS, MOE_BLOCK * parts, LANES), F32),
                        pltpu.SemaphoreType.DMA((EXPERT_X_SLOTS,)),
                        pltpu.SMEM((map_len,), jnp.int32),
                        pltpu.VMEM((d, de), F32), pltpu.VMEM((d, de), F32), pltpu.VMEM((de, d), F32),
                        pltpu.VMEM((d, de), BF16), pltpu.VMEM((d, de), BF16), pltpu.VMEM((de, d), BF16),
                        pltpu.SemaphoreType.DMA((3 * WEIGHT_PIECES + 1,))],
    )
    return pl.pallas_call(
        functools.partial(_expert_kernel, layer=layer),
        grid_spec=grid_spec,
        out_shape=jax.ShapeDtypeStruct((p * parts, LANES), F32),
        compiler_params=_params("arbitrary"),
        name="experts",
    )(plan["block_expert"], plan["n_valid"], plan["first"], plan["next_expert"], plan["dest"],
      plan["n_used"], h2_tm, wg, wu, wd, jnp.zeros((map_len,), jnp.int32))


COMBINE_TILE = 64
COMBINE_BATCHES = 8


def _combine_kernel(pos_ref, y_hbm, x1_ref, ew_ref, gate_ref, gain_ref, o_ref, *scratch):
    i = pl.program_id(0)
    tm = COMBINE_TILE
    nbuf = COMBINE_BATCHES
    rows = TOP_K * tm
    bufs, sems = scratch[:nbuf], scratch[nbuf]

    parts = x1_ref.shape[1] // LANES

    def gather(tile, k):
        _start_token_gather(pos_ref, tile * rows, rows, parts, y_hbm, bufs[k], sems.at[k], alternate=True)

    def wait(k):
        _wait_token_gather(rows, parts, y_hbm, bufs[k], sems.at[k])

    def finish(k):
        sl = slice(k * tm, (k + 1) * tm)
        ew = ew_ref[sl, :]
        moe = ew[:, 0:1] * _load_token_major(bufs[k], tm, parts)
        for j in range(1, TOP_K):
            moe = moe + ew[:, j:j + 1] * _load_token_major(bufs[k], tm, parts, row0=j * tm)
        x2 = x1_ref[sl, :] + gate_ref[0] * moe
        o_ref[sl, :] = x2 * lax.rsqrt(jnp.mean(x2 * x2, axis=-1, keepdims=True) + EPS) * gain_ref[...]

    ahead = nbuf - 1

    @pl.when(i == 0)
    def _():
        for q in range(ahead):
            gather(q, q)

    for q in range(nbuf):
        wait(q)
        gather(nbuf * i + q + ahead, (q + ahead) % nbuf)
        finish(q)

    @pl.when(i == pl.num_programs(0) - 1)
    def _():
        for q in range(ahead):
            wait(q)


def _combine(ys, pos_tiles, x1, ew, gate2, final_gain, seq):
    n, d = x1.shape
    bsz = n // seq
    tm = COMBINE_BATCHES * COMBINE_TILE
    tpb = seq // tm
    grid_spec = pltpu.PrefetchScalarGridSpec(
        num_scalar_prefetch=1,
        grid=(n // tm,),
        in_specs=[pl.BlockSpec(memory_space=pl.ANY),
                  pl.BlockSpec((tm, d), lambda i, pos: (i, 0)),
                  pl.BlockSpec((tm, LANES), lambda i, pos: (i, 0)),
                  pl.BlockSpec((1, 1, d), lambda i, pos: (i // tpb, 0, 0)),
                  pl.BlockSpec((1, d), lambda i, pos: (0, 0))],
        out_specs=pl.BlockSpec((tm, d), lambda i, pos: (i, 0)),
        scratch_shapes=[pltpu.VMEM((TOP_K * COMBINE_TILE * (d // LANES), LANES), F32)] * COMBINE_BATCHES
        + [pltpu.SemaphoreType.DMA((COMBINE_BATCHES,))],
    )
    return pl.pallas_call(
        _combine_kernel,
        grid_spec=grid_spec,
        out_shape=jax.ShapeDtypeStruct((n, d), F32),
        compiler_params=_params("arbitrary"),
        name="combine_norm",
    )(pos_tiles, ys, x1, ew, gate2.reshape(bsz, 1, d), final_gain.reshape(1, d))


def _dispatch_plan(route, counts):
    n = route.shape[1]
    a = n * TOP_K
    eid = route[:TOP_K]
    rank = route[TOP_K:2 * TOP_K]
    cnt = counts[0, :N_EXPERTS]
    padded = (cnt + MOE_BLOCK - 1) // MOE_BLOCK * MOE_BLOCK
    pad_ends = jnp.cumsum(padded)
    pad_starts = pad_ends - padded
    experts = jnp.arange(N_EXPERTS, dtype=jnp.int32)
    dest = (jnp.sum(jnp.where(eid[..., None] == experts, pad_starts, 0), axis=-1) + rank).astype(jnp.int32)
    p = ((a + MOE_BLOCK - 1) // MOE_BLOCK + N_EXPERTS) * MOE_BLOCK
    nb = p // MOE_BLOCK
    n_used = (pad_ends[-1] // MOE_BLOCK).astype(jnp.int32)
    blk = jnp.arange(nb, dtype=jnp.int32)
    blk_start = jnp.minimum(blk, n_used - 1) * MOE_BLOCK
    block_expert = jnp.minimum(jnp.sum((pad_ends[None, :] <= blk_start[:, None]).astype(jnp.int32), axis=1),
                               N_EXPERTS - 1).astype(jnp.int32)
    def lookup(table, idx):
        return jnp.sum(jnp.where(idx[..., None] == jnp.arange(table.shape[0], dtype=jnp.int32), table, 0), axis=-1)

    blk_pad_start = lookup(pad_starts, block_expert)
    n_valid = jnp.clip(blk_pad_start + lookup(cnt, block_expert) - blk_start, 0, MOE_BLOCK).astype(jnp.int32)
    first = ((blk_start == blk_pad_start) & (blk < n_used)).astype(jnp.int32)
    nxt_blk = lookup(pad_ends, block_expert) // MOE_BLOCK
    next_expert = jnp.where(nxt_blk < n_used, lookup(block_expert, jnp.minimum(nxt_blk, nb - 1)), -1).astype(jnp.int32)
    pos_tiles = dest.reshape(TOP_K, n // COMBINE_TILE, COMBINE_TILE).transpose(1, 0, 2).reshape(a)
    pos_tiles = jnp.concatenate(
        [pos_tiles, jnp.zeros(((COMBINE_BATCHES - 1) * TOP_K * COMBINE_TILE,), jnp.int32)])
    plan = dict(block_expert=block_expert, n_valid=n_valid, first=first, next_expert=next_expert,
                dest=dest.reshape(a), n_used=n_used.reshape(1), rows=p)
    return plan, pos_tiles


def kernel(x, c, w_ada, b_ada, norm1, w_in, hg_lb, hg_norm, rg_conv_w, rg_conv_b, rg_w_a, rg_b_a,
           rg_w_x, rg_b_x, rg_lambda, w_proj_a, w_proj_b, w_out, norm2, w_group, w_router, w_gate,
           w_up, w_down, final_norm):
    bsz, seq, d = x.shape
    depth = w_ada.shape[0]
    n = bsz * seq
    x2 = x.reshape(n, d)
    lb_table = hg_lb.reshape(hg_lb.shape[0], -1)
    for layer in range(depth):
        mod = _adaln(c, w_ada[layer], b_ada[layer])
        shift1, scale1, gate1, shift2, scale2, gate2 = jnp.split(mod, 6, axis=-1)
        logf, acts = _inproj(x2, norm1[layer], scale1, shift1, w_in, lb_table, layer, seq)
        acts3 = acts.reshape(bsz, seq, acts.shape[1])
        ya = _hgrn(acts3, logf.reshape(bsz, seq, logf.shape[1]), hg_norm[layer])
        yb = _rglru(acts3, rg_conv_w[layer], rg_conv_b[layer], rg_w_a[layer], rg_b_a[layer],
                    rg_w_x[layer], rg_b_x[layer], rg_lambda[layer])
        w_route = jnp.concatenate(
            [w_group[layer], w_router[layer].reshape(d, N_EXPERTS),
             jnp.zeros((d, LANES - N_GROUPS - N_EXPERTS), F32)], axis=1)
        w_route_hi = w_route.astype(BF16)
        w_route = jnp.concatenate([w_route_hi, (w_route - w_route_hi.astype(F32)).astype(BF16)], axis=1)
        x1, h2, route, ew, counts = _mixout(
            ya.reshape(n, -1), yb.reshape(n, -1), acts, x2, gate1, scale2, shift2, norm2[layer],
            w_proj_a[layer].astype(BF16), w_proj_b[layer].astype(BF16), w_out[layer].astype(BF16), w_route, seq)
        plan, pos_tiles = _dispatch_plan(route, counts)
        ys = _experts(h2, w_gate, w_up, w_down, layer, plan)
        if layer + 1 < depth:
            raise NotImplementedError("the fused combine + final norm assumes a single layer")
        x2 = _combine(ys, pos_tiles, x1, ew, gate2, final_norm, seq)
    return x2.reshape(bsz, seq, d)
```
